```python
import math
import jax, jax.numpy as jnp
from jax import lax
import numpy as np

D_MODEL = 2048
BATCH = 1
SEQ = 8192
DEPTH = 2
DEC_BATCH = 32
DEC_SEQ = 16
PAST_LEN = 2048

CHUNK = 64
N_MIXERS = 2
N_RET_LAYERS = (DEPTH + 1) // 2
N_S5_LAYERS = DEPTH // 2
RET_HEADS = 8
RET_DK = D_MODEL // RET_HEADS
RET_DV = 2 * RET_DK
RET_VW = RET_HEADS * RET_DV
ROPE_BASE = 10000.0
S5_GC = 16
S5_GROUPS = D_MODEL // S5_GC
S5_P = 64
MOE_GROUPS = 4
MOE_PER_GROUP = 8
N_EXP = MOE_GROUPS * MOE_PER_GROUP
TOP_K = 2
D_FF_E = 512
EPS = 1e-6

kernel_name = "retnet_s5_hmoe_stream_step"

F32 = jnp.float32


def rmsnorm(x, g):
    xf = x.astype(F32)
    y = xf * lax.rsqrt(jnp.mean(xf * xf, axis=-1, keepdims=True) + EPS) * g.astype(F32)
    return y.astype(x.dtype)


def n_chunks(L):
    return max(L // CHUNK, 1)


def to_chunks(t, nc):
    return jnp.moveaxis(t.reshape(t.shape[0], nc, t.shape[1] // nc, *t.shape[2:]), 1, 0)


def from_chunks(t):
    t = jnp.moveaxis(t, 0, 1)
    return t.reshape(t.shape[0], t.shape[1] * t.shape[2], *t.shape[3:])


def rope(t, pos):
    half = t.shape[-1] // 2
    freqs = ROPE_BASE ** (-jnp.arange(half, dtype=F32) / half)
    ang = pos[:, None] * freqs[None, :]
    cos = jnp.cos(ang)[None, :, None, :]
    sin = jnp.sin(ang)[None, :, None, :]
    t1, t2 = t[..., :half], t[..., half:]
    return jnp.concatenate([t1 * cos - t2 * sin, t1 * sin + t2 * cos], axis=-1)


def ret_log_gammas():
    return jnp.log(jnp.asarray(1.0 - 2.0 ** (-5.0 - np.arange(RET_HEADS)), dtype=F32))


def retention_block(S, q, k, v, log_g):
    l = q.shape[1]
    idx = jnp.arange(l, dtype=F32)
    dist = jnp.abs(idx[:, None] - idx[None, :])
    dec = jnp.exp(dist[None] * log_g[:, None, None])
    dq = jnp.exp((idx + 1.0)[:, None] * log_g[None, :])
    dk = jnp.exp((l - 1.0 - idx)[:, None] * log_g[None, :])
    s = jnp.einsum('blhd,bmhd->bhlm', q, k) * dec[None]
    o = (jnp.einsum('bhlm,bmhe->blhe', s, v)
         + jnp.einsum('blhd,bhde->blhe', q * dq[None, :, :, None], S))
    S_new = (S * jnp.exp(l * log_g)[None, :, None, None]
             + jnp.einsum('blhd,blhe->bhde', k * dk[None, :, :, None], v))
    return S_new, o


def retention_mixer(xn, S0, pos0, w_in, gn_g, w_out):
    B, L, _ = xn.shape
    proj = xn @ w_in
    q, k, v, g = jnp.split(proj, [D_MODEL, 2 * D_MODEL, 2 * D_MODEL + RET_VW], axis=-1)
    pos = pos0 + jnp.arange(L, dtype=F32)
    q = rope(q.reshape(B, L, RET_HEADS, RET_DK).astype(F32), pos)
    k = rope(k.reshape(B, L, RET_HEADS, RET_DK).astype(F32), pos) * (RET_DK ** -0.5)
    v = v.reshape(B, L, RET_HEADS, RET_DV).astype(F32)
    nc = n_chunks(L)
    log_g = ret_log_gammas()
    S, o = lax.scan(lambda S, c: retention_block(S, c[0], c[1], c[2], log_g),
                    S0.astype(F32), (to_chunks(q, nc), to_chunks(k, nc), to_chunks(v, nc)))
    o = from_chunks(o)
    mu = jnp.mean(o, axis=-1, keepdims=True)
    var = jnp.mean(jnp.square(o - mu), axis=-1, keepdims=True)
    o = ((o - mu) * lax.rsqrt(var + EPS)).reshape(B, L, RET_VW) * gn_g.astype(F32)
    out = (jax.nn.silu(g) * o.astype(xn.dtype)) @ w_out
    return out, S


def _lin_comb(e1, e2):
    a1, b1 = e1
    a2, b2 = e2
    return a1 * a2, a2 * b1 + b2


def s5_block(h, u, lam_bar, b_bar, c_re, c_im):
    bu = jnp.einsum('blgc,gpc->blgp', u.astype(jnp.complex64), b_bar)
    bu = bu.at[:, 0].add(lam_bar[None] * h)
    a = jnp.broadcast_to(lam_bar, bu.shape)
    _, hs = lax.associative_scan(_lin_comb, (a, bu), axis=1)
    y = (jnp.einsum('blgp,gcp->blgc', hs.real, c_re)
         - jnp.einsum('blgp,gcp->blgc', hs.imag, c_im))
    return hs[:, -1], y


def s5_mixer(xn, h0, w_in, a_re, a_im, log_dt, b_re, b_im, c_re, c_im, d_skip, w_out):
    B, L, _ = xn.shape
    u = (xn @ w_in).astype(F32)
    lam = lax.complex(a_re.astype(F32), a_im.astype(F32))
    dt = jnp.exp(log_dt.astype(F32))[:, None]
    lam_bar = jnp.exp(lam * dt)
    b_bar = ((lam_bar - 1.0) / lam)[..., None] * lax.complex(b_re.astype(F32), b_im.astype(F32))
    cr, ci = c_re.astype(F32), c_im.astype(F32)
    nc = n_chunks(L)
    h, y = lax.scan(lambda h, uc: s5_block(h, uc, lam_bar, b_bar, cr, ci),
                    h0, to_chunks(u.reshape(B, L, S5_GROUPS, S5_GC), nc))
    y = from_chunks(y).reshape(B, L, D_MODEL) + d_skip.astype(F32) * u
    z = jax.nn.gelu(y).astype(xn.dtype)
    a, gt = jnp.split(z @ w_out, 2, axis=-1)
    return a * jax.nn.sigmoid(gt), h


def hier_moe(xn, w_grp, b_grp, w_exp, b_exp, w_gu, w_down):
    B, L, D = xn.shape
    t = xn.reshape(B * L, D)
    gp = jax.nn.softmax((t @ w_grp).astype(F32) + b_grp.astype(F32), axis=-1)
    gw, gsel = lax.top_k(gp, 1)
    el = ((t @ w_exp).astype(F32) + b_exp.astype(F32)).reshape(-1, MOE_GROUPS, MOE_PER_GROUP)
    el = jnp.take_along_axis(el, gsel[:, :, None], axis=1)[:, 0]
    ep = jax.nn.softmax(el, axis=-1)
    ew, eidx = lax.top_k(ep, TOP_K)
    ew = ew / jnp.sum(ew, axis=-1, keepdims=True) * gw
    eid = gsel * MOE_PER_GROUP + eidx
    gates = jnp.sum(jax.nn.one_hot(eid, N_EXP, dtype=F32) * ew[..., None], axis=1)

    def body(y, e):
        wgu, wd, ge = e
        hg, hu = jnp.split(t @ wgu, 2, axis=-1)
        return y + ge[:, None].astype(t.dtype) * ((jax.nn.silu(hg) * hu) @ wd), None

    y, _ = lax.scan(body, jnp.zeros_like(t), (w_gu, w_down, gates.T))
    return y.reshape(B, L, D)


def setup_inputs(seed: int = 0) -> dict:
    key = jax.random.key(seed)
    ks = jax.random.split(key, 32)

    def nrm(k, shape, scale):
        return jax.random.normal(k, shape, F32) * scale

    return {
        "x_prompt": nrm(ks[0], (BATCH, SEQ, D_MODEL), 1.0),
        "x_sample": nrm(ks[1], (DEC_BATCH, DEC_SEQ, D_MODEL), 1.0),
        "state_ret": nrm(ks[2], (N_RET_LAYERS, DEC_BATCH, RET_HEADS, RET_DK, RET_DV), 0.05),
        "state_s5_re": nrm(ks[3], (N_S5_LAYERS, DEC_BATCH, S5_GROUPS, S5_P), 0.5),
        "state_s5_im": nrm(ks[4], (N_S5_LAYERS, DEC_BATCH, S5_GROUPS, S5_P), 0.5),
        "norm_mix_g": 1.0 + nrm(ks[5], (DEPTH, D_MODEL), 0.02),
        "norm_ffn_g": 1.0 + nrm(ks[6], (DEPTH, D_MODEL), 0.02),
        "norm_final_g": 1.0 + nrm(ks[7], (D_MODEL,), 0.02),
        "ret_w_in": nrm(ks[8], (N_RET_LAYERS, D_MODEL, 2 * D_MODEL + 2 * RET_VW), D_MODEL ** -0.5),
        "ret_gn_g": 1.0 + nrm(ks[9], (N_RET_LAYERS, RET_VW), 0.02),
        "ret_w_out": nrm(ks[10], (N_RET_LAYERS, RET_VW, D_MODEL), RET_VW ** -0.5),
        "s5_w_in": nrm(ks[11], (N_S5_LAYERS, D_MODEL, D_MODEL), D_MODEL ** -0.5),
        "s5_a_re": -0.5 + nrm(ks[12], (N_S5_LAYERS, S5_GROUPS, S5_P), 0.01),
        "s5_a_im": math.pi * jnp.arange(S5_P, dtype=F32) + nrm(ks[13], (N_S5_LAYERS, S5_GROUPS, S5_P), 0.01),
        "s5_log_dt": jax.random.uniform(ks[14], (N_S5_LAYERS, S5_GROUPS), F32, math.log(1e-3), math.log(1e-1)),
        "s5_b_re": nrm(ks[15], (N_S5_LAYERS, S5_GROUPS, S5_P, S5_GC), (2 * S5_GC) ** -0.5),
        "s5_b_im": nrm(ks[16], (N_S5_LAYERS, S5_GROUPS, S5_P, S5_GC), (2 * S5_GC) ** -0.5),
        "s5_c_re": nrm(ks[17], (N_S5_LAYERS, S5_GROUPS, S5_GC, S5_P), (2 * S5_P) ** -0.5),
        "s5_c_im": nrm(ks[18], (N_S5_LAYERS, S5_GROUPS, S5_GC, S5_P), (2 * S5_P) ** -0.5),
        "s5_d": nrm(ks[19], (N_S5_LAYERS, D_MODEL), 1.0),
        "s5_w_out": nrm(ks[20], (N_S5_LAYERS, D_MODEL, 2 * D_MODEL), D_MODEL ** -0.5),
        "moe_w_grp": nrm(ks[21], (DEPTH, D_MODEL, MOE_GROUPS), D_MODEL ** -0.5),
        "moe_b_grp": nrm(ks[22], (DEPTH, MOE_GROUPS), 0.01),
        "moe_w_exp": nrm(ks[23], (DEPTH, D_MODEL, N_EXP), D_MODEL ** -0.5),
        "moe_b_exp": nrm(ks[24], (DEPTH, N_EXP), 0.01),
        "moe_w_gu": nrm(ks[25], (DEPTH, N_EXP, D_MODEL, 2 * D_FF_E), D_MODEL ** -0.5),
        "moe_w_down": nrm(ks[26], (DEPTH, N_EXP, D_FF_E, D_MODEL), D_FF_E ** -0.5),
    }


def reference(x_prompt, x_sample, state_ret, state_s5_re, state_s5_im,
              norm_mix_g, norm_ffn_g, norm_final_g,
              ret_w_in, ret_gn_g, ret_w_out,
              s5_w_in, s5_a_re, s5_a_im, s5_log_dt, s5_b_re, s5_b_im, s5_c_re, s5_c_im, s5_d, s5_w_out,
              moe_w_grp, moe_b_grp, moe_w_exp, moe_b_exp, moe_w_gu, moe_w_down):
    xp, xs = x_prompt, x_sample
    bp = xp.shape[0]
    ret_p, ret_s, s5_p, s5_s = [], [], [], []
    for i in range(DEPTH):
        hp = rmsnorm(xp, norm_mix_g[i])
        hs = rmsnorm(xs, norm_mix_g[i])
        j = i // N_MIXERS
        if i % N_MIXERS == 0:
            w = (ret_w_in[j], ret_gn_g[j], ret_w_out[j])
            s0 = jnp.zeros((bp, RET_HEADS, RET_DK, RET_DV), F32)
            mp, sp = retention_mixer(hp, s0, 0, *w)
            ms, ss = retention_mixer(hs, state_ret[j], PAST_LEN, *w)
            ret_p.append(sp)
            ret_s.append(ss)
        else:
            w = (s5_w_in[j], s5_a_re[j], s5_a_im[j], s5_log_dt[j], s5_b_re[j], s5_b_im[j],
                 s5_c_re[j], s5_c_im[j], s5_d[j], s5_w_out[j])
            h0p = jnp.zeros((bp, S5_GROUPS, S5_P), jnp.complex64)
            h0s = lax.complex(state_s5_re[j].astype(F32), state_s5_im[j].astype(F32))
            mp, sp = s5_mixer(hp, h0p, *w)
            ms, ss = s5_mixer(hs, h0s, *w)
            s5_p.append(sp)
            s5_s.append(ss)
        xp = xp + mp
        xs = xs + ms
        mo = (moe_w_grp[i], moe_b_grp[i], moe_w_exp[i], moe_b_exp[i], moe_w_gu[i], moe_w_down[i])
        xp = xp + hier_moe(rmsnorm(xp, norm_ffn_g[i]), *mo)
        xs = xs + hier_moe(rmsnorm(xs, norm_ffn_g[i]), *mo)
    y_prompt = rmsnorm(xp, norm_final_g)
    y_sample = rmsnorm(xs, norm_final_g)
    ret_state_prompt = jnp.stack(ret_p)
    ret_state_sample = jnp.stack(ret_s)
    s5_re_prompt = jnp.stack([h.real for h in s5_p])
    s5_im_prompt = jnp.stack([h.imag for h in s5_p])
    s5_re_sample = jnp.stack([h.real for h in s5_s])
    s5_im_sample = jnp.stack([h.imag for h in s5_s])
    return (y_prompt, y_sample, ret_state_prompt, ret_state_sample,
            s5_re_prompt, s5_im_prompt, s5_re_sample, s5_im_sample)
```

```python
import functools
import math

import numpy as np
import jax
import jax.numpy as jnp
from jax import lax
from jax.experimental import pallas as pl
from jax.experimental.pallas import tpu as pltpu

F32 = jnp.float32
BF16 = jnp.bfloat16

D_MODEL = 2048
PAST_LEN = 2048
CHUNK = 64
RET_HEADS = 8
RET_DK = D_MODEL // RET_HEADS
RET_DV = 2 * RET_DK
RET_VW = RET_HEADS * RET_DV
ROPE_BASE = 10000.0
S5_GC = 16
S5_GROUPS = D_MODEL // S5_GC
S5_P = 64
MOE_GROUPS = 4
MOE_PER_GROUP = 8
N_EXP = MOE_GROUPS * MOE_PER_GROUP
D_FF_E = 512
EPS = 1e-6

LANES = 128
SUBLANES = 8
ROW_TILE = 512
RET_ROWS = 256
MOE_TILE = 256
S5_ROWS = 128
S5_FB = S5_GROUPS // 16
S5_PITCH = S5_ROWS + 8
MIB = 1024 * 1024


def _params(sem, vmem_mib):
    return pltpu.CompilerParams(dimension_semantics=sem, vmem_limit_bytes=vmem_mib * MIB)


def _rms(x, g):
    ms = jnp.mean(x * x, axis=-1, keepdims=True)
    return x * lax.rsqrt(ms + EPS) * g


def _sigmoid(x):
    return 1.0 / (1.0 + jnp.exp(-x))


def _part_specs(parts, rows, cols, rowcol):
    specs = []
    start = 0
    for p in parts:
        nt = p.shape[0] // rows

        def imap(*g, start=start, nt=nt):
            i, jc = rowcol(*g)
            return (jnp.clip(i - start, 0, nt - 1), jc)

        specs.append(pl.BlockSpec((rows, cols), imap))
        start += nt
    return specs


def _part_tiles(parts, rows):
    return tuple(p.shape[0] // rows for p in parts)


def _for_each_part(i, tiles, fn):
    if len(tiles) == 1:
        fn(0)
        return
    start = 0
    for p, nt in enumerate(tiles):
        pl.when(jnp.logical_and(i >= start, i < start + nt))(functools.partial(fn, p))
        start += nt


def _pick(refs, p):
    return refs[min(p, len(refs) - 1)]


def _norm_kernel(*refs, tiles):
    n = len(tiles)
    x_refs, g_ref, o_ref = refs[:n], refs[n], refs[n + 1]

    def run(p):
        o_ref[...] = _rms(x_refs[p][...], g_ref[...]).astype(o_ref.dtype)

    _for_each_part(pl.program_id(0), tiles, run)


def _norm_call(x_parts, g, out_dtype):
    d = x_parts[0].shape[1]
    t = sum(p.shape[0] for p in x_parts)
    return pl.pallas_call(
        functools.partial(_norm_kernel, tiles=_part_tiles(x_parts, ROW_TILE)),
        grid=(t // ROW_TILE,),
        in_specs=_part_specs(x_parts, ROW_TILE, d, lambda i: (i, 0)) + [pl.BlockSpec((1, d), lambda i: (0, 0))],
        out_specs=pl.BlockSpec((ROW_TILE, d), lambda i: (i, 0)),
        out_shape=jax.ShapeDtypeStruct((t, d), out_dtype),
        compiler_params=_params(("arbitrary",), 32),
        name="rmsnorm",
    )(*x_parts, g.reshape(1, d))


def _mm_kernel(*refs, tiles, n_x, n_w, n_extra, epi):
    x_refs = refs[:n_x]
    w_refs = refs[n_x:n_x + n_w]
    pos = n_x + n_w
    extra = []
    for n in n_extra:
        extra.append(refs[pos:pos + n])
        pos += n
    o_ref = refs[pos]
    wb = refs[pos + 1:]

    @pl.when(pl.program_id(1) == 0)
    def _():
        for w_ref, b in zip(w_refs, wb):
            b[...] = w_ref[...].astype(BF16)

    def run(p):
        x = _pick(x_refs, p)[...]
        accs = [jnp.dot(x, b[...], preferred_element_type=F32) for b in wb]
        epi(accs, [_pick(e, p) for e in extra], o_ref)

    _for_each_part(pl.program_id(1), tiles, run)


def _mm_call(x_parts, w, w_col_offsets, tn, n_out, extras, out_dtype, epi, vmem_mib, name):
    k = x_parts[0].shape[1]
    t = sum(p.shape[0] for p in x_parts)
    n_w = len(w_col_offsets)
    split = [parts for parts in [x_parts] + [e[0] for e in extras] if len(parts) > 1]
    tiles = _part_tiles(split[0], ROW_TILE) if split else (t // ROW_TILE,)
    assert all(_part_tiles(parts, ROW_TILE) == tiles for parts in split)
    in_specs = _part_specs(x_parts, ROW_TILE, k, lambda j, i: (i, 0))
    for off in w_col_offsets:
        in_specs.append(pl.BlockSpec((k, tn), lambda j, i, off=off: (0, j + off)))
    extra_arrays = []
    for parts, cols, colfn in extras:
        in_specs += _part_specs(parts, ROW_TILE, cols, lambda j, i, colfn=colfn: (i, colfn(j)))
        extra_arrays += list(parts)
    return pl.pallas_call(
        functools.partial(_mm_kernel, tiles=tiles, n_x=len(x_parts), n_w=n_w,
                          n_extra=tuple(len(e[0]) for e in extras), epi=epi),
        grid=(n_out // tn, t // ROW_TILE),
        in_specs=in_specs,
        out_specs=pl.BlockSpec((ROW_TILE, tn), lambda j, i: (i, j)),
        out_shape=jax.ShapeDtypeStruct((t, n_out), out_dtype),
        scratch_shapes=[pltpu.VMEM((k, tn), BF16) for _ in range(n_w)],
        compiler_params=_params(("arbitrary", "arbitrary"), vmem_mib),
        name=name,
    )(*x_parts, *([w] * n_w), *extra_arrays)


def _epi_plain(accs, extra, o_ref):
    o_ref[...] = accs[0].astype(o_ref.dtype)


def _epi_residual(accs, extra, o_ref):
    o_ref[...] = (extra[0][...] + accs[0]).astype(o_ref.dtype)


def _epi_glu_residual(accs, extra, o_ref):
    a, gt = accs
    o_ref[...] = (extra[0][...] + a * _sigmoid(gt)).astype(o_ref.dtype)


def _epi_rope(accs, extra, o_ref, *, n_qk_blocks):
    acc = accs[0]
    cos_ref, sin_ref = extra
    j = pl.program_id(0)
    half = RET_DK // 2

    @pl.when(j < n_qk_blocks)
    def _():
        scale = jnp.where(j >= n_qk_blocks // 2, RET_DK ** -0.5, 1.0).astype(F32)
        c = cos_ref[...]
        s = sin_ref[...]
        for hh in range(acc.shape[1] // RET_DK):
            lo = hh * RET_DK
            t1 = acc[:, lo:lo + half]
            t2 = acc[:, lo + half:lo + RET_DK]
            o_ref[:, lo:lo + half] = ((t1 * c - t2 * s) * scale).astype(o_ref.dtype)
            o_ref[:, lo + half:lo + RET_DK] = ((t1 * s + t2 * c) * scale).astype(o_ref.dtype)

    @pl.when(j >= n_qk_blocks)
    def _():
        o_ref[...] = acc.astype(o_ref.dtype)


def _ret_tables(rows, chunk):
    lg = jnp.log(jnp.asarray(1.0 - 2.0 ** (-5.0 - np.arange(RET_HEADS)), dtype=F32))[:, None, None]
    n = jnp.arange(rows, dtype=F32)
    ci = np.arange(rows) // chunk
    same_or_earlier = jnp.asarray(ci[None, :] <= ci[:, None])
    dmat = jnp.where(same_or_earlier[None], jnp.exp(jnp.abs(n[:, None] - n[None, :])[None] * lg), 0.0)
    dq = jnp.exp((n + 1.0)[None, :, None] * lg) * jnp.ones((1, 1, LANES), F32)
    dk = jnp.exp((rows - 1.0 - n)[None, :, None] * lg) * jnp.ones((1, 1, LANES), F32)
    gr = jnp.exp(rows * lg) * jnp.ones((1, 1, RET_DV), F32)
    return dmat, dq, dk, gr


def _ret_block(q, k, v, g, s_prev, dmat, dq, dk, gr, gn):
    s = lax.dot_general(q, k, (((1,), (1,)), ((), ())), preferred_element_type=F32) * dmat
    inter = jnp.dot(q, s_prev.astype(BF16), preferred_element_type=F32)
    o = jnp.dot(s.astype(BF16), v, preferred_element_type=F32) + jnp.concatenate([dq] * (RET_DV // LANES), axis=1) * inter
    kd = (k.astype(F32) * jnp.concatenate([dk] * (RET_DK // LANES), axis=1)).astype(BF16)
    s_new = s_prev * gr + lax.dot_general(kd, v, (((0,), (0,)), ((), ())), preferred_element_type=F32)
    mu = jnp.mean(o, axis=-1, keepdims=True)
    oc = o - mu
    var = jnp.mean(oc * oc, axis=-1, keepdims=True)
    on = oc * lax.rsqrt(var + EPS) * gn
    gf = g.astype(F32)
    return (gf * _sigmoid(gf) * on).astype(BF16), s_new


def _ret_prompt_kernel(q_ref, k_ref, v_ref, g_ref, dm_ref, dq_ref, dk_ref, gr_ref, gn_ref,
                       og_ref, st_ref, s_scr):
    @pl.when(pl.program_id(1) == 0)
    def _():
        s_scr[...] = jnp.zeros_like(s_scr)

    og, s_new = _ret_block(q_ref[...], k_ref[...], v_ref[...], g_ref[...], s_scr[...],
                           dm_ref[0], dq_ref[0], dk_ref[0], gr_ref[0], gn_ref[...])
    og_ref[...] = og
    s_scr[...] = s_new
    st_ref[0] = s_new


def _ret_prompt_call(proj, gn, t_prompt):
    dm, dq, dk, gr = _ret_tables(RET_ROWS, CHUNK)
    h = RET_HEADS
    kv_blk = 2 * D_MODEL // RET_DV
    g_blk = (2 * D_MODEL + RET_VW) // RET_DV
    return pl.pallas_call(
        _ret_prompt_kernel,
        grid=(h, t_prompt // RET_ROWS),
        in_specs=[
            pl.BlockSpec((RET_ROWS, RET_DK), lambda hh, c: (c, hh)),
            pl.BlockSpec((RET_ROWS, RET_DK), lambda hh, c: (c, h + hh)),
            pl.BlockSpec((RET_ROWS, RET_DV), lambda hh, c: (c, kv_blk + hh)),
            pl.BlockSpec((RET_ROWS, RET_DV), lambda hh, c: (c, g_blk + hh)),
            pl.BlockSpec((1, RET_ROWS, RET_ROWS), lambda hh, c: (hh, 0, 0)),
            pl.BlockSpec((1, RET_ROWS, LANES), lambda hh, c: (hh, 0, 0)),
            pl.BlockSpec((1, RET_ROWS, LANES), lambda hh, c: (hh, 0, 0)),
            pl.BlockSpec((1, 1, RET_DV), lambda hh, c: (hh, 0, 0)),
            pl.BlockSpec((1, RET_DV), lambda hh, c: (0, hh)),
        ],
        out_specs=[
            pl.BlockSpec((RET_ROWS, RET_DV), lambda hh, c: (c, hh)),
            pl.BlockSpec((1, RET_DK, RET_DV), lambda hh, c: (hh, 0, 0)),
        ],
        out_shape=[
            jax.ShapeDtypeStruct((t_prompt, RET_VW), BF16),
            jax.ShapeDtypeStruct((h, RET_DK, RET_DV), F32),
        ],
        scratch_shapes=[pltpu.VMEM((RET_DK, RET_DV), F32)],
        compiler_params=_params(("arbitrary", "arbitrary"), 32),
        name="retention_prompt",
    )(proj, proj, proj, proj, dm, dq, dk, gr, gn.reshape(1, RET_VW))


def _ret_sample_kernel(q_ref, k_ref, v_ref, g_ref, s0_ref, dm_ref, dq_ref, dk_ref, gr_ref, gn_ref,
                       og_ref, st_ref):
    for hh in range(RET_HEADS):
        ks = slice(hh * RET_DK, (hh + 1) * RET_DK)
        vs = slice(hh * RET_DV, (hh + 1) * RET_DV)
        og, s_new = _ret_block(q_ref[:, ks], k_ref[:, ks], v_ref[:, vs], g_ref[:, vs], s0_ref[0, hh],
                               dm_ref[hh], dq_ref[hh], dk_ref[hh], gr_ref[hh], gn_ref[:, vs])
        og_ref[:, vs] = og
        st_ref[0, hh] = s_new


def _ret_sample_call(proj, gn, state, t_prompt, dec_seq):
    dm, dq, dk, gr = _ret_tables(dec_seq, dec_seq)
    b = state.shape[0]
    r0 = t_prompt // dec_seq
    const3 = lambda bb: (0, 0, 0)
    return pl.pallas_call(
        _ret_sample_kernel,
        grid=(b,),
        in_specs=[
            pl.BlockSpec((dec_seq, D_MODEL), lambda bb: (r0 + bb, 0)),
            pl.BlockSpec((dec_seq, D_MODEL), lambda bb: (r0 + bb, 1)),
            pl.BlockSpec((dec_seq, RET_VW), lambda bb: (r0 + bb, 1)),
            pl.BlockSpec((dec_seq, RET_VW), lambda bb: (r0 + bb, 2)),
            pl.BlockSpec((1, RET_HEADS, RET_DK, RET_DV), lambda bb: (bb, 0, 0, 0)),
            pl.BlockSpec(dm.shape, const3),
            pl.BlockSpec(dq.shape, const3),
            pl.BlockSpec(dk.shape, const3),
            pl.BlockSpec(gr.shape, const3),
            pl.BlockSpec((1, RET_VW), lambda bb: (0, 0)),
        ],
        out_specs=[
            pl.BlockSpec((dec_seq, RET_VW), lambda bb: (bb, 0)),
            pl.BlockSpec((1, RET_HEADS, RET_DK, RET_DV), lambda bb: (bb, 0, 0, 0)),
        ],
        out_shape=[
            jax.ShapeDtypeStruct((b * dec_seq, RET_VW), BF16),
            jax.ShapeDtypeStruct(state.shape, F32),
        ],
        compiler_params=_params(("arbitrary",), 40),
        name="retention_sample",
    )(proj, proj, proj, proj, state, dm, dq, dk, gr, gn.reshape(1, RET_VW))


def _router_kernel(x_ref, g_ref, wr_ref, br_ref, xn_ref, id_ref, w_ref):
    xn = _rms(x_ref[...], g_ref[...])
    xn_ref[...] = xn
    xh = xn.astype(BF16)
    xl = (xn - xh.astype(F32)).astype(BF16)
    w = wr_ref[...]
    wh = w.astype(BF16)
    wl = (w - wh.astype(F32)).astype(BF16)
    lg = (jnp.dot(xh, wh, preferred_element_type=F32) + jnp.dot(xh, wl, preferred_element_type=F32)
          + jnp.dot(xl, wh, preferred_element_type=F32)) + br_ref[...]
    lane = lax.broadcasted_iota(jnp.int32, lg.shape, 1).astype(F32)
    neg = jnp.float32(-jnp.inf)
    big = jnp.float32(LANES)
    gl = jnp.where(lane < MOE_GROUPS, lg, neg)
    gmax = jnp.max(gl, axis=-1, keepdims=True)
    gsel = jnp.min(jnp.where(gl == gmax, lane, big), axis=-1, keepdims=True)
    gw = 1.0 / jnp.sum(jnp.exp(gl - gmax), axis=-1, keepdims=True)
    lo = MOE_GROUPS + gsel * MOE_PER_GROUP
    el = jnp.where((lane >= lo) & (lane < lo + MOE_PER_GROUP), lg, neg)
    m1 = jnp.max(el, axis=-1, keepdims=True)
    i1 = jnp.min(jnp.where(el == m1, lane, big), axis=-1, keepdims=True)
    el2 = jnp.where(lane == i1, neg, el)
    m2 = jnp.max(el2, axis=-1, keepdims=True)
    i2 = jnp.min(jnp.where(el2 == m2, lane, big), axis=-1, keepdims=True)
    z = jnp.sum(jnp.exp(el - m1), axis=-1, keepdims=True)
    p1 = 1.0 / z
    p2 = jnp.exp(m2 - m1) / z
    tot = p1 + p2
    w1 = p1 / tot * gw
    w2 = p2 / tot * gw
    ids = jnp.where(lane == 0, i1 - MOE_GROUPS, jnp.where(lane == 1, i2 - MOE_GROUPS, 0.0))
    id_ref[...] = ids.astype(jnp.int32)
    w_ref[...] = jnp.where(lane == 0, w1, jnp.where(lane == 1, w2, 0.0))


def _router_call(x, g, w_grp, b_grp, w_exp, b_exp):
    t, d = x.shape
    pad = LANES - MOE_GROUPS - N_EXP
    wr = jnp.concatenate([w_grp, w_exp, jnp.zeros((d, pad), F32)], axis=1)
    br = jnp.concatenate([b_grp, b_exp, jnp.zeros((pad,), F32)]).reshape(1, LANES)
    row = lambda i: (i, 0)
    fixed = lambda i: (0, 0)
    return pl.pallas_call(
        _router_kernel,
        grid=(t // ROW_TILE,),
        in_specs=[pl.BlockSpec((ROW_TILE, d), row), pl.BlockSpec((1, d), fixed),
                  pl.BlockSpec((d, LANES), fixed), pl.BlockSpec((1, LANES), fixed)],
        out_specs=[pl.BlockSpec((ROW_TILE, d), row), pl.BlockSpec((ROW_TILE, LANES), row),
                   pl.BlockSpec((ROW_TILE, LANES), row)],
        out_shape=[jax.ShapeDtypeStruct((t, d), F32), jax.ShapeDtypeStruct((t, LANES), jnp.int32),
                   jax.ShapeDtypeStruct((t, LANES), F32)],
        compiler_params=_params(("arbitrary",), 40),
        name="moe_router",
    )(x, g.reshape(1, d), wr, br)


def _moe_plan(eid, ew):
    t = eid.shape[0]
    a = 2 * t
    n_tiles = a // MOE_TILE + N_EXP
    e_flat = eid.reshape(a)
    onehot = (e_flat[:, None] == jnp.arange(N_EXP, dtype=jnp.int32)[None, :]).astype(jnp.int32)
    csum = jnp.cumsum(onehot, axis=0)
    counts = csum[-1]
    rank = jnp.take_along_axis(csum, e_flat[:, None], axis=1)[:, 0] - 1
    tiles_e = (counts + MOE_TILE - 1) // MOE_TILE
    tiles_end = jnp.cumsum(tiles_e)
    row_start = (tiles_end - tiles_e) * MOE_TILE
    pos = row_start[e_flat] + rank
    n_valid = tiles_end[-1]
    tile_id = jnp.minimum(jnp.arange(n_tiles, dtype=jnp.int32), n_valid - 1)
    tile_expert = jnp.sum((tile_id[:, None] >= tiles_end[None, :]).astype(jnp.int32), axis=1)
    tile_expert = jnp.minimum(tile_expert, N_EXP - 1)
    rows = n_tiles * MOE_TILE
    tok_of_row = jnp.zeros((rows,), jnp.int32).at[pos].set(jnp.arange(a, dtype=jnp.int32) // 2)
    w_of_row = jnp.zeros((rows,), F32).at[pos].set(ew.reshape(a))
    return tile_expert, n_valid.reshape(1).astype(jnp.int32), tok_of_row, w_of_row.reshape(rows, 1), pos.astype(jnp.int32)


def _moe_ffn_kernel(te_ref, nv_ref, tok_ref, x_hbm, wgu_ref, wd_ref, wrow_ref, o_ref,
                    xbuf, sems, wgu_b, wd_b):
    i = pl.program_id(0)
    nv = nv_ref[0]
    slot = lax.rem(i, 2)

    def gather(tile, dst_slot):
        def body(r, carry):
            tok = tok_ref[tile * MOE_TILE + r]
            pltpu.make_async_copy(x_hbm.at[pl.ds(tok, 1)], xbuf.at[dst_slot, pl.ds(r, 1)],
                                  sems.at[dst_slot]).start()
            return carry
        lax.fori_loop(0, MOE_TILE, body, 0)

    @pl.when(i == 0)
    def _():
        gather(0, 0)

    @pl.when(i + 1 < nv)
    def _():
        gather(i + 1, 1 - slot)

    new_expert = jnp.logical_or(i == 0, te_ref[i] != te_ref[jnp.maximum(i - 1, 0)])

    @pl.when(jnp.logical_and(new_expert, i < nv))
    def _():
        wgu_b[...] = wgu_ref[0].astype(BF16)
        wd_b[...] = wd_ref[0].astype(BF16)

    @pl.when(i < nv)
    def _():
        pltpu.make_async_copy(x_hbm.at[pl.ds(0, MOE_TILE)], xbuf.at[slot], sems.at[slot]).wait()
        x = xbuf[slot].astype(BF16)
        h = jnp.dot(x, wgu_b[...], preferred_element_type=F32)
        hg = h[:, :D_FF_E]
        hu = h[:, D_FF_E:]
        act = (hg * _sigmoid(hg) * hu).astype(BF16)
        y = jnp.dot(act, wd_b[...], preferred_element_type=F32)
        o_ref[...] = y * wrow_ref[...]

    @pl.when(i >= nv)
    def _():
        o_ref[...] = jnp.zeros_like(o_ref)


def _moe_ffn_call(xn, w_gu, w_down, tile_expert, n_valid, tok_of_row, w_of_row):
    t, d = xn.shape
    n_tiles = tile_expert.shape[0]
    rows = n_tiles * MOE_TILE
    wrow_row = lambda i, te, nv, tok: (jnp.minimum(i, nv[0]), 0)
    grid_spec = pltpu.PrefetchScalarGridSpec(
        num_scalar_prefetch=3,
        grid=(n_tiles,),
        in_specs=[
            pl.BlockSpec(memory_space=pl.ANY),
            pl.BlockSpec((1, d, 2 * D_FF_E), lambda i, te, nv, tok: (te[i], 0, 0)),
            pl.BlockSpec((1, D_FF_E, d), lambda i, te, nv, tok: (te[i], 0, 0)),
            pl.BlockSpec((MOE_TILE, 1), wrow_row),
        ],
        out_specs=pl.BlockSpec((MOE_TILE, d), lambda i, te, nv, tok: (i, 0)),
        scratch_shapes=[
            pltpu.VMEM((2, MOE_TILE, d), F32),
            pltpu.SemaphoreType.DMA((2,)),
            pltpu.VMEM((d, 2 * D_FF_E), BF16),
            pltpu.VMEM((D_FF_E, d), BF16),
        ],
    )
    return pl.pallas_call(
        _moe_ffn_kernel,
        grid_spec=grid_spec,
        out_shape=jax.ShapeDtypeStruct((rows, d), F32),
        compiler_params=_params(("arbitrary",), 48),
        name="moe_experts",
    )(tile_expert, n_valid, tok_of_row, xn, w_gu, w_down, w_of_row)


def _combine_kernel(pos_ref, x_ref, y_hbm, g_ref, *rest, emit_x, norm_tiles):
    outs = rest[:int(emit_x) + len(norm_tiles)]
    buf, sems = rest[int(emit_x) + len(norm_tiles):]
    i = pl.program_id(0)
    n = pl.num_programs(0)
    slot = lax.rem(i, 2)
    tm = x_ref.shape[0]

    def gather(tile, dst_slot):
        def body(r, carry):
            for kk in range(2):
                p = pos_ref[(tile * tm + r) * 2 + kk]
                pltpu.make_async_copy(y_hbm.at[pl.ds(p, 1)], buf.at[dst_slot, kk, pl.ds(r, 1)],
                                      sems.at[dst_slot]).start()
            return carry
        lax.fori_loop(0, tm, body, 0)

    @pl.when(i == 0)
    def _():
        gather(0, 0)

    @pl.when(i + 1 < n)
    def _():
        gather(i + 1, 1 - slot)

    for kk in range(2):
        pltpu.make_async_copy(y_hbm.at[pl.ds(0, tm)], buf.at[slot, kk], sems.at[slot]).wait()
    xo = x_ref[...] + buf[slot, 0] + buf[slot, 1]
    norm_refs = outs[int(emit_x):]
    if emit_x:
        outs[0][...] = xo

    def write_norm(p):
        norm_refs[p][...] = _rms(xo, g_ref[...]).astype(norm_refs[p].dtype)

    _for_each_part(i, norm_tiles, write_norm)


def _combine_call(x, y_rows, pos, g, emit_x, norm_dtype, norm_rows):
    t, d = x.shape
    tm = MOE_TILE
    row = lambda i, pos_ref: (i, 0)
    out_specs, out_shape = [], []
    if emit_x:
        out_specs.append(pl.BlockSpec((tm, d), row))
        out_shape.append(jax.ShapeDtypeStruct((t, d), F32))
    norm_parts = [jax.ShapeDtypeStruct((r, d), norm_dtype) for r in norm_rows]
    out_specs += _part_specs(norm_parts, tm, d, row)
    out_shape += norm_parts
    grid_spec = pltpu.PrefetchScalarGridSpec(
        num_scalar_prefetch=1,
        grid=(t // tm,),
        in_specs=[pl.BlockSpec((tm, d), row), pl.BlockSpec(memory_space=pl.ANY),
                  pl.BlockSpec((1, d), lambda i, pos_ref: (0, 0))],
        out_specs=out_specs,
        scratch_shapes=[pltpu.VMEM((2, 2, tm, d), F32), pltpu.SemaphoreType.DMA((2,))],
    )
    return pl.pallas_call(
        functools.partial(_combine_kernel, emit_x=emit_x, norm_tiles=_part_tiles(norm_parts, tm)),
        grid_spec=grid_spec,
        out_shape=out_shape,
        compiler_params=_params(("arbitrary",), 40),
        name="moe_combine",
    )(pos, x, y_rows, g.reshape(1, d))


def _moe_layer(x, g_ffn, w_grp, b_grp, w_exp, b_exp, w_gu, w_down, g_next, emit_x, norm_dtype, norm_rows):
    xn, ids, ws = _router_call(x, g_ffn, w_grp, b_grp, w_exp, b_exp)
    tile_expert, n_valid, tok_of_row, w_of_row, pos = _moe_plan(ids[:, :2], ws[:, :2])
    y_rows = _moe_ffn_call(xn, w_gu, w_down, tile_expert, n_valid, tok_of_row, w_of_row)
    return _combine_call(x, y_rows, pos, g_next, emit_x, norm_dtype, norm_rows)


def _s5_prep_kernel(are_ref, aim_ref, ldt_ref, bre_ref, bim_ref, lre_ref, lim_ref, bbr_ref, bbi_ref):
    are = are_ref[...]
    aim = aim_ref[...]
    dt = jnp.exp(ldt_ref[...])
    mag = jnp.exp(are * dt)
    lr = mag * jnp.cos(aim * dt)
    li = mag * jnp.sin(aim * dt)
    lre_ref[...] = lr
    lim_ref[...] = li
    den = are * are + aim * aim
    nr = lr - 1.0
    cr = (nr * are + li * aim) / den
    ci = (li * are - nr * aim) / den
    bre = bre_ref[...]
    bim = bim_ref[...]
    bbr_ref[...] = cr * bre - ci * bim
    bbi_ref[...] = cr * bim + ci * bre


def _s5_prep_call(a_re, a_im, log_dt, b_re, b_im):
    g, p, c = b_re.shape
    shp = (g, p * c)
    expand = lambda a: jnp.broadcast_to(a[:, :, None], (g, p, c)).reshape(shp)
    ldt = jnp.broadcast_to(log_dt[:, None], shp)
    outs = pl.pallas_call(
        _s5_prep_kernel,
        out_shape=[jax.ShapeDtypeStruct(shp, F32)] * 4,
        name="s5_discretize",
    )(expand(a_re), expand(a_im), ldt, b_re.reshape(shp), b_im.reshape(shp))
    lre, lim, bbr, bbi = outs
    return lre[:, ::c], lim[:, ::c], bbr, bbi


def _s5_mats(bbr, bbi, c_re, c_im):
    eye = jnp.eye(16, dtype=F32)

    def bd_part(bb):
        xx = bb.reshape(S5_FB, 16, S5_P, S5_GC).transpose(0, 1, 3, 2)
        yy = xx[:, :, :, None, :] * eye[None, :, None, :, None]
        return yy.reshape(S5_FB, 16 * S5_GC, 16 * S5_P)

    def cd_part(cc):
        xx = cc.reshape(S5_FB, 16, S5_GC, S5_P).transpose(0, 1, 3, 2)
        yy = xx[:, :, :, None, :] * eye[None, :, None, :, None]
        return yy.reshape(S5_FB, 16 * S5_P, 16 * S5_GC)

    bd = jnp.concatenate([bd_part(bbr), bd_part(bbi)], axis=-1).astype(BF16)
    cd = jnp.concatenate([cd_part(c_re), -cd_part(c_im)], axis=1).astype(BF16)
    return bd, cd


def _s5_kernel(u_ref, bd_ref, cd_ref, lre_ref, lim_ref, d_ref, h0r_ref, h0i_ref,
               z_ref, hr_ref, hi_ref, scr, hst, *, seg, carry):
    rows = u_ref.shape[0]
    n_seq = rows // seg
    ch = 16 * S5_GC
    half = SUBLANES * S5_PITCH
    blk = 2 * half

    for fb in range(S5_FB):
        bu = jnp.dot(u_ref[:, fb * ch:(fb + 1) * ch].astype(BF16), bd_ref[fb], preferred_element_type=F32)
        for c in range(2 * SUBLANES):
            r0 = fb * blk + c * S5_PITCH
            scr[r0:r0 + rows, :] = bu[:, c * LANES:(c + 1) * LANES]

    lre = [lre_ref[fb * SUBLANES:(fb + 1) * SUBLANES, :] for fb in range(S5_FB)]
    lim = [lim_ref[fb * SUBLANES:(fb + 1) * SUBLANES, :] for fb in range(S5_FB)]

    def scan(row0, state):
        def step(l, st):
            new = []
            for fb in range(S5_FB):
                hr, hi = st[2 * fb], st[2 * fb + 1]
                ire = pl.ds(fb * blk + row0 + l, SUBLANES, stride=S5_PITCH)
                iim = pl.ds(fb * blk + half + row0 + l, SUBLANES, stride=S5_PITCH)
                nr = lre[fb] * hr - lim[fb] * hi + scr[ire, :]
                ni = lre[fb] * hi + lim[fb] * hr + scr[iim, :]
                scr[ire, :] = nr
                scr[iim, :] = ni
                new += [nr, ni]
            return tuple(new)
        return lax.fori_loop(0, seg, step, state)

    def load_state(rref, iref, lead):
        st = []
        for fb in range(S5_FB):
            st += [rref[lead + (slice(fb * SUBLANES, (fb + 1) * SUBLANES), slice(None))],
                   iref[lead + (slice(fb * SUBLANES, (fb + 1) * SUBLANES), slice(None))]]
        return tuple(st)

    def store_state(st, rref, iref, lead):
        for fb in range(S5_FB):
            rref[lead + (slice(fb * SUBLANES, (fb + 1) * SUBLANES), slice(None))] = st[2 * fb]
            iref[lead + (slice(fb * SUBLANES, (fb + 1) * SUBLANES), slice(None))] = st[2 * fb + 1]

    if carry:
        @pl.when(pl.program_id(0) == 0)
        def _():
            hst[0] = h0r_ref[0]
            hst[1] = h0i_ref[0]
        hre_scr = hst.at[0]
        him_scr = hst.at[1]
        st = scan(0, load_state(hre_scr, him_scr, ()))
        store_state(st, hre_scr, him_scr, ())
        store_state(st, hr_ref, hi_ref, (0,))
    else:
        for s in range(n_seq):
            st = scan(s * seg, load_state(h0r_ref, h0i_ref, (s,)))
            store_state(st, hr_ref, hi_ref, (s,))

    for fb in range(S5_FB):
        hf = jnp.concatenate(
            [scr[fb * blk + c * S5_PITCH: fb * blk + c * S5_PITCH + rows, :] for c in range(2 * SUBLANES)],
            axis=1).astype(BF16)
        y = jnp.dot(hf, cd_ref[fb], preferred_element_type=F32)
        cs = slice(fb * ch, (fb + 1) * ch)
        yy = y + d_ref[:, cs] * u_ref[:, cs]
        z_ref[:, cs] = jax.nn.gelu(yy).astype(z_ref.dtype)


def _s5_call(u, bd, cd, lre, lim, d_skip, h0r, h0i, row_block0, n_blocks, seg, carry):
    t, d = u.shape
    n_seq = S5_ROWS // seg
    ns_total = h0r.shape[0]
    st_idx = (lambda i: (0, 0, 0)) if carry else (lambda i: (i, 0, 0))
    fixed2 = lambda i: (0, 0)
    fixed3 = lambda i: (0, 0, 0)
    st_rows = S5_FB * SUBLANES
    return pl.pallas_call(
        functools.partial(_s5_kernel, seg=seg, carry=carry),
        grid=(n_blocks,),
        in_specs=[
            pl.BlockSpec((S5_ROWS, d), lambda i: (row_block0 + i, 0)),
            pl.BlockSpec(bd.shape, fixed3),
            pl.BlockSpec(cd.shape, fixed3),
            pl.BlockSpec(lre.shape, fixed2),
            pl.BlockSpec(lim.shape, fixed2),
            pl.BlockSpec((1, d), fixed2),
            pl.BlockSpec((n_seq, st_rows, LANES), st_idx),
            pl.BlockSpec((n_seq, st_rows, LANES), st_idx),
        ],
        out_specs=[
            pl.BlockSpec((S5_ROWS, d), lambda i: (i, 0)),
            pl.BlockSpec((n_seq, st_rows, LANES), st_idx),
            pl.BlockSpec((n_seq, st_rows, LANES), st_idx),
        ],
        out_shape=[
            jax.ShapeDtypeStruct((n_blocks * S5_ROWS, d), BF16),
            jax.ShapeDtypeStruct((ns_total, st_rows, LANES), F32),
            jax.ShapeDtypeStruct((ns_total, st_rows, LANES), F32),
        ],
        scratch_shapes=[
            pltpu.VMEM((2 * SUBLANES * S5_FB * S5_PITCH, LANES), F32),
            pltpu.VMEM((2, st_rows, LANES), F32),
        ],
        compiler_params=_params(("arbitrary",), 56),
        name="s5_scan",
    )(u, bd, cd, lre, lim, d_skip.reshape(1, d), h0r, h0i)


def kernel(x_prompt, x_sample, state_ret, state_s5_re, state_s5_im, norm_mix_g, norm_ffn_g, norm_final_g,
           ret_w_in, ret_gn_g, ret_w_out, s5_w_in, s5_a_re, s5_a_im, s5_log_dt, s5_b_re, s5_b_im,
           s5_c_re, s5_c_im, s5_d, s5_w_out, moe_w_grp, moe_b_grp, moe_w_exp, moe_b_exp, moe_w_gu, moe_w_down):
    bp, seq, d = x_prompt.shape
    db, dseq, _ = x_sample.shape
    assert d == D_MODEL and bp == 1
    assert seq % max(ROW_TILE, RET_ROWS) == 0 and seq % CHUNK == 0
    assert dseq < CHUNK and dseq % 16 == 0 and S5_ROWS % dseq == 0
    tp = bp * seq
    ts = db * dseq
    t = tp + ts
    assert ts % ROW_TILE == 0 and ts % S5_ROWS == 0

    x_parts = [x_prompt.reshape(tp, d), x_sample.reshape(ts, d)]

    half = RET_DK // 2
    freqs = ROPE_BASE ** (-jnp.arange(half, dtype=F32) / half)
    pos = jnp.concatenate([jnp.arange(seq, dtype=F32), jnp.tile(PAST_LEN + jnp.arange(dseq, dtype=F32), db)])
    ang = pos[:, None] * freqs[None, :]
    cos_t = jnp.cos(ang)
    sin_t = jnp.sin(ang)

    xn = _norm_call(x_parts, norm_mix_g[0], BF16)
    n_proj = 2 * D_MODEL + 2 * RET_VW
    tn = 1024
    first_col = lambda j: 0
    proj = _mm_call([xn], ret_w_in[0], [0], tn, n_proj, [([cos_t], half, first_col), ([sin_t], half, first_col)],
                    BF16, functools.partial(_epi_rope, n_qk_blocks=2 * D_MODEL // tn), 40, "ret_in_proj")
    og_p, ret_p = _ret_prompt_call(proj, ret_gn_g[0], tp)
    og_s, ret_s = _ret_sample_call(proj, ret_gn_g[0], state_ret[0], tp, dseq)
    tn = 512
    same_col = lambda j: j
    x1 = _mm_call([og_p, og_s], ret_w_out[0], [0], tn, d, [(x_parts, tn, same_col)], F32, _epi_residual,
                  40, "ret_out_proj")

    x2, xn = _moe_layer(x1, norm_ffn_g[0], moe_w_grp[0], moe_b_grp[0], moe_w_exp[0], moe_b_exp[0],
                        moe_w_gu[0], moe_w_down[0], norm_mix_g[1], True, BF16, (t,))

    u = _mm_call([xn], s5_w_in[0], [0], 1024, d, [], F32, _epi_plain, 40, "s5_in_proj")
    lre, lim, bbr, bbi = _s5_prep_call(s5_a_re[0], s5_a_im[0], s5_log_dt[0], s5_b_re[0], s5_b_im[0])
    bd, cd = _s5_mats(bbr, bbi, s5_c_re[0], s5_c_im[0])
    st_shape = (S5_FB * SUBLANES, LANES)
    lre = lre.reshape(st_shape)
    lim = lim.reshape(st_shape)
    zero_state = jnp.zeros((1,) + st_shape, F32)
    z_p, hpr, hpi = _s5_call(u, bd, cd, lre, lim, s5_d[0], zero_state, zero_state,
                             0, tp // S5_ROWS, S5_ROWS, True)
    z_s, hsr, hsi = _s5_call(u, bd, cd, lre, lim, s5_d[0],
                             state_s5_re[0].reshape((db,) + st_shape), state_s5_im[0].reshape((db,) + st_shape),
                             tp // S5_ROWS, ts // S5_ROWS, dseq, False)
    x3 = _mm_call([z_p, z_s], s5_w_out[0], [0, d // tn], tn, d, [([x2], tn, same_col)], F32, _epi_glu_residual,
                  40, "s5_out_proj")

    y_p, y_s = _moe_layer(x3, norm_ffn_g[1], moe_w_grp[1], moe_b_grp[1], moe_w_exp[1], moe_b_exp[1],
                          moe_w_gu[1], moe_w_down[1], norm_final_g, False, F32, (tp, ts))

    s5_shape = (1, -1, S5_GROUPS, S5_P)
    return (y_p.reshape(bp, seq, d), y_s.reshape(db, dseq, d),
            ret_p.reshape(1, bp, RET_HEADS, RET_DK, RET_DV), ret_s.reshape((1,) + ret_s.shape),
            hpr.reshape(s5_shape), hpi.reshape(s5_shape), hsr.reshape(s5_shape), hsi.reshape(s5_shape))
```

```python
import functools
import math

import numpy as np
import jax
import jax.numpy as jnp
from jax import lax
from jax.experimental import pallas as pl
from jax.experimental.pallas import tpu as pltpu

F32 = jnp.float32
BF16 = jnp.bfloat16

D_MODEL = 2048
PAST_LEN = 2048
CHUNK = 64
RET_HEADS = 8
RET_DK = D_MODEL // RET_HEADS
RET_DV = 2 * RET_DK
RET_VW = RET_HEADS * RET_DV
ROPE_BASE = 10000.0
S5_GC = 16
S5_GROUPS = D_MODEL // S5_GC
S5_P = 64
MOE_GROUPS = 4
MOE_PER_GROUP = 8
N_EXP = MOE_GROUPS * MOE_PER_GROUP
D_FF_E = 512
EPS = 1e-6

LANES = 128
SUBLANES = 8
ROW_TILE = 512
RET_ROWS = 256
MOE_TILE = 256
S5_ROWS = 128
S5_FB = S5_GROUPS // 16
S5_PITCH = S5_ROWS + 8
MIB = 1024 * 1024


def _params(sem, vmem_mib):
    return pltpu.CompilerParams(dimension_semantics=sem, vmem_limit_bytes=vmem_mib * MIB)


def _rms(x, g):
    ms = jnp.mean(x * x, axis=-1, keepdims=True)
    return x * lax.rsqrt(ms + EPS) * g


def _sigmoid(x):
    return 1.0 / (1.0 + jnp.exp(-x))


def _part_specs(parts, rows, cols, rowcol):
    specs = []
    start = 0
    for p in parts:
        nt = p.shape[0] // rows

        def imap(*g, start=start, nt=nt):
            i, jc = rowcol(*g)
            return (jnp.clip(i - start, 0, nt - 1), jc)

        specs.append(pl.BlockSpec((rows, cols), imap))
        start += nt
    return specs


def _part_tiles(parts, rows):
    return tuple(p.shape[0] // rows for p in parts)


def _for_each_part(i, tiles, fn):
    if len(tiles) == 1:
        fn(0)
        return
    start = 0
    for p, nt in enumerate(tiles):
        pl.when(jnp.logical_and(i >= start, i < start + nt))(functools.partial(fn, p))
        start += nt


def _pick(refs, p):
    return refs[min(p, len(refs) - 1)]


def _norm_kernel(*refs, tiles):
    n = len(tiles)
    x_refs, g_ref, o_ref = refs[:n], refs[n], refs[n + 1]

    def run(p):
        o_ref[...] = _rms(x_refs[p][...], g_ref[...]).astype(o_ref.dtype)

    _for_each_part(pl.program_id(0), tiles, run)


def _norm_call(x_parts, g, out_dtype):
    d = x_parts[0].shape[1]
    t = sum(p.shape[0] for p in x_parts)
    return pl.pallas_call(
        functools.partial(_norm_kernel, tiles=_part_tiles(x_parts, ROW_TILE)),
        grid=(t // ROW_TILE,),
        in_specs=_part_specs(x_parts, ROW_TILE, d, lambda i: (i, 0)) + [pl.BlockSpec((1, d), lambda i: (0, 0))],
        out_specs=pl.BlockSpec((ROW_TILE, d), lambda i: (i, 0)),
        out_shape=jax.ShapeDtypeStruct((t, d), out_dtype),
        compiler_params=_params(("arbitrary",), 32),
        name="rmsnorm",
    )(*x_parts, g.reshape(1, d))


def _mm_kernel(*refs, tiles, n_x, n_w, n_extra, epi):
    x_refs = refs[:n_x]
    w_refs = refs[n_x:n_x + n_w]
    pos = n_x + n_w
    extra = []
    for n in n_extra:
        extra.append(refs[pos:pos + n])
        pos += n
    o_ref = refs[pos]
    wb = refs[pos + 1:]

    @pl.when(pl.program_id(1) == 0)
    def _():
        for w_ref, b in zip(w_refs, wb):
            b[...] = w_ref[...].astype(BF16)

    def run(p):
        x = _pick(x_refs, p)[...]
        accs = [jnp.dot(x, b[...], preferred_element_type=F32) for b in wb]
        epi(accs, [_pick(e, p) for e in extra], o_ref)

    _for_each_part(pl.program_id(1), tiles, run)


def _mm_call(x_parts, w, w_col_offsets, tn, n_out, extras, out_dtype, epi, vmem_mib, name):
    k = x_parts[0].shape[1]
    t = sum(p.shape[0] for p in x_parts)
    n_w = len(w_col_offsets)
    split = [parts for parts in [x_parts] + [e[0] for e in extras] if len(parts) > 1]
    tiles = _part_tiles(split[0], ROW_TILE) if split else (t // ROW_TILE,)
    assert all(_part_tiles(parts, ROW_TILE) == tiles for parts in split)
    in_specs = _part_specs(x_parts, ROW_TILE, k, lambda j, i: (i, 0))
    for off in w_col_offsets:
        in_specs.append(pl.BlockSpec((k, tn), lambda j, i, off=off: (0, j + off)))
    extra_arrays = []
    for parts, cols, colfn in extras:
        in_specs += _part_specs(parts, ROW_TILE, cols, lambda j, i, colfn=colfn: (i, colfn(j)))
        extra_arrays += list(parts)
    return pl.pallas_call(
        functools.partial(_mm_kernel, tiles=tiles, n_x=len(x_parts), n_w=n_w,
                          n_extra=tuple(len(e[0]) for e in extras), epi=epi),
        grid=(n_out // tn, t // ROW_TILE),
        in_specs=in_specs,
        out_specs=pl.BlockSpec((ROW_TILE, tn), lambda j, i: (i, j)),
        out_shape=jax.ShapeDtypeStruct((t, n_out), out_dtype),
        scratch_shapes=[pltpu.VMEM((k, tn), BF16) for _ in range(n_w)],
        compiler_params=_params(("arbitrary", "arbitrary"), vmem_mib),
        name=name,
    )(*x_parts, *([w] * n_w), *extra_arrays)


def _epi_plain(accs, extra, o_ref):
    o_ref[...] = accs[0].astype(o_ref.dtype)


def _epi_residual(accs, extra, o_ref):
    o_ref[...] = (extra[0][...] + accs[0]).astype(o_ref.dtype)


def _epi_glu_residual(accs, extra, o_ref):
    a, gt = accs
    o_ref[...] = (extra[0][...] + a * _sigmoid(gt)).astype(o_ref.dtype)


def _epi_rope(accs, extra, o_ref, *, n_qk_blocks):
    acc = accs[0]
    cos_ref, sin_ref = extra
    j = pl.program_id(0)
    half = RET_DK // 2

    @pl.when(j < n_qk_blocks)
    def _():
        scale = jnp.where(j >= n_qk_blocks // 2, RET_DK ** -0.5, 1.0).astype(F32)
        c = cos_ref[...]
        s = sin_ref[...]
        for hh in range(acc.shape[1] // RET_DK):
            lo = hh * RET_DK
            t1 = acc[:, lo:lo + half]
            t2 = acc[:, lo + half:lo + RET_DK]
            o_ref[:, lo:lo + half] = ((t1 * c - t2 * s) * scale).astype(o_ref.dtype)
            o_ref[:, lo + half:lo + RET_DK] = ((t1 * s + t2 * c) * scale).astype(o_ref.dtype)

    @pl.when(j >= n_qk_blocks)
    def _():
        o_ref[...] = acc.astype(o_ref.dtype)


def _ret_tables(rows, chunk):
    lg = jnp.log(jnp.asarray(1.0 - 2.0 ** (-5.0 - np.arange(RET_HEADS)), dtype=F32))[:, None, None]
    n = jnp.arange(rows, dtype=F32)
    ci = np.arange(rows) // chunk
    same_or_earlier = jnp.asarray(ci[None, :] <= ci[:, None])
    dmat = jnp.where(same_or_earlier[None], jnp.exp(jnp.abs(n[:, None] - n[None, :])[None] * lg), 0.0)
    dq = jnp.exp((n + 1.0)[None, :, None] * lg) * jnp.ones((1, 1, LANES), F32)
    dk = jnp.exp((rows - 1.0 - n)[None, :, None] * lg) * jnp.ones((1, 1, LANES), F32)
    gr = jnp.exp(rows * lg) * jnp.ones((1, 1, RET_DV), F32)
    return dmat, dq, dk, gr


def _ret_block(q, k, v, g, s_prev, dmat, dq, dk, gr, gn):
    s = lax.dot_general(q, k, (((1,), (1,)), ((), ())), preferred_element_type=F32) * dmat
    inter = jnp.dot(q, s_prev.astype(BF16), preferred_element_type=F32)
    o = jnp.dot(s.astype(BF16), v, preferred_element_type=F32) + jnp.concatenate([dq] * (RET_DV // LANES), axis=1) * inter
    kd = (k.astype(F32) * jnp.concatenate([dk] * (RET_DK // LANES), axis=1)).astype(BF16)
    s_new = s_prev * gr + lax.dot_general(kd, v, (((0,), (0,)), ((), ())), preferred_element_type=F32)
    mu = jnp.mean(o, axis=-1, keepdims=True)
    oc = o - mu
    var = jnp.mean(oc * oc, axis=-1, keepdims=True)
    on = oc * lax.rsqrt(var + EPS) * gn
    gf = g.astype(F32)
    return (gf * _sigmoid(gf) * on).astype(BF16), s_new


def _ret_prompt_kernel(q_ref, k_ref, v_ref, g_ref, dm_ref, dq_ref, dk_ref, gr_ref, gn_ref,
                       og_ref, st_ref, s_scr):
    @pl.when(pl.program_id(1) == 0)
    def _():
        s_scr[...] = jnp.zeros_like(s_scr)

    og, s_new = _ret_block(q_ref[...], k_ref[...], v_ref[...], g_ref[...], s_scr[...],
                           dm_ref[0], dq_ref[0], dk_ref[0], gr_ref[0], gn_ref[...])
    og_ref[...] = og
    s_scr[...] = s_new
    st_ref[0] = s_new


def _ret_prompt_call(proj, gn, t_prompt):
    dm, dq, dk, gr = _ret_tables(RET_ROWS, CHUNK)
    h = RET_HEADS
    kv_blk = 2 * D_MODEL // RET_DV
    g_blk = (2 * D_MODEL + RET_VW) // RET_DV
    return pl.pallas_call(
        _ret_prompt_kernel,
        grid=(h, t_prompt // RET_ROWS),
        in_specs=[
            pl.BlockSpec((RET_ROWS, RET_DK), lambda hh, c: (c, hh)),
            pl.BlockSpec((RET_ROWS, RET_DK), lambda hh, c: (c, h + hh)),
            pl.BlockSpec((RET_ROWS, RET_DV), lambda hh, c: (c, kv_blk + hh)),
            pl.BlockSpec((RET_ROWS, RET_DV), lambda hh, c: (c, g_blk + hh)),
            pl.BlockSpec((1, RET_ROWS, RET_ROWS), lambda hh, c: (hh, 0, 0)),
            pl.BlockSpec((1, RET_ROWS, LANES), lambda hh, c: (hh, 0, 0)),
            pl.BlockSpec((1, RET_ROWS, LANES), lambda hh, c: (hh, 0, 0)),
            pl.BlockSpec((1, 1, RET_DV), lambda hh, c: (hh, 0, 0)),
            pl.BlockSpec((1, RET_DV), lambda hh, c: (0, hh)),
        ],
        out_specs=[
            pl.BlockSpec((RET_ROWS, RET_DV), lambda hh, c: (c, hh)),
            pl.BlockSpec((1, RET_DK, RET_DV), lambda hh, c: (hh, 0, 0)),
        ],
        out_shape=[
            jax.ShapeDtypeStruct((t_prompt, RET_VW), BF16),
            jax.ShapeDtypeStruct((h, RET_DK, RET_DV), F32),
        ],
        scratch_shapes=[pltpu.VMEM((RET_DK, RET_DV), F32)],
        compiler_params=_params(("arbitrary", "arbitrary"), 32),
        name="retention_prompt",
    )(proj, proj, proj, proj, dm, dq, dk, gr, gn.reshape(1, RET_VW))


def _ret_sample_kernel(q_ref, k_ref, v_ref, g_ref, s0_ref, dm_ref, dq_ref, dk_ref, gr_ref, gn_ref,
                       og_ref, st_ref):
    for hh in range(RET_HEADS):
        ks = slice(hh * RET_DK, (hh + 1) * RET_DK)
        vs = slice(hh * RET_DV, (hh + 1) * RET_DV)
        og, s_new = _ret_block(q_ref[:, ks], k_ref[:, ks], v_ref[:, vs], g_ref[:, vs], s0_ref[0, hh],
                               dm_ref[hh], dq_ref[hh], dk_ref[hh], gr_ref[hh], gn_ref[:, vs])
        og_ref[:, vs] = og
        st_ref[0, hh] = s_new


def _ret_sample_call(proj, gn, state, t_prompt, dec_seq):
    dm, dq, dk, gr = _ret_tables(dec_seq, dec_seq)
    b = state.shape[0]
    r0 = t_prompt // dec_seq
    const3 = lambda bb: (0, 0, 0)
    return pl.pallas_call(
        _ret_sample_kernel,
        grid=(b,),
        in_specs=[
            pl.BlockSpec((dec_seq, D_MODEL), lambda bb: (r0 + bb, 0)),
            pl.BlockSpec((dec_seq, D_MODEL), lambda bb: (r0 + bb, 1)),
            pl.BlockSpec((dec_seq, RET_VW), lambda bb: (r0 + bb, 1)),
            pl.BlockSpec((dec_seq, RET_VW), lambda bb: (r0 + bb, 2)),
            pl.BlockSpec((1, RET_HEADS, RET_DK, RET_DV), lambda bb: (bb, 0, 0, 0)),
            pl.BlockSpec(dm.shape, const3),
            pl.BlockSpec(dq.shape, const3),
            pl.BlockSpec(dk.shape, const3),
            pl.BlockSpec(gr.shape, const3),
            pl.BlockSpec((1, RET_VW), lambda bb: (0, 0)),
        ],
        out_specs=[
            pl.BlockSpec((dec_seq, RET_VW), lambda bb: (bb, 0)),
            pl.BlockSpec((1, RET_HEADS, RET_DK, RET_DV), lambda bb: (bb, 0, 0, 0)),
        ],
        out_shape=[
            jax.ShapeDtypeStruct((b * dec_seq, RET_VW), BF16),
            jax.ShapeDtypeStruct(state.shape, F32),
        ],
        compiler_params=_params(("arbitrary",), 40),
        name="retention_sample",
    )(proj, proj, proj, proj, state, dm, dq, dk, gr, gn.reshape(1, RET_VW))


def _router_kernel(x_ref, g_ref, wr_ref, br_ref, xn_ref, id_ref, w_ref):
    xn = _rms(x_ref[...], g_ref[...])
    xn_ref[...] = xn
    xh = xn.astype(BF16)
    xl = (xn - xh.astype(F32)).astype(BF16)
    w = wr_ref[...]
    wh = w.astype(BF16)
    wl = (w - wh.astype(F32)).astype(BF16)
    lg = (jnp.dot(xh, wh, preferred_element_type=F32) + jnp.dot(xh, wl, preferred_element_type=F32)
          + jnp.dot(xl, wh, preferred_element_type=F32)) + br_ref[...]
    lane = lax.broadcasted_iota(jnp.int32, lg.shape, 1).astype(F32)
    neg = jnp.float32(-jnp.inf)
    big = jnp.float32(LANES)
    gl = jnp.where(lane < MOE_GROUPS, lg, neg)
    gmax = jnp.max(gl, axis=-1, keepdims=True)
    gsel = jnp.min(jnp.where(gl == gmax, lane, big), axis=-1, keepdims=True)
    gw = 1.0 / jnp.sum(jnp.exp(gl - gmax), axis=-1, keepdims=True)
    lo = MOE_GROUPS + gsel * MOE_PER_GROUP
    el = jnp.where((lane >= lo) & (lane < lo + MOE_PER_GROUP), lg, neg)
    m1 = jnp.max(el, axis=-1, keepdims=True)
    i1 = jnp.min(jnp.where(el == m1, lane, big), axis=-1, keepdims=True)
    el2 = jnp.where(lane == i1, neg, el)
    m2 = jnp.max(el2, axis=-1, keepdims=True)
    i2 = jnp.min(jnp.where(el2 == m2, lane, big), axis=-1, keepdims=True)
    z = jnp.sum(jnp.exp(el - m1), axis=-1, keepdims=True)
    p1 = 1.0 / z
    p2 = jnp.exp(m2 - m1) / z
    tot = p1 + p2
    w1 = p1 / tot * gw
    w2 = p2 / tot * gw
    ids = jnp.where(lane == 0, i1 - MOE_GROUPS, jnp.where(lane == 1, i2 - MOE_GROUPS, 0.0))
    id_ref[...] = ids.astype(jnp.int32)
    w_ref[...] = jnp.where(lane == 0, w1, jnp.where(lane == 1, w2, 0.0))


def _router_call(x, g, w_grp, b_grp, w_exp, b_exp):
    t, d = x.shape
    pad = LANES - MOE_GROUPS - N_EXP
    wr = jnp.concatenate([w_grp, w_exp, jnp.zeros((d, pad), F32)], axis=1)
    br = jnp.concatenate([b_grp, b_exp, jnp.zeros((pad,), F32)]).reshape(1, LANES)
    row = lambda i: (i, 0)
    fixed = lambda i: (0, 0)
    return pl.pallas_call(
        _router_kernel,
        grid=(t // ROW_TILE,),
        in_specs=[pl.BlockSpec((ROW_TILE, d), row), pl.BlockSpec((1, d), fixed),
                  pl.BlockSpec((d, LANES), fixed), pl.BlockSpec((1, LANES), fixed)],
        out_specs=[pl.BlockSpec((ROW_TILE, d), row), pl.BlockSpec((ROW_TILE, LANES), row),
                   pl.BlockSpec((ROW_TILE, LANES), row)],
        out_shape=[jax.ShapeDtypeStruct((t, d), F32), jax.ShapeDtypeStruct((t, LANES), jnp.int32),
                   jax.ShapeDtypeStruct((t, LANES), F32)],
        compiler_params=_params(("arbitrary",), 40),
        name="moe_router",
    )(x, g.reshape(1, d), wr, br)


def _row_tokens_kernel(pos_ref, tok_ref):
    def clear(r, carry):
        tok_ref[r] = 0
        return carry
    lax.fori_loop(0, tok_ref.shape[0], clear, 0, unroll=8)

    def place(a, carry):
        tok_ref[pos_ref[a]] = lax.shift_right_logical(a, 1)
        return carry
    lax.fori_loop(0, pos_ref.shape[0], place, 0, unroll=8)


def _row_tokens_call(pos, rows):
    smem = pl.BlockSpec(memory_space=pltpu.SMEM)
    return pl.pallas_call(
        _row_tokens_kernel,
        in_specs=[smem],
        out_specs=smem,
        out_shape=jax.ShapeDtypeStruct((rows,), jnp.int32),
        name="moe_row_tokens",
    )(pos)


def _moe_plan(eid):
    t = eid.shape[0]
    a = 2 * t
    n_tiles = a // MOE_TILE + N_EXP
    e_flat = eid.reshape(a)
    onehot = (e_flat[:, None] == jnp.arange(N_EXP, dtype=jnp.int32)[None, :]).astype(jnp.int32)
    csum = jnp.cumsum(onehot, axis=0)
    counts = csum[-1]
    rank = jnp.take_along_axis(csum, e_flat[:, None], axis=1)[:, 0] - 1
    tiles_e = (counts + MOE_TILE - 1) // MOE_TILE
    tiles_end = jnp.cumsum(tiles_e)
    row_start = (tiles_end - tiles_e) * MOE_TILE
    pos = row_start[e_flat] + rank
    n_valid = tiles_end[-1]
    tile_id = jnp.minimum(jnp.arange(n_tiles, dtype=jnp.int32), n_valid - 1)
    tile_expert = jnp.sum((tile_id[:, None] >= tiles_end[None, :]).astype(jnp.int32), axis=1)
    tile_expert = jnp.minimum(tile_expert, N_EXP - 1)
    pos = pos.astype(jnp.int32)
    tok_of_row = _row_tokens_call(pos, n_tiles * MOE_TILE)
    return tile_expert, n_valid.reshape(1).astype(jnp.int32), tok_of_row, pos


def _moe_ffn_kernel(te_ref, nv_ref, tok_ref, x_hbm, wgu_ref, wd_ref, o_ref,
                    xbuf, sems, wgu_b, wd_b):
    i = pl.program_id(0)
    nv = nv_ref[0]
    slot = lax.rem(i, 2)
    ff_chunk = 256
    n_chunk = D_FF_E // ff_chunk
    row_groups = np.array_split(np.arange(MOE_TILE), 3 * n_chunk)

    def row_copy(row0, r, dst_slot):
        return pltpu.make_async_copy(x_hbm.at[pl.ds(tok_ref[row0 + r], 1)], xbuf.at[dst_slot, pl.ds(r, 1)],
                                     sems.at[dst_slot])

    def wait_tile(s):
        pltpu.make_async_copy(x_hbm.at[pl.ds(0, MOE_TILE)], xbuf.at[s], sems.at[s]).wait()

    @pl.when(i == 0)
    def _():
        for r in range(MOE_TILE):
            row_copy(0, r, 0).start()

    new_expert = jnp.logical_or(i == 0, te_ref[i] != te_ref[jnp.maximum(i - 1, 0)])

    @pl.when(jnp.logical_and(new_expert, i < nv))
    def _():
        wgu_b[...] = wgu_ref[0, 0].astype(BF16)
        wd_b[...] = wd_ref[0, 0].astype(BF16)

    def compute(s):
        wait_tile(s)
        x = xbuf[s].astype(BF16)
        nxt = (i + 1) * MOE_TILE
        groups = iter(row_groups)

        def request_rows():
            for r in next(groups):
                row_copy(nxt, int(r), 1 - s).start()

        for c in range(n_chunk):
            lo = c * ff_chunk
            hg = jnp.dot(x, wgu_b[:, lo:lo + ff_chunk], preferred_element_type=F32)
            request_rows()
            hu = jnp.dot(x, wgu_b[:, D_FF_E + lo:D_FF_E + lo + ff_chunk], preferred_element_type=F32)
            request_rows()
            act = (hg * _sigmoid(hg) * hu).astype(BF16)
            y = jnp.dot(act, wd_b[lo:lo + ff_chunk, :], preferred_element_type=F32)
            if c == 0:
                o_ref[...] = y
            else:
                o_ref[...] += y
            request_rows()

    for s in range(2):
        pl.when(jnp.logical_and(i < nv, slot == s))(functools.partial(compute, s))

    @pl.when(i == nv)
    def _():
        wait_tile(slot)

    @pl.when(i >= nv)
    def _():
        o_ref[...] = jnp.zeros_like(o_ref)


def _moe_ffn_call(xn, w_gu, w_down, layer, tile_expert, n_valid, tok_of_row):
    t, d = xn.shape
    n_tiles = tile_expert.shape[0]
    rows = n_tiles * MOE_TILE
    grid_spec = pltpu.PrefetchScalarGridSpec(
        num_scalar_prefetch=3,
        grid=(n_tiles,),
        in_specs=[
            pl.BlockSpec(memory_space=pl.ANY),
            pl.BlockSpec((1, 1, d, 2 * D_FF_E), lambda i, te, nv, tok: (layer, te[i], 0, 0)),
            pl.BlockSpec((1, 1, D_FF_E, d), lambda i, te, nv, tok: (layer, te[i], 0, 0)),
        ],
        out_specs=pl.BlockSpec((MOE_TILE, d), lambda i, te, nv, tok: (i, 0)),
        scratch_shapes=[
            pltpu.VMEM((2, MOE_TILE, d), F32),
            pltpu.SemaphoreType.DMA((2,)),
            pltpu.VMEM((d, 2 * D_FF_E), BF16),
            pltpu.VMEM((D_FF_E, d), BF16),
        ],
    )
    return pl.pallas_call(
        _moe_ffn_kernel,
        grid_spec=grid_spec,
        out_shape=jax.ShapeDtypeStruct((rows, d), F32),
        compiler_params=_params(("arbitrary",), 48),
        name="moe_experts",
    )(tile_expert, n_valid, tok_of_row, xn, w_gu, w_down)


def _combine_kernel(pos_ref, x_ref, w_ref, y_hbm, g_ref, *rest, emit_x, norm_tiles):
    outs = rest[:int(emit_x) + len(norm_tiles)]
    buf, sems = rest[int(emit_x) + len(norm_tiles):]
    i = pl.program_id(0)
    n = pl.num_programs(0)
    slot = lax.rem(i, 2)
    tm = x_ref.shape[0]

    def gather(tile, dst_slot):
        def body(r, carry):
            for kk in range(2):
                p = pos_ref[(tile * tm + r) * 2 + kk]
                pltpu.make_async_copy(y_hbm.at[pl.ds(p, 1)], buf.at[dst_slot, kk, pl.ds(r, 1)],
                                      sems.at[dst_slot]).start()
            return carry
        lax.fori_loop(0, tm, body, 0, unroll=8)

    @pl.when(i == 0)
    def _():
        gather(0, 0)

    @pl.when(i + 1 < n)
    def _():
        gather(i + 1, 1 - slot)

    for kk in range(2):
        pltpu.make_async_copy(y_hbm.at[pl.ds(0, tm)], buf.at[slot, kk], sems.at[slot]).wait()
    xo = x_ref[...] + (w_ref[:, 0:1] * buf[slot, 0] + w_ref[:, 1:2] * buf[slot, 1])
    norm_refs = outs[int(emit_x):]
    if emit_x:
        outs[0][...] = xo

    def write_norm(p):
        norm_refs[p][...] = _rms(xo, g_ref[...]).astype(norm_refs[p].dtype)

    _for_each_part(i, norm_tiles, write_norm)


def _combine_call(x, ws, y_rows, pos, g, emit_x, norm_dtype, norm_rows):
    t, d = x.shape
    tm = MOE_TILE
    row = lambda i, pos_ref: (i, 0)
    out_specs, out_shape = [], []
    if emit_x:
        out_specs.append(pl.BlockSpec((tm, d), row))
        out_shape.append(jax.ShapeDtypeStruct((t, d), F32))
    norm_parts = [jax.ShapeDtypeStruct((r, d), norm_dtype) for r in norm_rows]
    out_specs += _part_specs(norm_parts, tm, d, row)
    out_shape += norm_parts
    grid_spec = pltpu.PrefetchScalarGridSpec(
        num_scalar_prefetch=1,
        grid=(t // tm,),
        in_specs=[pl.BlockSpec((tm, d), row), pl.BlockSpec((tm, LANES), row), pl.BlockSpec(memory_space=pl.ANY),
                  pl.BlockSpec((1, d), lambda i, pos_ref: (0, 0))],
        out_specs=out_specs,
        scratch_shapes=[pltpu.VMEM((2, 2, tm, d), F32), pltpu.SemaphoreType.DMA((2,))],
    )
    return pl.pallas_call(
        functools.partial(_combine_kernel, emit_x=emit_x, norm_tiles=_part_tiles(norm_parts, tm)),
        grid_spec=grid_spec,
        out_shape=out_shape,
        compiler_params=_params(("arbitrary",), 40),
        name="moe_combine",
    )(pos, x, ws, y_rows, g.reshape(1, d))


def _moe_layer(x, g_ffn, w_grp, b_grp, w_exp, b_exp, w_gu, w_down, layer, g_next, emit_x, norm_dtype, norm_rows):
    xn, ids, ws = _router_call(x, g_ffn, w_grp, b_grp, w_exp, b_exp)
    tile_expert, n_valid, tok_of_row, pos = _moe_plan(ids[:, :2])
    y_rows = _moe_ffn_call(xn, w_gu, w_down, layer, tile_expert, n_valid, tok_of_row)
    return _combine_call(x, ws, y_rows, pos, g_next, emit_x, norm_dtype, norm_rows)


def _s5_prep_kernel(are_ref, aim_ref, ldt_ref, bre_ref, bim_ref, lre_ref, lim_ref, bbr_ref, bbi_ref):
    are = are_ref[...]
    aim = aim_ref[...]
    dt = jnp.exp(ldt_ref[...])
    mag = jnp.exp(are * dt)
    lr = mag * jnp.cos(aim * dt)
    li = mag * jnp.sin(aim * dt)
    lre_ref[...] = lr
    lim_ref[...] = li
    den = are * are + aim * aim
    nr = lr - 1.0
    cr = (nr * are + li * aim) / den
    ci = (li * are - nr * aim) / den
    bre = bre_ref[...]
    bim = bim_ref[...]
    bbr_ref[...] = cr * bre - ci * bim
    bbi_ref[...] = cr * bim + ci * bre


def _s5_prep_call(a_re, a_im, log_dt, b_re, b_im):
    g, p, c = b_re.shape
    shp = (g, p * c)
    expand = lambda a: jnp.broadcast_to(a[:, :, None], (g, p, c)).reshape(shp)
    ldt = jnp.broadcast_to(log_dt[:, None], shp)
    outs = pl.pallas_call(
        _s5_prep_kernel,
        out_shape=[jax.ShapeDtypeStruct(shp, F32)] * 4,
        name="s5_discretize",
    )(expand(a_re), expand(a_im), ldt, b_re.reshape(shp), b_im.reshape(shp))
    lre, lim, bbr, bbi = outs
    return lre[:, ::c], lim[:, ::c], bbr, bbi


def _s5_mats(bbr, bbi, c_re, c_im):
    eye = jnp.eye(16, dtype=F32)

    def bd_part(bb):
        xx = bb.reshape(S5_FB, 16, S5_P, S5_GC).transpose(0, 1, 3, 2)
        yy = xx[:, :, :, None, :] * eye[None, :, None, :, None]
        return yy.reshape(S5_FB, 16 * S5_GC, 16 * S5_P)

    def cd_part(cc):
        xx = cc.reshape(S5_FB, 16, S5_GC, S5_P).transpose(0, 1, 3, 2)
        yy = xx[:, :, :, None, :] * eye[None, :, None, :, None]
        return yy.reshape(S5_FB, 16 * S5_P, 16 * S5_GC)

    bd = jnp.concatenate([bd_part(bbr), bd_part(bbi)], axis=-1).astype(BF16)
    cd = jnp.concatenate([cd_part(c_re), -cd_part(c_im)], axis=1).astype(BF16)
    return bd, cd


def _s5_kernel(u_ref, bd_ref, cd_ref, lre_ref, lim_ref, d_ref, h0r_ref, h0i_ref,
               z_ref, hr_ref, hi_ref, scr, hst, *, seg, carry):
    rows = u_ref.shape[0]
    n_seq = rows // seg
    ch = 16 * S5_GC
    half = SUBLANES * S5_PITCH
    blk = 2 * half

    for fb in range(S5_FB):
        bu = jnp.dot(u_ref[:, fb * ch:(fb + 1) * ch].astype(BF16), bd_ref[fb], preferred_element_type=F32)
        for c in range(2 * SUBLANES):
            r0 = fb * blk + c * S5_PITCH
            scr[r0:r0 + rows, :] = bu[:, c * LANES:(c + 1) * LANES]

    lre = [lre_ref[fb * SUBLANES:(fb + 1) * SUBLANES, :] for fb in range(S5_FB)]
    lim = [lim_ref[fb * SUBLANES:(fb + 1) * SUBLANES, :] for fb in range(S5_FB)]

    def scan(row0, state):
        def step(l, st):
            new = []
            for fb in range(S5_FB):
                hr, hi = st[2 * fb], st[2 * fb + 1]
                ire = pl.ds(fb * blk + row0 + l, SUBLANES, stride=S5_PITCH)
                iim = pl.ds(fb * blk + half + row0 + l, SUBLANES, stride=S5_PITCH)
                nr = lre[fb] * hr - lim[fb] * hi + scr[ire, :]
                ni = lre[fb] * hi + lim[fb] * hr + scr[iim, :]
                scr[ire, :] = nr
                scr[iim, :] = ni
                new += [nr, ni]
            return tuple(new)
        return lax.fori_loop(0, seg, step, state)

    def load_state(rref, iref, lead):
        st = []
        for fb in range(S5_FB):
            st += [rref[lead + (slice(fb * SUBLANES, (fb + 1) * SUBLANES), slice(None))],
                   iref[lead + (slice(fb * SUBLANES, (fb + 1) * SUBLANES), slice(None))]]
        return tuple(st)

    def store_state(st, rref, iref, lead):
        for fb in range(S5_FB):
            rref[lead + (slice(fb * SUBLANES, (fb + 1) * SUBLANES), slice(None))] = st[2 * fb]
            iref[lead + (slice(fb * SUBLANES, (fb + 1) * SUBLANES), slice(None))] = st[2 * fb + 1]

    if carry:
        @pl.when(pl.program_id(0) == 0)
        def _():
            hst[0] = h0r_ref[0]
            hst[1] = h0i_ref[0]
        hre_scr = hst.at[0]
        him_scr = hst.at[1]
        st = scan(0, load_state(hre_scr, him_scr, ()))
        store_state(st, hre_scr, him_scr, ())
        store_state(st, hr_ref, hi_ref, (0,))
    else:
        for s in range(n_seq):
            st = scan(s * seg, load_state(h0r_ref, h0i_ref, (s,)))
            store_state(st, hr_ref, hi_ref, (s,))

    for fb in range(S5_FB):
        hf = jnp.concatenate(
            [scr[fb * blk + c * S5_PITCH: fb * blk + c * S5_PITCH + rows, :] for c in range(2 * SUBLANES)],
            axis=1).astype(BF16)
        y = jnp.dot(hf, cd_ref[fb], preferred_element_type=F32)
        cs = slice(fb * ch, (fb + 1) * ch)
        yy = y + d_ref[:, cs] * u_ref[:, cs]
        z_ref[:, cs] = jax.nn.gelu(yy).astype(z_ref.dtype)


def _s5_call(u, bd, cd, lre, lim, d_skip, h0r, h0i, row_block0, n_blocks, seg, carry):
    t, d = u.shape
    n_seq = S5_ROWS // seg
    ns_total = h0r.shape[0]
    st_idx = (lambda i: (0, 0, 0)) if carry else (lambda i: (i, 0, 0))
    fixed2 = lambda i: (0, 0)
    fixed3 = lambda i: (0, 0, 0)
    st_rows = S5_FB * SUBLANES
    return pl.pallas_call(
        functools.partial(_s5_kernel, seg=seg, carry=carry),
        grid=(n_blocks,),
        in_specs=[
            pl.BlockSpec((S5_ROWS, d), lambda i: (row_block0 + i, 0)),
            pl.BlockSpec(bd.shape, fixed3),
            pl.BlockSpec(cd.shape, fixed3),
            pl.BlockSpec(lre.shape, fixed2),
            pl.BlockSpec(lim.shape, fixed2),
            pl.BlockSpec((1, d), fixed2),
            pl.BlockSpec((n_seq, st_rows, LANES), st_idx),
            pl.BlockSpec((n_seq, st_rows, LANES), st_idx),
        ],
        out_specs=[
            pl.BlockSpec((S5_ROWS, d), lambda i: (i, 0)),
            pl.BlockSpec((n_seq, st_rows, LANES), st_idx),
            pl.BlockSpec((n_seq, st_rows, LANES), st_idx),
        ],
        out_shape=[
            jax.ShapeDtypeStruct((n_blocks * S5_ROWS, d), BF16),
            jax.ShapeDtypeStruct((ns_total, st_rows, LANES), F32),
            jax.ShapeDtypeStruct((ns_total, st_rows, LANES), F32),
        ],
        scratch_shapes=[
            pltpu.VMEM((2 * SUBLANES * S5_FB * S5_PITCH, LANES), F32),
            pltpu.VMEM((2, st_rows, LANES), F32),
        ],
        compiler_params=_params(("arbitrary",), 56),
        name="s5_scan",
    )(u, bd, cd, lre, lim, d_skip.reshape(1, d), h0r, h0i)


def kernel(x_prompt, x_sample, state_ret, state_s5_re, state_s5_im, norm_mix_g, norm_ffn_g, norm_final_g,
           ret_w_in, ret_gn_g, ret_w_out, s5_w_in, s5_a_re, s5_a_im, s5_log_dt, s5_b_re, s5_b_im,
           s5_c_re, s5_c_im, s5_d, s5_w_out, moe_w_grp, moe_b_grp, moe_w_exp, moe_b_exp, moe_w_gu, moe_w_down):
    bp, seq, d = x_prompt.shape
    db, dseq, _ = x_sample.shape
    assert d == D_MODEL and bp == 1
    assert seq % max(ROW_TILE, RET_ROWS) == 0 and seq % CHUNK == 0
    assert dseq < CHUNK and dseq % 16 == 0 and S5_ROWS % dseq == 0
    tp = bp * seq
    ts = db * dseq
    t = tp + ts
    assert ts % ROW_TILE == 0 and ts % S5_ROWS == 0

    x_parts = [x_prompt.reshape(tp, d), x_sample.reshape(ts, d)]

    half = RET_DK // 2
    freqs = ROPE_BASE ** (-jnp.arange(half, dtype=F32) / half)
    pos = jnp.concatenate([jnp.arange(seq, dtype=F32), jnp.tile(PAST_LEN + jnp.arange(dseq, dtype=F32), db)])
    ang = pos[:, None] * freqs[None, :]
    cos_t = jnp.cos(ang)
    sin_t = jnp.sin(ang)

    xn = _norm_call(x_parts, norm_mix_g[0], BF16)
    n_proj = 2 * D_MODEL + 2 * RET_VW
    tn = 1024
    first_col = lambda j: 0
    proj = _mm_call([xn], ret_w_in[0], [0], tn, n_proj, [([cos_t], half, first_col), ([sin_t], half, first_col)],
                    BF16, functools.partial(_epi_rope, n_qk_blocks=2 * D_MODEL // tn), 40, "ret_in_proj")
    og_p, ret_p = _ret_prompt_call(proj, ret_gn_g[0], tp)
    og_s, ret_s = _ret_sample_call(proj, ret_gn_g[0], state_ret[0], tp, dseq)
    tn = 512
    same_col = lambda j: j
    x1 = _mm_call([og_p, og_s], ret_w_out[0], [0], tn, d, [(x_parts, tn, same_col)], F32, _epi_residual,
                  40, "ret_out_proj")

    x2, xn = _moe_layer(x1, norm_ffn_g[0], moe_w_grp[0], moe_b_grp[0], moe_w_exp[0], moe_b_exp[0],
                        moe_w_gu, moe_w_down, 0, norm_mix_g[1], True, BF16, (t,))

    u = _mm_call([xn], s5_w_in[0], [0], 1024, d, [], F32, _epi_plain, 40, "s5_in_proj")
    lre, lim, bbr, bbi = _s5_prep_call(s5_a_re[0], s5_a_im[0], s5_log_dt[0], s5_b_re[0], s5_b_im[0])
    bd, cd = _s5_mats(bbr, bbi, s5_c_re[0], s5_c_im[0])
    st_shape = (S5_FB * SUBLANES, LANES)
    lre = lre.reshape(st_shape)
    lim = lim.reshape(st_shape)
    zero_state = jnp.zeros((1,) + st_shape, F32)
    z_p, hpr, hpi = _s5_call(u, bd, cd, lre, lim, s5_d[0], zero_state, zero_state,
                             0, tp // S5_ROWS, S5_ROWS, True)
    z_s, hsr, hsi = _s5_call(u, bd, cd, lre, lim, s5_d[0],
                             state_s5_re[0].reshape((db,) + st_shape), state_s5_im[0].reshape((db,) + st_shape),
                             tp // S5_ROWS, ts // S5_ROWS, dseq, False)
    x3 = _mm_call([z_p, z_s], s5_w_out[0], [0, d // tn], tn, d, [([x2], tn, same_col)], F32, _epi_glu_residual,
                  40, "s5_out_proj")

    y_p, y_s = _moe_layer(x3, norm_ffn_g[1], moe_w_grp[1], moe_b_grp[1], moe_w_exp[1], moe_b_exp[1],
                          moe_w_gu, moe_w_down, 1, norm_final_g, False, F32, (tp, ts))

    s5_shape = (1, -1, S5_GROUPS, S5_P)
    return (y_p.reshape(bp, seq, d), y_s.reshape(db, dseq, d),
            ret_p.reshape(1, bp, RET_HEADS, RET_DK, RET_DV), ret_s.reshape((1,) + ret_s.shape),
            hpr.reshape(s5_shape), hpi.reshape(s5_shape), hsr.reshape(s5_shape), hsi.reshape(s5_shape))
```

```python
import functools
import math

import numpy as np
import jax
import jax.numpy as jnp
from jax import lax
from jax.experimental import pallas as pl
from jax.experimental.pallas import tpu as pltpu

F32 = jnp.float32
BF16 = jnp.bfloat16

D_MODEL = 2048
PAST_LEN = 2048
CHUNK = 64
RET_HEADS = 8
RET_DK = D_MODEL // RET_HEADS
RET_DV = 2 * RET_DK
RET_VW = RET_HEADS * RET_DV
ROPE_BASE = 10000.0
S5_GC = 16
S5_GROUPS = D_MODEL // S5_GC
S5_P = 64
MOE_GROUPS = 4
MOE_PER_GROUP = 8
N_EXP = MOE_GROUPS * MOE_PER_GROUP
D_FF_E = 512
EPS = 1e-6

LANES = 128
SUBLANES = 8
ROW_TILE = 512
RET_ROWS = 256
MOE_TILE = 256
S5_ROWS = 128
S5_FB = S5_GROUPS // 16
S5_PITCH = S5_ROWS + 8
MIB = 1024 * 1024


def _params(sem, vmem_mib):
    return pltpu.CompilerParams(dimension_semantics=sem, vmem_limit_bytes=vmem_mib * MIB)


def _rms(x, g):
    ms = jnp.mean(x * x, axis=-1, keepdims=True)
    return x * lax.rsqrt(ms + EPS) * g


def _sigmoid(x):
    return 1.0 / (1.0 + jnp.exp(-x))


def _part_specs(parts, rows, cols, rowcol):
    specs = []
    start = 0
    for p in parts:
        nt = p.shape[0] // rows

        def imap(*g, start=start, nt=nt):
            i, jc = rowcol(*g)
            return (jnp.clip(i - start, 0, nt - 1), jc)

        specs.append(pl.BlockSpec((rows, cols), imap))
        start += nt
    return specs


def _part_tiles(parts, rows):
    return tuple(p.shape[0] // rows for p in parts)


def _for_each_part(i, tiles, fn):
    if len(tiles) == 1:
        fn(0)
        return
    start = 0
    for p, nt in enumerate(tiles):
        pl.when(jnp.logical_and(i >= start, i < start + nt))(functools.partial(fn, p))
        start += nt


def _pick(refs, p):
    return refs[min(p, len(refs) - 1)]


def _norm_kernel(*refs, tiles):
    n = len(tiles)
    x_refs, g_ref, o_ref = refs[:n], refs[n], refs[n + 1]

    def run(p):
        o_ref[...] = _rms(x_refs[p][...], g_ref[...]).astype(o_ref.dtype)

    _for_each_part(pl.program_id(0), tiles, run)


def _norm_call(x_parts, g, out_dtype):
    d = x_parts[0].shape[1]
    t = sum(p.shape[0] for p in x_parts)
    return pl.pallas_call(
        functools.partial(_norm_kernel, tiles=_part_tiles(x_parts, ROW_TILE)),
        grid=(t // ROW_TILE,),
        in_specs=_part_specs(x_parts, ROW_TILE, d, lambda i: (i, 0)) + [pl.BlockSpec((1, d), lambda i: (0, 0))],
        out_specs=pl.BlockSpec((ROW_TILE, d), lambda i: (i, 0)),
        out_shape=jax.ShapeDtypeStruct((t, d), out_dtype),
        compiler_params=_params(("arbitrary",), 32),
        name="rmsnorm",
    )(*x_parts, g.reshape(1, d))


def _mm_kernel(*refs, tiles, n_x, n_w, n_extra, epi):
    x_refs = refs[:n_x]
    w_refs = refs[n_x:n_x + n_w]
    pos = n_x + n_w
    extra = []
    for n in n_extra:
        extra.append(refs[pos:pos + n])
        pos += n
    o_ref = refs[pos]
    wb = refs[pos + 1:]

    @pl.when(pl.program_id(1) == 0)
    def _():
        for w_ref, b in zip(w_refs, wb):
            b[...] = w_ref[...].astype(BF16)

    def run(p):
        x = _pick(x_refs, p)[...]
        accs = [jnp.dot(x, b[...], preferred_element_type=F32) for b in wb]
        epi(accs, [_pick(e, p) for e in extra], o_ref)

    _for_each_part(pl.program_id(1), tiles, run)


def _mm_call(x_parts, w, w_col_offsets, tn, n_out, extras, out_dtype, epi, vmem_mib, name):
    k = x_parts[0].shape[1]
    t = sum(p.shape[0] for p in x_parts)
    n_w = len(w_col_offsets)
    split = [parts for parts in [x_parts] + [e[0] for e in extras] if len(parts) > 1]
    tiles = _part_tiles(split[0], ROW_TILE) if split else (t // ROW_TILE,)
    assert all(_part_tiles(parts, ROW_TILE) == tiles for parts in split)
    in_specs = _part_specs(x_parts, ROW_TILE, k, lambda j, i: (i, 0))
    for off in w_col_offsets:
        in_specs.append(pl.BlockSpec((k, tn), lambda j, i, off=off: (0, j + off)))
    extra_arrays = []
    for parts, cols, colfn in extras:
        in_specs += _part_specs(parts, ROW_TILE, cols, lambda j, i, colfn=colfn: (i, colfn(j)))
        extra_arrays += list(parts)
    return pl.pallas_call(
        functools.partial(_mm_kernel, tiles=tiles, n_x=len(x_parts), n_w=n_w,
                          n_extra=tuple(len(e[0]) for e in extras), epi=epi),
        grid=(n_out // tn, t // ROW_TILE),
        in_specs=in_specs,
        out_specs=pl.BlockSpec((ROW_TILE, tn), lambda j, i: (i, j)),
        out_shape=jax.ShapeDtypeStruct((t, n_out), out_dtype),
        scratch_shapes=[pltpu.VMEM((k, tn), BF16) for _ in range(n_w)],
        compiler_params=_params(("arbitrary", "arbitrary"), vmem_mib),
        name=name,
    )(*x_parts, *([w] * n_w), *extra_arrays)


def _epi_plain(accs, extra, o_ref):
    o_ref[...] = accs[0].astype(o_ref.dtype)


def _epi_residual(accs, extra, o_ref):
    o_ref[...] = (extra[0][...] + accs[0]).astype(o_ref.dtype)


def _epi_glu_residual(accs, extra, o_ref):
    a, gt = accs
    o_ref[...] = (extra[0][...] + a * _sigmoid(gt)).astype(o_ref.dtype)


def _epi_rope(accs, extra, o_ref, *, n_qk_blocks):
    acc = accs[0]
    cos_ref, sin_ref = extra
    j = pl.program_id(0)
    half = RET_DK // 2

    @pl.when(j < n_qk_blocks)
    def _():
        scale = jnp.where(j >= n_qk_blocks // 2, RET_DK ** -0.5, 1.0).astype(F32)
        c = cos_ref[...]
        s = sin_ref[...]
        for hh in range(acc.shape[1] // RET_DK):
            lo = hh * RET_DK
            t1 = acc[:, lo:lo + half]
            t2 = acc[:, lo + half:lo + RET_DK]
            o_ref[:, lo:lo + half] = ((t1 * c - t2 * s) * scale).astype(o_ref.dtype)
            o_ref[:, lo + half:lo + RET_DK] = ((t1 * s + t2 * c) * scale).astype(o_ref.dtype)

    @pl.when(j >= n_qk_blocks)
    def _():
        o_ref[...] = acc.astype(o_ref.dtype)


def _ret_tables(rows, chunk):
    lg = jnp.log(jnp.asarray(1.0 - 2.0 ** (-5.0 - np.arange(RET_HEADS)), dtype=F32))[:, None, None]
    n = jnp.arange(rows, dtype=F32)
    ci = np.arange(rows) // chunk
    same_or_earlier = jnp.asarray(ci[None, :] <= ci[:, None])
    dmat = jnp.where(same_or_earlier[None], jnp.exp(jnp.abs(n[:, None] - n[None, :])[None] * lg), 0.0)
    dq = jnp.exp((n + 1.0)[None, :, None] * lg) * jnp.ones((1, 1, LANES), F32)
    dk = jnp.exp((rows - 1.0 - n)[None, :, None] * lg) * jnp.ones((1, 1, LANES), F32)
    gr = jnp.exp(rows * lg) * jnp.ones((1, 1, RET_DV), F32)
    return dmat, dq, dk, gr


def _ret_block(q, k, v, g, s_prev, dmat, dq, dk, gr, gn):
    s = lax.dot_general(q, k, (((1,), (1,)), ((), ())), preferred_element_type=F32) * dmat
    inter = jnp.dot(q, s_prev.astype(BF16), preferred_element_type=F32)
    o = jnp.dot(s.astype(BF16), v, preferred_element_type=F32) + jnp.concatenate([dq] * (RET_DV // LANES), axis=1) * inter
    kd = (k.astype(F32) * jnp.concatenate([dk] * (RET_DK // LANES), axis=1)).astype(BF16)
    s_new = s_prev * gr + lax.dot_general(kd, v, (((0,), (0,)), ((), ())), preferred_element_type=F32)
    mu = jnp.mean(o, axis=-1, keepdims=True)
    oc = o - mu
    var = jnp.mean(oc * oc, axis=-1, keepdims=True)
    on = oc * lax.rsqrt(var + EPS) * gn
    gf = g.astype(F32)
    return (gf * _sigmoid(gf) * on).astype(BF16), s_new


def _ret_prompt_kernel(q_ref, k_ref, v_ref, g_ref, dm_ref, dq_ref, dk_ref, gr_ref, gn_ref,
                       og_ref, st_ref, s_scr):
    @pl.when(pl.program_id(1) == 0)
    def _():
        s_scr[...] = jnp.zeros_like(s_scr)

    og, s_new = _ret_block(q_ref[...], k_ref[...], v_ref[...], g_ref[...], s_scr[...],
                           dm_ref[0], dq_ref[0], dk_ref[0], gr_ref[0], gn_ref[...])
    og_ref[...] = og
    s_scr[...] = s_new
    st_ref[0] = s_new


def _ret_prompt_call(proj, gn, t_prompt):
    dm, dq, dk, gr = _ret_tables(RET_ROWS, CHUNK)
    h = RET_HEADS
    kv_blk = 2 * D_MODEL // RET_DV
    g_blk = (2 * D_MODEL + RET_VW) // RET_DV
    return pl.pallas_call(
        _ret_prompt_kernel,
        grid=(h, t_prompt // RET_ROWS),
        in_specs=[
            pl.BlockSpec((RET_ROWS, RET_DK), lambda hh, c: (c, hh)),
            pl.BlockSpec((RET_ROWS, RET_DK), lambda hh, c: (c, h + hh)),
            pl.BlockSpec((RET_ROWS, RET_DV), lambda hh, c: (c, kv_blk + hh)),
            pl.BlockSpec((RET_ROWS, RET_DV), lambda hh, c: (c, g_blk + hh)),
            pl.BlockSpec((1, RET_ROWS, RET_ROWS), lambda hh, c: (hh, 0, 0)),
            pl.BlockSpec((1, RET_ROWS, LANES), lambda hh, c: (hh, 0, 0)),
            pl.BlockSpec((1, RET_ROWS, LANES), lambda hh, c: (hh, 0, 0)),
            pl.BlockSpec((1, 1, RET_DV), lambda hh, c: (hh, 0, 0)),
            pl.BlockSpec((1, RET_DV), lambda hh, c: (0, hh)),
        ],
        out_specs=[
            pl.BlockSpec((RET_ROWS, RET_DV), lambda hh, c: (c, hh)),
            pl.BlockSpec((1, RET_DK, RET_DV), lambda hh, c: (hh, 0, 0)),
        ],
        out_shape=[
            jax.ShapeDtypeStruct((t_prompt, RET_VW), BF16),
            jax.ShapeDtypeStruct((h, RET_DK, RET_DV), F32),
        ],
        scratch_shapes=[pltpu.VMEM((RET_DK, RET_DV), F32)],
        compiler_params=_params(("arbitrary", "arbitrary"), 32),
        name="retention_prompt",
    )(proj, proj, proj, proj, dm, dq, dk, gr, gn.reshape(1, RET_VW))


def _ret_sample_kernel(q_ref, k_ref, v_ref, g_ref, s0_ref, dm_ref, dq_ref, dk_ref, gr_ref, gn_ref,
                       og_ref, st_ref):
    for hh in range(RET_HEADS):
        ks = slice(hh * RET_DK, (hh + 1) * RET_DK)
        vs = slice(hh * RET_DV, (hh + 1) * RET_DV)
        og, s_new = _ret_block(q_ref[:, ks], k_ref[:, ks], v_ref[:, vs], g_ref[:, vs], s0_ref[0, hh],
                               dm_ref[hh], dq_ref[hh], dk_ref[hh], gr_ref[hh], gn_ref[:, vs])
        og_ref[:, vs] = og
        st_ref[0, hh] = s_new


def _ret_sample_call(proj, gn, state, t_prompt, dec_seq):
    dm, dq, dk, gr = _ret_tables(dec_seq, dec_seq)
    b = state.shape[0]
    r0 = t_prompt // dec_seq
    const3 = lambda bb: (0, 0, 0)
    return pl.pallas_call(
        _ret_sample_kernel,
        grid=(b,),
        in_specs=[
            pl.BlockSpec((dec_seq, D_MODEL), lambda bb: (r0 + bb, 0)),
            pl.BlockSpec((dec_seq, D_MODEL), lambda bb: (r0 + bb, 1)),
            pl.BlockSpec((dec_seq, RET_VW), lambda bb: (r0 + bb, 1)),
            pl.BlockSpec((dec_seq, RET_VW), lambda bb: (r0 + bb, 2)),
            pl.BlockSpec((1, RET_HEADS, RET_DK, RET_DV), lambda bb: (bb, 0, 0, 0)),
            pl.BlockSpec(dm.shape, const3),
            pl.BlockSpec(dq.shape, const3),
            pl.BlockSpec(dk.shape, const3),
            pl.BlockSpec(gr.shape, const3),
            pl.BlockSpec((1, RET_VW), lambda bb: (0, 0)),
        ],
        out_specs=[
            pl.BlockSpec((dec_seq, RET_VW), lambda bb: (bb, 0)),
            pl.BlockSpec((1, RET_HEADS, RET_DK, RET_DV), lambda bb: (bb, 0, 0, 0)),
        ],
        out_shape=[
            jax.ShapeDtypeStruct((b * dec_seq, RET_VW), BF16),
            jax.ShapeDtypeStruct(state.shape, F32),
        ],
        compiler_params=_params(("arbitrary",), 40),
        name="retention_sample",
    )(proj, proj, proj, proj, state, dm, dq, dk, gr, gn.reshape(1, RET_VW))


def _router_kernel(x_ref, g_ref, wr_ref, br_ref, xn_ref, id_ref, w_ref):
    xn = _rms(x_ref[...], g_ref[...])
    xn_ref[...] = xn
    xh = xn.astype(BF16)
    xl = (xn - xh.astype(F32)).astype(BF16)
    w = wr_ref[...]
    wh = w.astype(BF16)
    wl = (w - wh.astype(F32)).astype(BF16)
    lg = (jnp.dot(xh, wh, preferred_element_type=F32) + jnp.dot(xh, wl, preferred_element_type=F32)
          + jnp.dot(xl, wh, preferred_element_type=F32)) + br_ref[...]
    lane = lax.broadcasted_iota(jnp.int32, lg.shape, 1).astype(F32)
    neg = jnp.float32(-jnp.inf)
    big = jnp.float32(LANES)
    gl = jnp.where(lane < MOE_GROUPS, lg, neg)
    gmax = jnp.max(gl, axis=-1, keepdims=True)
    gsel = jnp.min(jnp.where(gl == gmax, lane, big), axis=-1, keepdims=True)
    gw = 1.0 / jnp.sum(jnp.exp(gl - gmax), axis=-1, keepdims=True)
    lo = MOE_GROUPS + gsel * MOE_PER_GROUP
    el = jnp.where((lane >= lo) & (lane < lo + MOE_PER_GROUP), lg, neg)
    m1 = jnp.max(el, axis=-1, keepdims=True)
    i1 = jnp.min(jnp.where(el == m1, lane, big), axis=-1, keepdims=True)
    el2 = jnp.where(lane == i1, neg, el)
    m2 = jnp.max(el2, axis=-1, keepdims=True)
    i2 = jnp.min(jnp.where(el2 == m2, lane, big), axis=-1, keepdims=True)
    z = jnp.sum(jnp.exp(el - m1), axis=-1, keepdims=True)
    p1 = 1.0 / z
    p2 = jnp.exp(m2 - m1) / z
    tot = p1 + p2
    w1 = p1 / tot * gw
    w2 = p2 / tot * gw
    ids = jnp.where(lane == 0, i1 - MOE_GROUPS, jnp.where(lane == 1, i2 - MOE_GROUPS, 0.0))
    id_ref[...] = ids.astype(jnp.int32)
    w_ref[...] = jnp.where(lane == 0, w1, jnp.where(lane == 1, w2, 0.0))


def _router_call(x, g, w_grp, b_grp, w_exp, b_exp):
    t, d = x.shape
    pad = LANES - MOE_GROUPS - N_EXP
    wr = jnp.concatenate([w_grp, w_exp, jnp.zeros((d, pad), F32)], axis=1)
    br = jnp.concatenate([b_grp, b_exp, jnp.zeros((pad,), F32)]).reshape(1, LANES)
    row = lambda i: (i, 0)
    fixed = lambda i: (0, 0)
    return pl.pallas_call(
        _router_kernel,
        grid=(t // ROW_TILE,),
        in_specs=[pl.BlockSpec((ROW_TILE, d), row), pl.BlockSpec((1, d), fixed),
                  pl.BlockSpec((d, LANES), fixed), pl.BlockSpec((1, LANES), fixed)],
        out_specs=[pl.BlockSpec((ROW_TILE, d), row), pl.BlockSpec((ROW_TILE, LANES), row),
                   pl.BlockSpec((ROW_TILE, LANES), row)],
        out_shape=[jax.ShapeDtypeStruct((t, d), F32), jax.ShapeDtypeStruct((t, LANES), jnp.int32),
                   jax.ShapeDtypeStruct((t, LANES), F32)],
        compiler_params=_params(("arbitrary",), 40),
        name="moe_router",
    )(x, g.reshape(1, d), wr, br)


def _row_tokens_kernel(pos_ref, tok_ref):
    def clear(r, carry):
        tok_ref[r] = 0
        return carry
    lax.fori_loop(0, tok_ref.shape[0], clear, 0, unroll=8)

    def place(a, carry):
        tok_ref[pos_ref[a]] = lax.shift_right_logical(a, 1)
        return carry
    lax.fori_loop(0, pos_ref.shape[0], place, 0, unroll=8)


def _row_tokens_call(pos, rows):
    smem = pl.BlockSpec(memory_space=pltpu.SMEM)
    return pl.pallas_call(
        _row_tokens_kernel,
        in_specs=[smem],
        out_specs=smem,
        out_shape=jax.ShapeDtypeStruct((rows,), jnp.int32),
        name="moe_row_tokens",
    )(pos)


def _moe_plan(eid):
    t = eid.shape[0]
    a = 2 * t
    n_tiles = a // MOE_TILE + N_EXP
    e_flat = eid.reshape(a)
    onehot = (e_flat[:, None] == jnp.arange(N_EXP, dtype=jnp.int32)[None, :]).astype(jnp.int32)
    csum = jnp.cumsum(onehot, axis=0)
    counts = csum[-1]
    rank = jnp.take_along_axis(csum, e_flat[:, None], axis=1)[:, 0] - 1
    tiles_e = (counts + MOE_TILE - 1) // MOE_TILE
    tiles_end = jnp.cumsum(tiles_e)
    row_start = (tiles_end - tiles_e) * MOE_TILE
    pos = row_start[e_flat] + rank
    n_valid = tiles_end[-1]
    tile_id = jnp.minimum(jnp.arange(n_tiles, dtype=jnp.int32), n_valid - 1)
    tile_expert = jnp.sum((tile_id[:, None] >= tiles_end[None, :]).astype(jnp.int32), axis=1)
    tile_expert = jnp.minimum(tile_expert, N_EXP - 1)
    changed = jnp.concatenate([jnp.ones((1,), bool), tile_expert[1:] != tile_expert[:-1]])
    first = jnp.logical_and(changed, jnp.arange(n_tiles) < n_valid).astype(jnp.int32)
    wslot = (jnp.cumsum(first) - 1) % 2
    next_start = tiles_end[tile_expert]
    next_expert = jnp.where(next_start < n_valid, tile_expert[jnp.minimum(next_start, n_tiles - 1)], -1)
    tile_meta = jnp.stack([tile_expert, first, wslot, next_expert]).astype(jnp.int32)
    pos = pos.astype(jnp.int32)
    tok_of_row = _row_tokens_call(pos, n_tiles * MOE_TILE)
    return tile_meta, n_valid.reshape(1).astype(jnp.int32), tok_of_row, pos


def _moe_ffn_kernel(tm_ref, nv_ref, tok_ref, x_hbm, wgu_hbm, wd_hbm, o_ref,
                    xbuf, sems, wgu_f, wd_f, wsems, wgu_b, wd_b, *, layer):
    i = pl.program_id(0)
    nv = nv_ref[0]
    slot = lax.rem(i, 2)
    expert, first, wslot, next_expert = tm_ref[0, i], tm_ref[1, i], tm_ref[2, i], tm_ref[3, i]

    def weight_copies(e, s):
        return (pltpu.make_async_copy(wgu_hbm.at[layer, e], wgu_f.at[s], wsems.at[0, s]),
                pltpu.make_async_copy(wd_hbm.at[layer, e], wd_f.at[s], wsems.at[1, s]))
    ff_chunk = 256
    n_chunk = D_FF_E // ff_chunk
    row_groups = np.array_split(np.arange(MOE_TILE), 3 * n_chunk)

    def row_copy(row0, r, dst_slot):
        return pltpu.make_async_copy(x_hbm.at[pl.ds(tok_ref[row0 + r], 1)], xbuf.at[dst_slot, pl.ds(r, 1)],
                                     sems.at[dst_slot])

    def wait_tile(s):
        pltpu.make_async_copy(x_hbm.at[pl.ds(0, MOE_TILE)], xbuf.at[s], sems.at[s]).wait()

    @pl.when(i == 0)
    def _():
        for c in weight_copies(expert, 0):
            c.start()
        for r in range(MOE_TILE):
            row_copy(0, r, 0).start()

    @pl.when(first == 1)
    def _():
        for c in weight_copies(expert, wslot):
            c.wait()
        wgu_b[...] = wgu_f[wslot].astype(BF16)
        wd_b[...] = wd_f[wslot].astype(BF16)

        @pl.when(next_expert >= 0)
        def _():
            for c in weight_copies(next_expert, 1 - wslot):
                c.start()

    def compute(s):
        wait_tile(s)
        x = xbuf[s].astype(BF16)
        nxt = (i + 1) * MOE_TILE
        groups = iter(row_groups)

        def request_rows():
            for r in next(groups):
                row_copy(nxt, int(r), 1 - s).start()

        for c in range(n_chunk):
            lo = c * ff_chunk
            hg = jnp.dot(x, wgu_b[:, lo:lo + ff_chunk], preferred_element_type=F32)
            request_rows()
            hu = jnp.dot(x, wgu_b[:, D_FF_E + lo:D_FF_E + lo + ff_chunk], preferred_element_type=F32)
            request_rows()
            act = (hg * _sigmoid(hg) * hu).astype(BF16)
            y = jnp.dot(act, wd_b[lo:lo + ff_chunk, :], preferred_element_type=F32)
            if c == 0:
                o_ref[...] = y
            else:
                o_ref[...] += y
            request_rows()

    for s in range(2):
        pl.when(jnp.logical_and(i < nv, slot == s))(functools.partial(compute, s))

    @pl.when(i == nv)
    def _():
        wait_tile(slot)

    @pl.when(i >= nv)
    def _():
        o_ref[...] = jnp.zeros_like(o_ref)


def _moe_ffn_call(xn, w_gu, w_down, layer, tile_meta, n_valid, tok_of_row):
    t, d = xn.shape
    n_tiles = tile_meta.shape[1]
    rows = n_tiles * MOE_TILE
    grid_spec = pltpu.PrefetchScalarGridSpec(
        num_scalar_prefetch=3,
        grid=(n_tiles,),
        in_specs=[pl.BlockSpec(memory_space=pl.ANY)] * 3,
        out_specs=pl.BlockSpec((MOE_TILE, d), lambda i, tm, nv, tok: (i, 0)),
        scratch_shapes=[
            pltpu.VMEM((2, MOE_TILE, d), F32),
            pltpu.SemaphoreType.DMA((2,)),
            pltpu.VMEM((2, d, 2 * D_FF_E), F32),
            pltpu.VMEM((2, D_FF_E, d), F32),
            pltpu.SemaphoreType.DMA((2, 2)),
            pltpu.VMEM((d, 2 * D_FF_E), BF16),
            pltpu.VMEM((D_FF_E, d), BF16),
        ],
    )
    return pl.pallas_call(
        functools.partial(_moe_ffn_kernel, layer=layer),
        grid_spec=grid_spec,
        out_shape=jax.ShapeDtypeStruct((rows, d), F32),
        compiler_params=_params(("arbitrary",), 48),
        name="moe_experts",
    )(tile_meta, n_valid, tok_of_row, xn, w_gu, w_down)


def _combine_kernel(pos_ref, x_ref, w_ref, y_hbm, g_ref, *rest, emit_x, norm_tiles):
    outs = rest[:int(emit_x) + len(norm_tiles)]
    buf, sems = rest[int(emit_x) + len(norm_tiles):]
    i = pl.program_id(0)
    n = pl.num_programs(0)
    slot = lax.rem(i, 2)
    tm = x_ref.shape[0]

    def gather(tile, dst_slot):
        def body(r, carry):
            for kk in range(2):
                p = pos_ref[(tile * tm + r) * 2 + kk]
                pltpu.make_async_copy(y_hbm.at[pl.ds(p, 1)], buf.at[dst_slot, kk, pl.ds(r, 1)],
                                      sems.at[dst_slot]).start()
            return carry
        lax.fori_loop(0, tm, body, 0, unroll=8)

    @pl.when(i == 0)
    def _():
        gather(0, 0)

    @pl.when(i + 1 < n)
    def _():
        gather(i + 1, 1 - slot)

    for kk in range(2):
        pltpu.make_async_copy(y_hbm.at[pl.ds(0, tm)], buf.at[slot, kk], sems.at[slot]).wait()
    xo = x_ref[...] + (w_ref[:, 0:1] * buf[slot, 0] + w_ref[:, 1:2] * buf[slot, 1])
    norm_refs = outs[int(emit_x):]
    if emit_x:
        outs[0][...] = xo

    def write_norm(p):
        norm_refs[p][...] = _rms(xo, g_ref[...]).astype(norm_refs[p].dtype)

    _for_each_part(i, norm_tiles, write_norm)


def _combine_call(x, ws, y_rows, pos, g, emit_x, norm_dtype, norm_rows):
    t, d = x.shape
    tm = MOE_TILE
    row = lambda i, pos_ref: (i, 0)
    out_specs, out_shape = [], []
    if emit_x:
        out_specs.append(pl.BlockSpec((tm, d), row))
        out_shape.append(jax.ShapeDtypeStruct((t, d), F32))
    norm_parts = [jax.ShapeDtypeStruct((r, d), norm_dtype) for r in norm_rows]
    out_specs += _part_specs(norm_parts, tm, d, row)
    out_shape += norm_parts
    grid_spec = pltpu.PrefetchScalarGridSpec(
        num_scalar_prefetch=1,
        grid=(t // tm,),
        in_specs=[pl.BlockSpec((tm, d), row), pl.BlockSpec((tm, LANES), row), pl.BlockSpec(memory_space=pl.ANY),
                  pl.BlockSpec((1, d), lambda i, pos_ref: (0, 0))],
        out_specs=out_specs,
        scratch_shapes=[pltpu.VMEM((2, 2, tm, d), F32), pltpu.SemaphoreType.DMA((2,))],
    )
    return pl.pallas_call(
        functools.partial(_combine_kernel, emit_x=emit_x, norm_tiles=_part_tiles(norm_parts, tm)),
        grid_spec=grid_spec,
        out_shape=out_shape,
        compiler_params=_params(("arbitrary",), 40),
        name="moe_combine",
    )(pos, x, ws, y_rows, g.reshape(1, d))


def _moe_layer(x, g_ffn, w_grp, b_grp, w_exp, b_exp, w_gu, w_down, layer, g_next, emit_x, norm_dtype, norm_rows):
    xn, ids, ws = _router_call(x, g_ffn, w_grp, b_grp, w_exp, b_exp)
    tile_meta, n_valid, tok_of_row, pos = _moe_plan(ids[:, :2])
    y_rows = _moe_ffn_call(xn, w_gu, w_down, layer, tile_meta, n_valid, tok_of_row)
    return _combine_call(x, ws, y_rows, pos, g_next, emit_x, norm_dtype, norm_rows)


def _s5_prep_kernel(are_ref, aim_ref, ldt_ref, bre_ref, bim_ref, lre_ref, lim_ref, bbr_ref, bbi_ref):
    are = are_ref[...]
    aim = aim_ref[...]
    dt = jnp.exp(ldt_ref[...])
    mag = jnp.exp(are * dt)
    lr = mag * jnp.cos(aim * dt)
    li = mag * jnp.sin(aim * dt)
    lre_ref[...] = lr
    lim_ref[...] = li
    den = are * are + aim * aim
    nr = lr - 1.0
    cr = (nr * are + li * aim) / den
    ci = (li * are - nr * aim) / den
    bre = bre_ref[...]
    bim = bim_ref[...]
    bbr_ref[...] = cr * bre - ci * bim
    bbi_ref[...] = cr * bim + ci * bre


def _s5_prep_call(a_re, a_im, log_dt, b_re, b_im):
    g, p, c = b_re.shape
    shp = (g, p * c)
    expand = lambda a: jnp.broadcast_to(a[:, :, None], (g, p, c)).reshape(shp)
    ldt = jnp.broadcast_to(log_dt[:, None], shp)
    outs = pl.pallas_call(
        _s5_prep_kernel,
        out_shape=[jax.ShapeDtypeStruct(shp, F32)] * 4,
        name="s5_discretize",
    )(expand(a_re), expand(a_im), ldt, b_re.reshape(shp), b_im.reshape(shp))
    lre, lim, bbr, bbi = outs
    return lre[:, ::c], lim[:, ::c], bbr, bbi


def _s5_mats(bbr, bbi, c_re, c_im):
    eye = jnp.eye(16, dtype=F32)

    def bd_part(bb):
        xx = bb.reshape(S5_FB, 16, S5_P, S5_GC).transpose(0, 1, 3, 2)
        yy = xx[:, :, :, None, :] * eye[None, :, None, :, None]
        return yy.reshape(S5_FB, 16 * S5_GC, 16 * S5_P)

    def cd_part(cc):
        xx = cc.reshape(S5_FB, 16, S5_GC, S5_P).transpose(0, 1, 3, 2)
        yy = xx[:, :, :, None, :] * eye[None, :, None, :, None]
        return yy.reshape(S5_FB, 16 * S5_P, 16 * S5_GC)

    bd = jnp.concatenate([bd_part(bbr), bd_part(bbi)], axis=-1).astype(BF16)
    cd = jnp.concatenate([cd_part(c_re), -cd_part(c_im)], axis=1).astype(BF16)
    return bd, cd


def _s5_kernel(u_ref, bd_ref, cd_ref, lre_ref, lim_ref, d_ref, h0r_ref, h0i_ref,
               z_ref, hr_ref, hi_ref, scr, hst, *, seg, carry):
    rows = u_ref.shape[0]
    n_seq = rows // seg
    ch = 16 * S5_GC
    half = SUBLANES * S5_PITCH
    blk = 2 * half

    for fb in range(S5_FB):
        bu = jnp.dot(u_ref[:, fb * ch:(fb + 1) * ch].astype(BF16), bd_ref[fb], preferred_element_type=F32)
        for c in range(2 * SUBLANES):
            r0 = fb * blk + c * S5_PITCH
            scr[r0:r0 + rows, :] = bu[:, c * LANES:(c + 1) * LANES]

    lre = [lre_ref[fb * SUBLANES:(fb + 1) * SUBLANES, :] for fb in range(S5_FB)]
    lim = [lim_ref[fb * SUBLANES:(fb + 1) * SUBLANES, :] for fb in range(S5_FB)]

    def scan(row0, state):
        def step(l, st):
            new = []
            for fb in range(S5_FB):
                hr, hi = st[2 * fb], st[2 * fb + 1]
                ire = pl.ds(fb * blk + row0 + l, SUBLANES, stride=S5_PITCH)
                iim = pl.ds(fb * blk + half + row0 + l, SUBLANES, stride=S5_PITCH)
                nr = lre[fb] * hr - lim[fb] * hi + scr[ire, :]
                ni = lre[fb] * hi + lim[fb] * hr + scr[iim, :]
                scr[ire, :] = nr
                scr[iim, :] = ni
                new += [nr, ni]
            return tuple(new)
        return lax.fori_loop(0, seg, step, state)

    def load_state(rref, iref, lead):
        st = []
        for fb in range(S5_FB):
            st += [rref[lead + (slice(fb * SUBLANES, (fb + 1) * SUBLANES), slice(None))],
                   iref[lead + (slice(fb * SUBLANES, (fb + 1) * SUBLANES), slice(None))]]
        return tuple(st)

    def store_state(st, rref, iref, lead):
        for fb in range(S5_FB):
            rref[lead + (slice(fb * SUBLANES, (fb + 1) * SUBLANES), slice(None))] = st[2 * fb]
            iref[lead + (slice(fb * SUBLANES, (fb + 1) * SUBLANES), slice(None))] = st[2 * fb + 1]

    if carry:
        @pl.when(pl.program_id(0) == 0)
        def _():
            hst[0] = h0r_ref[0]
            hst[1] = h0i_ref[0]
        hre_scr = hst.at[0]
        him_scr = hst.at[1]
        st = scan(0, load_state(hre_scr, him_scr, ()))
        store_state(st, hre_scr, him_scr, ())
        store_state(st, hr_ref, hi_ref, (0,))
    else:
        for s in range(n_seq):
            st = scan(s * seg, load_state(h0r_ref, h0i_ref, (s,)))
            store_state(st, hr_ref, hi_ref, (s,))

    for fb in range(S5_FB):
        hf = jnp.concatenate(
            [scr[fb * blk + c * S5_PITCH: fb * blk + c * S5_PITCH + rows, :] for c in range(2 * SUBLANES)],
            axis=1).astype(BF16)
        y = jnp.dot(hf, cd_ref[fb], preferred_element_type=F32)
        cs = slice(fb * ch, (fb + 1) * ch)
        yy = y + d_ref[:, cs] * u_ref[:, cs]
        z_ref[:, cs] = jax.nn.gelu(yy).astype(z_ref.dtype)


def _s5_call(u, bd, cd, lre, lim, d_skip, h0r, h0i, row_block0, n_blocks, seg, carry):
    t, d = u.shape
    n_seq = S5_ROWS // seg
    ns_total = h0r.shape[0]
    st_idx = (lambda i: (0, 0, 0)) if carry else (lambda i: (i, 0, 0))
    fixed2 = lambda i: (0, 0)
    fixed3 = lambda i: (0, 0, 0)
    st_rows = S5_FB * SUBLANES
    return pl.pallas_call(
        functools.partial(_s5_kernel, seg=seg, carry=carry),
        grid=(n_blocks,),
        in_specs=[
            pl.BlockSpec((S5_ROWS, d), lambda i: (row_block0 + i, 0)),
            pl.BlockSpec(bd.shape, fixed3),
            pl.BlockSpec(cd.shape, fixed3),
            pl.BlockSpec(lre.shape, fixed2),
            pl.BlockSpec(lim.shape, fixed2),
            pl.BlockSpec((1, d), fixed2),
            pl.BlockSpec((n_seq, st_rows, LANES), st_idx),
            pl.BlockSpec((n_seq, st_rows, LANES), st_idx),
        ],
        out_specs=[
            pl.BlockSpec((S5_ROWS, d), lambda i: (i, 0)),
            pl.BlockSpec((n_seq, st_rows, LANES), st_idx),
            pl.BlockSpec((n_seq, st_rows, LANES), st_idx),
        ],
        out_shape=[
            jax.ShapeDtypeStruct((n_blocks * S5_ROWS, d), BF16),
            jax.ShapeDtypeStruct((ns_total, st_rows, LANES), F32),
            jax.ShapeDtypeStruct((ns_total, st_rows, LANES), F32),
        ],
        scratch_shapes=[
            pltpu.VMEM((2 * SUBLANES * S5_FB * S5_PITCH, LANES), F32),
            pltpu.VMEM((2, st_rows, LANES), F32),
        ],
        compiler_params=_params(("arbitrary",), 56),
        name="s5_scan",
    )(u, bd, cd, lre, lim, d_skip.reshape(1, d), h0r, h0i)


def kernel(x_prompt, x_sample, state_ret, state_s5_re, state_s5_im, norm_mix_g, norm_ffn_g, norm_final_g,
           ret_w_in, ret_gn_g, ret_w_out, s5_w_in, s5_a_re, s5_a_im, s5_log_dt, s5_b_re, s5_b_im,
           s5_c_re, s5_c_im, s5_d, s5_w_out, moe_w_grp, moe_b_grp, moe_w_exp, moe_b_exp, moe_w_gu, moe_w_down):
    bp, seq, d = x_prompt.shape
    db, dseq, _ = x_sample.shape
    assert d == D_MODEL and bp == 1
    assert seq % max(ROW_TILE, RET_ROWS) == 0 and seq % CHUNK == 0
    assert dseq < CHUNK and dseq % 16 == 0 and S5_ROWS % dseq == 0
    tp = bp * seq
    ts = db * dseq
    t = tp + ts
    assert ts % ROW_TILE == 0 and ts % S5_ROWS == 0

    x_parts = [x_prompt.reshape(tp, d), x_sample.reshape(ts, d)]

    half = RET_DK // 2
    freqs = ROPE_BASE ** (-jnp.arange(half, dtype=F32) / half)
    pos = jnp.concatenate([jnp.arange(seq, dtype=F32), jnp.tile(PAST_LEN + jnp.arange(dseq, dtype=F32), db)])
    ang = pos[:, None] * freqs[None, :]
    cos_t = jnp.cos(ang)
    sin_t = jnp.sin(ang)

    xn = _norm_call(x_parts, norm_mix_g[0], BF16)
    n_proj = 2 * D_MODEL + 2 * RET_VW
    tn = 1024
    first_col = lambda j: 0
    proj = _mm_call([xn], ret_w_in[0], [0], tn, n_proj, [([cos_t], half, first_col), ([sin_t], half, first_col)],
                    BF16, functools.partial(_epi_rope, n_qk_blocks=2 * D_MODEL // tn), 40, "ret_in_proj")
    og_p, ret_p = _ret_prompt_call(proj, ret_gn_g[0], tp)
    og_s, ret_s = _ret_sample_call(proj, ret_gn_g[0], state_ret[0], tp, dseq)
    tn = 512
    same_col = lambda j: j
    x1 = _mm_call([og_p, og_s], ret_w_out[0], [0], tn, d, [(x_parts, tn, same_col)], F32, _epi_residual,
                  40, "ret_out_proj")

    x2, xn = _moe_layer(x1, norm_ffn_g[0], moe_w_grp[0], moe_b_grp[0], moe_w_exp[0], moe_b_exp[0],
                        moe_w_gu, moe_w_down, 0, norm_mix_g[1], True, BF16, (t,))

    u = _mm_call([xn], s5_w_in[0], [0], 1024, d, [], F32, _epi_plain, 40, "s5_in_proj")
    lre, lim, bbr, bbi = _s5_prep_call(s5_a_re[0], s5_a_im[0], s5_log_dt[0], s5_b_re[0], s5_b_im[0])
    bd, cd = _s5_mats(bbr, bbi, s5_c_re[0], s5_c_im[0])
    st_shape = (S5_FB * SUBLANES, LANES)
    lre = lre.reshape(st_shape)
    lim = lim.reshape(st_shape)
    zero_state = jnp.zeros((1,) + st_shape, F32)
    z_p, hpr, hpi = _s5_call(u, bd, cd, lre, lim, s5_d[0], zero_state, zero_state,
                             0, tp // S5_ROWS, S5_ROWS, True)
    z_s, hsr, hsi = _s5_call(u, bd, cd, lre, lim, s5_d[0],
                             state_s5_re[0].reshape((db,) + st_shape), state_s5_im[0].reshape((db,) + st_shape),
                             tp // S5_ROWS, ts // S5_ROWS, dseq, False)
    x3 = _mm_call([z_p, z_s], s5_w_out[0], [0, d // tn], tn, d, [([x2], tn, same_col)], F32, _epi_glu_residual,
                  40, "s5_out_proj")

    y_p, y_s = _moe_layer(x3, norm_ffn_g[1], moe_w_grp[1], moe_b_grp[1], moe_w_exp[1], moe_b_exp[1],
                          moe_w_gu, moe_w_down, 1, norm_final_g, False, F32, (tp, ts))

    s5_shape = (1, -1, S5_GROUPS, S5_P)
    return (y_p.reshape(bp, seq, d), y_s.reshape(db, dseq, d),
            ret_p.reshape(1, bp, RET_HEADS, RET_DK, RET_DV), ret_s.reshape((1,) + ret_s.shape),
            hpr.reshape(s5_shape), hpi.reshape(s5_shape), hsr.reshape(s5_shape), hsi.reshape(s5_shape))
```

```python
import functools
import math

import numpy as np
import jax
import jax.numpy as jnp
from jax import lax
from jax.experimental import pallas as pl
from jax.experimental.pallas import tpu as pltpu

F32 = jnp.float32
BF16 = jnp.bfloat16

D_MODEL = 2048
PAST_LEN = 2048
CHUNK = 64
RET_HEADS = 8
RET_DK = D_MODEL // RET_HEADS
RET_DV = 2 * RET_DK
RET_VW = RET_HEADS * RET_DV
ROPE_BASE = 10000.0
S5_GC = 16
S5_GROUPS = D_MODEL // S5_GC
S5_P = 64
MOE_GROUPS = 4
MOE_PER_GROUP = 8
N_EXP = MOE_GROUPS * MOE_PER_GROUP
D_FF_E = 512
EPS = 1e-6

LANES = 128
SUBLANES = 8
ROW_TILE = 512
RET_ROWS = 256
MOE_TILE = 256
S5_ROWS = 128
S5_FB = S5_GROUPS // 16
S5_PITCH = S5_ROWS + 8
MIB = 1024 * 1024


def _params(sem, vmem_mib):
    return pltpu.CompilerParams(dimension_semantics=sem, vmem_limit_bytes=vmem_mib * MIB)


def _rms(x, g):
    ms = jnp.mean(x * x, axis=-1, keepdims=True)
    return x * lax.rsqrt(ms + EPS) * g


def _sigmoid(x):
    return 1.0 / (1.0 + jnp.exp(-x))


def _part_specs(parts, rows, cols, rowcol):
    specs = []
    start = 0
    for p in parts:
        nt = p.shape[0] // rows

        def imap(*g, start=start, nt=nt):
            i, jc = rowcol(*g)
            return (jnp.clip(i - start, 0, nt - 1), jc)

        specs.append(pl.BlockSpec((rows, cols), imap))
        start += nt
    return specs


def _part_tiles(parts, rows):
    return tuple(p.shape[0] // rows for p in parts)


def _for_each_part(i, tiles, fn):
    if len(tiles) == 1:
        fn(0)
        return
    start = 0
    for p, nt in enumerate(tiles):
        pl.when(jnp.logical_and(i >= start, i < start + nt))(functools.partial(fn, p))
        start += nt


def _pick(refs, p):
    return refs[min(p, len(refs) - 1)]


def _norm_kernel(*refs, tiles):
    n = len(tiles)
    x_refs, g_ref, o_ref = refs[:n], refs[n], refs[n + 1]

    def run(p):
        o_ref[...] = _rms(x_refs[p][...], g_ref[...]).astype(o_ref.dtype)

    _for_each_part(pl.program_id(0), tiles, run)


def _norm_call(x_parts, g, out_dtype):
    d = x_parts[0].shape[1]
    t = sum(p.shape[0] for p in x_parts)
    return pl.pallas_call(
        functools.partial(_norm_kernel, tiles=_part_tiles(x_parts, ROW_TILE)),
        grid=(t // ROW_TILE,),
        in_specs=_part_specs(x_parts, ROW_TILE, d, lambda i: (i, 0)) + [pl.BlockSpec((1, d), lambda i: (0, 0))],
        out_specs=pl.BlockSpec((ROW_TILE, d), lambda i: (i, 0)),
        out_shape=jax.ShapeDtypeStruct((t, d), out_dtype),
        compiler_params=_params(("arbitrary",), 32),
        name="rmsnorm",
    )(*x_parts, g.reshape(1, d))


def _mm_kernel(*refs, tiles, n_x, n_w, n_extra, epi):
    x_refs = refs[:n_x]
    w_refs = refs[n_x:n_x + n_w]
    pos = n_x + n_w
    extra = []
    for n in n_extra:
        extra.append(refs[pos:pos + n])
        pos += n
    o_ref = refs[pos]
    wb = refs[pos + 1:]

    @pl.when(pl.program_id(1) == 0)
    def _():
        for w_ref, b in zip(w_refs, wb):
            b[...] = w_ref[...].astype(BF16)

    def run(p):
        x = _pick(x_refs, p)[...]
        accs = [jnp.dot(x, b[...], preferred_element_type=F32) for b in wb]
        epi(accs, [_pick(e, p) for e in extra], o_ref)

    _for_each_part(pl.program_id(1), tiles, run)


def _mm_call(x_parts, w, w_col_offsets, tn, n_out, extras, out_dtype, epi, vmem_mib, name):
    k = x_parts[0].shape[1]
    t = sum(p.shape[0] for p in x_parts)
    n_w = len(w_col_offsets)
    split = [parts for parts in [x_parts] + [e[0] for e in extras] if len(parts) > 1]
    tiles = _part_tiles(split[0], ROW_TILE) if split else (t // ROW_TILE,)
    assert all(_part_tiles(parts, ROW_TILE) == tiles for parts in split)
    in_specs = _part_specs(x_parts, ROW_TILE, k, lambda j, i: (i, 0))
    for off in w_col_offsets:
        in_specs.append(pl.BlockSpec((k, tn), lambda j, i, off=off: (0, j + off)))
    extra_arrays = []
    for parts, cols, colfn in extras:
        in_specs += _part_specs(parts, ROW_TILE, cols, lambda j, i, colfn=colfn: (i, colfn(j)))
        extra_arrays += list(parts)
    return pl.pallas_call(
        functools.partial(_mm_kernel, tiles=tiles, n_x=len(x_parts), n_w=n_w,
                          n_extra=tuple(len(e[0]) for e in extras), epi=epi),
        grid=(n_out // tn, t // ROW_TILE),
        in_specs=in_specs,
        out_specs=pl.BlockSpec((ROW_TILE, tn), lambda j, i: (i, j)),
        out_shape=jax.ShapeDtypeStruct((t, n_out), out_dtype),
        scratch_shapes=[pltpu.VMEM((k, tn), BF16) for _ in range(n_w)],
        compiler_params=_params(("arbitrary", "arbitrary"), vmem_mib),
        name=name,
    )(*x_parts, *([w] * n_w), *extra_arrays)


def _epi_plain(accs, extra, o_ref):
    o_ref[...] = accs[0].astype(o_ref.dtype)


def _epi_residual(accs, extra, o_ref):
    o_ref[...] = (extra[0][...] + accs[0]).astype(o_ref.dtype)


def _epi_glu_residual(accs, extra, o_ref):
    a, gt = accs
    o_ref[...] = (extra[0][...] + a * _sigmoid(gt)).astype(o_ref.dtype)


def _epi_rope(accs, extra, o_ref, *, n_qk_blocks):
    acc = accs[0]
    cos_ref, sin_ref = extra
    j = pl.program_id(0)
    half = RET_DK // 2

    @pl.when(j < n_qk_blocks)
    def _():
        scale = jnp.where(j >= n_qk_blocks // 2, RET_DK ** -0.5, 1.0).astype(F32)
        c = cos_ref[...]
        s = sin_ref[...]
        for hh in range(acc.shape[1] // RET_DK):
            lo = hh * RET_DK
            t1 = acc[:, lo:lo + half]
            t2 = acc[:, lo + half:lo + RET_DK]
            o_ref[:, lo:lo + half] = ((t1 * c - t2 * s) * scale).astype(o_ref.dtype)
            o_ref[:, lo + half:lo + RET_DK] = ((t1 * s + t2 * c) * scale).astype(o_ref.dtype)

    @pl.when(j >= n_qk_blocks)
    def _():
        o_ref[...] = acc.astype(o_ref.dtype)


def _ret_tables(rows, chunk):
    lg = jnp.log(jnp.asarray(1.0 - 2.0 ** (-5.0 - np.arange(RET_HEADS)), dtype=F32))[:, None, None]
    n = jnp.arange(rows, dtype=F32)
    ci = np.arange(rows) // chunk
    same_or_earlier = jnp.asarray(ci[None, :] <= ci[:, None])
    dmat = jnp.where(same_or_earlier[None], jnp.exp(jnp.abs(n[:, None] - n[None, :])[None] * lg), 0.0)
    dq = jnp.exp((n + 1.0)[None, :, None] * lg) * jnp.ones((1, 1, LANES), F32)
    dk = jnp.exp((rows - 1.0 - n)[None, :, None] * lg) * jnp.ones((1, 1, LANES), F32)
    gr = jnp.exp(rows * lg) * jnp.ones((1, 1, RET_DV), F32)
    return dmat, dq, dk, gr


def _ret_block(q, k, v, g, s_prev, dmat, dq, dk, gr, gn):
    s = lax.dot_general(q, k, (((1,), (1,)), ((), ())), preferred_element_type=F32) * dmat
    inter = jnp.dot(q, s_prev.astype(BF16), preferred_element_type=F32)
    o = jnp.dot(s.astype(BF16), v, preferred_element_type=F32) + jnp.concatenate([dq] * (RET_DV // LANES), axis=1) * inter
    kd = (k.astype(F32) * jnp.concatenate([dk] * (RET_DK // LANES), axis=1)).astype(BF16)
    s_new = s_prev * gr + lax.dot_general(kd, v, (((0,), (0,)), ((), ())), preferred_element_type=F32)
    mu = jnp.mean(o, axis=-1, keepdims=True)
    oc = o - mu
    var = jnp.mean(oc * oc, axis=-1, keepdims=True)
    on = oc * lax.rsqrt(var + EPS) * gn
    gf = g.astype(F32)
    return (gf * _sigmoid(gf) * on).astype(BF16), s_new


def _ret_prompt_kernel(q_ref, k_ref, v_ref, g_ref, dm_ref, dq_ref, dk_ref, gr_ref, gn_ref,
                       og_ref, st_ref, s_scr):
    @pl.when(pl.program_id(1) == 0)
    def _():
        s_scr[...] = jnp.zeros_like(s_scr)

    og, s_new = _ret_block(q_ref[...], k_ref[...], v_ref[...], g_ref[...], s_scr[...],
                           dm_ref[0], dq_ref[0], dk_ref[0], gr_ref[0], gn_ref[...])
    og_ref[...] = og
    s_scr[...] = s_new
    st_ref[0] = s_new


def _ret_prompt_call(proj, gn, t_prompt):
    dm, dq, dk, gr = _ret_tables(RET_ROWS, CHUNK)
    h = RET_HEADS
    kv_blk = 2 * D_MODEL // RET_DV
    g_blk = (2 * D_MODEL + RET_VW) // RET_DV
    return pl.pallas_call(
        _ret_prompt_kernel,
        grid=(h, t_prompt // RET_ROWS),
        in_specs=[
            pl.BlockSpec((RET_ROWS, RET_DK), lambda hh, c: (c, hh)),
            pl.BlockSpec((RET_ROWS, RET_DK), lambda hh, c: (c, h + hh)),
            pl.BlockSpec((RET_ROWS, RET_DV), lambda hh, c: (c, kv_blk + hh)),
            pl.BlockSpec((RET_ROWS, RET_DV), lambda hh, c: (c, g_blk + hh)),
            pl.BlockSpec((1, RET_ROWS, RET_ROWS), lambda hh, c: (hh, 0, 0)),
            pl.BlockSpec((1, RET_ROWS, LANES), lambda hh, c: (hh, 0, 0)),
            pl.BlockSpec((1, RET_ROWS, LANES), lambda hh, c: (hh, 0, 0)),
            pl.BlockSpec((1, 1, RET_DV), lambda hh, c: (hh, 0, 0)),
            pl.BlockSpec((1, RET_DV), lambda hh, c: (0, hh)),
        ],
        out_specs=[
            pl.BlockSpec((RET_ROWS, RET_DV), lambda hh, c: (c, hh)),
            pl.BlockSpec((1, RET_DK, RET_DV), lambda hh, c: (hh, 0, 0)),
        ],
        out_shape=[
            jax.ShapeDtypeStruct((t_prompt, RET_VW), BF16),
            jax.ShapeDtypeStruct((h, RET_DK, RET_DV), F32),
        ],
        scratch_shapes=[pltpu.VMEM((RET_DK, RET_DV), F32)],
        compiler_params=_params(("arbitrary", "arbitrary"), 32),
        name="retention_prompt",
    )(proj, proj, proj, proj, dm, dq, dk, gr, gn.reshape(1, RET_VW))


def _ret_sample_kernel(q_ref, k_ref, v_ref, g_ref, s0_ref, dm_ref, dq_ref, dk_ref, gr_ref, gn_ref,
                       og_ref, st_ref):
    for hh in range(RET_HEADS):
        ks = slice(hh * RET_DK, (hh + 1) * RET_DK)
        vs = slice(hh * RET_DV, (hh + 1) * RET_DV)
        og, s_new = _ret_block(q_ref[:, ks], k_ref[:, ks], v_ref[:, vs], g_ref[:, vs], s0_ref[0, hh],
                               dm_ref[hh], dq_ref[hh], dk_ref[hh], gr_ref[hh], gn_ref[:, vs])
        og_ref[:, vs] = og
        st_ref[0, hh] = s_new


def _ret_sample_call(proj, gn, state, t_prompt, dec_seq):
    dm, dq, dk, gr = _ret_tables(dec_seq, dec_seq)
    b = state.shape[0]
    r0 = t_prompt // dec_seq
    const3 = lambda bb: (0, 0, 0)
    return pl.pallas_call(
        _ret_sample_kernel,
        grid=(b,),
        in_specs=[
            pl.BlockSpec((dec_seq, D_MODEL), lambda bb: (r0 + bb, 0)),
            pl.BlockSpec((dec_seq, D_MODEL), lambda bb: (r0 + bb, 1)),
            pl.BlockSpec((dec_seq, RET_VW), lambda bb: (r0 + bb, 1)),
            pl.BlockSpec((dec_seq, RET_VW), lambda bb: (r0 + bb, 2)),
            pl.BlockSpec((1, RET_HEADS, RET_DK, RET_DV), lambda bb: (bb, 0, 0, 0)),
            pl.BlockSpec(dm.shape, const3),
            pl.BlockSpec(dq.shape, const3),
            pl.BlockSpec(dk.shape, const3),
            pl.BlockSpec(gr.shape, const3),
            pl.BlockSpec((1, RET_VW), lambda bb: (0, 0)),
        ],
        out_specs=[
            pl.BlockSpec((dec_seq, RET_VW), lambda bb: (bb, 0)),
            pl.BlockSpec((1, RET_HEADS, RET_DK, RET_DV), lambda bb: (bb, 0, 0, 0)),
        ],
        out_shape=[
            jax.ShapeDtypeStruct((b * dec_seq, RET_VW), BF16),
            jax.ShapeDtypeStruct(state.shape, F32),
        ],
        compiler_params=_params(("arbitrary",), 40),
        name="retention_sample",
    )(proj, proj, proj, proj, state, dm, dq, dk, gr, gn.reshape(1, RET_VW))


def _router_kernel(x_ref, g_ref, wr_ref, br_ref, xn_ref, id_ref, w_ref):
    xn = _rms(x_ref[...], g_ref[...])
    xn_ref[...] = xn
    xh = xn.astype(BF16)
    xl = (xn - xh.astype(F32)).astype(BF16)
    w = wr_ref[...]
    wh = w.astype(BF16)
    wl = (w - wh.astype(F32)).astype(BF16)
    lg = (jnp.dot(xh, wh, preferred_element_type=F32) + jnp.dot(xh, wl, preferred_element_type=F32)
          + jnp.dot(xl, wh, preferred_element_type=F32)) + br_ref[...]
    lane = lax.broadcasted_iota(jnp.int32, lg.shape, 1).astype(F32)
    neg = jnp.float32(-jnp.inf)
    big = jnp.float32(LANES)
    gl = jnp.where(lane < MOE_GROUPS, lg, neg)
    gmax = jnp.max(gl, axis=-1, keepdims=True)
    gsel = jnp.min(jnp.where(gl == gmax, lane, big), axis=-1, keepdims=True)
    gw = 1.0 / jnp.sum(jnp.exp(gl - gmax), axis=-1, keepdims=True)
    lo = MOE_GROUPS + gsel * MOE_PER_GROUP
    el = jnp.where((lane >= lo) & (lane < lo + MOE_PER_GROUP), lg, neg)
    m1 = jnp.max(el, axis=-1, keepdims=True)
    i1 = jnp.min(jnp.where(el == m1, lane, big), axis=-1, keepdims=True)
    el2 = jnp.where(lane == i1, neg, el)
    m2 = jnp.max(el2, axis=-1, keepdims=True)
    i2 = jnp.min(jnp.where(el2 == m2, lane, big), axis=-1, keepdims=True)
    z = jnp.sum(jnp.exp(el - m1), axis=-1, keepdims=True)
    p1 = 1.0 / z
    p2 = jnp.exp(m2 - m1) / z
    tot = p1 + p2
    w1 = p1 / tot * gw
    w2 = p2 / tot * gw
    ids = jnp.where(lane == 0, i1 - MOE_GROUPS, jnp.where(lane == 1, i2 - MOE_GROUPS, 0.0))
    id_ref[...] = ids.astype(jnp.int32)
    w_ref[...] = jnp.where(lane == 0, w1, jnp.where(lane == 1, w2, 0.0))


def _router_call(x, g, w_grp, b_grp, w_exp, b_exp):
    t, d = x.shape
    pad = LANES - MOE_GROUPS - N_EXP
    wr = jnp.concatenate([w_grp, w_exp, jnp.zeros((d, pad), F32)], axis=1)
    br = jnp.concatenate([b_grp, b_exp, jnp.zeros((pad,), F32)]).reshape(1, LANES)
    row = lambda i: (i, 0)
    fixed = lambda i: (0, 0)
    return pl.pallas_call(
        _router_kernel,
        grid=(t // ROW_TILE,),
        in_specs=[pl.BlockSpec((ROW_TILE, d), row), pl.BlockSpec((1, d), fixed),
                  pl.BlockSpec((d, LANES), fixed), pl.BlockSpec((1, LANES), fixed)],
        out_specs=[pl.BlockSpec((ROW_TILE, d), row), pl.BlockSpec((ROW_TILE, LANES), row),
                   pl.BlockSpec((ROW_TILE, LANES), row)],
        out_shape=[jax.ShapeDtypeStruct((t, d), F32), jax.ShapeDtypeStruct((t, LANES), jnp.int32),
                   jax.ShapeDtypeStruct((t, LANES), F32)],
        compiler_params=_params(("arbitrary",), 40),
        name="moe_router",
    )(x, g.reshape(1, d), wr, br)


def _row_tokens_kernel(pos_ref, tok_ref):
    def clear(r, carry):
        tok_ref[r] = 0
        return carry
    lax.fori_loop(0, tok_ref.shape[0], clear, 0, unroll=8)

    def place(a, carry):
        tok_ref[pos_ref[a]] = lax.shift_right_logical(a, 1)
        return carry
    lax.fori_loop(0, pos_ref.shape[0], place, 0, unroll=8)


def _row_tokens_call(pos, rows):
    smem = pl.BlockSpec(memory_space=pltpu.SMEM)
    return pl.pallas_call(
        _row_tokens_kernel,
        in_specs=[smem],
        out_specs=smem,
        out_shape=jax.ShapeDtypeStruct((rows,), jnp.int32),
        name="moe_row_tokens",
    )(pos)


def _moe_plan(eid):
    t = eid.shape[0]
    a = 2 * t
    n_tiles = a // MOE_TILE + N_EXP
    e_flat = eid.reshape(a)
    onehot = (e_flat[:, None] == jnp.arange(N_EXP, dtype=jnp.int32)[None, :]).astype(jnp.int32)
    csum = jnp.cumsum(onehot, axis=0)
    counts = csum[-1]
    rank = jnp.take_along_axis(csum, e_flat[:, None], axis=1)[:, 0] - 1
    tiles_e = (counts + MOE_TILE - 1) // MOE_TILE
    tiles_end = jnp.cumsum(tiles_e)
    row_start = (tiles_end - tiles_e) * MOE_TILE
    pos = row_start[e_flat] + rank
    n_valid = tiles_end[-1]
    tile_id = jnp.minimum(jnp.arange(n_tiles, dtype=jnp.int32), n_valid - 1)
    tile_expert = jnp.sum((tile_id[:, None] >= tiles_end[None, :]).astype(jnp.int32), axis=1)
    tile_expert = jnp.minimum(tile_expert, N_EXP - 1)
    changed = jnp.concatenate([jnp.ones((1,), bool), tile_expert[1:] != tile_expert[:-1]])
    first = jnp.logical_and(changed, jnp.arange(n_tiles) < n_valid).astype(jnp.int32)
    wslot = (jnp.cumsum(first) - 1) % 2
    next_start = tiles_end[tile_expert]
    next_expert = jnp.where(next_start < n_valid, tile_expert[jnp.minimum(next_start, n_tiles - 1)], -1)
    tile_meta = jnp.stack([tile_expert, first, wslot, next_expert]).astype(jnp.int32)
    pos = pos.astype(jnp.int32)
    tok_of_row = _row_tokens_call(pos, n_tiles * MOE_TILE)
    return tile_meta, n_valid.reshape(1).astype(jnp.int32), tok_of_row, pos


def _moe_ffn_kernel(tm_ref, nv_ref, tok_ref, x_hbm, wgu_hbm, wd_hbm, o_ref,
                    xbuf, sems, wgu_f, wd_f, wsems, wgu_b, wd_b, *, layer):
    i = pl.program_id(0)
    nv = nv_ref[0]
    slot = lax.rem(i, 2)
    expert, first, wslot, next_expert = tm_ref[0, i], tm_ref[1, i], tm_ref[2, i], tm_ref[3, i]

    def weight_copies(e, s):
        return (pltpu.make_async_copy(wgu_hbm.at[layer, e], wgu_f.at[s], wsems.at[0, s]),
                pltpu.make_async_copy(wd_hbm.at[layer, e], wd_f.at[s], wsems.at[1, s]))
    ff_chunk = 256
    n_chunk = D_FF_E // ff_chunk
    row_groups = np.array_split(np.arange(MOE_TILE), 3 * n_chunk - 2) + [[], []]

    def row_copy(row0, r, dst_slot):
        return pltpu.make_async_copy(x_hbm.at[pl.ds(tok_ref[row0 + r], 1)], xbuf.at[dst_slot, pl.ds(r, 1)],
                                     sems.at[dst_slot])

    def wait_tile(s):
        pltpu.make_async_copy(x_hbm.at[pl.ds(0, MOE_TILE)], xbuf.at[s], sems.at[s]).wait()

    @pl.when(i == 0)
    def _():
        for c in weight_copies(expert, 0):
            c.start(priority=1)
        for r in range(MOE_TILE):
            row_copy(0, r, 0).start()

    @pl.when(first == 1)
    def _():
        for c in weight_copies(expert, wslot):
            c.wait()
        wgu_b[...] = wgu_f[wslot].astype(BF16)
        wd_b[...] = wd_f[wslot].astype(BF16)

        @pl.when(next_expert >= 0)
        def _():
            for c in weight_copies(next_expert, 1 - wslot):
                c.start(priority=1)

    def compute(s):
        wait_tile(s)
        x = xbuf[s].astype(BF16)
        nxt = (i + 1) * MOE_TILE
        groups = iter(row_groups)

        def request_rows():
            for r in next(groups):
                row_copy(nxt, int(r), 1 - s).start()

        for c in range(n_chunk):
            lo = c * ff_chunk
            hg = jnp.dot(x, wgu_b[:, lo:lo + ff_chunk], preferred_element_type=F32)
            request_rows()
            hu = jnp.dot(x, wgu_b[:, D_FF_E + lo:D_FF_E + lo + ff_chunk], preferred_element_type=F32)
            request_rows()
            act = (hg * _sigmoid(hg) * hu).astype(BF16)
            y = jnp.dot(act, wd_b[lo:lo + ff_chunk, :], preferred_element_type=F32)
            if c == 0:
                o_ref[...] = y
            else:
                o_ref[...] += y
            request_rows()

    for s in range(2):
        pl.when(jnp.logical_and(i < nv, slot == s))(functools.partial(compute, s))

    @pl.when(i == nv)
    def _():
        wait_tile(slot)

    @pl.when(i >= nv)
    def _():
        o_ref[...] = jnp.zeros_like(o_ref)


def _moe_ffn_call(xn, w_gu, w_down, layer, tile_meta, n_valid, tok_of_row):
    t, d = xn.shape
    n_tiles = tile_meta.shape[1]
    rows = n_tiles * MOE_TILE
    grid_spec = pltpu.PrefetchScalarGridSpec(
        num_scalar_prefetch=3,
        grid=(n_tiles,),
        in_specs=[pl.BlockSpec(memory_space=pl.ANY)] * 3,
        out_specs=pl.BlockSpec((MOE_TILE, d), lambda i, tm, nv, tok: (i, 0)),
        scratch_shapes=[
            pltpu.VMEM((2, MOE_TILE, d), F32),
            pltpu.SemaphoreType.DMA((2,)),
            pltpu.VMEM((2, d, 2 * D_FF_E), F32),
            pltpu.VMEM((2, D_FF_E, d), F32),
            pltpu.SemaphoreType.DMA((2, 2)),
            pltpu.VMEM((d, 2 * D_FF_E), BF16),
            pltpu.VMEM((D_FF_E, d), BF16),
        ],
    )
    return pl.pallas_call(
        functools.partial(_moe_ffn_kernel, layer=layer),
        grid_spec=grid_spec,
        out_shape=jax.ShapeDtypeStruct((rows, d), F32),
        compiler_params=_params(("arbitrary",), 48),
        name="moe_experts",
    )(tile_meta, n_valid, tok_of_row, xn, w_gu, w_down)


def _combine_kernel(pos_ref, x_ref, w_ref, y_hbm, g_ref, *rest, emit_x, norm_tiles):
    outs = rest[:int(emit_x) + len(norm_tiles)]
    buf, sems = rest[int(emit_x) + len(norm_tiles):]
    i = pl.program_id(0)
    n = pl.num_programs(0)
    slot = lax.rem(i, 2)
    tm = x_ref.shape[0]

    def gather(tile, dst_slot):
        base = tile * (2 * tm)
        for r in range(tm):
            for kk in range(2):
                pltpu.make_async_copy(y_hbm.at[pl.ds(pos_ref[base + 2 * r + kk], 1)],
                                      buf.at[dst_slot, kk, pl.ds(r, 1)], sems.at[dst_slot]).start(priority=kk)

    @pl.when(i == 0)
    def _():
        gather(0, 0)

    for s in range(2):
        pl.when(jnp.logical_and(i + 1 < n, slot == s))(functools.partial(gather, i + 1, 1 - s))

    for kk in range(2):
        pltpu.make_async_copy(y_hbm.at[pl.ds(0, tm)], buf.at[slot, kk], sems.at[slot]).wait()
    xo = x_ref[...] + (w_ref[:, 0:1] * buf[slot, 0] + w_ref[:, 1:2] * buf[slot, 1])
    norm_refs = outs[int(emit_x):]
    if emit_x:
        outs[0][...] = xo

    def write_norm(p):
        norm_refs[p][...] = _rms(xo, g_ref[...]).astype(norm_refs[p].dtype)

    _for_each_part(i, norm_tiles, write_norm)


def _combine_call(x, ws, y_rows, pos, g, emit_x, norm_dtype, norm_rows):
    t, d = x.shape
    tm = MOE_TILE
    row = lambda i, pos_ref: (i, 0)
    out_specs, out_shape = [], []
    if emit_x:
        out_specs.append(pl.BlockSpec((tm, d), row))
        out_shape.append(jax.ShapeDtypeStruct((t, d), F32))
    norm_parts = [jax.ShapeDtypeStruct((r, d), norm_dtype) for r in norm_rows]
    out_specs += _part_specs(norm_parts, tm, d, row)
    out_shape += norm_parts
    grid_spec = pltpu.PrefetchScalarGridSpec(
        num_scalar_prefetch=1,
        grid=(t // tm,),
        in_specs=[pl.BlockSpec((tm, d), row), pl.BlockSpec((tm, LANES), row), pl.BlockSpec(memory_space=pl.ANY),
                  pl.BlockSpec((1, d), lambda i, pos_ref: (0, 0))],
        out_specs=out_specs,
        scratch_shapes=[pltpu.VMEM((2, 2, tm, d), F32), pltpu.SemaphoreType.DMA((2,))],
    )
    return pl.pallas_call(
        functools.partial(_combine_kernel, emit_x=emit_x, norm_tiles=_part_tiles(norm_parts, tm)),
        grid_spec=grid_spec,
        out_shape=out_shape,
        compiler_params=_params(("arbitrary",), 40),
        name="moe_combine",
    )(pos, x, ws, y_rows, g.reshape(1, d))


def _moe_layer(x, g_ffn, w_grp, b_grp, w_exp, b_exp, w_gu, w_down, layer, g_next, emit_x, norm_dtype, norm_rows):
    xn, ids, ws = _router_call(x, g_ffn, w_grp, b_grp, w_exp, b_exp)
    tile_meta, n_valid, tok_of_row, pos = _moe_plan(ids[:, :2])
    y_rows = _moe_ffn_call(xn, w_gu, w_down, layer, tile_meta, n_valid, tok_of_row)
    return _combine_call(x, ws, y_rows, pos, g_next, emit_x, norm_dtype, norm_rows)


def _s5_prep_kernel(are_ref, aim_ref, ldt_ref, bre_ref, bim_ref, lre_ref, lim_ref, bbr_ref, bbi_ref):
    are = are_ref[...]
    aim = aim_ref[...]
    dt = jnp.exp(ldt_ref[...])
    mag = jnp.exp(are * dt)
    lr = mag * jnp.cos(aim * dt)
    li = mag * jnp.sin(aim * dt)
    lre_ref[...] = lr
    lim_ref[...] = li
    den = are * are + aim * aim
    nr = lr - 1.0
    cr = (nr * are + li * aim) / den
    ci = (li * are - nr * aim) / den
    bre = bre_ref[...]
    bim = bim_ref[...]
    bbr_ref[...] = cr * bre - ci * bim
    bbi_ref[...] = cr * bim + ci * bre


def _s5_prep_call(a_re, a_im, log_dt, b_re, b_im):
    g, p, c = b_re.shape
    shp = (g, p * c)
    expand = lambda a: jnp.broadcast_to(a[:, :, None], (g, p, c)).reshape(shp)
    ldt = jnp.broadcast_to(log_dt[:, None], shp)
    outs = pl.pallas_call(
        _s5_prep_kernel,
        out_shape=[jax.ShapeDtypeStruct(shp, F32)] * 4,
        name="s5_discretize",
    )(expand(a_re), expand(a_im), ldt, b_re.reshape(shp), b_im.reshape(shp))
    lre, lim, bbr, bbi = outs
    return lre[:, ::c], lim[:, ::c], bbr, bbi


def _s5_mats(bbr, bbi, c_re, c_im):
    eye = jnp.eye(16, dtype=F32)

    def bd_part(bb):
        xx = bb.reshape(S5_FB, 16, S5_P, S5_GC).transpose(0, 1, 3, 2)
        yy = xx[:, :, :, None, :] * eye[None, :, None, :, None]
        return yy.reshape(S5_FB, 16 * S5_GC, 16 * S5_P)

    def cd_part(cc):
        xx = cc.reshape(S5_FB, 16, S5_GC, S5_P).transpose(0, 1, 3, 2)
        yy = xx[:, :, :, None, :] * eye[None, :, None, :, None]
        return yy.reshape(S5_FB, 16 * S5_P, 16 * S5_GC)

    bd = jnp.concatenate([bd_part(bbr), bd_part(bbi)], axis=-1).astype(BF16)
    cd = jnp.concatenate([cd_part(c_re), -cd_part(c_im)], axis=1).astype(BF16)
    return bd, cd


def _s5_kernel(u_ref, bd_ref, cd_ref, lre_ref, lim_ref, d_ref, h0r_ref, h0i_ref,
               z_ref, hr_ref, hi_ref, scr, hst, *, seg, carry):
    rows = u_ref.shape[0]
    n_seq = rows // seg
    ch = 16 * S5_GC
    half = SUBLANES * S5_PITCH
    blk = 2 * half

    for fb in range(S5_FB):
        bu = jnp.dot(u_ref[:, fb * ch:(fb + 1) * ch].astype(BF16), bd_ref[fb], preferred_element_type=F32)
        for c in range(2 * SUBLANES):
            r0 = fb * blk + c * S5_PITCH
            scr[r0:r0 + rows, :] = bu[:, c * LANES:(c + 1) * LANES]

    lre = [lre_ref[fb * SUBLANES:(fb + 1) * SUBLANES, :] for fb in range(S5_FB)]
    lim = [lim_ref[fb * SUBLANES:(fb + 1) * SUBLANES, :] for fb in range(S5_FB)]

    def scan(row0, state):
        def step(l, st):
            new = []
            for fb in range(S5_FB):
                hr, hi = st[2 * fb], st[2 * fb + 1]
                ire = pl.ds(fb * blk + row0 + l, SUBLANES, stride=S5_PITCH)
                iim = pl.ds(fb * blk + half + row0 + l, SUBLANES, stride=S5_PITCH)
                nr = lre[fb] * hr - lim[fb] * hi + scr[ire, :]
                ni = lre[fb] * hi + lim[fb] * hr + scr[iim, :]
                scr[ire, :] = nr
                scr[iim, :] = ni
                new += [nr, ni]
            return tuple(new)
        return lax.fori_loop(0, seg, step, state)

    def load_state(rref, iref, lead):
        st = []
        for fb in range(S5_FB):
            st += [rref[lead + (slice(fb * SUBLANES, (fb + 1) * SUBLANES), slice(None))],
                   iref[lead + (slice(fb * SUBLANES, (fb + 1) * SUBLANES), slice(None))]]
        return tuple(st)

    def store_state(st, rref, iref, lead):
        for fb in range(S5_FB):
            rref[lead + (slice(fb * SUBLANES, (fb + 1) * SUBLANES), slice(None))] = st[2 * fb]
            iref[lead + (slice(fb * SUBLANES, (fb + 1) * SUBLANES), slice(None))] = st[2 * fb + 1]

    if carry:
        @pl.when(pl.program_id(0) == 0)
        def _():
            hst[0] = h0r_ref[0]
            hst[1] = h0i_ref[0]
        hre_scr = hst.at[0]
        him_scr = hst.at[1]
        st = scan(0, load_state(hre_scr, him_scr, ()))
        store_state(st, hre_scr, him_scr, ())
        store_state(st, hr_ref, hi_ref, (0,))
    else:
        for s in range(n_seq):
            st = scan(s * seg, load_state(h0r_ref, h0i_ref, (s,)))
            store_state(st, hr_ref, hi_ref, (s,))

    for fb in range(S5_FB):
        hf = jnp.concatenate(
            [scr[fb * blk + c * S5_PITCH: fb * blk + c * S5_PITCH + rows, :] for c in range(2 * SUBLANES)],
            axis=1).astype(BF16)
        y = jnp.dot(hf, cd_ref[fb], preferred_element_type=F32)
        cs = slice(fb * ch, (fb + 1) * ch)
        yy = y + d_ref[:, cs] * u_ref[:, cs]
        z_ref[:, cs] = jax.nn.gelu(yy).astype(z_ref.dtype)


def _s5_call(u, bd, cd, lre, lim, d_skip, h0r, h0i, row_block0, n_blocks, seg, carry):
    t, d = u.shape
    n_seq = S5_ROWS // seg
    ns_total = h0r.shape[0]
    st_idx = (lambda i: (0, 0, 0)) if carry else (lambda i: (i, 0, 0))
    fixed2 = lambda i: (0, 0)
    fixed3 = lambda i: (0, 0, 0)
    st_rows = S5_FB * SUBLANES
    return pl.pallas_call(
        functools.partial(_s5_kernel, seg=seg, carry=carry),
        grid=(n_blocks,),
        in_specs=[
            pl.BlockSpec((S5_ROWS, d), lambda i: (row_block0 + i, 0)),
            pl.BlockSpec(bd.shape, fixed3),
            pl.BlockSpec(cd.shape, fixed3),
            pl.BlockSpec(lre.shape, fixed2),
            pl.BlockSpec(lim.shape, fixed2),
            pl.BlockSpec((1, d), fixed2),
            pl.BlockSpec((n_seq, st_rows, LANES), st_idx),
            pl.BlockSpec((n_seq, st_rows, LANES), st_idx),
        ],
        out_specs=[
            pl.BlockSpec((S5_ROWS, d), lambda i: (i, 0)),
            pl.BlockSpec((n_seq, st_rows, LANES), st_idx),
            pl.BlockSpec((n_seq, st_rows, LANES), st_idx),
        ],
        out_shape=[
            jax.ShapeDtypeStruct((n_blocks * S5_ROWS, d), BF16),
            jax.ShapeDtypeStruct((ns_total, st_rows, LANES), F32),
            jax.ShapeDtypeStruct((ns_total, st_rows, LANES), F32),
        ],
        scratch_shapes=[
            pltpu.VMEM((2 * SUBLANES * S5_FB * S5_PITCH, LANES), F32),
            pltpu.VMEM((2, st_rows, LANES), F32),
        ],
        compiler_params=_params(("arbitrary",), 56),
        name="s5_scan",
    )(u, bd, cd, lre, lim, d_skip.reshape(1, d), h0r, h0i)


def kernel(x_prompt, x_sample, state_ret, state_s5_re, state_s5_im, norm_mix_g, norm_ffn_g, norm_final_g,
           ret_w_in, ret_gn_g, ret_w_out, s5_w_in, s5_a_re, s5_a_im, s5_log_dt, s5_b_re, s5_b_im,
           s5_c_re, s5_c_im, s5_d, s5_w_out, moe_w_grp, moe_b_grp, moe_w_exp, moe_b_exp, moe_w_gu, moe_w_down):
    bp, seq, d = x_prompt.shape
    db, dseq, _ = x_sample.shape
    assert d == D_MODEL and bp == 1
    assert seq % max(ROW_TILE, RET_ROWS) == 0 and seq % CHUNK == 0
    assert dseq < CHUNK and dseq % 16 == 0 and S5_ROWS % dseq == 0
    tp = bp * seq
    ts = db * dseq
    t = tp + ts
    assert ts % ROW_TILE == 0 and ts % S5_ROWS == 0

    x_parts = [x_prompt.reshape(tp, d), x_sample.reshape(ts, d)]

    half = RET_DK // 2
    freqs = ROPE_BASE ** (-jnp.arange(half, dtype=F32) / half)
    pos = jnp.concatenate([jnp.arange(seq, dtype=F32), jnp.tile(PAST_LEN + jnp.arange(dseq, dtype=F32), db)])
    ang = pos[:, None] * freqs[None, :]
    cos_t = jnp.cos(ang)
    sin_t = jnp.sin(ang)

    xn = _norm_call(x_parts, norm_mix_g[0], BF16)
    n_proj = 2 * D_MODEL + 2 * RET_VW
    tn = 1024
    first_col = lambda j: 0
    proj = _mm_call([xn], ret_w_in[0], [0], tn, n_proj, [([cos_t], half, first_col), ([sin_t], half, first_col)],
                    BF16, functools.partial(_epi_rope, n_qk_blocks=2 * D_MODEL // tn), 40, "ret_in_proj")
    og_p, ret_p = _ret_prompt_call(proj, ret_gn_g[0], tp)
    og_s, ret_s = _ret_sample_call(proj, ret_gn_g[0], state_ret[0], tp, dseq)
    tn = 512
    same_col = lambda j: j
    x1 = _mm_call([og_p, og_s], ret_w_out[0], [0], tn, d, [(x_parts, tn, same_col)], F32, _epi_residual,
                  40, "ret_out_proj")

    x2, xn = _moe_layer(x1, norm_ffn_g[0], moe_w_grp[0], moe_b_grp[0], moe_w_exp[0], moe_b_exp[0],
                        moe_w_gu, moe_w_down, 0, norm_mix_g[1], True, BF16, (t,))

    u = _mm_call([xn], s5_w_in[0], [0], 1024, d, [], F32, _epi_plain, 40, "s5_in_proj")
    lre, lim, bbr, bbi = _s5_prep_call(s5_a_re[0], s5_a_im[0], s5_log_dt[0], s5_b_re[0], s5_b_im[0])
    bd, cd = _s5_mats(bbr, bbi, s5_c_re[0], s5_c_im[0])
    st_shape = (S5_FB * SUBLANES, LANES)
    lre = lre.reshape(st_shape)
    lim = lim.reshape(st_shape)
    zero_state = jnp.zeros((1,) + st_shape, F32)
    z_p, hpr, hpi = _s5_call(u, bd, cd, lre, lim, s5_d[0], zero_state, zero_state,
                             0, tp // S5_ROWS, S5_ROWS, True)
    z_s, hsr, hsi = _s5_call(u, bd, cd, lre, lim, s5_d[0],
                             state_s5_re[0].reshape((db,) + st_shape), state_s5_im[0].reshape((db,) + st_shape),
                             tp // S5_ROWS, ts // S5_ROWS, dseq, False)
    x3 = _mm_call([z_p, z_s], s5_w_out[0], [0, d // tn], tn, d, [([x2], tn, same_col)], F32, _epi_glu_residual,
                  40, "s5_out_proj")

    y_p, y_s = _moe_layer(x3, norm_ffn_g[1], moe_w_grp[1], moe_b_grp[1], moe_w_exp[1], moe_b_exp[1],
                          moe_w_gu, moe_w_down, 1, norm_final_g, False, F32, (tp, ts))

    s5_shape = (1, -1, S5_GROUPS, S5_P)
    return (y_p.reshape(bp, seq, d), y_s.reshape(db, dseq, d),
            ret_p.reshape(1, bp, RET_HEADS, RET_DK, RET_DV), ret_s.reshape((1,) + ret_s.shape),
            hpr.reshape(s5_shape), hpi.reshape(s5_shape), hsr.reshape(s5_shape), hsi.reshape(s5_shape))
```

```python
import functools
import math

import numpy as np
import jax
import jax.numpy as jnp
from jax import lax
from jax.experimental import pallas as pl
from jax.experimental.pallas import tpu as pltpu

F32 = jnp.float32
BF16 = jnp.bfloat16

D_MODEL = 2048
PAST_LEN = 2048
CHUNK = 64
RET_HEADS = 8
RET_DK = D_MODEL // RET_HEADS
RET_DV = 2 * RET_DK
RET_VW = RET_HEADS * RET_DV
ROPE_BASE = 10000.0
S5_GC = 16
S5_GROUPS = D_MODEL // S5_GC
S5_P = 64
MOE_GROUPS = 4
MOE_PER_GROUP = 8
N_EXP = MOE_GROUPS * MOE_PER_GROUP
D_FF_E = 512
EPS = 1e-6

LANES = 128
SUBLANES = 8
ROW_TILE = 512
RET_ROWS = 256
MOE_TILE = 256
S5_ROWS = 128
S5_FB = S5_GROUPS // 16
S5_PITCH = S5_ROWS + 8
MIB = 1024 * 1024


def _params(sem, vmem_mib):
    return pltpu.CompilerParams(dimension_semantics=sem, vmem_limit_bytes=vmem_mib * MIB)


def _rms(x, g):
    ms = jnp.mean(x * x, axis=-1, keepdims=True)
    return x * lax.rsqrt(ms + EPS) * g


def _sigmoid(x):
    return 1.0 / (1.0 + jnp.exp(-x))


def _part_specs(parts, rows, cols, rowcol):
    specs = []
    start = 0
    for p in parts:
        nt = p.shape[0] // rows

        def imap(*g, start=start, nt=nt):
            i, jc = rowcol(*g)
            return (jnp.clip(i - start, 0, nt - 1), jc)

        specs.append(pl.BlockSpec((rows, cols), imap))
        start += nt
    return specs


def _part_tiles(parts, rows):
    return tuple(p.shape[0] // rows for p in parts)


def _for_each_part(i, tiles, fn):
    if len(tiles) == 1:
        fn(0)
        return
    start = 0
    for p, nt in enumerate(tiles):
        pl.when(jnp.logical_and(i >= start, i < start + nt))(functools.partial(fn, p))
        start += nt


def _pick(refs, p):
    return refs[min(p, len(refs) - 1)]


def _norm_kernel(*refs, tiles):
    n = len(tiles)
    x_refs, g_ref, o_ref = refs[:n], refs[n], refs[n + 1]

    def run(p):
        o_ref[...] = _rms(x_refs[p][...], g_ref[...]).astype(o_ref.dtype)

    _for_each_part(pl.program_id(0), tiles, run)


def _norm_call(x_parts, g, out_dtype):
    d = x_parts[0].shape[1]
    t = sum(p.shape[0] for p in x_parts)
    return pl.pallas_call(
        functools.partial(_norm_kernel, tiles=_part_tiles(x_parts, ROW_TILE)),
        grid=(t // ROW_TILE,),
        in_specs=_part_specs(x_parts, ROW_TILE, d, lambda i: (i, 0)) + [pl.BlockSpec((1, d), lambda i: (0, 0))],
        out_specs=pl.BlockSpec((ROW_TILE, d), lambda i: (i, 0)),
        out_shape=jax.ShapeDtypeStruct((t, d), out_dtype),
        compiler_params=_params(("arbitrary",), 32),
        name="rmsnorm",
    )(*x_parts, g.reshape(1, d))


def _mm_kernel(*refs, tiles, n_x, n_w, n_extra, epi):
    x_refs = refs[:n_x]
    w_refs = refs[n_x:n_x + n_w]
    pos = n_x + n_w
    extra = []
    for n in n_extra:
        extra.append(refs[pos:pos + n])
        pos += n
    o_ref = refs[pos]
    wb = refs[pos + 1:]

    @pl.when(pl.program_id(1) == 0)
    def _():
        for w_ref, b in zip(w_refs, wb):
            b[...] = w_ref[...].astype(BF16)

    def run(p):
        x = _pick(x_refs, p)[...]
        accs = [jnp.dot(x, b[...], preferred_element_type=F32) for b in wb]
        epi(accs, [_pick(e, p) for e in extra], o_ref)

    _for_each_part(pl.program_id(1), tiles, run)


def _mm_call(x_parts, w, w_col_offsets, tn, n_out, extras, out_dtype, epi, vmem_mib, name):
    k = x_parts[0].shape[1]
    t = sum(p.shape[0] for p in x_parts)
    n_w = len(w_col_offsets)
    split = [parts for parts in [x_parts] + [e[0] for e in extras] if len(parts) > 1]
    tiles = _part_tiles(split[0], ROW_TILE) if split else (t // ROW_TILE,)
    assert all(_part_tiles(parts, ROW_TILE) == tiles for parts in split)
    in_specs = _part_specs(x_parts, ROW_TILE, k, lambda j, i: (i, 0))
    for off in w_col_offsets:
        in_specs.append(pl.BlockSpec((k, tn), lambda j, i, off=off: (0, j + off)))
    extra_arrays = []
    for parts, cols, colfn in extras:
        in_specs += _part_specs(parts, ROW_TILE, cols, lambda j, i, colfn=colfn: (i, colfn(j)))
        extra_arrays += list(parts)
    return pl.pallas_call(
        functools.partial(_mm_kernel, tiles=tiles, n_x=len(x_parts), n_w=n_w,
                          n_extra=tuple(len(e[0]) for e in extras), epi=epi),
        grid=(n_out // tn, t // ROW_TILE),
        in_specs=in_specs,
        out_specs=pl.BlockSpec((ROW_TILE, tn), lambda j, i: (i, j)),
        out_shape=jax.ShapeDtypeStruct((t, n_out), out_dtype),
        scratch_shapes=[pltpu.VMEM((k, tn), BF16) for _ in range(n_w)],
        compiler_params=_params(("arbitrary", "arbitrary"), vmem_mib),
        name=name,
    )(*x_parts, *([w] * n_w), *extra_arrays)


def _epi_plain(accs, extra, o_ref):
    o_ref[...] = accs[0].astype(o_ref.dtype)


def _epi_residual(accs, extra, o_ref):
    o_ref[...] = (extra[0][...] + accs[0]).astype(o_ref.dtype)


def _epi_glu_residual(accs, extra, o_ref):
    a, gt = accs
    o_ref[...] = (extra[0][...] + a * _sigmoid(gt)).astype(o_ref.dtype)


def _epi_rope(accs, extra, o_ref, *, n_qk_blocks):
    acc = accs[0]
    cos_ref, sin_ref = extra
    j = pl.program_id(0)
    half = RET_DK // 2

    @pl.when(j < n_qk_blocks)
    def _():
        scale = jnp.where(j >= n_qk_blocks // 2, RET_DK ** -0.5, 1.0).astype(F32)
        c = cos_ref[...]
        s = sin_ref[...]
        for hh in range(acc.shape[1] // RET_DK):
            lo = hh * RET_DK
            t1 = acc[:, lo:lo + half]
            t2 = acc[:, lo + half:lo + RET_DK]
            o_ref[:, lo:lo + half] = ((t1 * c - t2 * s) * scale).astype(o_ref.dtype)
            o_ref[:, lo + half:lo + RET_DK] = ((t1 * s + t2 * c) * scale).astype(o_ref.dtype)

    @pl.when(j >= n_qk_blocks)
    def _():
        o_ref[...] = acc.astype(o_ref.dtype)


def _ret_tables(rows, chunk):
    lg = jnp.log(jnp.asarray(1.0 - 2.0 ** (-5.0 - np.arange(RET_HEADS)), dtype=F32))[:, None, None]
    n = jnp.arange(rows, dtype=F32)
    ci = np.arange(rows) // chunk
    same_or_earlier = jnp.asarray(ci[None, :] <= ci[:, None])
    dmat = jnp.where(same_or_earlier[None], jnp.exp(jnp.abs(n[:, None] - n[None, :])[None] * lg), 0.0)
    dq = jnp.exp((n + 1.0)[None, :, None] * lg) * jnp.ones((1, 1, LANES), F32)
    dk = jnp.exp((rows - 1.0 - n)[None, :, None] * lg) * jnp.ones((1, 1, LANES), F32)
    gr = jnp.exp(rows * lg) * jnp.ones((1, 1, RET_DV), F32)
    return dmat, dq, dk, gr


def _ret_block(q, k, v, g, s_prev, dmat, dq, dk, gr, gn):
    s = lax.dot_general(q, k, (((1,), (1,)), ((), ())), preferred_element_type=F32) * dmat
    inter = jnp.dot(q, s_prev.astype(BF16), preferred_element_type=F32)
    o = jnp.dot(s.astype(BF16), v, preferred_element_type=F32) + jnp.concatenate([dq] * (RET_DV // LANES), axis=1) * inter
    kd = (k.astype(F32) * jnp.concatenate([dk] * (RET_DK // LANES), axis=1)).astype(BF16)
    s_new = s_prev * gr + lax.dot_general(kd, v, (((0,), (0,)), ((), ())), preferred_element_type=F32)
    mu = jnp.mean(o, axis=-1, keepdims=True)
    oc = o - mu
    var = jnp.mean(oc * oc, axis=-1, keepdims=True)
    on = oc * lax.rsqrt(var + EPS) * gn
    gf = g.astype(F32)
    return (gf * _sigmoid(gf) * on).astype(BF16), s_new


def _ret_prompt_kernel(q_ref, k_ref, v_ref, g_ref, dm_ref, dq_ref, dk_ref, gr_ref, gn_ref,
                       og_ref, st_ref, s_scr):
    @pl.when(pl.program_id(1) == 0)
    def _():
        s_scr[...] = jnp.zeros_like(s_scr)

    og, s_new = _ret_block(q_ref[...], k_ref[...], v_ref[...], g_ref[...], s_scr[...],
                           dm_ref[0], dq_ref[0], dk_ref[0], gr_ref[0], gn_ref[...])
    og_ref[...] = og
    s_scr[...] = s_new
    st_ref[0] = s_new


def _ret_prompt_call(proj, gn, t_prompt):
    dm, dq, dk, gr = _ret_tables(RET_ROWS, CHUNK)
    h = RET_HEADS
    kv_blk = 2 * D_MODEL // RET_DV
    g_blk = (2 * D_MODEL + RET_VW) // RET_DV
    return pl.pallas_call(
        _ret_prompt_kernel,
        grid=(h, t_prompt // RET_ROWS),
        in_specs=[
            pl.BlockSpec((RET_ROWS, RET_DK), lambda hh, c: (c, hh)),
            pl.BlockSpec((RET_ROWS, RET_DK), lambda hh, c: (c, h + hh)),
            pl.BlockSpec((RET_ROWS, RET_DV), lambda hh, c: (c, kv_blk + hh)),
            pl.BlockSpec((RET_ROWS, RET_DV), lambda hh, c: (c, g_blk + hh)),
            pl.BlockSpec((1, RET_ROWS, RET_ROWS), lambda hh, c: (hh, 0, 0)),
            pl.BlockSpec((1, RET_ROWS, LANES), lambda hh, c: (hh, 0, 0)),
            pl.BlockSpec((1, RET_ROWS, LANES), lambda hh, c: (hh, 0, 0)),
            pl.BlockSpec((1, 1, RET_DV), lambda hh, c: (hh, 0, 0)),
            pl.BlockSpec((1, RET_DV), lambda hh, c: (0, hh)),
        ],
        out_specs=[
            pl.BlockSpec((RET_ROWS, RET_DV), lambda hh, c: (c, hh)),
            pl.BlockSpec((1, RET_DK, RET_DV), lambda hh, c: (hh, 0, 0)),
        ],
        out_shape=[
            jax.ShapeDtypeStruct((t_prompt, RET_VW), BF16),
            jax.ShapeDtypeStruct((h, RET_DK, RET_DV), F32),
        ],
        scratch_shapes=[pltpu.VMEM((RET_DK, RET_DV), F32)],
        compiler_params=_params(("arbitrary", "arbitrary"), 32),
        name="retention_prompt",
    )(proj, proj, proj, proj, dm, dq, dk, gr, gn.reshape(1, RET_VW))


def _ret_sample_kernel(q_ref, k_ref, v_ref, g_ref, s0_ref, dm_ref, dq_ref, dk_ref, gr_ref, gn_ref,
                       og_ref, st_ref):
    for hh in range(RET_HEADS):
        ks = slice(hh * RET_DK, (hh + 1) * RET_DK)
        vs = slice(hh * RET_DV, (hh + 1) * RET_DV)
        og, s_new = _ret_block(q_ref[:, ks], k_ref[:, ks], v_ref[:, vs], g_ref[:, vs], s0_ref[0, hh],
                               dm_ref[hh], dq_ref[hh], dk_ref[hh], gr_ref[hh], gn_ref[:, vs])
        og_ref[:, vs] = og
        st_ref[0, hh] = s_new


def _ret_sample_call(proj, gn, state, t_prompt, dec_seq):
    dm, dq, dk, gr = _ret_tables(dec_seq, dec_seq)
    b = state.shape[0]
    r0 = t_prompt // dec_seq
    const3 = lambda bb: (0, 0, 0)
    return pl.pallas_call(
        _ret_sample_kernel,
        grid=(b,),
        in_specs=[
            pl.BlockSpec((dec_seq, D_MODEL), lambda bb: (r0 + bb, 0)),
            pl.BlockSpec((dec_seq, D_MODEL), lambda bb: (r0 + bb, 1)),
            pl.BlockSpec((dec_seq, RET_VW), lambda bb: (r0 + bb, 1)),
            pl.BlockSpec((dec_seq, RET_VW), lambda bb: (r0 + bb, 2)),
            pl.BlockSpec((1, RET_HEADS, RET_DK, RET_DV), lambda bb: (bb, 0, 0, 0)),
            pl.BlockSpec(dm.shape, const3),
            pl.BlockSpec(dq.shape, const3),
            pl.BlockSpec(dk.shape, const3),
            pl.BlockSpec(gr.shape, const3),
            pl.BlockSpec((1, RET_VW), lambda bb: (0, 0)),
        ],
        out_specs=[
            pl.BlockSpec((dec_seq, RET_VW), lambda bb: (bb, 0)),
            pl.BlockSpec((1, RET_HEADS, RET_DK, RET_DV), lambda bb: (bb, 0, 0, 0)),
        ],
        out_shape=[
            jax.ShapeDtypeStruct((b * dec_seq, RET_VW), BF16),
            jax.ShapeDtypeStruct(state.shape, F32),
        ],
        compiler_params=_params(("arbitrary",), 40),
        name="retention_sample",
    )(proj, proj, proj, proj, state, dm, dq, dk, gr, gn.reshape(1, RET_VW))


def _router_kernel(x_ref, g_ref, wr_ref, br_ref, xn_ref, id_ref, w_ref):
    xn = _rms(x_ref[...], g_ref[...])
    n_sub = xn.shape[1] // LANES
    for j in range(n_sub):
        xn_ref[pl.ds(j, xn.shape[0], stride=n_sub), :] = xn[:, j * LANES:(j + 1) * LANES]
    xh = xn.astype(BF16)
    xl = (xn - xh.astype(F32)).astype(BF16)
    w = wr_ref[...]
    wh = w.astype(BF16)
    wl = (w - wh.astype(F32)).astype(BF16)
    lg = (jnp.dot(xh, wh, preferred_element_type=F32) + jnp.dot(xh, wl, preferred_element_type=F32)
          + jnp.dot(xl, wh, preferred_element_type=F32)) + br_ref[...]
    lane = lax.broadcasted_iota(jnp.int32, lg.shape, 1).astype(F32)
    neg = jnp.float32(-jnp.inf)
    big = jnp.float32(LANES)
    gl = jnp.where(lane < MOE_GROUPS, lg, neg)
    gmax = jnp.max(gl, axis=-1, keepdims=True)
    gsel = jnp.min(jnp.where(gl == gmax, lane, big), axis=-1, keepdims=True)
    gw = 1.0 / jnp.sum(jnp.exp(gl - gmax), axis=-1, keepdims=True)
    lo = MOE_GROUPS + gsel * MOE_PER_GROUP
    el = jnp.where((lane >= lo) & (lane < lo + MOE_PER_GROUP), lg, neg)
    m1 = jnp.max(el, axis=-1, keepdims=True)
    i1 = jnp.min(jnp.where(el == m1, lane, big), axis=-1, keepdims=True)
    el2 = jnp.where(lane == i1, neg, el)
    m2 = jnp.max(el2, axis=-1, keepdims=True)
    i2 = jnp.min(jnp.where(el2 == m2, lane, big), axis=-1, keepdims=True)
    z = jnp.sum(jnp.exp(el - m1), axis=-1, keepdims=True)
    p1 = 1.0 / z
    p2 = jnp.exp(m2 - m1) / z
    tot = p1 + p2
    w1 = p1 / tot * gw
    w2 = p2 / tot * gw
    ids = jnp.where(lane == 0, i1 - MOE_GROUPS, jnp.where(lane == 1, i2 - MOE_GROUPS, 0.0))
    id_ref[...] = ids.astype(jnp.int32)
    w_ref[...] = jnp.where(lane == 0, w1, jnp.where(lane == 1, w2, 0.0))


def _router_call(x, g, w_grp, b_grp, w_exp, b_exp):
    t, d = x.shape
    pad = LANES - MOE_GROUPS - N_EXP
    wr = jnp.concatenate([w_grp, w_exp, jnp.zeros((d, pad), F32)], axis=1)
    br = jnp.concatenate([b_grp, b_exp, jnp.zeros((pad,), F32)]).reshape(1, LANES)
    row = lambda i: (i, 0)
    fixed = lambda i: (0, 0)
    n_sub = d // LANES
    xn, ids, ws = pl.pallas_call(
        _router_kernel,
        grid=(t // ROW_TILE,),
        in_specs=[pl.BlockSpec((ROW_TILE, d), row), pl.BlockSpec((1, d), fixed),
                  pl.BlockSpec((d, LANES), fixed), pl.BlockSpec((1, LANES), fixed)],
        out_specs=[pl.BlockSpec((ROW_TILE * n_sub, LANES), row), pl.BlockSpec((ROW_TILE, LANES), row),
                   pl.BlockSpec((ROW_TILE, LANES), row)],
        out_shape=[jax.ShapeDtypeStruct((t * n_sub, LANES), F32), jax.ShapeDtypeStruct((t, LANES), jnp.int32),
                   jax.ShapeDtypeStruct((t, LANES), F32)],
        compiler_params=_params(("arbitrary",), 40),
        name="moe_router",
    )(x, g.reshape(1, d), wr, br)
    return xn.reshape(t, n_sub, LANES), ids, ws


def _row_tokens_kernel(pos_ref, tok_ref):
    def clear(r, carry):
        tok_ref[r] = 0
        return carry
    lax.fori_loop(0, tok_ref.shape[0], clear, 0, unroll=8)

    def place(a, carry):
        tok_ref[pos_ref[a]] = lax.shift_right_logical(a, 1)
        return carry
    lax.fori_loop(0, pos_ref.shape[0], place, 0, unroll=8)


def _row_tokens_call(pos, rows):
    smem = pl.BlockSpec(memory_space=pltpu.SMEM)
    return pl.pallas_call(
        _row_tokens_kernel,
        in_specs=[smem],
        out_specs=smem,
        out_shape=jax.ShapeDtypeStruct((rows,), jnp.int32),
        name="moe_row_tokens",
    )(pos)


def _moe_plan(eid):
    t = eid.shape[0]
    a = 2 * t
    n_tiles = a // MOE_TILE + N_EXP
    e_flat = eid.reshape(a)
    onehot = (e_flat[:, None] == jnp.arange(N_EXP, dtype=jnp.int32)[None, :]).astype(jnp.int32)
    csum = jnp.cumsum(onehot, axis=0)
    counts = csum[-1]
    rank = jnp.take_along_axis(csum, e_flat[:, None], axis=1)[:, 0] - 1
    tiles_e = (counts + MOE_TILE - 1) // MOE_TILE
    tiles_end = jnp.cumsum(tiles_e)
    row_start = (tiles_end - tiles_e) * MOE_TILE
    pos = row_start[e_flat] + rank
    n_valid = tiles_end[-1]
    tile_id = jnp.minimum(jnp.arange(n_tiles, dtype=jnp.int32), n_valid - 1)
    tile_expert = jnp.sum((tile_id[:, None] >= tiles_end[None, :]).astype(jnp.int32), axis=1)
    tile_expert = jnp.minimum(tile_expert, N_EXP - 1)
    changed = jnp.concatenate([jnp.ones((1,), bool), tile_expert[1:] != tile_expert[:-1]])
    first = jnp.logical_and(changed, jnp.arange(n_tiles) < n_valid).astype(jnp.int32)
    wslot = (jnp.cumsum(first) - 1) % 2
    next_start = tiles_end[tile_expert]
    next_expert = jnp.where(next_start < n_valid, tile_expert[jnp.minimum(next_start, n_tiles - 1)], -1)
    tile_meta = jnp.stack([tile_expert, first, wslot, next_expert]).astype(jnp.int32)
    pos = pos.astype(jnp.int32)
    tok_of_row = _row_tokens_call(pos, n_tiles * MOE_TILE)
    return tile_meta, n_valid.reshape(1).astype(jnp.int32), tok_of_row, pos


def _moe_ffn_kernel(tm_ref, nv_ref, tok_ref, x_hbm, wgu_hbm, wd_hbm, o_ref,
                    xbuf, sems, wgu_f, wd_f, wsems, wgu_b, wd_b, *, layer):
    i = pl.program_id(0)
    nv = nv_ref[0]
    slot = lax.rem(i, 2)
    expert, first, wslot, next_expert = tm_ref[0, i], tm_ref[1, i], tm_ref[2, i], tm_ref[3, i]

    def weight_copies(e, s):
        return (pltpu.make_async_copy(wgu_hbm.at[layer, e], wgu_f.at[s], wsems.at[0, s]),
                pltpu.make_async_copy(wd_hbm.at[layer, e], wd_f.at[s], wsems.at[1, s]))
    ff_chunk = 256
    n_chunk = D_FF_E // ff_chunk
    row_groups = np.array_split(np.arange(MOE_TILE), 3 * n_chunk - 2) + [[], []]

    n_sub = x_hbm.shape[1]

    def row_copy(row0, r, dst_slot):
        return pltpu.make_async_copy(x_hbm.at[tok_ref[row0 + r]], xbuf.at[dst_slot, pl.ds(r * n_sub, n_sub)],
                                     sems.at[dst_slot])

    def wait_tile(s):
        pltpu.make_async_copy(xbuf.at[s], xbuf.at[s], sems.at[s]).wait()

    @pl.when(i == 0)
    def _():
        for c in weight_copies(expert, 0):
            c.start(priority=1)
        for r in range(MOE_TILE):
            row_copy(0, r, 0).start()

    @pl.when(first == 1)
    def _():
        for c in weight_copies(expert, wslot):
            c.wait()
        wgu_b[...] = wgu_f[wslot].astype(BF16)
        wd_b[...] = wd_f[wslot].astype(BF16)

        @pl.when(next_expert >= 0)
        def _():
            for c in weight_copies(next_expert, 1 - wslot):
                c.start(priority=1)

    def compute(s):
        wait_tile(s)
        x = jnp.concatenate([xbuf[s, pl.ds(j, MOE_TILE, stride=n_sub), :] for j in range(n_sub)],
                            axis=1).astype(BF16)
        nxt = (i + 1) * MOE_TILE
        groups = iter(row_groups)

        def request_rows():
            for r in next(groups):
                row_copy(nxt, int(r), 1 - s).start()

        for c in range(n_chunk):
            lo = c * ff_chunk
            hg = jnp.dot(x, wgu_b[:, lo:lo + ff_chunk], preferred_element_type=F32)
            request_rows()
            hu = jnp.dot(x, wgu_b[:, D_FF_E + lo:D_FF_E + lo + ff_chunk], preferred_element_type=F32)
            request_rows()
            act = (hg * _sigmoid(hg) * hu).astype(BF16)
            y = jnp.dot(act, wd_b[lo:lo + ff_chunk, :], preferred_element_type=F32)
            if c == 0:
                o_ref[...] = y
            else:
                o_ref[...] += y
            request_rows()

    for s in range(2):
        pl.when(jnp.logical_and(i < nv, slot == s))(functools.partial(compute, s))

    @pl.when(i == nv)
    def _():
        wait_tile(slot)

    @pl.when(i >= nv)
    def _():
        o_ref[...] = jnp.zeros_like(o_ref)


def _moe_ffn_call(xn, w_gu, w_down, layer, tile_meta, n_valid, tok_of_row):
    t, n_sub, _ = xn.shape
    d = n_sub * LANES
    n_tiles = tile_meta.shape[1]
    rows = n_tiles * MOE_TILE
    grid_spec = pltpu.PrefetchScalarGridSpec(
        num_scalar_prefetch=3,
        grid=(n_tiles,),
        in_specs=[pl.BlockSpec(memory_space=pl.ANY)] * 3,
        out_specs=pl.BlockSpec((MOE_TILE, d), lambda i, tm, nv, tok: (i, 0)),
        scratch_shapes=[
            pltpu.VMEM((2, MOE_TILE * n_sub, LANES), F32),
            pltpu.SemaphoreType.DMA((2,)),
            pltpu.VMEM((2, d, 2 * D_FF_E), F32),
            pltpu.VMEM((2, D_FF_E, d), F32),
            pltpu.SemaphoreType.DMA((2, 2)),
            pltpu.VMEM((d, 2 * D_FF_E), BF16),
            pltpu.VMEM((D_FF_E, d), BF16),
        ],
    )
    return pl.pallas_call(
        functools.partial(_moe_ffn_kernel, layer=layer),
        grid_spec=grid_spec,
        out_shape=jax.ShapeDtypeStruct((rows, d), F32),
        compiler_params=_params(("arbitrary",), 48),
        name="moe_experts",
    )(tile_meta, n_valid, tok_of_row, xn, w_gu, w_down)


def _combine_kernel(pos_ref, x_ref, w_ref, y_hbm, g_ref, *rest, emit_x, norm_tiles):
    outs = rest[:int(emit_x) + len(norm_tiles)]
    buf, sems = rest[int(emit_x) + len(norm_tiles):]
    i = pl.program_id(0)
    n = pl.num_programs(0)
    slot = lax.rem(i, 2)
    tm = x_ref.shape[0]

    def gather(tile, dst_slot):
        base = tile * (2 * tm)
        for r in range(tm):
            for kk in range(2):
                pltpu.make_async_copy(y_hbm.at[pl.ds(pos_ref[base + 2 * r + kk], 1)],
                                      buf.at[dst_slot, kk, pl.ds(r, 1)], sems.at[dst_slot]).start(priority=kk)

    @pl.when(i == 0)
    def _():
        gather(0, 0)

    for s in range(2):
        pl.when(jnp.logical_and(i + 1 < n, slot == s))(functools.partial(gather, i + 1, 1 - s))

    for kk in range(2):
        pltpu.make_async_copy(y_hbm.at[pl.ds(0, tm)], buf.at[slot, kk], sems.at[slot]).wait()
    xo = x_ref[...] + (w_ref[:, 0:1] * buf[slot, 0] + w_ref[:, 1:2] * buf[slot, 1])
    norm_refs = outs[int(emit_x):]
    if emit_x:
        outs[0][...] = xo

    def write_norm(p):
        norm_refs[p][...] = _rms(xo, g_ref[...]).astype(norm_refs[p].dtype)

    _for_each_part(i, norm_tiles, write_norm)


def _combine_call(x, ws, y_rows, pos, g, emit_x, norm_dtype, norm_rows):
    t, d = x.shape
    tm = MOE_TILE
    row = lambda i, pos_ref: (i, 0)
    out_specs, out_shape = [], []
    if emit_x:
        out_specs.append(pl.BlockSpec((tm, d), row))
        out_shape.append(jax.ShapeDtypeStruct((t, d), F32))
    norm_parts = [jax.ShapeDtypeStruct((r, d), norm_dtype) for r in norm_rows]
    out_specs += _part_specs(norm_parts, tm, d, row)
    out_shape += norm_parts
    grid_spec = pltpu.PrefetchScalarGridSpec(
        num_scalar_prefetch=1,
        grid=(t // tm,),
        in_specs=[pl.BlockSpec((tm, d), row), pl.BlockSpec((tm, LANES), row), pl.BlockSpec(memory_space=pl.ANY),
                  pl.BlockSpec((1, d), lambda i, pos_ref: (0, 0))],
        out_specs=out_specs,
        scratch_shapes=[pltpu.VMEM((2, 2, tm, d), F32), pltpu.SemaphoreType.DMA((2,))],
    )
    return pl.pallas_call(
        functools.partial(_combine_kernel, emit_x=emit_x, norm_tiles=_part_tiles(norm_parts, tm)),
        grid_spec=grid_spec,
        out_shape=out_shape,
        compiler_params=_params(("arbitrary",), 40),
        name="moe_combine",
    )(pos, x, ws, y_rows, g.reshape(1, d))


def _moe_layer(x, g_ffn, w_grp, b_grp, w_exp, b_exp, w_gu, w_down, layer, g_next, emit_x, norm_dtype, norm_rows):
    xn, ids, ws = _router_call(x, g_ffn, w_grp, b_grp, w_exp, b_exp)
    tile_meta, n_valid, tok_of_row, pos = _moe_plan(ids[:, :2])
    y_rows = _moe_ffn_call(xn, w_gu, w_down, layer, tile_meta, n_valid, tok_of_row)
    return _combine_call(x, ws, y_rows, pos, g_next, emit_x, norm_dtype, norm_rows)


def _s5_prep_kernel(are_ref, aim_ref, ldt_ref, bre_ref, bim_ref, lre_ref, lim_ref, bbr_ref, bbi_ref):
    are = are_ref[...]
    aim = aim_ref[...]
    dt = jnp.exp(ldt_ref[...])
    mag = jnp.exp(are * dt)
    lr = mag * jnp.cos(aim * dt)
    li = mag * jnp.sin(aim * dt)
    lre_ref[...] = lr
    lim_ref[...] = li
    den = are * are + aim * aim
    nr = lr - 1.0
    cr = (nr * are + li * aim) / den
    ci = (li * are - nr * aim) / den
    bre = bre_ref[...]
    bim = bim_ref[...]
    bbr_ref[...] = cr * bre - ci * bim
    bbi_ref[...] = cr * bim + ci * bre


def _s5_prep_call(a_re, a_im, log_dt, b_re, b_im):
    g, p, c = b_re.shape
    shp = (g, p * c)
    expand = lambda a: jnp.broadcast_to(a[:, :, None], (g, p, c)).reshape(shp)
    ldt = jnp.broadcast_to(log_dt[:, None], shp)
    outs = pl.pallas_call(
        _s5_prep_kernel,
        out_shape=[jax.ShapeDtypeStruct(shp, F32)] * 4,
        name="s5_discretize",
    )(expand(a_re), expand(a_im), ldt, b_re.reshape(shp), b_im.reshape(shp))
    lre, lim, bbr, bbi = outs
    return lre[:, ::c], lim[:, ::c], bbr, bbi


def _s5_mats(bbr, bbi, c_re, c_im):
    eye = jnp.eye(16, dtype=F32)

    def bd_part(bb):
        xx = bb.reshape(S5_FB, 16, S5_P, S5_GC).transpose(0, 1, 3, 2)
        yy = xx[:, :, :, None, :] * eye[None, :, None, :, None]
        return yy.reshape(S5_FB, 16 * S5_GC, 16 * S5_P)

    def cd_part(cc):
        xx = cc.reshape(S5_FB, 16, S5_GC, S5_P).transpose(0, 1, 3, 2)
        yy = xx[:, :, :, None, :] * eye[None, :, None, :, None]
        return yy.reshape(S5_FB, 16 * S5_P, 16 * S5_GC)

    bd = jnp.concatenate([bd_part(bbr), bd_part(bbi)], axis=-1).astype(BF16)
    cd = jnp.concatenate([cd_part(c_re), -cd_part(c_im)], axis=1).astype(BF16)
    return bd, cd


def _s5_kernel(u_ref, bd_ref, cd_ref, lre_ref, lim_ref, d_ref, h0r_ref, h0i_ref,
               z_ref, hr_ref, hi_ref, scr, hst, *, seg, carry):
    rows = u_ref.shape[0]
    n_seq = rows // seg
    ch = 16 * S5_GC
    half = SUBLANES * S5_PITCH
    blk = 2 * half

    for fb in range(S5_FB):
        bu = jnp.dot(u_ref[:, fb * ch:(fb + 1) * ch].astype(BF16), bd_ref[fb], preferred_element_type=F32)
        for c in range(2 * SUBLANES):
            r0 = fb * blk + c * S5_PITCH
            scr[r0:r0 + rows, :] = bu[:, c * LANES:(c + 1) * LANES]

    lre = [lre_ref[fb * SUBLANES:(fb + 1) * SUBLANES, :] for fb in range(S5_FB)]
    lim = [lim_ref[fb * SUBLANES:(fb + 1) * SUBLANES, :] for fb in range(S5_FB)]

    def scan(row0, state):
        def step(l, st):
            new = []
            for fb in range(S5_FB):
                hr, hi = st[2 * fb], st[2 * fb + 1]
                ire = pl.ds(fb * blk + row0 + l, SUBLANES, stride=S5_PITCH)
                iim = pl.ds(fb * blk + half + row0 + l, SUBLANES, stride=S5_PITCH)
                nr = lre[fb] * hr - lim[fb] * hi + scr[ire, :]
                ni = lre[fb] * hi + lim[fb] * hr + scr[iim, :]
                scr[ire, :] = nr
                scr[iim, :] = ni
                new += [nr, ni]
            return tuple(new)
        return lax.fori_loop(0, seg, step, state)

    def load_state(rref, iref, lead):
        st = []
        for fb in range(S5_FB):
            st += [rref[lead + (slice(fb * SUBLANES, (fb + 1) * SUBLANES), slice(None))],
                   iref[lead + (slice(fb * SUBLANES, (fb + 1) * SUBLANES), slice(None))]]
        return tuple(st)

    def store_state(st, rref, iref, lead):
        for fb in range(S5_FB):
            rref[lead + (slice(fb * SUBLANES, (fb + 1) * SUBLANES), slice(None))] = st[2 * fb]
            iref[lead + (slice(fb * SUBLANES, (fb + 1) * SUBLANES), slice(None))] = st[2 * fb + 1]

    if carry:
        @pl.when(pl.program_id(0) == 0)
        def _():
            hst[0] = h0r_ref[0]
            hst[1] = h0i_ref[0]
        hre_scr = hst.at[0]
        him_scr = hst.at[1]
        st = scan(0, load_state(hre_scr, him_scr, ()))
        store_state(st, hre_scr, him_scr, ())
        store_state(st, hr_ref, hi_ref, (0,))
    else:
        for s in range(n_seq):
            st = scan(s * seg, load_state(h0r_ref, h0i_ref, (s,)))
            store_state(st, hr_ref, hi_ref, (s,))

    for fb in range(S5_FB):
        hf = jnp.concatenate(
            [scr[fb * blk + c * S5_PITCH: fb * blk + c * S5_PITCH + rows, :] for c in range(2 * SUBLANES)],
            axis=1).astype(BF16)
        y = jnp.dot(hf, cd_ref[fb], preferred_element_type=F32)
        cs = slice(fb * ch, (fb + 1) * ch)
        yy = y + d_ref[:, cs] * u_ref[:, cs]
        z_ref[:, cs] = jax.nn.gelu(yy).astype(z_ref.dtype)


def _s5_call(u, bd, cd, lre, lim, d_skip, h0r, h0i, row_block0, n_blocks, seg, carry):
    t, d = u.shape
    n_seq = S5_ROWS // seg
    ns_total = h0r.shape[0]
    st_idx = (lambda i: (0, 0, 0)) if carry else (lambda i: (i, 0, 0))
    fixed2 = lambda i: (0, 0)
    fixed3 = lambda i: (0, 0, 0)
    st_rows = S5_FB * SUBLANES
    return pl.pallas_call(
        functools.partial(_s5_kernel, seg=seg, carry=carry),
        grid=(n_blocks,),
        in_specs=[
            pl.BlockSpec((S5_ROWS, d), lambda i: (row_block0 + i, 0)),
            pl.BlockSpec(bd.shape, fixed3),
            pl.BlockSpec(cd.shape, fixed3),
            pl.BlockSpec(lre.shape, fixed2),
            pl.BlockSpec(lim.shape, fixed2),
            pl.BlockSpec((1, d), fixed2),
            pl.BlockSpec((n_seq, st_rows, LANES), st_idx),
            pl.BlockSpec((n_seq, st_rows, LANES), st_idx),
        ],
        out_specs=[
            pl.BlockSpec((S5_ROWS, d), lambda i: (i, 0)),
            pl.BlockSpec((n_seq, st_rows, LANES), st_idx),
            pl.BlockSpec((n_seq, st_rows, LANES), st_idx),
        ],
        out_shape=[
            jax.ShapeDtypeStruct((n_blocks * S5_ROWS, d), BF16),
            jax.ShapeDtypeStruct((ns_total, st_rows, LANES), F32),
            jax.ShapeDtypeStruct((ns_total, st_rows, LANES), F32),
        ],
        scratch_shapes=[
            pltpu.VMEM((2 * SUBLANES * S5_FB * S5_PITCH, LANES), F32),
            pltpu.VMEM((2, st_rows, LANES), F32),
        ],
        compiler_params=_params(("arbitrary",), 56),
        name="s5_scan",
    )(u, bd, cd, lre, lim, d_skip.reshape(1, d), h0r, h0i)


def kernel(x_prompt, x_sample, state_ret, state_s5_re, state_s5_im, norm_mix_g, norm_ffn_g, norm_final_g,
           ret_w_in, ret_gn_g, ret_w_out, s5_w_in, s5_a_re, s5_a_im, s5_log_dt, s5_b_re, s5_b_im,
           s5_c_re, s5_c_im, s5_d, s5_w_out, moe_w_grp, moe_b_grp, moe_w_exp, moe_b_exp, moe_w_gu, moe_w_down):
    bp, seq, d = x_prompt.shape
    db, dseq, _ = x_sample.shape
    assert d == D_MODEL and bp == 1
    assert seq % max(ROW_TILE, RET_ROWS) == 0 and seq % CHUNK == 0
    assert dseq < CHUNK and dseq % 16 == 0 and S5_ROWS % dseq == 0
    tp = bp * seq
    ts = db * dseq
    t = tp + ts
    assert ts % ROW_TILE == 0 and ts % S5_ROWS == 0

    x_parts = [x_prompt.reshape(tp, d), x_sample.reshape(ts, d)]

    half = RET_DK // 2
    freqs = ROPE_BASE ** (-jnp.arange(half, dtype=F32) / half)
    pos = jnp.concatenate([jnp.arange(seq, dtype=F32), jnp.tile(PAST_LEN + jnp.arange(dseq, dtype=F32), db)])
    ang = pos[:, None] * freqs[None, :]
    cos_t = jnp.cos(ang)
    sin_t = jnp.sin(ang)

    xn = _norm_call(x_parts, norm_mix_g[0], BF16)
    n_proj = 2 * D_MODEL + 2 * RET_VW
    tn = 1024
    first_col = lambda j: 0
    proj = _mm_call([xn], ret_w_in[0], [0], tn, n_proj, [([cos_t], half, first_col), ([sin_t], half, first_col)],
                    BF16, functools.partial(_epi_rope, n_qk_blocks=2 * D_MODEL // tn), 40, "ret_in_proj")
    og_p, ret_p = _ret_prompt_call(proj, ret_gn_g[0], tp)
    og_s, ret_s = _ret_sample_call(proj, ret_gn_g[0], state_ret[0], tp, dseq)
    tn = 512
    same_col = lambda j: j
    x1 = _mm_call([og_p, og_s], ret_w_out[0], [0], tn, d, [(x_parts, tn, same_col)], F32, _epi_residual,
                  40, "ret_out_proj")

    x2, xn = _moe_layer(x1, norm_ffn_g[0], moe_w_grp[0], moe_b_grp[0], moe_w_exp[0], moe_b_exp[0],
                        moe_w_gu, moe_w_down, 0, norm_mix_g[1], True, BF16, (t,))

    u = _mm_call([xn], s5_w_in[0], [0], 1024, d, [], F32, _epi_plain, 40, "s5_in_proj")
    lre, lim, bbr, bbi = _s5_prep_call(s5_a_re[0], s5_a_im[0], s5_log_dt[0], s5_b_re[0], s5_b_im[0])
    bd, cd = _s5_mats(bbr, bbi, s5_c_re[0], s5_c_im[0])
    st_shape = (S5_FB * SUBLANES, LANES)
    lre = lre.reshape(st_shape)
    lim = lim.reshape(st_shape)
    zero_state = jnp.zeros((1,) + st_shape, F32)
    z_p, hpr, hpi = _s5_call(u, bd, cd, lre, lim, s5_d[0], zero_state, zero_state,
                             0, tp // S5_ROWS, S5_ROWS, True)
    z_s, hsr, hsi = _s5_call(u, bd, cd, lre, lim, s5_d[0],
                             state_s5_re[0].reshape((db,) + st_shape), state_s5_im[0].reshape((db,) + st_shape),
                             tp // S5_ROWS, ts // S5_ROWS, dseq, False)
    x3 = _mm_call([z_p, z_s], s5_w_out[0], [0, d // tn], tn, d, [([x2], tn, same_col)], F32, _epi_glu_residual,
                  40, "s5_out_proj")

    y_p, y_s = _moe_layer(x3, norm_ffn_g[1], moe_w_grp[1], moe_b_grp[1], moe_w_exp[1], moe_b_exp[1],
                          moe_w_gu, moe_w_down, 1, norm_final_g, False, F32, (tp, ts))

    s5_shape = (1, -1, S5_GROUPS, S5_P)
    return (y_p.reshape(bp, seq, d), y_s.reshape(db, dseq, d),
            ret_p.reshape(1, bp, RET_HEADS, RET_DK, RET_DV), ret_s.reshape((1,) + ret_s.shape),
            hpr.reshape(s5_shape), hpi.reshape(s5_shape), hsr.reshape(s5_shape), hsi.reshape(s5_shape))
```

```python
import functools
import math

import numpy as np
import jax
import jax.numpy as jnp
from jax import lax
from jax.experimental import pallas as pl
from jax.experimental.pallas import tpu as pltpu

F32 = jnp.float32
BF16 = jnp.bfloat16

D_MODEL = 2048
PAST_LEN = 2048
CHUNK = 64
RET_HEADS = 8
RET_DK = D_MODEL // RET_HEADS
RET_DV = 2 * RET_DK
RET_VW = RET_HEADS * RET_DV
ROPE_BASE = 10000.0
S5_GC = 16
S5_GROUPS = D_MODEL // S5_GC
S5_P = 64
MOE_GROUPS = 4
MOE_PER_GROUP = 8
N_EXP = MOE_GROUPS * MOE_PER_GROUP
D_FF_E = 512
EPS = 1e-6

LANES = 128
SUBLANES = 8
ROW_TILE = 512
RET_ROWS = 256
MOE_TILE = 256
S5_ROWS = 128
S5_FB = S5_GROUPS // 16
S5_PITCH = S5_ROWS + 8
MIB = 1024 * 1024


def _params(sem, vmem_mib):
    return pltpu.CompilerParams(dimension_semantics=sem, vmem_limit_bytes=vmem_mib * MIB)


def _rms(x, g):
    ms = jnp.mean(x * x, axis=-1, keepdims=True)
    return x * lax.rsqrt(ms + EPS) * g


def _sigmoid(x):
    return 1.0 / (1.0 + jnp.exp(-x))


def _part_specs(parts, rows, cols, rowcol):
    specs = []
    start = 0
    for p in parts:
        nt = p.shape[0] // rows

        def imap(*g, start=start, nt=nt):
            i, jc = rowcol(*g)
            return (jnp.clip(i - start, 0, nt - 1), jc)

        specs.append(pl.BlockSpec((rows, cols), imap))
        start += nt
    return specs


def _part_tiles(parts, rows):
    return tuple(p.shape[0] // rows for p in parts)


def _for_each_part(i, tiles, fn):
    if len(tiles) == 1:
        fn(0)
        return
    start = 0
    for p, nt in enumerate(tiles):
        pl.when(jnp.logical_and(i >= start, i < start + nt))(functools.partial(fn, p))
        start += nt


def _pick(refs, p):
    return refs[min(p, len(refs) - 1)]


def _norm_kernel(*refs, tiles):
    n = len(tiles)
    x_refs, g_ref, o_ref = refs[:n], refs[n], refs[n + 1]

    def run(p):
        o_ref[...] = _rms(x_refs[p][...], g_ref[...]).astype(o_ref.dtype)

    _for_each_part(pl.program_id(0), tiles, run)


def _norm_call(x_parts, g, out_dtype):
    d = x_parts[0].shape[1]
    t = sum(p.shape[0] for p in x_parts)
    return pl.pallas_call(
        functools.partial(_norm_kernel, tiles=_part_tiles(x_parts, ROW_TILE)),
        grid=(t // ROW_TILE,),
        in_specs=_part_specs(x_parts, ROW_TILE, d, lambda i: (i, 0)) + [pl.BlockSpec((1, d), lambda i: (0, 0))],
        out_specs=pl.BlockSpec((ROW_TILE, d), lambda i: (i, 0)),
        out_shape=jax.ShapeDtypeStruct((t, d), out_dtype),
        compiler_params=_params(("arbitrary",), 32),
        name="rmsnorm",
    )(*x_parts, g.reshape(1, d))


def _mm_kernel(*refs, tiles, n_x, n_w, n_extra, epi):
    x_refs = refs[:n_x]
    w_refs = refs[n_x:n_x + n_w]
    pos = n_x + n_w
    extra = []
    for n in n_extra:
        extra.append(refs[pos:pos + n])
        pos += n
    o_ref = refs[pos]
    wb = refs[pos + 1:]

    @pl.when(pl.program_id(1) == 0)
    def _():
        for w_ref, b in zip(w_refs, wb):
            b[...] = w_ref[...].astype(BF16)

    def run(p):
        x = _pick(x_refs, p)[...]
        accs = [jnp.dot(x, b[...], preferred_element_type=F32) for b in wb]
        epi(accs, [_pick(e, p) for e in extra], o_ref)

    _for_each_part(pl.program_id(1), tiles, run)


def _mm_call(x_parts, w, w_col_offsets, tn, n_out, extras, out_dtype, epi, vmem_mib, name):
    k = x_parts[0].shape[1]
    t = sum(p.shape[0] for p in x_parts)
    n_w = len(w_col_offsets)
    split = [parts for parts in [x_parts] + [e[0] for e in extras] if len(parts) > 1]
    tiles = _part_tiles(split[0], ROW_TILE) if split else (t // ROW_TILE,)
    assert all(_part_tiles(parts, ROW_TILE) == tiles for parts in split)
    in_specs = _part_specs(x_parts, ROW_TILE, k, lambda j, i: (i, 0))
    for off in w_col_offsets:
        in_specs.append(pl.BlockSpec((k, tn), lambda j, i, off=off: (0, j + off)))
    extra_arrays = []
    for parts, cols, colfn in extras:
        in_specs += _part_specs(parts, ROW_TILE, cols, lambda j, i, colfn=colfn: (i, colfn(j)))
        extra_arrays += list(parts)
    return pl.pallas_call(
        functools.partial(_mm_kernel, tiles=tiles, n_x=len(x_parts), n_w=n_w,
                          n_extra=tuple(len(e[0]) for e in extras), epi=epi),
        grid=(n_out // tn, t // ROW_TILE),
        in_specs=in_specs,
        out_specs=pl.BlockSpec((ROW_TILE, tn), lambda j, i: (i, j)),
        out_shape=jax.ShapeDtypeStruct((t, n_out), out_dtype),
        scratch_shapes=[pltpu.VMEM((k, tn), BF16) for _ in range(n_w)],
        compiler_params=_params(("arbitrary", "arbitrary"), vmem_mib),
        name=name,
    )(*x_parts, *([w] * n_w), *extra_arrays)


def _epi_plain(accs, extra, o_ref):
    o_ref[...] = accs[0].astype(o_ref.dtype)


def _epi_residual(accs, extra, o_ref):
    o_ref[...] = (extra[0][...] + accs[0]).astype(o_ref.dtype)


def _epi_glu_residual(accs, extra, o_ref):
    a, gt = accs
    o_ref[...] = (extra[0][...] + a * _sigmoid(gt)).astype(o_ref.dtype)


def _epi_rope(accs, extra, o_ref, *, n_qk_blocks):
    acc = accs[0]
    cos_ref, sin_ref = extra
    j = pl.program_id(0)
    half = RET_DK // 2

    @pl.when(j < n_qk_blocks)
    def _():
        scale = jnp.where(j >= n_qk_blocks // 2, RET_DK ** -0.5, 1.0).astype(F32)
        c = cos_ref[...]
        s = sin_ref[...]
        for hh in range(acc.shape[1] // RET_DK):
            lo = hh * RET_DK
            t1 = acc[:, lo:lo + half]
            t2 = acc[:, lo + half:lo + RET_DK]
            o_ref[:, lo:lo + half] = ((t1 * c - t2 * s) * scale).astype(o_ref.dtype)
            o_ref[:, lo + half:lo + RET_DK] = ((t1 * s + t2 * c) * scale).astype(o_ref.dtype)

    @pl.when(j >= n_qk_blocks)
    def _():
        o_ref[...] = acc.astype(o_ref.dtype)


def _ret_tables(rows, chunk):
    lg = jnp.log(jnp.asarray(1.0 - 2.0 ** (-5.0 - np.arange(RET_HEADS)), dtype=F32))[:, None, None]
    n = jnp.arange(rows, dtype=F32)
    ci = np.arange(rows) // chunk
    same_or_earlier = jnp.asarray(ci[None, :] <= ci[:, None])
    dmat = jnp.where(same_or_earlier[None], jnp.exp(jnp.abs(n[:, None] - n[None, :])[None] * lg), 0.0)
    dq = jnp.exp((n + 1.0)[None, :, None] * lg) * jnp.ones((1, 1, LANES), F32)
    dk = jnp.exp((rows - 1.0 - n)[None, :, None] * lg) * jnp.ones((1, 1, LANES), F32)
    gr = jnp.exp(rows * lg) * jnp.ones((1, 1, RET_DV), F32)
    return dmat, dq, dk, gr


def _ret_block(q, k, v, g, s_prev, dmat, dq, dk, gr, gn):
    s = lax.dot_general(q, k, (((1,), (1,)), ((), ())), preferred_element_type=F32) * dmat
    inter = jnp.dot(q, s_prev.astype(BF16), preferred_element_type=F32)
    o = jnp.dot(s.astype(BF16), v, preferred_element_type=F32) + jnp.concatenate([dq] * (RET_DV // LANES), axis=1) * inter
    kd = (k.astype(F32) * jnp.concatenate([dk] * (RET_DK // LANES), axis=1)).astype(BF16)
    s_new = s_prev * gr + lax.dot_general(kd, v, (((0,), (0,)), ((), ())), preferred_element_type=F32)
    mu = jnp.mean(o, axis=-1, keepdims=True)
    oc = o - mu
    var = jnp.mean(oc * oc, axis=-1, keepdims=True)
    on = oc * lax.rsqrt(var + EPS) * gn
    gf = g.astype(F32)
    return (gf * _sigmoid(gf) * on).astype(BF16), s_new


def _ret_prompt_kernel(q_ref, k_ref, v_ref, g_ref, dm_ref, dq_ref, dk_ref, gr_ref, gn_ref,
                       og_ref, st_ref, s_scr):
    @pl.when(pl.program_id(1) == 0)
    def _():
        s_scr[...] = jnp.zeros_like(s_scr)

    og, s_new = _ret_block(q_ref[...], k_ref[...], v_ref[...], g_ref[...], s_scr[...],
                           dm_ref[0], dq_ref[0], dk_ref[0], gr_ref[0], gn_ref[...])
    og_ref[...] = og
    s_scr[...] = s_new
    st_ref[0] = s_new


def _ret_prompt_call(proj, gn, t_prompt):
    dm, dq, dk, gr = _ret_tables(RET_ROWS, CHUNK)
    h = RET_HEADS
    kv_blk = 2 * D_MODEL // RET_DV
    g_blk = (2 * D_MODEL + RET_VW) // RET_DV
    return pl.pallas_call(
        _ret_prompt_kernel,
        grid=(h, t_prompt // RET_ROWS),
        in_specs=[
            pl.BlockSpec((RET_ROWS, RET_DK), lambda hh, c: (c, hh)),
            pl.BlockSpec((RET_ROWS, RET_DK), lambda hh, c: (c, h + hh)),
            pl.BlockSpec((RET_ROWS, RET_DV), lambda hh, c: (c, kv_blk + hh)),
            pl.BlockSpec((RET_ROWS, RET_DV), lambda hh, c: (c, g_blk + hh)),
            pl.BlockSpec((1, RET_ROWS, RET_ROWS), lambda hh, c: (hh, 0, 0)),
            pl.BlockSpec((1, RET_ROWS, LANES), lambda hh, c: (hh, 0, 0)),
            pl.BlockSpec((1, RET_ROWS, LANES), lambda hh, c: (hh, 0, 0)),
            pl.BlockSpec((1, 1, RET_DV), lambda hh, c: (hh, 0, 0)),
            pl.BlockSpec((1, RET_DV), lambda hh, c: (0, hh)),
        ],
        out_specs=[
            pl.BlockSpec((RET_ROWS, RET_DV), lambda hh, c: (c, hh)),
            pl.BlockSpec((1, RET_DK, RET_DV), lambda hh, c: (hh, 0, 0)),
        ],
        out_shape=[
            jax.ShapeDtypeStruct((t_prompt, RET_VW), BF16),
            jax.ShapeDtypeStruct((h, RET_DK, RET_DV), F32),
        ],
        scratch_shapes=[pltpu.VMEM((RET_DK, RET_DV), F32)],
        compiler_params=_params(("arbitrary", "arbitrary"), 32),
        name="retention_prompt",
    )(proj, proj, proj, proj, dm, dq, dk, gr, gn.reshape(1, RET_VW))


def _ret_sample_kernel(q_ref, k_ref, v_ref, g_ref, s0_ref, dm_ref, dq_ref, dk_ref, gr_ref, gn_ref,
                       og_ref, st_ref):
    for hh in range(RET_HEADS):
        ks = slice(hh * RET_DK, (hh + 1) * RET_DK)
        vs = slice(hh * RET_DV, (hh + 1) * RET_DV)
        og, s_new = _ret_block(q_ref[:, ks], k_ref[:, ks], v_ref[:, vs], g_ref[:, vs], s0_ref[0, hh],
                               dm_ref[hh], dq_ref[hh], dk_ref[hh], gr_ref[hh], gn_ref[:, vs])
        og_ref[:, vs] = og
        st_ref[0, hh] = s_new


def _ret_sample_call(proj, gn, state, t_prompt, dec_seq):
    dm, dq, dk, gr = _ret_tables(dec_seq, dec_seq)
    b = state.shape[0]
    r0 = t_prompt // dec_seq
    const3 = lambda bb: (0, 0, 0)
    return pl.pallas_call(
        _ret_sample_kernel,
        grid=(b,),
        in_specs=[
            pl.BlockSpec((dec_seq, D_MODEL), lambda bb: (r0 + bb, 0)),
            pl.BlockSpec((dec_seq, D_MODEL), lambda bb: (r0 + bb, 1)),
            pl.BlockSpec((dec_seq, RET_VW), lambda bb: (r0 + bb, 1)),
            pl.BlockSpec((dec_seq, RET_VW), lambda bb: (r0 + bb, 2)),
            pl.BlockSpec((1, RET_HEADS, RET_DK, RET_DV), lambda bb: (bb, 0, 0, 0)),
            pl.BlockSpec(dm.shape, const3),
            pl.BlockSpec(dq.shape, const3),
            pl.BlockSpec(dk.shape, const3),
            pl.BlockSpec(gr.shape, const3),
            pl.BlockSpec((1, RET_VW), lambda bb: (0, 0)),
        ],
        out_specs=[
            pl.BlockSpec((dec_seq, RET_VW), lambda bb: (bb, 0)),
            pl.BlockSpec((1, RET_HEADS, RET_DK, RET_DV), lambda bb: (bb, 0, 0, 0)),
        ],
        out_shape=[
            jax.ShapeDtypeStruct((b * dec_seq, RET_VW), BF16),
            jax.ShapeDtypeStruct(state.shape, F32),
        ],
        compiler_params=_params(("arbitrary",), 40),
        name="retention_sample",
    )(proj, proj, proj, proj, state, dm, dq, dk, gr, gn.reshape(1, RET_VW))


def _router_kernel(x_ref, g_ref, wr_ref, br_ref, xn_ref, id_ref, w_ref):
    xn = _rms(x_ref[...], g_ref[...])
    xn_ref[...] = xn
    xh = xn.astype(BF16)
    xl = (xn - xh.astype(F32)).astype(BF16)
    w = wr_ref[...]
    wh = w.astype(BF16)
    wl = (w - wh.astype(F32)).astype(BF16)
    lg = (jnp.dot(xh, wh, preferred_element_type=F32) + jnp.dot(xh, wl, preferred_element_type=F32)
          + jnp.dot(xl, wh, preferred_element_type=F32)) + br_ref[...]
    lane = lax.broadcasted_iota(jnp.int32, lg.shape, 1).astype(F32)
    neg = jnp.float32(-jnp.inf)
    big = jnp.float32(LANES)
    gl = jnp.where(lane < MOE_GROUPS, lg, neg)
    gmax = jnp.max(gl, axis=-1, keepdims=True)
    gsel = jnp.min(jnp.where(gl == gmax, lane, big), axis=-1, keepdims=True)
    gw = 1.0 / jnp.sum(jnp.exp(gl - gmax), axis=-1, keepdims=True)
    lo = MOE_GROUPS + gsel * MOE_PER_GROUP
    el = jnp.where((lane >= lo) & (lane < lo + MOE_PER_GROUP), lg, neg)
    m1 = jnp.max(el, axis=-1, keepdims=True)
    i1 = jnp.min(jnp.where(el == m1, lane, big), axis=-1, keepdims=True)
    el2 = jnp.where(lane == i1, neg, el)
    m2 = jnp.max(el2, axis=-1, keepdims=True)
    i2 = jnp.min(jnp.where(el2 == m2, lane, big), axis=-1, keepdims=True)
    z = jnp.sum(jnp.exp(el - m1), axis=-1, keepdims=True)
    p1 = 1.0 / z
    p2 = jnp.exp(m2 - m1) / z
    tot = p1 + p2
    w1 = p1 / tot * gw
    w2 = p2 / tot * gw
    ids = jnp.where(lane == 0, i1 - MOE_GROUPS, jnp.where(lane == 1, i2 - MOE_GROUPS, 0.0))
    id_ref[...] = ids.astype(jnp.int32)
    w_ref[...] = jnp.where(lane == 0, w1, jnp.where(lane == 1, w2, 0.0))


def _router_call(x, g, w_grp, b_grp, w_exp, b_exp):
    t, d = x.shape
    pad = LANES - MOE_GROUPS - N_EXP
    wr = jnp.concatenate([w_grp, w_exp, jnp.zeros((d, pad), F32)], axis=1)
    br = jnp.concatenate([b_grp, b_exp, jnp.zeros((pad,), F32)]).reshape(1, LANES)
    row = lambda i: (i, 0)
    fixed = lambda i: (0, 0)
    return pl.pallas_call(
        _router_kernel,
        grid=(t // ROW_TILE,),
        in_specs=[pl.BlockSpec((ROW_TILE, d), row), pl.BlockSpec((1, d), fixed),
                  pl.BlockSpec((d, LANES), fixed), pl.BlockSpec((1, LANES), fixed)],
        out_specs=[pl.BlockSpec((ROW_TILE, d), row), pl.BlockSpec((ROW_TILE, LANES), row),
                   pl.BlockSpec((ROW_TILE, LANES), row)],
        out_shape=[jax.ShapeDtypeStruct((t, d), F32), jax.ShapeDtypeStruct((t, LANES), jnp.int32),
                   jax.ShapeDtypeStruct((t, LANES), F32)],
        compiler_params=_params(("arbitrary",), 40),
        name="moe_router",
    )(x, g.reshape(1, d), wr, br)


def _row_tokens_kernel(pos_ref, tok_ref):
    def clear(r, carry):
        tok_ref[r] = 0
        return carry
    lax.fori_loop(0, tok_ref.shape[0], clear, 0, unroll=8)

    def place(a, carry):
        tok_ref[pos_ref[a]] = lax.shift_right_logical(a, 1)
        return carry
    lax.fori_loop(0, pos_ref.shape[0], place, 0, unroll=8)


def _row_tokens_call(pos, rows):
    smem = pl.BlockSpec(memory_space=pltpu.SMEM)
    return pl.pallas_call(
        _row_tokens_kernel,
        in_specs=[smem],
        out_specs=smem,
        out_shape=jax.ShapeDtypeStruct((rows,), jnp.int32),
        name="moe_row_tokens",
    )(pos)


def _moe_plan(eid):
    t = eid.shape[0]
    a = 2 * t
    n_tiles = a // MOE_TILE + N_EXP
    e_flat = eid.reshape(a)
    onehot = (e_flat[:, None] == jnp.arange(N_EXP, dtype=jnp.int32)[None, :]).astype(jnp.int32)
    csum = jnp.cumsum(onehot, axis=0)
    counts = csum[-1]
    rank = jnp.take_along_axis(csum, e_flat[:, None], axis=1)[:, 0] - 1
    tiles_e = (counts + MOE_TILE - 1) // MOE_TILE
    tiles_end = jnp.cumsum(tiles_e)
    row_start = (tiles_end - tiles_e) * MOE_TILE
    pos = row_start[e_flat] + rank
    n_valid = tiles_end[-1]
    tile_id = jnp.minimum(jnp.arange(n_tiles, dtype=jnp.int32), n_valid - 1)
    tile_expert = jnp.sum((tile_id[:, None] >= tiles_end[None, :]).astype(jnp.int32), axis=1)
    tile_expert = jnp.minimum(tile_expert, N_EXP - 1)
    changed = jnp.concatenate([jnp.ones((1,), bool), tile_expert[1:] != tile_expert[:-1]])
    first = jnp.logical_and(changed, jnp.arange(n_tiles) < n_valid).astype(jnp.int32)
    wslot = (jnp.cumsum(first) - 1) % 2
    next_start = tiles_end[tile_expert]
    next_expert = jnp.where(next_start < n_valid, tile_expert[jnp.minimum(next_start, n_tiles - 1)], -1)
    tile_meta = jnp.stack([tile_expert, first, wslot, next_expert]).astype(jnp.int32)
    pos = pos.astype(jnp.int32)
    tok_of_row = _row_tokens_call(pos, n_tiles * MOE_TILE)
    return tile_meta, n_valid.reshape(1).astype(jnp.int32), tok_of_row, pos


def _dispatch_kernel(nv_ref, tok_ref, x_hbm, o_ref, buf, sems):
    i = pl.program_id(0)
    nv = nv_ref[0]
    slot = lax.rem(i, 2)

    def gather(tile, dst_slot):
        base = tile * MOE_TILE
        for r in range(MOE_TILE):
            pltpu.make_async_copy(x_hbm.at[pl.ds(tok_ref[base + r], 1)], buf.at[dst_slot, pl.ds(r, 1)],
                                  sems.at[dst_slot]).start(priority=r % 2)

    @pl.when(i == 0)
    def _():
        gather(0, 0)

    for s in range(2):
        pl.when(jnp.logical_and(i + 1 < nv, slot == s))(functools.partial(gather, i + 1, 1 - s))

    @pl.when(i < nv)
    def _():
        pltpu.make_async_copy(x_hbm.at[pl.ds(0, MOE_TILE)], buf.at[slot], sems.at[slot]).wait()
        o_ref[...] = buf[slot].astype(o_ref.dtype)

    @pl.when(i >= nv)
    def _():
        o_ref[...] = jnp.zeros_like(o_ref)


def _dispatch_call(xn, n_valid, tok_of_row):
    t, d = xn.shape
    rows = tok_of_row.shape[0]
    grid_spec = pltpu.PrefetchScalarGridSpec(
        num_scalar_prefetch=2,
        grid=(rows // MOE_TILE,),
        in_specs=[pl.BlockSpec(memory_space=pl.ANY)],
        out_specs=pl.BlockSpec((MOE_TILE, d), lambda i, nv, tok: (i, 0)),
        scratch_shapes=[pltpu.VMEM((2, MOE_TILE, d), F32), pltpu.SemaphoreType.DMA((2,))],
    )
    return pl.pallas_call(
        _dispatch_kernel,
        grid_spec=grid_spec,
        out_shape=jax.ShapeDtypeStruct((rows, d), BF16),
        compiler_params=_params(("arbitrary",), 32),
        name="moe_dispatch",
    )(n_valid, tok_of_row, xn)


def _moe_ffn_kernel(tm_ref, nv_ref, x_ref, wgu_hbm, wd_hbm, o_ref,
                    wgu_f, wd_f, wsems, wgu_b, wd_b, *, layer):
    i = pl.program_id(0)
    nv = nv_ref[0]
    expert, first, wslot, next_expert = tm_ref[0, i], tm_ref[1, i], tm_ref[2, i], tm_ref[3, i]

    def weight_copies(e, s):
        return (pltpu.make_async_copy(wgu_hbm.at[layer, e], wgu_f.at[s], wsems.at[0, s]),
                pltpu.make_async_copy(wd_hbm.at[layer, e], wd_f.at[s], wsems.at[1, s]))

    @pl.when(i == 0)
    def _():
        for c in weight_copies(expert, 0):
            c.start()

    @pl.when(first == 1)
    def _():
        for c in weight_copies(expert, wslot):
            c.wait()
        wgu_b[...] = wgu_f[wslot].astype(BF16)
        wd_b[...] = wd_f[wslot].astype(BF16)

        @pl.when(next_expert >= 0)
        def _():
            for c in weight_copies(next_expert, 1 - wslot):
                c.start()

    @pl.when(i < nv)
    def _():
        h = jnp.dot(x_ref[...], wgu_b[...], preferred_element_type=F32)
        hg = h[:, :D_FF_E]
        hu = h[:, D_FF_E:]
        act = (hg * _sigmoid(hg) * hu).astype(BF16)
        o_ref[...] = jnp.dot(act, wd_b[...], preferred_element_type=F32)

    @pl.when(i >= nv)
    def _():
        o_ref[...] = jnp.zeros_like(o_ref)


def _moe_ffn_call(xs, w_gu, w_down, layer, tile_meta, n_valid):
    rows, d = xs.shape
    n_tiles = tile_meta.shape[1]
    grid_spec = pltpu.PrefetchScalarGridSpec(
        num_scalar_prefetch=2,
        grid=(n_tiles,),
        in_specs=[pl.BlockSpec((MOE_TILE, d), lambda i, tm, nv: (jnp.minimum(i, nv[0] - 1), 0)),
                  pl.BlockSpec(memory_space=pl.ANY), pl.BlockSpec(memory_space=pl.ANY)],
        out_specs=pl.BlockSpec((MOE_TILE, d), lambda i, tm, nv: (i, 0)),
        scratch_shapes=[
            pltpu.VMEM((2, d, 2 * D_FF_E), F32),
            pltpu.VMEM((2, D_FF_E, d), F32),
            pltpu.SemaphoreType.DMA((2, 2)),
            pltpu.VMEM((d, 2 * D_FF_E), BF16),
            pltpu.VMEM((D_FF_E, d), BF16),
        ],
    )
    return pl.pallas_call(
        functools.partial(_moe_ffn_kernel, layer=layer),
        grid_spec=grid_spec,
        out_shape=jax.ShapeDtypeStruct((rows, d), F32),
        compiler_params=_params(("arbitrary",), 48),
        name="moe_experts",
    )(tile_meta, n_valid, xs, w_gu, w_down)


def _combine_kernel(pos_ref, x_ref, w_ref, y_hbm, g_ref, *rest, emit_x, norm_tiles):
    outs = rest[:int(emit_x) + len(norm_tiles)]
    buf, sems = rest[int(emit_x) + len(norm_tiles):]
    i = pl.program_id(0)
    n = pl.num_programs(0)
    slot = lax.rem(i, 2)
    tm = x_ref.shape[0]

    def gather(tile, dst_slot):
        base = tile * (2 * tm)
        for r in range(tm):
            for kk in range(2):
                pltpu.make_async_copy(y_hbm.at[pl.ds(pos_ref[base + 2 * r + kk], 1)],
                                      buf.at[dst_slot, kk, pl.ds(r, 1)], sems.at[dst_slot]).start(priority=kk)

    @pl.when(i == 0)
    def _():
        gather(0, 0)

    for s in range(2):
        pl.when(jnp.logical_and(i + 1 < n, slot == s))(functools.partial(gather, i + 1, 1 - s))

    for kk in range(2):
        pltpu.make_async_copy(y_hbm.at[pl.ds(0, tm)], buf.at[slot, kk], sems.at[slot]).wait()
    xo = x_ref[...] + (w_ref[:, 0:1] * buf[slot, 0] + w_ref[:, 1:2] * buf[slot, 1])
    norm_refs = outs[int(emit_x):]
    if emit_x:
        outs[0][...] = xo

    def write_norm(p):
        norm_refs[p][...] = _rms(xo, g_ref[...]).astype(norm_refs[p].dtype)

    _for_each_part(i, norm_tiles, write_norm)


def _combine_call(x, ws, y_rows, pos, g, emit_x, norm_dtype, norm_rows):
    t, d = x.shape
    tm = MOE_TILE
    row = lambda i, pos_ref: (i, 0)
    out_specs, out_shape = [], []
    if emit_x:
        out_specs.append(pl.BlockSpec((tm, d), row))
        out_shape.append(jax.ShapeDtypeStruct((t, d), F32))
    norm_parts = [jax.ShapeDtypeStruct((r, d), norm_dtype) for r in norm_rows]
    out_specs += _part_specs(norm_parts, tm, d, row)
    out_shape += norm_parts
    grid_spec = pltpu.PrefetchScalarGridSpec(
        num_scalar_prefetch=1,
        grid=(t // tm,),
        in_specs=[pl.BlockSpec((tm, d), row), pl.BlockSpec((tm, LANES), row), pl.BlockSpec(memory_space=pl.ANY),
                  pl.BlockSpec((1, d), lambda i, pos_ref: (0, 0))],
        out_specs=out_specs,
        scratch_shapes=[pltpu.VMEM((2, 2, tm, d), F32), pltpu.SemaphoreType.DMA((2,))],
    )
    return pl.pallas_call(
        functools.partial(_combine_kernel, emit_x=emit_x, norm_tiles=_part_tiles(norm_parts, tm)),
        grid_spec=grid_spec,
        out_shape=out_shape,
        compiler_params=_params(("arbitrary",), 40),
        name="moe_combine",
    )(pos, x, ws, y_rows, g.reshape(1, d))


def _moe_layer(x, g_ffn, w_grp, b_grp, w_exp, b_exp, w_gu, w_down, layer, g_next, emit_x, norm_dtype, norm_rows):
    xn, ids, ws = _router_call(x, g_ffn, w_grp, b_grp, w_exp, b_exp)
    tile_meta, n_valid, tok_of_row, pos = _moe_plan(ids[:, :2])
    xs = _dispatch_call(xn, n_valid, tok_of_row)
    y_rows = _moe_ffn_call(xs, w_gu, w_down, layer, tile_meta, n_valid)
    return _combine_call(x, ws, y_rows, pos, g_next, emit_x, norm_dtype, norm_rows)


def _s5_prep_kernel(are_ref, aim_ref, ldt_ref, bre_ref, bim_ref, lre_ref, lim_ref, bbr_ref, bbi_ref):
    are = are_ref[...]
    aim = aim_ref[...]
    dt = jnp.exp(ldt_ref[...])
    mag = jnp.exp(are * dt)
    lr = mag * jnp.cos(aim * dt)
    li = mag * jnp.sin(aim * dt)
    lre_ref[...] = lr
    lim_ref[...] = li
    den = are * are + aim * aim
    nr = lr - 1.0
    cr = (nr * are + li * aim) / den
    ci = (li * are - nr * aim) / den
    bre = bre_ref[...]
    bim = bim_ref[...]
    bbr_ref[...] = cr * bre - ci * bim
    bbi_ref[...] = cr * bim + ci * bre


def _s5_prep_call(a_re, a_im, log_dt, b_re, b_im):
    g, p, c = b_re.shape
    shp = (g, p * c)
    expand = lambda a: jnp.broadcast_to(a[:, :, None], (g, p, c)).reshape(shp)
    ldt = jnp.broadcast_to(log_dt[:, None], shp)
    outs = pl.pallas_call(
        _s5_prep_kernel,
        out_shape=[jax.ShapeDtypeStruct(shp, F32)] * 4,
        name="s5_discretize",
    )(expand(a_re), expand(a_im), ldt, b_re.reshape(shp), b_im.reshape(shp))
    lre, lim, bbr, bbi = outs
    return lre[:, ::c], lim[:, ::c], bbr, bbi


def _s5_mats(bbr, bbi, c_re, c_im):
    eye = jnp.eye(16, dtype=F32)

    def bd_part(bb):
        xx = bb.reshape(S5_FB, 16, S5_P, S5_GC).transpose(0, 1, 3, 2)
        yy = xx[:, :, :, None, :] * eye[None, :, None, :, None]
        return yy.reshape(S5_FB, 16 * S5_GC, 16 * S5_P)

    def cd_part(cc):
        xx = cc.reshape(S5_FB, 16, S5_GC, S5_P).transpose(0, 1, 3, 2)
        yy = xx[:, :, :, None, :] * eye[None, :, None, :, None]
        return yy.reshape(S5_FB, 16 * S5_P, 16 * S5_GC)

    bd = jnp.concatenate([bd_part(bbr), bd_part(bbi)], axis=-1).astype(BF16)
    cd = jnp.concatenate([cd_part(c_re), -cd_part(c_im)], axis=1).astype(BF16)
    return bd, cd


def _s5_kernel(u_ref, bd_ref, cd_ref, lre_ref, lim_ref, d_ref, h0r_ref, h0i_ref,
               z_ref, hr_ref, hi_ref, scr, hst, *, seg, carry):
    rows = u_ref.shape[0]
    n_seq = rows // seg
    ch = 16 * S5_GC
    half = SUBLANES * S5_PITCH
    blk = 2 * half

    for fb in range(S5_FB):
        bu = jnp.dot(u_ref[:, fb * ch:(fb + 1) * ch].astype(BF16), bd_ref[fb], preferred_element_type=F32)
        for c in range(2 * SUBLANES):
            r0 = fb * blk + c * S5_PITCH
            scr[r0:r0 + rows, :] = bu[:, c * LANES:(c + 1) * LANES]

    lre = [lre_ref[fb * SUBLANES:(fb + 1) * SUBLANES, :] for fb in range(S5_FB)]
    lim = [lim_ref[fb * SUBLANES:(fb + 1) * SUBLANES, :] for fb in range(S5_FB)]

    def scan(row0, state):
        def step(l, st):
            new = []
            for fb in range(S5_FB):
                hr, hi = st[2 * fb], st[2 * fb + 1]
                ire = pl.ds(fb * blk + row0 + l, SUBLANES, stride=S5_PITCH)
                iim = pl.ds(fb * blk + half + row0 + l, SUBLANES, stride=S5_PITCH)
                nr = lre[fb] * hr - lim[fb] * hi + scr[ire, :]
                ni = lre[fb] * hi + lim[fb] * hr + scr[iim, :]
                scr[ire, :] = nr
                scr[iim, :] = ni
                new += [nr, ni]
            return tuple(new)
        return lax.fori_loop(0, seg, step, state)

    def load_state(rref, iref, lead):
        st = []
        for fb in range(S5_FB):
            st += [rref[lead + (slice(fb * SUBLANES, (fb + 1) * SUBLANES), slice(None))],
                   iref[lead + (slice(fb * SUBLANES, (fb + 1) * SUBLANES), slice(None))]]
        return tuple(st)

    def store_state(st, rref, iref, lead):
        for fb in range(S5_FB):
            rref[lead + (slice(fb * SUBLANES, (fb + 1) * SUBLANES), slice(None))] = st[2 * fb]
            iref[lead + (slice(fb * SUBLANES, (fb + 1) * SUBLANES), slice(None))] = st[2 * fb + 1]

    if carry:
        @pl.when(pl.program_id(0) == 0)
        def _():
            hst[0] = h0r_ref[0]
            hst[1] = h0i_ref[0]
        hre_scr = hst.at[0]
        him_scr = hst.at[1]
        st = scan(0, load_state(hre_scr, him_scr, ()))
        store_state(st, hre_scr, him_scr, ())
        store_state(st, hr_ref, hi_ref, (0,))
    else:
        for s in range(n_seq):
            st = scan(s * seg, load_state(h0r_ref, h0i_ref, (s,)))
            store_state(st, hr_ref, hi_ref, (s,))

    for fb in range(S5_FB):
        hf = jnp.concatenate(
            [scr[fb * blk + c * S5_PITCH: fb * blk + c * S5_PITCH + rows, :] for c in range(2 * SUBLANES)],
            axis=1).astype(BF16)
        y = jnp.dot(hf, cd_ref[fb], preferred_element_type=F32)
        cs = slice(fb * ch, (fb + 1) * ch)
        yy = y + d_ref[:, cs] * u_ref[:, cs]
        z_ref[:, cs] = jax.nn.gelu(yy).astype(z_ref.dtype)


def _s5_call(u, bd, cd, lre, lim, d_skip, h0r, h0i, row_block0, n_blocks, seg, carry):
    t, d = u.shape
    n_seq = S5_ROWS // seg
    ns_total = h0r.shape[0]
    st_idx = (lambda i: (0, 0, 0)) if carry else (lambda i: (i, 0, 0))
    fixed2 = lambda i: (0, 0)
    fixed3 = lambda i: (0, 0, 0)
    st_rows = S5_FB * SUBLANES
    return pl.pallas_call(
        functools.partial(_s5_kernel, seg=seg, carry=carry),
        grid=(n_blocks,),
        in_specs=[
            pl.BlockSpec((S5_ROWS, d), lambda i: (row_block0 + i, 0)),
            pl.BlockSpec(bd.shape, fixed3),
            pl.BlockSpec(cd.shape, fixed3),
            pl.BlockSpec(lre.shape, fixed2),
            pl.BlockSpec(lim.shape, fixed2),
            pl.BlockSpec((1, d), fixed2),
            pl.BlockSpec((n_seq, st_rows, LANES), st_idx),
            pl.BlockSpec((n_seq, st_rows, LANES), st_idx),
        ],
        out_specs=[
            pl.BlockSpec((S5_ROWS, d), lambda i: (i, 0)),
            pl.BlockSpec((n_seq, st_rows, LANES), st_idx),
            pl.BlockSpec((n_seq, st_rows, LANES), st_idx),
        ],
        out_shape=[
            jax.ShapeDtypeStruct((n_blocks * S5_ROWS, d), BF16),
            jax.ShapeDtypeStruct((ns_total, st_rows, LANES), F32),
            jax.ShapeDtypeStruct((ns_total, st_rows, LANES), F32),
        ],
        scratch_shapes=[
            pltpu.VMEM((2 * SUBLANES * S5_FB * S5_PITCH, LANES), F32),
            pltpu.VMEM((2, st_rows, LANES), F32),
        ],
        compiler_params=_params(("arbitrary",), 56),
        name="s5_scan",
    )(u, bd, cd, lre, lim, d_skip.reshape(1, d), h0r, h0i)


def kernel(x_prompt, x_sample, state_ret, state_s5_re, state_s5_im, norm_mix_g, norm_ffn_g, norm_final_g,
           ret_w_in, ret_gn_g, ret_w_out, s5_w_in, s5_a_re, s5_a_im, s5_log_dt, s5_b_re, s5_b_im,
           s5_c_re, s5_c_im, s5_d, s5_w_out, moe_w_grp, moe_b_grp, moe_w_exp, moe_b_exp, moe_w_gu, moe_w_down):
    bp, seq, d = x_prompt.shape
    db, dseq, _ = x_sample.shape
    assert d == D_MODEL and bp == 1
    assert seq % max(ROW_TILE, RET_ROWS) == 0 and seq % CHUNK == 0
    assert dseq < CHUNK and dseq % 16 == 0 and S5_ROWS % dseq == 0
    tp = bp * seq
    ts = db * dseq
    t = tp + ts
    assert ts % ROW_TILE == 0 and ts % S5_ROWS == 0

    x_parts = [x_prompt.reshape(tp, d), x_sample.reshape(ts, d)]

    half = RET_DK // 2
    freqs = ROPE_BASE ** (-jnp.arange(half, dtype=F32) / half)
    pos = jnp.concatenate([jnp.arange(seq, dtype=F32), jnp.tile(PAST_LEN + jnp.arange(dseq, dtype=F32), db)])
    ang = pos[:, None] * freqs[None, :]
    cos_t = jnp.cos(ang)
    sin_t = jnp.sin(ang)

    xn = _norm_call(x_parts, norm_mix_g[0], BF16)
    n_proj = 2 * D_MODEL + 2 * RET_VW
    tn = 1024
    first_col = lambda j: 0
    proj = _mm_call([xn], ret_w_in[0], [0], tn, n_proj, [([cos_t], half, first_col), ([sin_t], half, first_col)],
                    BF16, functools.partial(_epi_rope, n_qk_blocks=2 * D_MODEL // tn), 40, "ret_in_proj")
    og_p, ret_p = _ret_prompt_call(proj, ret_gn_g[0], tp)
    og_s, ret_s = _ret_sample_call(proj, ret_gn_g[0], state_ret[0], tp, dseq)
    tn = 512
    same_col = lambda j: j
    x1 = _mm_call([og_p, og_s], ret_w_out[0], [0], tn, d, [(x_parts, tn, same_col)], F32, _epi_residual,
                  40, "ret_out_proj")

    x2, xn = _moe_layer(x1, norm_ffn_g[0], moe_w_grp[0], moe_b_grp[0], moe_w_exp[0], moe_b_exp[0],
                        moe_w_gu, moe_w_down, 0, norm_mix_g[1], True, BF16, (t,))

    u = _mm_call([xn], s5_w_in[0], [0], 1024, d, [], F32, _epi_plain, 40, "s5_in_proj")
    lre, lim, bbr, bbi = _s5_prep_call(s5_a_re[0], s5_a_im[0], s5_log_dt[0], s5_b_re[0], s5_b_im[0])
    bd, cd = _s5_mats(bbr, bbi, s5_c_re[0], s5_c_im[0])
    st_shape = (S5_FB * SUBLANES, LANES)
    lre = lre.reshape(st_shape)
    lim = lim.reshape(st_shape)
    zero_state = jnp.zeros((1,) + st_shape, F32)
    z_p, hpr, hpi = _s5_call(u, bd, cd, lre, lim, s5_d[0], zero_state, zero_state,
                             0, tp // S5_ROWS, S5_ROWS, True)
    z_s, hsr, hsi = _s5_call(u, bd, cd, lre, lim, s5_d[0],
                             state_s5_re[0].reshape((db,) + st_shape), state_s5_im[0].reshape((db,) + st_shape),
                             tp // S5_ROWS, ts // S5_ROWS, dseq, False)
    x3 = _mm_call([z_p, z_s], s5_w_out[0], [0, d // tn], tn, d, [([x2], tn, same_col)], F32, _epi_glu_residual,
                  40, "s5_out_proj")

    y_p, y_s = _moe_layer(x3, norm_ffn_g[1], moe_w_grp[1], moe_b_grp[1], moe_w_exp[1], moe_b_exp[1],
                          moe_w_gu, moe_w_down, 1, norm_final_g, False, F32, (tp, ts))

    s5_shape = (1, -1, S5_GROUPS, S5_P)
    return (y_p.reshape(bp, seq, d), y_s.reshape(db, dseq, d),
            ret_p.reshape(1, bp, RET_HEADS, RET_DK, RET_DV), ret_s.reshape((1,) + ret_s.shape),
            hpr.reshape(s5_shape), hpi.reshape(s5_shape), hsr.reshape(s5_shape), hsi.reshape(s5_shape))
```

```python
import functools
import math

import numpy as np
import jax
import jax.numpy as jnp
from jax import lax
from jax.experimental import pallas as pl
from jax.experimental.pallas import tpu as pltpu

F32 = jnp.float32
BF16 = jnp.bfloat16

D_MODEL = 2048
PAST_LEN = 2048
CHUNK = 64
RET_HEADS = 8
RET_DK = D_MODEL // RET_HEADS
RET_DV = 2 * RET_DK
RET_VW = RET_HEADS * RET_DV
ROPE_BASE = 10000.0
S5_GC = 16
S5_GROUPS = D_MODEL // S5_GC
S5_P = 64
MOE_GROUPS = 4
MOE_PER_GROUP = 8
N_EXP = MOE_GROUPS * MOE_PER_GROUP
D_FF_E = 512
EPS = 1e-6

LANES = 128
SUBLANES = 8
ROW_TILE = 512
RET_ROWS = 256
MOE_TILE = 256
S5_ROWS = 128
S5_FB = S5_GROUPS // 16
S5_PITCH = S5_ROWS + 8
MIB = 1024 * 1024


def _params(sem, vmem_mib):
    return pltpu.CompilerParams(dimension_semantics=sem, vmem_limit_bytes=vmem_mib * MIB)


def _rms(x, g):
    ms = jnp.mean(x * x, axis=-1, keepdims=True)
    return x * lax.rsqrt(ms + EPS) * g


def _sigmoid(x):
    return 1.0 / (1.0 + jnp.exp(-x))


def _part_specs(parts, rows, cols, rowcol):
    specs = []
    start = 0
    for p in parts:
        nt = p.shape[0] // rows

        def imap(*g, start=start, nt=nt):
            i, jc = rowcol(*g)
            return (jnp.clip(i - start, 0, nt - 1), jc)

        specs.append(pl.BlockSpec((rows, cols), imap))
        start += nt
    return specs


def _part_tiles(parts, rows):
    return tuple(p.shape[0] // rows for p in parts)


def _for_each_part(i, tiles, fn):
    if len(tiles) == 1:
        fn(0)
        return
    start = 0
    for p, nt in enumerate(tiles):
        pl.when(jnp.logical_and(i >= start, i < start + nt))(functools.partial(fn, p))
        start += nt


def _pick(refs, p):
    return refs[min(p, len(refs) - 1)]


def _norm_kernel(*refs, tiles):
    n = len(tiles)
    x_refs, g_ref, o_ref = refs[:n], refs[n], refs[n + 1]

    def run(p):
        o_ref[...] = _rms(x_refs[p][...], g_ref[...]).astype(o_ref.dtype)

    _for_each_part(pl.program_id(0), tiles, run)


def _norm_call(x_parts, g, out_dtype):
    d = x_parts[0].shape[1]
    t = sum(p.shape[0] for p in x_parts)
    return pl.pallas_call(
        functools.partial(_norm_kernel, tiles=_part_tiles(x_parts, ROW_TILE)),
        grid=(t // ROW_TILE,),
        in_specs=_part_specs(x_parts, ROW_TILE, d, lambda i: (i, 0)) + [pl.BlockSpec((1, d), lambda i: (0, 0))],
        out_specs=pl.BlockSpec((ROW_TILE, d), lambda i: (i, 0)),
        out_shape=jax.ShapeDtypeStruct((t, d), out_dtype),
        compiler_params=_params(("arbitrary",), 32),
        name="rmsnorm",
    )(*x_parts, g.reshape(1, d))


def _mm_kernel(*refs, tiles, n_x, n_w, n_extra, epi):
    x_refs = refs[:n_x]
    w_refs = refs[n_x:n_x + n_w]
    pos = n_x + n_w
    extra = []
    for n in n_extra:
        extra.append(refs[pos:pos + n])
        pos += n
    o_ref = refs[pos]
    wb = refs[pos + 1:]

    @pl.when(pl.program_id(1) == 0)
    def _():
        for w_ref, b in zip(w_refs, wb):
            b[...] = w_ref[...].astype(BF16)

    def run(p):
        x = _pick(x_refs, p)[...]
        accs = [jnp.dot(x, b[...], preferred_element_type=F32) for b in wb]
        epi(accs, [_pick(e, p) for e in extra], o_ref)

    _for_each_part(pl.program_id(1), tiles, run)


def _mm_call(x_parts, w, w_col_offsets, tn, n_out, extras, out_dtype, epi, vmem_mib, name):
    k = x_parts[0].shape[1]
    t = sum(p.shape[0] for p in x_parts)
    n_w = len(w_col_offsets)
    split = [parts for parts in [x_parts] + [e[0] for e in extras] if len(parts) > 1]
    tiles = _part_tiles(split[0], ROW_TILE) if split else (t // ROW_TILE,)
    assert all(_part_tiles(parts, ROW_TILE) == tiles for parts in split)
    in_specs = _part_specs(x_parts, ROW_TILE, k, lambda j, i: (i, 0))
    for off in w_col_offsets:
        in_specs.append(pl.BlockSpec((k, tn), lambda j, i, off=off: (0, j + off)))
    extra_arrays = []
    for parts, cols, colfn in extras:
        in_specs += _part_specs(parts, ROW_TILE, cols, lambda j, i, colfn=colfn: (i, colfn(j)))
        extra_arrays += list(parts)
    return pl.pallas_call(
        functools.partial(_mm_kernel, tiles=tiles, n_x=len(x_parts), n_w=n_w,
                          n_extra=tuple(len(e[0]) for e in extras), epi=epi),
        grid=(n_out // tn, t // ROW_TILE),
        in_specs=in_specs,
        out_specs=pl.BlockSpec((ROW_TILE, tn), lambda j, i: (i, j)),
        out_shape=jax.ShapeDtypeStruct((t, n_out), out_dtype),
        scratch_shapes=[pltpu.VMEM((k, tn), BF16) for _ in range(n_w)],
        compiler_params=_params(("arbitrary", "arbitrary"), vmem_mib),
        name=name,
    )(*x_parts, *([w] * n_w), *extra_arrays)


def _epi_plain(accs, extra, o_ref):
    o_ref[...] = accs[0].astype(o_ref.dtype)


def _epi_residual(accs, extra, o_ref):
    o_ref[...] = (extra[0][...] + accs[0]).astype(o_ref.dtype)


def _epi_glu_residual(accs, extra, o_ref):
    a, gt = accs
    o_ref[...] = (extra[0][...] + a * _sigmoid(gt)).astype(o_ref.dtype)


def _epi_rope(accs, extra, o_ref, *, n_qk_blocks):
    acc = accs[0]
    cos_ref, sin_ref = extra
    j = pl.program_id(0)
    half = RET_DK // 2

    @pl.when(j < n_qk_blocks)
    def _():
        scale = jnp.where(j >= n_qk_blocks // 2, RET_DK ** -0.5, 1.0).astype(F32)
        c = cos_ref[...]
        s = sin_ref[...]
        for hh in range(acc.shape[1] // RET_DK):
            lo = hh * RET_DK
            t1 = acc[:, lo:lo + half]
            t2 = acc[:, lo + half:lo + RET_DK]
            o_ref[:, lo:lo + half] = ((t1 * c - t2 * s) * scale).astype(o_ref.dtype)
            o_ref[:, lo + half:lo + RET_DK] = ((t1 * s + t2 * c) * scale).astype(o_ref.dtype)

    @pl.when(j >= n_qk_blocks)
    def _():
        o_ref[...] = acc.astype(o_ref.dtype)


def _ret_tables(rows, chunk):
    lg = jnp.log(jnp.asarray(1.0 - 2.0 ** (-5.0 - np.arange(RET_HEADS)), dtype=F32))[:, None, None]
    n = jnp.arange(rows, dtype=F32)
    ci = np.arange(rows) // chunk
    same_or_earlier = jnp.asarray(ci[None, :] <= ci[:, None])
    dmat = jnp.where(same_or_earlier[None], jnp.exp(jnp.abs(n[:, None] - n[None, :])[None] * lg), 0.0)
    dq = jnp.exp((n + 1.0)[None, :, None] * lg) * jnp.ones((1, 1, LANES), F32)
    dk = jnp.exp((rows - 1.0 - n)[None, :, None] * lg) * jnp.ones((1, 1, LANES), F32)
    gr = jnp.exp(rows * lg) * jnp.ones((1, 1, RET_DV), F32)
    return dmat, dq, dk, gr


def _ret_block(q, k, v, g, s_prev, dmat, dq, dk, gr, gn):
    s = lax.dot_general(q, k, (((1,), (1,)), ((), ())), preferred_element_type=F32) * dmat
    inter = jnp.dot(q, s_prev.astype(BF16), preferred_element_type=F32)
    o = jnp.dot(s.astype(BF16), v, preferred_element_type=F32) + jnp.concatenate([dq] * (RET_DV // LANES), axis=1) * inter
    kd = (k.astype(F32) * jnp.concatenate([dk] * (RET_DK // LANES), axis=1)).astype(BF16)
    s_new = s_prev * gr + lax.dot_general(kd, v, (((0,), (0,)), ((), ())), preferred_element_type=F32)
    mu = jnp.mean(o, axis=-1, keepdims=True)
    oc = o - mu
    var = jnp.mean(oc * oc, axis=-1, keepdims=True)
    on = oc * lax.rsqrt(var + EPS) * gn
    gf = g.astype(F32)
    return (gf * _sigmoid(gf) * on).astype(BF16), s_new


def _ret_prompt_kernel(q_ref, k_ref, v_ref, g_ref, dm_ref, dq_ref, dk_ref, gr_ref, gn_ref,
                       og_ref, st_ref, s_scr):
    @pl.when(pl.program_id(1) == 0)
    def _():
        s_scr[...] = jnp.zeros_like(s_scr)

    og, s_new = _ret_block(q_ref[...], k_ref[...], v_ref[...], g_ref[...], s_scr[...],
                           dm_ref[0], dq_ref[0], dk_ref[0], gr_ref[0], gn_ref[...])
    og_ref[...] = og
    s_scr[...] = s_new
    st_ref[0] = s_new


def _ret_prompt_call(proj, gn, t_prompt):
    dm, dq, dk, gr = _ret_tables(RET_ROWS, CHUNK)
    h = RET_HEADS
    kv_blk = 2 * D_MODEL // RET_DV
    g_blk = (2 * D_MODEL + RET_VW) // RET_DV
    return pl.pallas_call(
        _ret_prompt_kernel,
        grid=(h, t_prompt // RET_ROWS),
        in_specs=[
            pl.BlockSpec((RET_ROWS, RET_DK), lambda hh, c: (c, hh)),
            pl.BlockSpec((RET_ROWS, RET_DK), lambda hh, c: (c, h + hh)),
            pl.BlockSpec((RET_ROWS, RET_DV), lambda hh, c: (c, kv_blk + hh)),
            pl.BlockSpec((RET_ROWS, RET_DV), lambda hh, c: (c, g_blk + hh)),
            pl.BlockSpec((1, RET_ROWS, RET_ROWS), lambda hh, c: (hh, 0, 0)),
            pl.BlockSpec((1, RET_ROWS, LANES), lambda hh, c: (hh, 0, 0)),
            pl.BlockSpec((1, RET_ROWS, LANES), lambda hh, c: (hh, 0, 0)),
            pl.BlockSpec((1, 1, RET_DV), lambda hh, c: (hh, 0, 0)),
            pl.BlockSpec((1, RET_DV), lambda hh, c: (0, hh)),
        ],
        out_specs=[
            pl.BlockSpec((RET_ROWS, RET_DV), lambda hh, c: (c, hh)),
            pl.BlockSpec((1, RET_DK, RET_DV), lambda hh, c: (hh, 0, 0)),
        ],
        out_shape=[
            jax.ShapeDtypeStruct((t_prompt, RET_VW), BF16),
            jax.ShapeDtypeStruct((h, RET_DK, RET_DV), F32),
        ],
        scratch_shapes=[pltpu.VMEM((RET_DK, RET_DV), F32)],
        compiler_params=_params(("arbitrary", "arbitrary"), 32),
        name="retention_prompt",
    )(proj, proj, proj, proj, dm, dq, dk, gr, gn.reshape(1, RET_VW))


def _ret_sample_kernel(q_ref, k_ref, v_ref, g_ref, s0_ref, dm_ref, dq_ref, dk_ref, gr_ref, gn_ref,
                       og_ref, st_ref):
    for hh in range(RET_HEADS):
        ks = slice(hh * RET_DK, (hh + 1) * RET_DK)
        vs = slice(hh * RET_DV, (hh + 1) * RET_DV)
        og, s_new = _ret_block(q_ref[:, ks], k_ref[:, ks], v_ref[:, vs], g_ref[:, vs], s0_ref[0, hh],
                               dm_ref[hh], dq_ref[hh], dk_ref[hh], gr_ref[hh], gn_ref[:, vs])
        og_ref[:, vs] = og
        st_ref[0, hh] = s_new


def _ret_sample_call(proj, gn, state, t_prompt, dec_seq):
    dm, dq, dk, gr = _ret_tables(dec_seq, dec_seq)
    b = state.shape[0]
    r0 = t_prompt // dec_seq
    const3 = lambda bb: (0, 0, 0)
    return pl.pallas_call(
        _ret_sample_kernel,
        grid=(b,),
        in_specs=[
            pl.BlockSpec((dec_seq, D_MODEL), lambda bb: (r0 + bb, 0)),
            pl.BlockSpec((dec_seq, D_MODEL), lambda bb: (r0 + bb, 1)),
            pl.BlockSpec((dec_seq, RET_VW), lambda bb: (r0 + bb, 1)),
            pl.BlockSpec((dec_seq, RET_VW), lambda bb: (r0 + bb, 2)),
            pl.BlockSpec((1, RET_HEADS, RET_DK, RET_DV), lambda bb: (bb, 0, 0, 0)),
            pl.BlockSpec(dm.shape, const3),
            pl.BlockSpec(dq.shape, const3),
            pl.BlockSpec(dk.shape, const3),
            pl.BlockSpec(gr.shape, const3),
            pl.BlockSpec((1, RET_VW), lambda bb: (0, 0)),
        ],
        out_specs=[
            pl.BlockSpec((dec_seq, RET_VW), lambda bb: (bb, 0)),
            pl.BlockSpec((1, RET_HEADS, RET_DK, RET_DV), lambda bb: (bb, 0, 0, 0)),
        ],
        out_shape=[
            jax.ShapeDtypeStruct((b * dec_seq, RET_VW), BF16),
            jax.ShapeDtypeStruct(state.shape, F32),
        ],
        compiler_params=_params(("arbitrary",), 40),
        name="retention_sample",
    )(proj, proj, proj, proj, state, dm, dq, dk, gr, gn.reshape(1, RET_VW))


def _router_kernel(x_ref, g_ref, wr_ref, br_ref, xn_ref, id_ref, w_ref):
    xn = _rms(x_ref[...], g_ref[...])
    xn_ref[...] = xn
    xh = xn.astype(BF16)
    xl = (xn - xh.astype(F32)).astype(BF16)
    w = wr_ref[...]
    wh = w.astype(BF16)
    wl = (w - wh.astype(F32)).astype(BF16)
    lg = (jnp.dot(xh, wh, preferred_element_type=F32) + jnp.dot(xh, wl, preferred_element_type=F32)
          + jnp.dot(xl, wh, preferred_element_type=F32)) + br_ref[...]
    lane = lax.broadcasted_iota(jnp.int32, lg.shape, 1).astype(F32)
    neg = jnp.float32(-jnp.inf)
    big = jnp.float32(LANES)
    gl = jnp.where(lane < MOE_GROUPS, lg, neg)
    gmax = jnp.max(gl, axis=-1, keepdims=True)
    gsel = jnp.min(jnp.where(gl == gmax, lane, big), axis=-1, keepdims=True)
    gw = 1.0 / jnp.sum(jnp.exp(gl - gmax), axis=-1, keepdims=True)
    lo = MOE_GROUPS + gsel * MOE_PER_GROUP
    el = jnp.where((lane >= lo) & (lane < lo + MOE_PER_GROUP), lg, neg)
    m1 = jnp.max(el, axis=-1, keepdims=True)
    i1 = jnp.min(jnp.where(el == m1, lane, big), axis=-1, keepdims=True)
    el2 = jnp.where(lane == i1, neg, el)
    m2 = jnp.max(el2, axis=-1, keepdims=True)
    i2 = jnp.min(jnp.where(el2 == m2, lane, big), axis=-1, keepdims=True)
    z = jnp.sum(jnp.exp(el - m1), axis=-1, keepdims=True)
    p1 = 1.0 / z
    p2 = jnp.exp(m2 - m1) / z
    tot = p1 + p2
    w1 = p1 / tot * gw
    w2 = p2 / tot * gw
    ids = jnp.where(lane == 0, i1 - MOE_GROUPS, jnp.where(lane == 1, i2 - MOE_GROUPS, 0.0))
    id_ref[...] = ids.astype(jnp.int32)
    w_ref[...] = jnp.where(lane == 0, w1, jnp.where(lane == 1, w2, 0.0))


def _router_call(x, g, w_grp, b_grp, w_exp, b_exp):
    t, d = x.shape
    pad = LANES - MOE_GROUPS - N_EXP
    wr = jnp.concatenate([w_grp, w_exp, jnp.zeros((d, pad), F32)], axis=1)
    br = jnp.concatenate([b_grp, b_exp, jnp.zeros((pad,), F32)]).reshape(1, LANES)
    row = lambda i: (i, 0)
    fixed = lambda i: (0, 0)
    return pl.pallas_call(
        _router_kernel,
        grid=(t // ROW_TILE,),
        in_specs=[pl.BlockSpec((ROW_TILE, d), row), pl.BlockSpec((1, d), fixed),
                  pl.BlockSpec((d, LANES), fixed), pl.BlockSpec((1, LANES), fixed)],
        out_specs=[pl.BlockSpec((ROW_TILE, d), row), pl.BlockSpec((ROW_TILE, LANES), row),
                   pl.BlockSpec((ROW_TILE, LANES), row)],
        out_shape=[jax.ShapeDtypeStruct((t, d), F32), jax.ShapeDtypeStruct((t, LANES), jnp.int32),
                   jax.ShapeDtypeStruct((t, LANES), F32)],
        compiler_params=_params(("arbitrary",), 40),
        name="moe_router",
    )(x, g.reshape(1, d), wr, br)


def _moe_plan(eid):
    t = eid.shape[0]
    a = 2 * t
    n_tiles = a // MOE_TILE + N_EXP
    e_flat = eid.reshape(a)
    onehot = (e_flat[:, None] == jnp.arange(N_EXP, dtype=jnp.int32)[None, :]).astype(jnp.int32)
    csum = jnp.cumsum(onehot, axis=0)
    counts = csum[-1]
    rank = jnp.take_along_axis(csum, e_flat[:, None], axis=1)[:, 0] - 1
    tiles_e = (counts + MOE_TILE - 1) // MOE_TILE
    tiles_end = jnp.cumsum(tiles_e)
    row_start = (tiles_end - tiles_e) * MOE_TILE
    pos = row_start[e_flat] + rank
    n_valid = tiles_end[-1]
    tile_id = jnp.minimum(jnp.arange(n_tiles, dtype=jnp.int32), n_valid - 1)
    tile_expert = jnp.sum((tile_id[:, None] >= tiles_end[None, :]).astype(jnp.int32), axis=1)
    tile_expert = jnp.minimum(tile_expert, N_EXP - 1)
    changed = jnp.concatenate([jnp.ones((1,), bool), tile_expert[1:] != tile_expert[:-1]])
    first = jnp.logical_and(changed, jnp.arange(n_tiles) < n_valid).astype(jnp.int32)
    wslot = (jnp.cumsum(first) - 1) % 2
    next_start = tiles_end[tile_expert]
    next_expert = jnp.where(next_start < n_valid, tile_expert[jnp.minimum(next_start, n_tiles - 1)], -1)
    tile_meta = jnp.stack([tile_expert, first, wslot, next_expert]).astype(jnp.int32)
    pad_meta = jnp.stack([row_start + counts, tiles_e * MOE_TILE - counts]).astype(jnp.int32)
    return tile_meta, n_valid.reshape(1).astype(jnp.int32), pad_meta, pos.astype(jnp.int32)


def _dispatch_kernel(pos_ref, nv_ref, pad_ref, x_ref, xs_hbm, zeros, sem, zsem):
    i = pl.program_id(0)
    tm = x_ref.shape[0]
    n_tiles = xs_hbm.shape[0] // tm
    base = i * (2 * tm)

    for r in range(tm):
        for kk in range(2):
            pltpu.make_async_copy(x_ref.at[pl.ds(r, 1)], xs_hbm.at[pl.ds(pos_ref[base + 2 * r + kk], 1)],
                                  sem).start(priority=kk)

    def pad_row_copy(e, j):
        return pltpu.make_async_copy(zeros.at[pl.ds(0, 1)], xs_hbm.at[pl.ds(pad_ref[0, e] + j, 1)], zsem)

    def tail_tile_copy(tile):
        return pltpu.make_async_copy(zeros, xs_hbm.at[pl.ds(tile * tm, tm)], zsem)

    def for_each_fill(fn):
        def per_expert(e, carry):
            lax.fori_loop(0, pad_ref[1, e], lambda j, c: (fn(pad_row_copy(e, j)), c)[1], 0)
            return carry
        lax.fori_loop(0, N_EXP, per_expert, 0)
        lax.fori_loop(nv_ref[0], n_tiles, lambda tile, c: (fn(tail_tile_copy(tile)), c)[1], 0)

    @pl.when(i == 0)
    def _():
        zeros[...] = jnp.zeros_like(zeros)
        for_each_fill(lambda c: c.start())

    for kk in range(2):
        pltpu.make_async_copy(x_ref, xs_hbm.at[pl.ds(0, tm)], sem).wait()

    @pl.when(i == pl.num_programs(0) - 1)
    def _():
        for_each_fill(lambda c: c.wait())


def _dispatch_call(xn, pos, n_valid, pad_meta, rows):
    t, d = xn.shape
    grid_spec = pltpu.PrefetchScalarGridSpec(
        num_scalar_prefetch=3,
        grid=(t // MOE_TILE,),
        in_specs=[pl.BlockSpec((MOE_TILE, d), lambda i, pos_ref, nv, pad: (i, 0))],
        out_specs=pl.BlockSpec(memory_space=pl.ANY),
        scratch_shapes=[pltpu.VMEM((MOE_TILE, d), F32), pltpu.SemaphoreType.DMA(()), pltpu.SemaphoreType.DMA(())],
    )
    return pl.pallas_call(
        _dispatch_kernel,
        grid_spec=grid_spec,
        out_shape=jax.ShapeDtypeStruct((rows, d), F32),
        compiler_params=_params(("arbitrary",), 32),
        name="moe_dispatch",
    )(pos, n_valid, pad_meta, xn)


def _moe_ffn_kernel(tm_ref, nv_ref, x_ref, wgu_hbm, wd_hbm, o_ref,
                    wgu_f, wd_f, wsems, wgu_b, wd_b, *, layer):
    i = pl.program_id(0)
    nv = nv_ref[0]
    expert, first, wslot, next_expert = tm_ref[0, i], tm_ref[1, i], tm_ref[2, i], tm_ref[3, i]

    def weight_copies(e, s):
        return (pltpu.make_async_copy(wgu_hbm.at[layer, e], wgu_f.at[s], wsems.at[0, s]),
                pltpu.make_async_copy(wd_hbm.at[layer, e], wd_f.at[s], wsems.at[1, s]))

    @pl.when(i == 0)
    def _():
        for c in weight_copies(expert, 0):
            c.start()

    @pl.when(first == 1)
    def _():
        for c in weight_copies(expert, wslot):
            c.wait()
        wgu_b[...] = wgu_f[wslot].astype(BF16)
        wd_b[...] = wd_f[wslot].astype(BF16)

        @pl.when(next_expert >= 0)
        def _():
            for c in weight_copies(next_expert, 1 - wslot):
                c.start()

    @pl.when(i < nv)
    def _():
        h = jnp.dot(x_ref[...].astype(BF16), wgu_b[...], preferred_element_type=F32)
        hg = h[:, :D_FF_E]
        hu = h[:, D_FF_E:]
        act = (hg * _sigmoid(hg) * hu).astype(BF16)
        o_ref[...] = jnp.dot(act, wd_b[...], preferred_element_type=F32)

    @pl.when(i >= nv)
    def _():
        o_ref[...] = jnp.zeros_like(o_ref)


def _moe_ffn_call(xs, w_gu, w_down, layer, tile_meta, n_valid):
    rows, d = xs.shape
    n_tiles = tile_meta.shape[1]
    grid_spec = pltpu.PrefetchScalarGridSpec(
        num_scalar_prefetch=2,
        grid=(n_tiles,),
        in_specs=[pl.BlockSpec((MOE_TILE, d), lambda i, tm, nv: (jnp.minimum(i, nv[0] - 1), 0)),
                  pl.BlockSpec(memory_space=pl.ANY), pl.BlockSpec(memory_space=pl.ANY)],
        out_specs=pl.BlockSpec((MOE_TILE, d), lambda i, tm, nv: (i, 0)),
        scratch_shapes=[
            pltpu.VMEM((2, d, 2 * D_FF_E), F32),
            pltpu.VMEM((2, D_FF_E, d), F32),
            pltpu.SemaphoreType.DMA((2, 2)),
            pltpu.VMEM((d, 2 * D_FF_E), BF16),
            pltpu.VMEM((D_FF_E, d), BF16),
        ],
    )
    return pl.pallas_call(
        functools.partial(_moe_ffn_kernel, layer=layer),
        grid_spec=grid_spec,
        out_shape=jax.ShapeDtypeStruct((rows, d), F32),
        compiler_params=_params(("arbitrary",), 48),
        name="moe_experts",
    )(tile_meta, n_valid, xs, w_gu, w_down)


def _combine_kernel(pos_ref, x_ref, w_ref, y_hbm, g_ref, *rest, emit_x, norm_tiles):
    outs = rest[:int(emit_x) + len(norm_tiles)]
    buf, sems = rest[int(emit_x) + len(norm_tiles):]
    i = pl.program_id(0)
    n = pl.num_programs(0)
    slot = lax.rem(i, 2)
    tm = x_ref.shape[0]

    def gather(tile, dst_slot):
        base = tile * (2 * tm)
        for r in range(tm):
            for kk in range(2):
                pltpu.make_async_copy(y_hbm.at[pl.ds(pos_ref[base + 2 * r + kk], 1)],
                                      buf.at[dst_slot, kk, pl.ds(r, 1)], sems.at[dst_slot]).start(priority=kk)

    @pl.when(i == 0)
    def _():
        gather(0, 0)

    for s in range(2):
        pl.when(jnp.logical_and(i + 1 < n, slot == s))(functools.partial(gather, i + 1, 1 - s))

    for kk in range(2):
        pltpu.make_async_copy(y_hbm.at[pl.ds(0, tm)], buf.at[slot, kk], sems.at[slot]).wait()
    xo = x_ref[...] + (w_ref[:, 0:1] * buf[slot, 0] + w_ref[:, 1:2] * buf[slot, 1])
    norm_refs = outs[int(emit_x):]
    if emit_x:
        outs[0][...] = xo

    def write_norm(p):
        norm_refs[p][...] = _rms(xo, g_ref[...]).astype(norm_refs[p].dtype)

    _for_each_part(i, norm_tiles, write_norm)


def _combine_call(x, ws, y_rows, pos, g, emit_x, norm_dtype, norm_rows):
    t, d = x.shape
    tm = MOE_TILE
    row = lambda i, pos_ref: (i, 0)
    out_specs, out_shape = [], []
    if emit_x:
        out_specs.append(pl.BlockSpec((tm, d), row))
        out_shape.append(jax.ShapeDtypeStruct((t, d), F32))
    norm_parts = [jax.ShapeDtypeStruct((r, d), norm_dtype) for r in norm_rows]
    out_specs += _part_specs(norm_parts, tm, d, row)
    out_shape += norm_parts
    grid_spec = pltpu.PrefetchScalarGridSpec(
        num_scalar_prefetch=1,
        grid=(t // tm,),
        in_specs=[pl.BlockSpec((tm, d), row), pl.BlockSpec((tm, LANES), row), pl.BlockSpec(memory_space=pl.ANY),
                  pl.BlockSpec((1, d), lambda i, pos_ref: (0, 0))],
        out_specs=out_specs,
        scratch_shapes=[pltpu.VMEM((2, 2, tm, d), F32), pltpu.SemaphoreType.DMA((2,))],
    )
    return pl.pallas_call(
        functools.partial(_combine_kernel, emit_x=emit_x, norm_tiles=_part_tiles(norm_parts, tm)),
        grid_spec=grid_spec,
        out_shape=out_shape,
        compiler_params=_params(("arbitrary",), 40),
        name="moe_combine",
    )(pos, x, ws, y_rows, g.reshape(1, d))


def _moe_layer(x, g_ffn, w_grp, b_grp, w_exp, b_exp, w_gu, w_down, layer, g_next, emit_x, norm_dtype, norm_rows):
    xn, ids, ws = _router_call(x, g_ffn, w_grp, b_grp, w_exp, b_exp)
    tile_meta, n_valid, pad_meta, pos = _moe_plan(ids[:, :2])
    xs = _dispatch_call(xn, pos, n_valid, pad_meta, tile_meta.shape[1] * MOE_TILE)
    y_rows = _moe_ffn_call(xs, w_gu, w_down, layer, tile_meta, n_valid)
    return _combine_call(x, ws, y_rows, pos, g_next, emit_x, norm_dtype, norm_rows)


def _s5_prep_kernel(are_ref, aim_ref, ldt_ref, bre_ref, bim_ref, lre_ref, lim_ref, bbr_ref, bbi_ref):
    are = are_ref[...]
    aim = aim_ref[...]
    dt = jnp.exp(ldt_ref[...])
    mag = jnp.exp(are * dt)
    lr = mag * jnp.cos(aim * dt)
    li = mag * jnp.sin(aim * dt)
    lre_ref[...] = lr
    lim_ref[...] = li
    den = are * are + aim * aim
    nr = lr - 1.0
    cr = (nr * are + li * aim) / den
    ci = (li * are - nr * aim) / den
    bre = bre_ref[...]
    bim = bim_ref[...]
    bbr_ref[...] = cr * bre - ci * bim
    bbi_ref[...] = cr * bim + ci * bre


def _s5_prep_call(a_re, a_im, log_dt, b_re, b_im):
    g, p, c = b_re.shape
    shp = (g, p * c)
    expand = lambda a: jnp.broadcast_to(a[:, :, None], (g, p, c)).reshape(shp)
    ldt = jnp.broadcast_to(log_dt[:, None], shp)
    outs = pl.pallas_call(
        _s5_prep_kernel,
        out_shape=[jax.ShapeDtypeStruct(shp, F32)] * 4,
        name="s5_discretize",
    )(expand(a_re), expand(a_im), ldt, b_re.reshape(shp), b_im.reshape(shp))
    lre, lim, bbr, bbi = outs
    return lre[:, ::c], lim[:, ::c], bbr, bbi


def _s5_mats(bbr, bbi, c_re, c_im):
    eye = jnp.eye(16, dtype=F32)

    def bd_part(bb):
        xx = bb.reshape(S5_FB, 16, S5_P, S5_GC).transpose(0, 1, 3, 2)
        yy = xx[:, :, :, None, :] * eye[None, :, None, :, None]
        return yy.reshape(S5_FB, 16 * S5_GC, 16 * S5_P)

    def cd_part(cc):
        xx = cc.reshape(S5_FB, 16, S5_GC, S5_P).transpose(0, 1, 3, 2)
        yy = xx[:, :, :, None, :] * eye[None, :, None, :, None]
        return yy.reshape(S5_FB, 16 * S5_P, 16 * S5_GC)

    bd = jnp.concatenate([bd_part(bbr), bd_part(bbi)], axis=-1).astype(BF16)
    cd = jnp.concatenate([cd_part(c_re), -cd_part(c_im)], axis=1).astype(BF16)
    return bd, cd


def _s5_kernel(u_ref, bd_ref, cd_ref, lre_ref, lim_ref, d_ref, h0r_ref, h0i_ref,
               z_ref, hr_ref, hi_ref, scr, hst, *, seg, carry):
    rows = u_ref.shape[0]
    n_seq = rows // seg
    ch = 16 * S5_GC
    half = SUBLANES * S5_PITCH
    blk = 2 * half

    for fb in range(S5_FB):
        bu = jnp.dot(u_ref[:, fb * ch:(fb + 1) * ch].astype(BF16), bd_ref[fb], preferred_element_type=F32)
        for c in range(2 * SUBLANES):
            r0 = fb * blk + c * S5_PITCH
            scr[r0:r0 + rows, :] = bu[:, c * LANES:(c + 1) * LANES]

    lre = [lre_ref[fb * SUBLANES:(fb + 1) * SUBLANES, :] for fb in range(S5_FB)]
    lim = [lim_ref[fb * SUBLANES:(fb + 1) * SUBLANES, :] for fb in range(S5_FB)]

    def scan(row0, state):
        def step(l, st):
            new = []
            for fb in range(S5_FB):
                hr, hi = st[2 * fb], st[2 * fb + 1]
                ire = pl.ds(fb * blk + row0 + l, SUBLANES, stride=S5_PITCH)
                iim = pl.ds(fb * blk + half + row0 + l, SUBLANES, stride=S5_PITCH)
                nr = lre[fb] * hr - lim[fb] * hi + scr[ire, :]
                ni = lre[fb] * hi + lim[fb] * hr + scr[iim, :]
                scr[ire, :] = nr
                scr[iim, :] = ni
                new += [nr, ni]
            return tuple(new)
        return lax.fori_loop(0, seg, step, state)

    def load_state(rref, iref, lead):
        st = []
        for fb in range(S5_FB):
            st += [rref[lead + (slice(fb * SUBLANES, (fb + 1) * SUBLANES), slice(None))],
                   iref[lead + (slice(fb * SUBLANES, (fb + 1) * SUBLANES), slice(None))]]
        return tuple(st)

    def store_state(st, rref, iref, lead):
        for fb in range(S5_FB):
            rref[lead + (slice(fb * SUBLANES, (fb + 1) * SUBLANES), slice(None))] = st[2 * fb]
            iref[lead + (slice(fb * SUBLANES, (fb + 1) * SUBLANES), slice(None))] = st[2 * fb + 1]

    if carry:
        @pl.when(pl.program_id(0) == 0)
        def _():
            hst[0] = h0r_ref[0]
            hst[1] = h0i_ref[0]
        hre_scr = hst.at[0]
        him_scr = hst.at[1]
        st = scan(0, load_state(hre_scr, him_scr, ()))
        store_state(st, hre_scr, him_scr, ())
        store_state(st, hr_ref, hi_ref, (0,))
    else:
        for s in range(n_seq):
            st = scan(s * seg, load_state(h0r_ref, h0i_ref, (s,)))
            store_state(st, hr_ref, hi_ref, (s,))

    for fb in range(S5_FB):
        hf = jnp.concatenate(
            [scr[fb * blk + c * S5_PITCH: fb * blk + c * S5_PITCH + rows, :] for c in range(2 * SUBLANES)],
            axis=1).astype(BF16)
        y = jnp.dot(hf, cd_ref[fb], preferred_element_type=F32)
        cs = slice(fb * ch, (fb + 1) * ch)
        yy = y + d_ref[:, cs] * u_ref[:, cs]
        z_ref[:, cs] = jax.nn.gelu(yy).astype(z_ref.dtype)


def _s5_call(u, bd, cd, lre, lim, d_skip, h0r, h0i, row_block0, n_blocks, seg, carry):
    t, d = u.shape
    n_seq = S5_ROWS // seg
    ns_total = h0r.shape[0]
    st_idx = (lambda i: (0, 0, 0)) if carry else (lambda i: (i, 0, 0))
    fixed2 = lambda i: (0, 0)
    fixed3 = lambda i: (0, 0, 0)
    st_rows = S5_FB * SUBLANES
    return pl.pallas_call(
        functools.partial(_s5_kernel, seg=seg, carry=carry),
        grid=(n_blocks,),
        in_specs=[
            pl.BlockSpec((S5_ROWS, d), lambda i: (row_block0 + i, 0)),
            pl.BlockSpec(bd.shape, fixed3),
            pl.BlockSpec(cd.shape, fixed3),
            pl.BlockSpec(lre.shape, fixed2),
            pl.BlockSpec(lim.shape, fixed2),
            pl.BlockSpec((1, d), fixed2),
            pl.BlockSpec((n_seq, st_rows, LANES), st_idx),
            pl.BlockSpec((n_seq, st_rows, LANES), st_idx),
        ],
        out_specs=[
            pl.BlockSpec((S5_ROWS, d), lambda i: (i, 0)),
            pl.BlockSpec((n_seq, st_rows, LANES), st_idx),
            pl.BlockSpec((n_seq, st_rows, LANES), st_idx),
        ],
        out_shape=[
            jax.ShapeDtypeStruct((n_blocks * S5_ROWS, d), BF16),
            jax.ShapeDtypeStruct((ns_total, st_rows, LANES), F32),
            jax.ShapeDtypeStruct((ns_total, st_rows, LANES), F32),
        ],
        scratch_shapes=[
            pltpu.VMEM((2 * SUBLANES * S5_FB * S5_PITCH, LANES), F32),
            pltpu.VMEM((2, st_rows, LANES), F32),
        ],
        compiler_params=_params(("arbitrary",), 56),
        name="s5_scan",
    )(u, bd, cd, lre, lim, d_skip.reshape(1, d), h0r, h0i)


def kernel(x_prompt, x_sample, state_ret, state_s5_re, state_s5_im, norm_mix_g, norm_ffn_g, norm_final_g,
           ret_w_in, ret_gn_g, ret_w_out, s5_w_in, s5_a_re, s5_a_im, s5_log_dt, s5_b_re, s5_b_im,
           s5_c_re, s5_c_im, s5_d, s5_w_out, moe_w_grp, moe_b_grp, moe_w_exp, moe_b_exp, moe_w_gu, moe_w_down):
    bp, seq, d = x_prompt.shape
    db, dseq, _ = x_sample.shape
    assert d == D_MODEL and bp == 1
    assert seq % max(ROW_TILE, RET_ROWS) == 0 and seq % CHUNK == 0
    assert dseq < CHUNK and dseq % 16 == 0 and S5_ROWS % dseq == 0
    tp = bp * seq
    ts = db * dseq
    t = tp + ts
    assert ts % ROW_TILE == 0 and ts % S5_ROWS == 0

    x_parts = [x_prompt.reshape(tp, d), x_sample.reshape(ts, d)]

    half = RET_DK // 2
    freqs = ROPE_BASE ** (-jnp.arange(half, dtype=F32) / half)
    pos = jnp.concatenate([jnp.arange(seq, dtype=F32), jnp.tile(PAST_LEN + jnp.arange(dseq, dtype=F32), db)])
    ang = pos[:, None] * freqs[None, :]
    cos_t = jnp.cos(ang)
    sin_t = jnp.sin(ang)

    xn = _norm_call(x_parts, norm_mix_g[0], BF16)
    n_proj = 2 * D_MODEL + 2 * RET_VW
    tn = 1024
    first_col = lambda j: 0
    proj = _mm_call([xn], ret_w_in[0], [0], tn, n_proj, [([cos_t], half, first_col), ([sin_t], half, first_col)],
                    BF16, functools.partial(_epi_rope, n_qk_blocks=2 * D_MODEL // tn), 40, "ret_in_proj")
    og_p, ret_p = _ret_prompt_call(proj, ret_gn_g[0], tp)
    og_s, ret_s = _ret_sample_call(proj, ret_gn_g[0], state_ret[0], tp, dseq)
    tn = 512
    same_col = lambda j: j
    x1 = _mm_call([og_p, og_s], ret_w_out[0], [0], tn, d, [(x_parts, tn, same_col)], F32, _epi_residual,
                  40, "ret_out_proj")

    x2, xn = _moe_layer(x1, norm_ffn_g[0], moe_w_grp[0], moe_b_grp[0], moe_w_exp[0], moe_b_exp[0],
                        moe_w_gu, moe_w_down, 0, norm_mix_g[1], True, BF16, (t,))

    u = _mm_call([xn], s5_w_in[0], [0], 1024, d, [], F32, _epi_plain, 40, "s5_in_proj")
    lre, lim, bbr, bbi = _s5_prep_call(s5_a_re[0], s5_a_im[0], s5_log_dt[0], s5_b_re[0], s5_b_im[0])
    bd, cd = _s5_mats(bbr, bbi, s5_c_re[0], s5_c_im[0])
    st_shape = (S5_FB * SUBLANES, LANES)
    lre = lre.reshape(st_shape)
    lim = lim.reshape(st_shape)
    zero_state = jnp.zeros((1,) + st_shape, F32)
    z_p, hpr, hpi = _s5_call(u, bd, cd, lre, lim, s5_d[0], zero_state, zero_state,
                             0, tp // S5_ROWS, S5_ROWS, True)
    z_s, hsr, hsi = _s5_call(u, bd, cd, lre, lim, s5_d[0],
                             state_s5_re[0].reshape((db,) + st_shape), state_s5_im[0].reshape((db,) + st_shape),
                             tp // S5_ROWS, ts // S5_ROWS, dseq, False)
    x3 = _mm_call([z_p, z_s], s5_w_out[0], [0, d // tn], tn, d, [([x2], tn, same_col)], F32, _epi_glu_residual,
                  40, "s5_out_proj")

    y_p, y_s = _moe_layer(x3, norm_ffn_g[1], moe_w_grp[1], moe_b_grp[1], moe_w_exp[1], moe_b_exp[1],
                          moe_w_gu, moe_w_down, 1, norm_final_g, False, F32, (tp, ts))

    s5_shape = (1, -1, S5_GROUPS, S5_P)
    return (y_p.reshape(bp, seq, d), y_s.reshape(db, dseq, d),
            ret_p.reshape(1, bp, RET_HEADS, RET_DK, RET_DV), ret_s.reshape((1,) + ret_s.shape),
            hpr.reshape(s5_shape), hpi.reshape(s5_shape), hsr.reshape(s5_shape), hsi.reshape(s5_shape))
```

```python
import functools
import math

import numpy as np
import jax
import jax.numpy as jnp
from jax import lax
from jax.experimental import pallas as pl
from jax.experimental.pallas import tpu as pltpu

F32 = jnp.float32
BF16 = jnp.bfloat16

D_MODEL = 2048
PAST_LEN = 2048
CHUNK = 64
RET_HEADS = 8
RET_DK = D_MODEL // RET_HEADS
RET_DV = 2 * RET_DK
RET_VW = RET_HEADS * RET_DV
ROPE_BASE = 10000.0
S5_GC = 16
S5_GROUPS = D_MODEL // S5_GC
S5_P = 64
MOE_GROUPS = 4
MOE_PER_GROUP = 8
N_EXP = MOE_GROUPS * MOE_PER_GROUP
D_FF_E = 512
EPS = 1e-6

LANES = 128
SUBLANES = 8
ROW_TILE = 512
MM_CHUNK = 256
RET_ROWS = 256
MOE_TILE = 256
S5_ROWS = 128
S5_FB = S5_GROUPS // 16
S5_PITCH = S5_ROWS + 8
MIB = 1024 * 1024


def _params(sem, vmem_mib):
    return pltpu.CompilerParams(dimension_semantics=sem, vmem_limit_bytes=vmem_mib * MIB)


def _rms(x, g):
    ms = jnp.mean(x * x, axis=-1, keepdims=True)
    return x * lax.rsqrt(ms + EPS) * g


def _sigmoid(x):
    return 1.0 / (1.0 + jnp.exp(-x))


def _part_specs(parts, rows, cols, rowcol):
    specs = []
    start = 0
    for p in parts:
        nt = p.shape[0] // rows

        def imap(*g, start=start, nt=nt):
            i, jc = rowcol(*g)
            return (jnp.clip(i - start, 0, nt - 1), jc)

        specs.append(pl.BlockSpec((rows, cols), imap))
        start += nt
    return specs


def _part_tiles(parts, rows):
    return tuple(p.shape[0] // rows for p in parts)


def _for_each_part(i, tiles, fn):
    if len(tiles) == 1:
        fn(0)
        return
    start = 0
    for p, nt in enumerate(tiles):
        pl.when(jnp.logical_and(i >= start, i < start + nt))(functools.partial(fn, p))
        start += nt


def _pick(refs, p):
    return refs[min(p, len(refs) - 1)]


def _norm_kernel(*refs, tiles):
    n = len(tiles)
    x_refs, g_ref, o_ref = refs[:n], refs[n], refs[n + 1]

    def run(p):
        o_ref[...] = _rms(x_refs[p][...], g_ref[...]).astype(o_ref.dtype)

    _for_each_part(pl.program_id(0), tiles, run)


def _norm_call(x_parts, g, out_dtype):
    d = x_parts[0].shape[1]
    t = sum(p.shape[0] for p in x_parts)
    return pl.pallas_call(
        functools.partial(_norm_kernel, tiles=_part_tiles(x_parts, ROW_TILE)),
        grid=(t // ROW_TILE,),
        in_specs=_part_specs(x_parts, ROW_TILE, d, lambda i: (i, 0)) + [pl.BlockSpec((1, d), lambda i: (0, 0))],
        out_specs=pl.BlockSpec((ROW_TILE, d), lambda i: (i, 0)),
        out_shape=jax.ShapeDtypeStruct((t, d), out_dtype),
        compiler_params=_params(("arbitrary",), 32),
        name="rmsnorm",
    )(*x_parts, g.reshape(1, d))


def _mm_kernel(*refs, tiles, n_x, n_w, n_extra, epi):
    x_refs = refs[:n_x]
    w_refs = refs[n_x:n_x + n_w]
    pos = n_x + n_w
    extra = []
    for n in n_extra:
        extra.append(refs[pos:pos + n])
        pos += n
    o_ref = refs[pos]
    wb = refs[pos + 1:]

    @pl.when(pl.program_id(1) == 0)
    def _():
        for w_ref, b in zip(w_refs, wb):
            b[...] = w_ref[...].astype(BF16)

    def run(p):
        x = _pick(x_refs, p)[...]
        for c in range(o_ref.shape[1] // MM_CHUNK):
            cs = slice(c * MM_CHUNK, (c + 1) * MM_CHUNK)
            accs = [jnp.dot(x, b[:, cs], preferred_element_type=F32) for b in wb]
            epi(accs, [_pick(e, p) for e in extra], o_ref, cs)

    _for_each_part(pl.program_id(1), tiles, run)


def _mm_call(x_parts, w, w_col_offsets, tn, n_out, extras, out_dtype, epi, vmem_mib, name):
    k = x_parts[0].shape[1]
    t = sum(p.shape[0] for p in x_parts)
    n_w = len(w_col_offsets)
    split = [parts for parts in [x_parts] + [e[0] for e in extras] if len(parts) > 1]
    tiles = _part_tiles(split[0], ROW_TILE) if split else (t // ROW_TILE,)
    assert all(_part_tiles(parts, ROW_TILE) == tiles for parts in split)
    in_specs = _part_specs(x_parts, ROW_TILE, k, lambda j, i: (i, 0))
    for off in w_col_offsets:
        in_specs.append(pl.BlockSpec((k, tn), lambda j, i, off=off: (0, j + off)))
    extra_arrays = []
    for parts, cols, colfn in extras:
        in_specs += _part_specs(parts, ROW_TILE, cols, lambda j, i, colfn=colfn: (i, colfn(j)))
        extra_arrays += list(parts)
    return pl.pallas_call(
        functools.partial(_mm_kernel, tiles=tiles, n_x=len(x_parts), n_w=n_w,
                          n_extra=tuple(len(e[0]) for e in extras), epi=epi),
        grid=(n_out // tn, t // ROW_TILE),
        in_specs=in_specs,
        out_specs=pl.BlockSpec((ROW_TILE, tn), lambda j, i: (i, j)),
        out_shape=jax.ShapeDtypeStruct((t, n_out), out_dtype),
        scratch_shapes=[pltpu.VMEM((k, tn), BF16) for _ in range(n_w)],
        compiler_params=_params(("arbitrary", "arbitrary"), vmem_mib),
        name=name,
    )(*x_parts, *([w] * n_w), *extra_arrays)


def _epi_plain(accs, extra, o_ref, cs):
    o_ref[:, cs] = accs[0].astype(o_ref.dtype)


def _epi_residual(accs, extra, o_ref, cs):
    o_ref[:, cs] = (extra[0][:, cs] + accs[0]).astype(o_ref.dtype)


def _epi_glu_residual(accs, extra, o_ref, cs):
    a, gt = accs
    o_ref[:, cs] = (extra[0][:, cs] + a * _sigmoid(gt)).astype(o_ref.dtype)


def _epi_rope(accs, extra, o_ref, cs, *, n_q_blocks):
    acc = accs[0]
    assert acc.shape[1] == RET_DK
    cos_ref, sin_ref = extra
    half = RET_DK // 2
    scale = jnp.where(pl.program_id(0) >= n_q_blocks, RET_DK ** -0.5, 1.0).astype(F32)
    c = cos_ref[...]
    s = sin_ref[...]
    t1 = acc[:, :half]
    t2 = acc[:, half:]
    o_ref[:, cs.start:cs.start + half] = ((t1 * c - t2 * s) * scale).astype(o_ref.dtype)
    o_ref[:, cs.start + half:cs.stop] = ((t1 * s + t2 * c) * scale).astype(o_ref.dtype)


def _ret_tables(rows, chunk):
    lg = jnp.log(jnp.asarray(1.0 - 2.0 ** (-5.0 - np.arange(RET_HEADS)), dtype=F32))[:, None, None]
    n = jnp.arange(rows, dtype=F32)
    ci = np.arange(rows) // chunk
    same_or_earlier = jnp.asarray(ci[None, :] <= ci[:, None])
    dmat = jnp.where(same_or_earlier[None], jnp.exp(jnp.abs(n[:, None] - n[None, :])[None] * lg), 0.0)
    dq = jnp.exp((n + 1.0)[None, :, None] * lg) * jnp.ones((1, 1, LANES), F32)
    dk = jnp.exp((rows - 1.0 - n)[None, :, None] * lg) * jnp.ones((1, 1, LANES), F32)
    gr = jnp.exp(rows * lg) * jnp.ones((1, 1, RET_DV), F32)
    return dmat, dq, dk, gr


def _ret_block(q, k, v, g, s_prev, dmat, dq, dk, gr, gn):
    s = lax.dot_general(q, k, (((1,), (1,)), ((), ())), preferred_element_type=F32) * dmat
    inter = jnp.dot(q, s_prev.astype(BF16), preferred_element_type=F32)
    o = jnp.dot(s.astype(BF16), v, preferred_element_type=F32) + jnp.concatenate([dq] * (RET_DV // LANES), axis=1) * inter
    kd = (k.astype(F32) * jnp.concatenate([dk] * (RET_DK // LANES), axis=1)).astype(BF16)
    s_new = s_prev * gr + lax.dot_general(kd, v, (((0,), (0,)), ((), ())), preferred_element_type=F32)
    mu = jnp.mean(o, axis=-1, keepdims=True)
    oc = o - mu
    var = jnp.mean(oc * oc, axis=-1, keepdims=True)
    on = oc * lax.rsqrt(var + EPS) * gn
    gf = g.astype(F32)
    return (gf * _sigmoid(gf) * on).astype(BF16), s_new


def _ret_prompt_kernel(q_ref, k_ref, v_ref, g_ref, dm_ref, dq_ref, dk_ref, gr_ref, gn_ref,
                       og_ref, st_ref, s_scr):
    @pl.when(pl.program_id(1) == 0)
    def _():
        s_scr[...] = jnp.zeros_like(s_scr)

    og, s_new = _ret_block(q_ref[...], k_ref[...], v_ref[...], g_ref[...], s_scr[...],
                           dm_ref[0], dq_ref[0], dk_ref[0], gr_ref[0], gn_ref[...])
    og_ref[...] = og
    s_scr[...] = s_new
    st_ref[0] = s_new


def _ret_prompt_call(qk, vg, gn, t_prompt):
    dm, dq, dk, gr = _ret_tables(RET_ROWS, CHUNK)
    h = RET_HEADS
    return pl.pallas_call(
        _ret_prompt_kernel,
        grid=(h, t_prompt // RET_ROWS),
        in_specs=[
            pl.BlockSpec((RET_ROWS, RET_DK), lambda hh, c: (c, hh)),
            pl.BlockSpec((RET_ROWS, RET_DK), lambda hh, c: (c, h + hh)),
            pl.BlockSpec((RET_ROWS, RET_DV), lambda hh, c: (c, hh)),
            pl.BlockSpec((RET_ROWS, RET_DV), lambda hh, c: (c, h + hh)),
            pl.BlockSpec((1, RET_ROWS, RET_ROWS), lambda hh, c: (hh, 0, 0)),
            pl.BlockSpec((1, RET_ROWS, LANES), lambda hh, c: (hh, 0, 0)),
            pl.BlockSpec((1, RET_ROWS, LANES), lambda hh, c: (hh, 0, 0)),
            pl.BlockSpec((1, 1, RET_DV), lambda hh, c: (hh, 0, 0)),
            pl.BlockSpec((1, RET_DV), lambda hh, c: (0, hh)),
        ],
        out_specs=[
            pl.BlockSpec((RET_ROWS, RET_DV), lambda hh, c: (c, hh)),
            pl.BlockSpec((1, RET_DK, RET_DV), lambda hh, c: (hh, 0, 0)),
        ],
        out_shape=[
            jax.ShapeDtypeStruct((t_prompt, RET_VW), BF16),
            jax.ShapeDtypeStruct((h, RET_DK, RET_DV), F32),
        ],
        scratch_shapes=[pltpu.VMEM((RET_DK, RET_DV), F32)],
        compiler_params=_params(("arbitrary", "arbitrary"), 32),
        name="retention_prompt",
    )(qk, qk, vg, vg, dm, dq, dk, gr, gn.reshape(1, RET_VW))


def _ret_sample_kernel(q_ref, k_ref, v_ref, g_ref, s0_ref, dm_ref, dq_ref, dk_ref, gr_ref, gn_ref,
                       og_ref, st_ref):
    for hh in range(RET_HEADS):
        ks = slice(hh * RET_DK, (hh + 1) * RET_DK)
        vs = slice(hh * RET_DV, (hh + 1) * RET_DV)
        og, s_new = _ret_block(q_ref[:, ks], k_ref[:, ks], v_ref[:, vs], g_ref[:, vs], s0_ref[0, hh],
                               dm_ref[hh], dq_ref[hh], dk_ref[hh], gr_ref[hh], gn_ref[:, vs])
        og_ref[:, vs] = og
        st_ref[0, hh] = s_new


def _ret_sample_call(qk, vg, gn, state, t_prompt, dec_seq):
    dm, dq, dk, gr = _ret_tables(dec_seq, dec_seq)
    b = state.shape[0]
    r0 = t_prompt // dec_seq
    const3 = lambda bb: (0, 0, 0)
    return pl.pallas_call(
        _ret_sample_kernel,
        grid=(b,),
        in_specs=[
            pl.BlockSpec((dec_seq, D_MODEL), lambda bb: (r0 + bb, 0)),
            pl.BlockSpec((dec_seq, D_MODEL), lambda bb: (r0 + bb, 1)),
            pl.BlockSpec((dec_seq, RET_VW), lambda bb: (r0 + bb, 0)),
            pl.BlockSpec((dec_seq, RET_VW), lambda bb: (r0 + bb, 1)),
            pl.BlockSpec((1, RET_HEADS, RET_DK, RET_DV), lambda bb: (bb, 0, 0, 0)),
            pl.BlockSpec(dm.shape, const3),
            pl.BlockSpec(dq.shape, const3),
            pl.BlockSpec(dk.shape, const3),
            pl.BlockSpec(gr.shape, const3),
            pl.BlockSpec((1, RET_VW), lambda bb: (0, 0)),
        ],
        out_specs=[
            pl.BlockSpec((dec_seq, RET_VW), lambda bb: (bb, 0)),
            pl.BlockSpec((1, RET_HEADS, RET_DK, RET_DV), lambda bb: (bb, 0, 0, 0)),
        ],
        out_shape=[
            jax.ShapeDtypeStruct((b * dec_seq, RET_VW), BF16),
            jax.ShapeDtypeStruct(state.shape, F32),
        ],
        compiler_params=_params(("arbitrary",), 40),
        name="retention_sample",
    )(qk, qk, vg, vg, state, dm, dq, dk, gr, gn.reshape(1, RET_VW))


def _router_kernel(x_ref, g_ref, wr_ref, br_ref, xn_ref, id_ref, w_ref):
    xn = _rms(x_ref[...], g_ref[...])
    xn_ref[...] = xn
    xh = xn.astype(BF16)
    xl = (xn - xh.astype(F32)).astype(BF16)
    w = wr_ref[...]
    wh = w.astype(BF16)
    wl = (w - wh.astype(F32)).astype(BF16)
    lg = (jnp.dot(xh, wh, preferred_element_type=F32) + jnp.dot(xh, wl, preferred_element_type=F32)
          + jnp.dot(xl, wh, preferred_element_type=F32)) + br_ref[...]
    lane = lax.broadcasted_iota(jnp.int32, lg.shape, 1).astype(F32)
    neg = jnp.float32(-jnp.inf)
    big = jnp.float32(LANES)
    gl = jnp.where(lane < MOE_GROUPS, lg, neg)
    gmax = jnp.max(gl, axis=-1, keepdims=True)
    gsel = jnp.min(jnp.where(gl == gmax, lane, big), axis=-1, keepdims=True)
    gw = 1.0 / jnp.sum(jnp.exp(gl - gmax), axis=-1, keepdims=True)
    lo = MOE_GROUPS + gsel * MOE_PER_GROUP
    el = jnp.where((lane >= lo) & (lane < lo + MOE_PER_GROUP), lg, neg)
    m1 = jnp.max(el, axis=-1, keepdims=True)
    i1 = jnp.min(jnp.where(el == m1, lane, big), axis=-1, keepdims=True)
    el2 = jnp.where(lane == i1, neg, el)
    m2 = jnp.max(el2, axis=-1, keepdims=True)
    i2 = jnp.min(jnp.where(el2 == m2, lane, big), axis=-1, keepdims=True)
    z = jnp.sum(jnp.exp(el - m1), axis=-1, keepdims=True)
    p1 = 1.0 / z
    p2 = jnp.exp(m2 - m1) / z
    tot = p1 + p2
    w1 = p1 / tot * gw
    w2 = p2 / tot * gw
    ids = jnp.where(lane == 0, i1 - MOE_GROUPS, jnp.where(lane == 1, i2 - MOE_GROUPS, 0.0))
    id_ref[...] = ids.astype(jnp.int32)
    w_ref[...] = jnp.where(lane == 0, w1, jnp.where(lane == 1, w2, 0.0))


def _router_call(x, g, w_grp, b_grp, w_exp, b_exp):
    t, d = x.shape
    pad = LANES - MOE_GROUPS - N_EXP
    wr = jnp.concatenate([w_grp, w_exp, jnp.zeros((d, pad), F32)], axis=1)
    br = jnp.concatenate([b_grp, b_exp, jnp.zeros((pad,), F32)]).reshape(1, LANES)
    row = lambda i: (i, 0)
    fixed = lambda i: (0, 0)
    return pl.pallas_call(
        _router_kernel,
        grid=(t // ROW_TILE,),
        in_specs=[pl.BlockSpec((ROW_TILE, d), row), pl.BlockSpec((1, d), fixed),
                  pl.BlockSpec((d, LANES), fixed), pl.BlockSpec((1, LANES), fixed)],
        out_specs=[pl.BlockSpec((ROW_TILE, d), row), pl.BlockSpec((ROW_TILE, LANES), row),
                   pl.BlockSpec((ROW_TILE, LANES), row)],
        out_shape=[jax.ShapeDtypeStruct((t, d), F32), jax.ShapeDtypeStruct((t, LANES), jnp.int32),
                   jax.ShapeDtypeStruct((t, LANES), F32)],
        compiler_params=_params(("arbitrary",), 40),
        name="moe_router",
    )(x, g.reshape(1, d), wr, br)


def _moe_plan(e0, e1):
    t = e0.shape[0]
    n_tiles = 2 * t // MOE_TILE + N_EXP
    experts = jnp.arange(N_EXP, dtype=jnp.int32)[None, :]
    hot0 = (e0[:, None] == experts).astype(jnp.int32)
    hot1 = (e1[:, None] == experts).astype(jnp.int32)
    csum0 = jnp.cumsum(hot0, axis=0)
    csum1 = jnp.cumsum(hot1, axis=0)
    counts0 = csum0[-1]
    counts = counts0 + csum1[-1]
    tiles_e = (counts + MOE_TILE - 1) // MOE_TILE
    tiles_end = jnp.cumsum(tiles_e)
    row_start = (tiles_end - tiles_e) * MOE_TILE
    pos0 = jnp.sum(hot0 * (row_start[None, :] + csum0 - 1), axis=1)
    pos1 = jnp.sum(hot1 * (row_start[None, :] + counts0[None, :] + csum1 - 1), axis=1)
    pos = jnp.concatenate([pos0, pos1])
    n_valid = tiles_end[-1]
    tile_id = jnp.minimum(jnp.arange(n_tiles, dtype=jnp.int32), n_valid - 1)
    tile_expert = jnp.sum((tile_id[:, None] >= tiles_end[None, :]).astype(jnp.int32), axis=1)
    tile_expert = jnp.minimum(tile_expert, N_EXP - 1)
    changed = jnp.concatenate([jnp.ones((1,), bool), tile_expert[1:] != tile_expert[:-1]])
    first = jnp.logical_and(changed, jnp.arange(n_tiles) < n_valid).astype(jnp.int32)
    wslot = (jnp.cumsum(first) - 1) % 2
    next_start = tiles_end[tile_expert]
    next_expert = jnp.where(next_start < n_valid, tile_expert[jnp.minimum(next_start, n_tiles - 1)], -1)
    tile_meta = jnp.stack([tile_expert, first, wslot, next_expert]).astype(jnp.int32)
    pad_meta = jnp.stack([row_start + counts, tiles_e * MOE_TILE - counts]).astype(jnp.int32)
    return tile_meta, n_valid.reshape(1).astype(jnp.int32), pad_meta, pos.astype(jnp.int32)


def _dispatch_kernel(pos_ref, nv_ref, pad_ref, x_ref, xs_hbm, zeros, sem, zsem):
    i = pl.program_id(0)
    tm = x_ref.shape[0]
    n_tiles = xs_hbm.shape[0] // tm
    n_tok = pl.num_programs(0) * tm

    for r in range(tm):
        for kk in range(2):
            pltpu.make_async_copy(x_ref.at[pl.ds(r, 1)], xs_hbm.at[pl.ds(pos_ref[kk * n_tok + i * tm + r], 1)],
                                  sem).start(priority=kk)

    def pad_row_copy(e, j):
        return pltpu.make_async_copy(zeros.at[pl.ds(0, 1)], xs_hbm.at[pl.ds(pad_ref[0, e] + j, 1)], zsem)

    def tail_tile_copy(tile):
        return pltpu.make_async_copy(zeros, xs_hbm.at[pl.ds(tile * tm, tm)], zsem)

    def for_each_fill(fn):
        def per_expert(e, carry):
            lax.fori_loop(0, pad_ref[1, e], lambda j, c: (fn(pad_row_copy(e, j)), c)[1], 0)
            return carry
        lax.fori_loop(0, N_EXP, per_expert, 0)
        lax.fori_loop(nv_ref[0], n_tiles, lambda tile, c: (fn(tail_tile_copy(tile)), c)[1], 0)

    @pl.when(i == 0)
    def _():
        zeros[...] = jnp.zeros_like(zeros)
        for_each_fill(lambda c: c.start())

    for kk in range(2):
        pltpu.make_async_copy(x_ref, xs_hbm.at[pl.ds(0, tm)], sem).wait()

    @pl.when(i == pl.num_programs(0) - 1)
    def _():
        for_each_fill(lambda c: c.wait())


def _dispatch_call(xn, pos, n_valid, pad_meta, rows):
    t, d = xn.shape
    grid_spec = pltpu.PrefetchScalarGridSpec(
        num_scalar_prefetch=3,
        grid=(t // MOE_TILE,),
        in_specs=[pl.BlockSpec((MOE_TILE, d), lambda i, pos_ref, nv, pad: (i, 0))],
        out_specs=pl.BlockSpec(memory_space=pl.ANY),
        scratch_shapes=[pltpu.VMEM((MOE_TILE, d), F32), pltpu.SemaphoreType.DMA(()), pltpu.SemaphoreType.DMA(())],
    )
    return pl.pallas_call(
        _dispatch_kernel,
        grid_spec=grid_spec,
        out_shape=jax.ShapeDtypeStruct((rows, d), F32),
        compiler_params=_params(("arbitrary",), 32),
        name="moe_dispatch",
    )(pos, n_valid, pad_meta, xn)


def _moe_ffn_kernel(tm_ref, nv_ref, x_ref, wgu_hbm, wd_hbm, o_ref,
                    wgu_f, wd_f, wsems, wgu_b, wd_b, *, layer):
    i = pl.program_id(0)
    nv = nv_ref[0]
    expert, first, wslot, next_expert = tm_ref[0, i], tm_ref[1, i], tm_ref[2, i], tm_ref[3, i]

    def weight_copies(e, s):
        return (pltpu.make_async_copy(wgu_hbm.at[layer, e], wgu_f.at[s], wsems.at[0, s]),
                pltpu.make_async_copy(wd_hbm.at[layer, e], wd_f.at[s], wsems.at[1, s]))

    @pl.when(i == 0)
    def _():
        for c in weight_copies(expert, 0):
            c.start()

    @pl.when(first == 1)
    def _():
        for c in weight_copies(expert, wslot):
            c.wait()
        wgu_b[...] = wgu_f[wslot].astype(BF16)
        wd_b[...] = wd_f[wslot].astype(BF16)

        @pl.when(next_expert >= 0)
        def _():
            for c in weight_copies(next_expert, 1 - wslot):
                c.start()

    @pl.when(i < nv)
    def _():
        h = jnp.dot(x_ref[...].astype(BF16), wgu_b[...], preferred_element_type=F32)
        hg = h[:, :D_FF_E]
        hu = h[:, D_FF_E:]
        act = (hg * _sigmoid(hg) * hu).astype(BF16)
        o_ref[...] = jnp.dot(act, wd_b[...], preferred_element_type=F32)

    @pl.when(i >= nv)
    def _():
        o_ref[...] = jnp.zeros_like(o_ref)


def _moe_ffn_call(xs, w_gu, w_down, layer, tile_meta, n_valid):
    rows, d = xs.shape
    n_tiles = tile_meta.shape[1]
    grid_spec = pltpu.PrefetchScalarGridSpec(
        num_scalar_prefetch=2,
        grid=(n_tiles,),
        in_specs=[pl.BlockSpec((MOE_TILE, d), lambda i, tm, nv: (jnp.minimum(i, nv[0] - 1), 0)),
                  pl.BlockSpec(memory_space=pl.ANY), pl.BlockSpec(memory_space=pl.ANY)],
        out_specs=pl.BlockSpec((MOE_TILE, d), lambda i, tm, nv: (i, 0)),
        scratch_shapes=[
            pltpu.VMEM((2, d, 2 * D_FF_E), F32),
            pltpu.VMEM((2, D_FF_E, d), F32),
            pltpu.SemaphoreType.DMA((2, 2)),
            pltpu.VMEM((d, 2 * D_FF_E), BF16),
            pltpu.VMEM((D_FF_E, d), BF16),
        ],
    )
    return pl.pallas_call(
        functools.partial(_moe_ffn_kernel, layer=layer),
        grid_spec=grid_spec,
        out_shape=jax.ShapeDtypeStruct((rows, d), F32),
        compiler_params=_params(("arbitrary",), 48),
        name="moe_experts",
    )(tile_meta, n_valid, xs, w_gu, w_down)


def _combine_kernel(pos_ref, x_ref, w_ref, y_hbm, g_ref, *rest, emit_x, norm_tiles):
    outs = rest[:int(emit_x) + len(norm_tiles)]
    buf, sems = rest[int(emit_x) + len(norm_tiles):]
    i = pl.program_id(0)
    n = pl.num_programs(0)
    slot = lax.rem(i, 2)
    tm = x_ref.shape[0]

    def gather(tile, dst_slot):
        n_tok = n * tm
        for r in range(tm):
            for kk in range(2):
                pltpu.make_async_copy(y_hbm.at[pl.ds(pos_ref[kk * n_tok + tile * tm + r], 1)],
                                      buf.at[dst_slot, kk, pl.ds(r, 1)], sems.at[dst_slot]).start(priority=kk)

    @pl.when(i == 0)
    def _():
        gather(0, 0)

    for s in range(2):
        pl.when(jnp.logical_and(i + 1 < n, slot == s))(functools.partial(gather, i + 1, 1 - s))

    for kk in range(2):
        pltpu.make_async_copy(y_hbm.at[pl.ds(0, tm)], buf.at[slot, kk], sems.at[slot]).wait()
    xo = x_ref[...] + (w_ref[:, 0:1] * buf[slot, 0] + w_ref[:, 1:2] * buf[slot, 1])
    norm_refs = outs[int(emit_x):]
    if emit_x:
        outs[0][...] = xo

    def write_norm(p):
        norm_refs[p][...] = _rms(xo, g_ref[...]).astype(norm_refs[p].dtype)

    _for_each_part(i, norm_tiles, write_norm)


def _combine_call(x, ws, y_rows, pos, g, emit_x, norm_dtype, norm_rows):
    t, d = x.shape
    tm = MOE_TILE
    row = lambda i, pos_ref: (i, 0)
    out_specs, out_shape = [], []
    if emit_x:
        out_specs.append(pl.BlockSpec((tm, d), row))
        out_shape.append(jax.ShapeDtypeStruct((t, d), F32))
    norm_parts = [jax.ShapeDtypeStruct((r, d), norm_dtype) for r in norm_rows]
    out_specs += _part_specs(norm_parts, tm, d, row)
    out_shape += norm_parts
    grid_spec = pltpu.PrefetchScalarGridSpec(
        num_scalar_prefetch=1,
        grid=(t // tm,),
        in_specs=[pl.BlockSpec((tm, d), row), pl.BlockSpec((tm, LANES), row), pl.BlockSpec(memory_space=pl.ANY),
                  pl.BlockSpec((1, d), lambda i, pos_ref: (0, 0))],
        out_specs=out_specs,
        scratch_shapes=[pltpu.VMEM((2, 2, tm, d), F32), pltpu.SemaphoreType.DMA((2,))],
    )
    return pl.pallas_call(
        functools.partial(_combine_kernel, emit_x=emit_x, norm_tiles=_part_tiles(norm_parts, tm)),
        grid_spec=grid_spec,
        out_shape=out_shape,
        compiler_params=_params(("arbitrary",), 40),
        name="moe_combine",
    )(pos, x, ws, y_rows, g.reshape(1, d))


def _moe_layer(x, g_ffn, w_grp, b_grp, w_exp, b_exp, w_gu, w_down, layer, g_next, emit_x, norm_dtype, norm_rows):
    xn, ids, ws = _router_call(x, g_ffn, w_grp, b_grp, w_exp, b_exp)
    tile_meta, n_valid, pad_meta, pos = _moe_plan(ids[:, 0], ids[:, 1])
    xs = _dispatch_call(xn, pos, n_valid, pad_meta, tile_meta.shape[1] * MOE_TILE)
    y_rows = _moe_ffn_call(xs, w_gu, w_down, layer, tile_meta, n_valid)
    return _combine_call(x, ws, y_rows, pos, g_next, emit_x, norm_dtype, norm_rows)


def _s5_prep_kernel(are_ref, aim_ref, ldt_ref, bre_ref, bim_ref, lre_ref, lim_ref, bbr_ref, bbi_ref):
    are = are_ref[...]
    aim = aim_ref[...]
    dt = jnp.exp(ldt_ref[...])
    mag = jnp.exp(are * dt)
    lr = mag * jnp.cos(aim * dt)
    li = mag * jnp.sin(aim * dt)
    lre_ref[...] = lr
    lim_ref[...] = li
    den = are * are + aim * aim
    nr = lr - 1.0
    cr = (nr * are + li * aim) / den
    ci = (li * are - nr * aim) / den
    bre = bre_ref[...]
    bim = bim_ref[...]
    bbr_ref[...] = cr * bre - ci * bim
    bbi_ref[...] = cr * bim + ci * bre


def _s5_prep_call(a_re, a_im, log_dt, b_re, b_im):
    g, p, c = b_re.shape
    shp = (g, p * c)
    expand = lambda a: jnp.broadcast_to(a[:, :, None], (g, p, c)).reshape(shp)
    ldt = jnp.broadcast_to(log_dt[:, None], shp)
    outs = pl.pallas_call(
        _s5_prep_kernel,
        out_shape=[jax.ShapeDtypeStruct(shp, F32)] * 4,
        name="s5_discretize",
    )(expand(a_re), expand(a_im), ldt, b_re.reshape(shp), b_im.reshape(shp))
    lre, lim, bbr, bbi = outs
    return lre[:, ::c], lim[:, ::c], bbr, bbi


def _s5_mats(bbr, bbi, c_re, c_im):
    eye = jnp.eye(16, dtype=F32)

    def bd_part(bb):
        xx = bb.reshape(S5_FB, 16, S5_P, S5_GC).transpose(0, 1, 3, 2)
        yy = xx[:, :, :, None, :] * eye[None, :, None, :, None]
        return yy.reshape(S5_FB, 16 * S5_GC, 16 * S5_P)

    def cd_part(cc):
        xx = cc.reshape(S5_FB, 16, S5_GC, S5_P).transpose(0, 1, 3, 2)
        yy = xx[:, :, :, None, :] * eye[None, :, None, :, None]
        return yy.reshape(S5_FB, 16 * S5_P, 16 * S5_GC)

    bd = jnp.concatenate([bd_part(bbr), bd_part(bbi)], axis=-1).astype(BF16)
    cd = jnp.concatenate([cd_part(c_re), -cd_part(c_im)], axis=1).astype(BF16)
    return bd, cd


def _s5_kernel(u_ref, bd_ref, cd_ref, lre_ref, lim_ref, d_ref, h0r_ref, h0i_ref,
               z_ref, hr_ref, hi_ref, scr, hst, *, seg, carry):
    rows = u_ref.shape[0]
    n_seq = rows // seg
    ch = 16 * S5_GC
    half = SUBLANES * S5_PITCH
    blk = 2 * half

    for fb in range(S5_FB):
        bu = jnp.dot(u_ref[:, fb * ch:(fb + 1) * ch].astype(BF16), bd_ref[fb], preferred_element_type=F32)
        for c in range(2 * SUBLANES):
            r0 = fb * blk + c * S5_PITCH
            scr[r0:r0 + rows, :] = bu[:, c * LANES:(c + 1) * LANES]

    lre = [lre_ref[fb * SUBLANES:(fb + 1) * SUBLANES, :] for fb in range(S5_FB)]
    lim = [lim_ref[fb * SUBLANES:(fb + 1) * SUBLANES, :] for fb in range(S5_FB)]

    def scan(row0, state):
        def step(l, st):
            new = []
            for fb in range(S5_FB):
                hr, hi = st[2 * fb], st[2 * fb + 1]
                ire = pl.ds(fb * blk + row0 + l, SUBLANES, stride=S5_PITCH)
                iim = pl.ds(fb * blk + half + row0 + l, SUBLANES, stride=S5_PITCH)
                nr = lre[fb] * hr - lim[fb] * hi + scr[ire, :]
                ni = lre[fb] * hi + lim[fb] * hr + scr[iim, :]
                scr[ire, :] = nr
                scr[iim, :] = ni
                new += [nr, ni]
            return tuple(new)
        return lax.fori_loop(0, seg, step, state)

    def load_state(rref, iref, lead):
        st = []
        for fb in range(S5_FB):
            st += [rref[lead + (slice(fb * SUBLANES, (fb + 1) * SUBLANES), slice(None))],
                   iref[lead + (slice(fb * SUBLANES, (fb + 1) * SUBLANES), slice(None))]]
        return tuple(st)

    def store_state(st, rref, iref, lead):
        for fb in range(S5_FB):
            rref[lead + (slice(fb * SUBLANES, (fb + 1) * SUBLANES), slice(None))] = st[2 * fb]
            iref[lead + (slice(fb * SUBLANES, (fb + 1) * SUBLANES), slice(None))] = st[2 * fb + 1]

    if carry:
        @pl.when(pl.program_id(0) == 0)
        def _():
            hst[0] = h0r_ref[0]
            hst[1] = h0i_ref[0]
        hre_scr = hst.at[0]
        him_scr = hst.at[1]
        st = scan(0, load_state(hre_scr, him_scr, ()))
        store_state(st, hre_scr, him_scr, ())
        store_state(st, hr_ref, hi_ref, (0,))
    else:
        for s in range(n_seq):
            st = scan(s * seg, load_state(h0r_ref, h0i_ref, (s,)))
            store_state(st, hr_ref, hi_ref, (s,))

    for fb in range(S5_FB):
        hf = jnp.concatenate(
            [scr[fb * blk + c * S5_PITCH: fb * blk + c * S5_PITCH + rows, :] for c in range(2 * SUBLANES)],
            axis=1).astype(BF16)
        y = jnp.dot(hf, cd_ref[fb], preferred_element_type=F32)
        cs = slice(fb * ch, (fb + 1) * ch)
        yy = y + d_ref[:, cs] * u_ref[:, cs]
        z_ref[:, cs] = jax.nn.gelu(yy).astype(z_ref.dtype)


def _s5_call(u, bd, cd, lre, lim, d_skip, h0r, h0i, row_block0, n_blocks, seg, carry):
    t, d = u.shape
    n_seq = S5_ROWS // seg
    ns_total = h0r.shape[0]
    st_idx = (lambda i: (0, 0, 0)) if carry else (lambda i: (i, 0, 0))
    fixed2 = lambda i: (0, 0)
    fixed3 = lambda i: (0, 0, 0)
    st_rows = S5_FB * SUBLANES
    return pl.pallas_call(
        functools.partial(_s5_kernel, seg=seg, carry=carry),
        grid=(n_blocks,),
        in_specs=[
            pl.BlockSpec((S5_ROWS, d), lambda i: (row_block0 + i, 0)),
            pl.BlockSpec(bd.shape, fixed3),
            pl.BlockSpec(cd.shape, fixed3),
            pl.BlockSpec(lre.shape, fixed2),
            pl.BlockSpec(lim.shape, fixed2),
            pl.BlockSpec((1, d), fixed2),
            pl.BlockSpec((n_seq, st_rows, LANES), st_idx),
            pl.BlockSpec((n_seq, st_rows, LANES), st_idx),
        ],
        out_specs=[
            pl.BlockSpec((S5_ROWS, d), lambda i: (i, 0)),
            pl.BlockSpec((n_seq, st_rows, LANES), st_idx),
            pl.BlockSpec((n_seq, st_rows, LANES), st_idx),
        ],
        out_shape=[
            jax.ShapeDtypeStruct((n_blocks * S5_ROWS, d), BF16),
            jax.ShapeDtypeStruct((ns_total, st_rows, LANES), F32),
            jax.ShapeDtypeStruct((ns_total, st_rows, LANES), F32),
        ],
        scratch_shapes=[
            pltpu.VMEM((2 * SUBLANES * S5_FB * S5_PITCH, LANES), F32),
            pltpu.VMEM((2, st_rows, LANES), F32),
        ],
        compiler_params=_params(("arbitrary",), 56),
        name="s5_scan",
    )(u, bd, cd, lre, lim, d_skip.reshape(1, d), h0r, h0i)


def kernel(x_prompt, x_sample, state_ret, state_s5_re, state_s5_im, norm_mix_g, norm_ffn_g, norm_final_g,
           ret_w_in, ret_gn_g, ret_w_out, s5_w_in, s5_a_re, s5_a_im, s5_log_dt, s5_b_re, s5_b_im,
           s5_c_re, s5_c_im, s5_d, s5_w_out, moe_w_grp, moe_b_grp, moe_w_exp, moe_b_exp, moe_w_gu, moe_w_down):
    bp, seq, d = x_prompt.shape
    db, dseq, _ = x_sample.shape
    assert d == D_MODEL and bp == 1
    assert seq % max(ROW_TILE, RET_ROWS) == 0 and seq % CHUNK == 0
    assert dseq < CHUNK and dseq % 16 == 0 and S5_ROWS % dseq == 0
    tp = bp * seq
    ts = db * dseq
    t = tp + ts
    assert ts % ROW_TILE == 0 and ts % S5_ROWS == 0

    x_parts = [x_prompt.reshape(tp, d), x_sample.reshape(ts, d)]

    half = RET_DK // 2
    freqs = ROPE_BASE ** (-jnp.arange(half, dtype=F32) / half)
    pos = jnp.concatenate([jnp.arange(seq, dtype=F32), jnp.tile(PAST_LEN + jnp.arange(dseq, dtype=F32), db)])
    ang = pos[:, None] * freqs[None, :]
    cos_t = jnp.cos(ang)
    sin_t = jnp.sin(ang)

    xn = _norm_call(x_parts, norm_mix_g[0], BF16)
    tn = 1024
    first_col = lambda j: 0
    qk = _mm_call([xn], ret_w_in[0], [0], tn, 2 * D_MODEL, [([cos_t], half, first_col), ([sin_t], half, first_col)],
                  BF16, functools.partial(_epi_rope, n_q_blocks=D_MODEL // tn), 40, "ret_qk_proj")
    vg = _mm_call([xn], ret_w_in[0], [2 * D_MODEL // tn], tn, 2 * RET_VW, [], BF16, _epi_plain, 40, "ret_vg_proj")
    og_p, ret_p = _ret_prompt_call(qk, vg, ret_gn_g[0], tp)
    og_s, ret_s = _ret_sample_call(qk, vg, ret_gn_g[0], state_ret[0], tp, dseq)
    tn = 512
    same_col = lambda j: j
    x1 = _mm_call([og_p, og_s], ret_w_out[0], [0], tn, d, [(x_parts, tn, same_col)], F32, _epi_residual,
                  40, "ret_out_proj")

    x2, xn = _moe_layer(x1, norm_ffn_g[0], moe_w_grp[0], moe_b_grp[0], moe_w_exp[0], moe_b_exp[0],
                        moe_w_gu, moe_w_down, 0, norm_mix_g[1], True, BF16, (t,))

    u = _mm_call([xn], s5_w_in[0], [0], 1024, d, [], F32, _epi_plain, 40, "s5_in_proj")
    lre, lim, bbr, bbi = _s5_prep_call(s5_a_re[0], s5_a_im[0], s5_log_dt[0], s5_b_re[0], s5_b_im[0])
    bd, cd = _s5_mats(bbr, bbi, s5_c_re[0], s5_c_im[0])
    st_shape = (S5_FB * SUBLANES, LANES)
    lre = lre.reshape(st_shape)
    lim = lim.reshape(st_shape)
    zero_state = jnp.zeros((1,) + st_shape, F32)
    z_p, hpr, hpi = _s5_call(u, bd, cd, lre, lim, s5_d[0], zero_state, zero_state,
                             0, tp // S5_ROWS, S5_ROWS, True)
    z_s, hsr, hsi = _s5_call(u, bd, cd, lre, lim, s5_d[0],
                             state_s5_re[0].reshape((db,) + st_shape), state_s5_im[0].reshape((db,) + st_shape),
                             tp // S5_ROWS, ts // S5_ROWS, dseq, False)
    x3 = _mm_call([z_p, z_s], s5_w_out[0], [0, d // tn], tn, d, [([x2], tn, same_col)], F32, _epi_glu_residual,
                  40, "s5_out_proj")

    y_p, y_s = _moe_layer(x3, norm_ffn_g[1], moe_w_grp[1], moe_b_grp[1], moe_w_exp[1], moe_b_exp[1],
                          moe_w_gu, moe_w_down, 1, norm_final_g, False, F32, (tp, ts))

    s5_shape = (1, -1, S5_GROUPS, S5_P)
    return (y_p.reshape(bp, seq, d), y_s.reshape(db, dseq, d),
            ret_p.reshape(1, bp, RET_HEADS, RET_DK, RET_DV), ret_s.reshape((1,) + ret_s.shape),
            hpr.reshape(s5_shape), hpi.reshape(s5_shape), hsr.reshape(s5_shape), hsi.reshape(s5_shape))
```

```python
import functools
import math

import numpy as np
import jax
import jax.numpy as jnp
from jax import lax
from jax.experimental import pallas as pl
from jax.experimental.pallas import tpu as pltpu

F32 = jnp.float32
BF16 = jnp.bfloat16

D_MODEL = 2048
PAST_LEN = 2048
CHUNK = 64
RET_HEADS = 8
RET_DK = D_MODEL // RET_HEADS
RET_DV = 2 * RET_DK
RET_VW = RET_HEADS * RET_DV
ROPE_BASE = 10000.0
S5_GC = 16
S5_GROUPS = D_MODEL // S5_GC
S5_P = 64
MOE_GROUPS = 4
MOE_PER_GROUP = 8
N_EXP = MOE_GROUPS * MOE_PER_GROUP
D_FF_E = 512
EPS = 1e-6

LANES = 128
SUBLANES = 8
ROW_TILE = 512
MM_CHUNK = 256
RET_ROWS = 256
MOE_TILE = 256
S5_ROWS = 256
S5_FB = S5_GROUPS // 16
S5_PITCH = S5_ROWS + 8
MIB = 1024 * 1024


def _params(sem, vmem_mib):
    return pltpu.CompilerParams(dimension_semantics=sem, vmem_limit_bytes=vmem_mib * MIB)


def _rms(x, g):
    ms = jnp.mean(x * x, axis=-1, keepdims=True)
    return x * lax.rsqrt(ms + EPS) * g


def _sigmoid(x):
    return 1.0 / (1.0 + jnp.exp(-x))


def _part_specs(parts, rows, cols, rowcol):
    specs = []
    start = 0
    for p in parts:
        nt = p.shape[0] // rows

        def imap(*g, start=start, nt=nt):
            i, jc = rowcol(*g)
            return (jnp.clip(i - start, 0, nt - 1), jc)

        specs.append(pl.BlockSpec((rows, cols), imap))
        start += nt
    return specs


def _part_tiles(parts, rows):
    return tuple(p.shape[0] // rows for p in parts)


def _for_each_part(i, tiles, fn):
    if len(tiles) == 1:
        fn(0)
        return
    start = 0
    for p, nt in enumerate(tiles):
        pl.when(jnp.logical_and(i >= start, i < start + nt))(functools.partial(fn, p))
        start += nt


def _pick(refs, p):
    return refs[min(p, len(refs) - 1)]


def _norm_kernel(*refs, tiles):
    n = len(tiles)
    x_refs, g_ref, o_ref = refs[:n], refs[n], refs[n + 1]

    def run(p):
        o_ref[...] = _rms(x_refs[p][...], g_ref[...]).astype(o_ref.dtype)

    _for_each_part(pl.program_id(0), tiles, run)


def _norm_call(x_parts, g, out_dtype):
    d = x_parts[0].shape[1]
    t = sum(p.shape[0] for p in x_parts)
    return pl.pallas_call(
        functools.partial(_norm_kernel, tiles=_part_tiles(x_parts, ROW_TILE)),
        grid=(t // ROW_TILE,),
        in_specs=_part_specs(x_parts, ROW_TILE, d, lambda i: (i, 0)) + [pl.BlockSpec((1, d), lambda i: (0, 0))],
        out_specs=pl.BlockSpec((ROW_TILE, d), lambda i: (i, 0)),
        out_shape=jax.ShapeDtypeStruct((t, d), out_dtype),
        compiler_params=_params(("arbitrary",), 32),
        name="rmsnorm",
    )(*x_parts, g.reshape(1, d))


def _mm_kernel(*refs, tiles, n_x, n_w, n_extra, epi):
    x_refs = refs[:n_x]
    w_refs = refs[n_x:n_x + n_w]
    pos = n_x + n_w
    extra = []
    for n in n_extra:
        extra.append(refs[pos:pos + n])
        pos += n
    o_ref = refs[pos]
    wb = refs[pos + 1:]

    @pl.when(pl.program_id(1) == 0)
    def _():
        for w_ref, b in zip(w_refs, wb):
            b[...] = w_ref[...].astype(BF16)

    def run(p):
        x = _pick(x_refs, p)[...]
        for c in range(o_ref.shape[1] // MM_CHUNK):
            cs = slice(c * MM_CHUNK, (c + 1) * MM_CHUNK)
            accs = [jnp.dot(x, b[:, cs], preferred_element_type=F32) for b in wb]
            epi(accs, [_pick(e, p) for e in extra], o_ref, cs)

    _for_each_part(pl.program_id(1), tiles, run)


def _mm_call(x_parts, w, w_col_offsets, tn, n_out, extras, out_dtype, epi, vmem_mib, name):
    k = x_parts[0].shape[1]
    t = sum(p.shape[0] for p in x_parts)
    n_w = len(w_col_offsets)
    split = [parts for parts in [x_parts] + [e[0] for e in extras] if len(parts) > 1]
    tiles = _part_tiles(split[0], ROW_TILE) if split else (t // ROW_TILE,)
    assert all(_part_tiles(parts, ROW_TILE) == tiles for parts in split)
    in_specs = _part_specs(x_parts, ROW_TILE, k, lambda j, i: (i, 0))
    for off in w_col_offsets:
        in_specs.append(pl.BlockSpec((k, tn), lambda j, i, off=off: (0, j + off)))
    extra_arrays = []
    for parts, cols, colfn in extras:
        in_specs += _part_specs(parts, ROW_TILE, cols, lambda j, i, colfn=colfn: (i, colfn(j)))
        extra_arrays += list(parts)
    return pl.pallas_call(
        functools.partial(_mm_kernel, tiles=tiles, n_x=len(x_parts), n_w=n_w,
                          n_extra=tuple(len(e[0]) for e in extras), epi=epi),
        grid=(n_out // tn, t // ROW_TILE),
        in_specs=in_specs,
        out_specs=pl.BlockSpec((ROW_TILE, tn), lambda j, i: (i, j)),
        out_shape=jax.ShapeDtypeStruct((t, n_out), out_dtype),
        scratch_shapes=[pltpu.VMEM((k, tn), BF16) for _ in range(n_w)],
        compiler_params=_params(("arbitrary", "arbitrary"), vmem_mib),
        name=name,
    )(*x_parts, *([w] * n_w), *extra_arrays)


def _epi_plain(accs, extra, o_ref, cs):
    o_ref[:, cs] = accs[0].astype(o_ref.dtype)


def _epi_residual(accs, extra, o_ref, cs):
    o_ref[:, cs] = (extra[0][:, cs] + accs[0]).astype(o_ref.dtype)


def _epi_glu_residual(accs, extra, o_ref, cs):
    a, gt = accs
    o_ref[:, cs] = (extra[0][:, cs] + a * _sigmoid(gt)).astype(o_ref.dtype)


def _epi_rope(accs, extra, o_ref, cs, *, n_q_blocks):
    acc = accs[0]
    assert acc.shape[1] == RET_DK
    cos_ref, sin_ref = extra
    half = RET_DK // 2
    scale = jnp.where(pl.program_id(0) >= n_q_blocks, RET_DK ** -0.5, 1.0).astype(F32)
    c = cos_ref[...]
    s = sin_ref[...]
    t1 = acc[:, :half]
    t2 = acc[:, half:]
    o_ref[:, cs.start:cs.start + half] = ((t1 * c - t2 * s) * scale).astype(o_ref.dtype)
    o_ref[:, cs.start + half:cs.stop] = ((t1 * s + t2 * c) * scale).astype(o_ref.dtype)


def _ret_tables(rows, chunk):
    lg = jnp.log(jnp.asarray(1.0 - 2.0 ** (-5.0 - np.arange(RET_HEADS)), dtype=F32))[:, None, None]
    n = jnp.arange(rows, dtype=F32)
    ci = np.arange(rows) // chunk
    same_or_earlier = jnp.asarray(ci[None, :] <= ci[:, None])
    dmat = jnp.where(same_or_earlier[None], jnp.exp(jnp.abs(n[:, None] - n[None, :])[None] * lg), 0.0)
    dq = jnp.exp((n + 1.0)[None, :, None] * lg) * jnp.ones((1, 1, LANES), F32)
    dk = jnp.exp((rows - 1.0 - n)[None, :, None] * lg) * jnp.ones((1, 1, LANES), F32)
    gr = jnp.exp(rows * lg) * jnp.ones((1, 1, RET_DV), F32)
    return dmat, dq, dk, gr


def _ret_block(q, k, v, g, s_prev, dmat, dq, dk, gr, gn):
    s = lax.dot_general(q, k, (((1,), (1,)), ((), ())), preferred_element_type=F32) * dmat
    inter = jnp.dot(q, s_prev.astype(BF16), preferred_element_type=F32)
    o = jnp.dot(s.astype(BF16), v, preferred_element_type=F32) + jnp.concatenate([dq] * (RET_DV // LANES), axis=1) * inter
    kd = (k.astype(F32) * jnp.concatenate([dk] * (RET_DK // LANES), axis=1)).astype(BF16)
    s_new = s_prev * gr + lax.dot_general(kd, v, (((0,), (0,)), ((), ())), preferred_element_type=F32)
    mu = jnp.mean(o, axis=-1, keepdims=True)
    oc = o - mu
    var = jnp.mean(oc * oc, axis=-1, keepdims=True)
    on = oc * lax.rsqrt(var + EPS) * gn
    gf = g.astype(F32)
    return (gf * _sigmoid(gf) * on).astype(BF16), s_new


def _ret_prompt_kernel(q_ref, k_ref, v_ref, g_ref, dm_ref, dq_ref, dk_ref, gr_ref, gn_ref,
                       og_ref, st_ref, s_scr):
    @pl.when(pl.program_id(1) == 0)
    def _():
        s_scr[...] = jnp.zeros_like(s_scr)

    og, s_new = _ret_block(q_ref[...], k_ref[...], v_ref[...], g_ref[...], s_scr[...],
                           dm_ref[0], dq_ref[0], dk_ref[0], gr_ref[0], gn_ref[...])
    og_ref[...] = og
    s_scr[...] = s_new
    st_ref[0] = s_new


def _ret_prompt_call(qk, vg, gn, t_prompt):
    dm, dq, dk, gr = _ret_tables(RET_ROWS, CHUNK)
    h = RET_HEADS
    return pl.pallas_call(
        _ret_prompt_kernel,
        grid=(h, t_prompt // RET_ROWS),
        in_specs=[
            pl.BlockSpec((RET_ROWS, RET_DK), lambda hh, c: (c, hh)),
            pl.BlockSpec((RET_ROWS, RET_DK), lambda hh, c: (c, h + hh)),
            pl.BlockSpec((RET_ROWS, RET_DV), lambda hh, c: (c, hh)),
            pl.BlockSpec((RET_ROWS, RET_DV), lambda hh, c: (c, h + hh)),
            pl.BlockSpec((1, RET_ROWS, RET_ROWS), lambda hh, c: (hh, 0, 0)),
            pl.BlockSpec((1, RET_ROWS, LANES), lambda hh, c: (hh, 0, 0)),
            pl.BlockSpec((1, RET_ROWS, LANES), lambda hh, c: (hh, 0, 0)),
            pl.BlockSpec((1, 1, RET_DV), lambda hh, c: (hh, 0, 0)),
            pl.BlockSpec((1, RET_DV), lambda hh, c: (0, hh)),
        ],
        out_specs=[
            pl.BlockSpec((RET_ROWS, RET_DV), lambda hh, c: (c, hh)),
            pl.BlockSpec((1, RET_DK, RET_DV), lambda hh, c: (hh, 0, 0)),
        ],
        out_shape=[
            jax.ShapeDtypeStruct((t_prompt, RET_VW), BF16),
            jax.ShapeDtypeStruct((h, RET_DK, RET_DV), F32),
        ],
        scratch_shapes=[pltpu.VMEM((RET_DK, RET_DV), F32)],
        compiler_params=_params(("arbitrary", "arbitrary"), 32),
        name="retention_prompt",
    )(qk, qk, vg, vg, dm, dq, dk, gr, gn.reshape(1, RET_VW))


def _ret_sample_kernel(q_ref, k_ref, v_ref, g_ref, s0_ref, dm_ref, dq_ref, dk_ref, gr_ref, gn_ref,
                       og_ref, st_ref):
    for hh in range(RET_HEADS):
        ks = slice(hh * RET_DK, (hh + 1) * RET_DK)
        vs = slice(hh * RET_DV, (hh + 1) * RET_DV)
        og, s_new = _ret_block(q_ref[:, ks], k_ref[:, ks], v_ref[:, vs], g_ref[:, vs], s0_ref[0, hh],
                               dm_ref[hh], dq_ref[hh], dk_ref[hh], gr_ref[hh], gn_ref[:, vs])
        og_ref[:, vs] = og
        st_ref[0, hh] = s_new


def _ret_sample_call(qk, vg, gn, state, t_prompt, dec_seq):
    dm, dq, dk, gr = _ret_tables(dec_seq, dec_seq)
    b = state.shape[0]
    r0 = t_prompt // dec_seq
    const3 = lambda bb: (0, 0, 0)
    return pl.pallas_call(
        _ret_sample_kernel,
        grid=(b,),
        in_specs=[
            pl.BlockSpec((dec_seq, D_MODEL), lambda bb: (r0 + bb, 0)),
            pl.BlockSpec((dec_seq, D_MODEL), lambda bb: (r0 + bb, 1)),
            pl.BlockSpec((dec_seq, RET_VW), lambda bb: (r0 + bb, 0)),
            pl.BlockSpec((dec_seq, RET_VW), lambda bb: (r0 + bb, 1)),
            pl.BlockSpec((1, RET_HEADS, RET_DK, RET_DV), lambda bb: (bb, 0, 0, 0)),
            pl.BlockSpec(dm.shape, const3),
            pl.BlockSpec(dq.shape, const3),
            pl.BlockSpec(dk.shape, const3),
            pl.BlockSpec(gr.shape, const3),
            pl.BlockSpec((1, RET_VW), lambda bb: (0, 0)),
        ],
        out_specs=[
            pl.BlockSpec((dec_seq, RET_VW), lambda bb: (bb, 0)),
            pl.BlockSpec((1, RET_HEADS, RET_DK, RET_DV), lambda bb: (bb, 0, 0, 0)),
        ],
        out_shape=[
            jax.ShapeDtypeStruct((b * dec_seq, RET_VW), BF16),
            jax.ShapeDtypeStruct(state.shape, F32),
        ],
        compiler_params=_params(("arbitrary",), 40),
        name="retention_sample",
    )(qk, qk, vg, vg, state, dm, dq, dk, gr, gn.reshape(1, RET_VW))


def _router_kernel(x_ref, g_ref, wr_ref, br_ref, xn_ref, id_ref, w_ref):
    xn = _rms(x_ref[...], g_ref[...])
    xn_ref[...] = xn
    xh = xn.astype(BF16)
    xl = (xn - xh.astype(F32)).astype(BF16)
    w = wr_ref[...]
    wh = w.astype(BF16)
    wl = (w - wh.astype(F32)).astype(BF16)
    lg = (jnp.dot(xh, wh, preferred_element_type=F32) + jnp.dot(xh, wl, preferred_element_type=F32)
          + jnp.dot(xl, wh, preferred_element_type=F32)) + br_ref[...]
    lane = lax.broadcasted_iota(jnp.int32, lg.shape, 1).astype(F32)
    neg = jnp.float32(-jnp.inf)
    big = jnp.float32(LANES)
    gl = jnp.where(lane < MOE_GROUPS, lg, neg)
    gmax = jnp.max(gl, axis=-1, keepdims=True)
    gsel = jnp.min(jnp.where(gl == gmax, lane, big), axis=-1, keepdims=True)
    gw = 1.0 / jnp.sum(jnp.exp(gl - gmax), axis=-1, keepdims=True)
    lo = MOE_GROUPS + gsel * MOE_PER_GROUP
    el = jnp.where((lane >= lo) & (lane < lo + MOE_PER_GROUP), lg, neg)
    m1 = jnp.max(el, axis=-1, keepdims=True)
    i1 = jnp.min(jnp.where(el == m1, lane, big), axis=-1, keepdims=True)
    el2 = jnp.where(lane == i1, neg, el)
    m2 = jnp.max(el2, axis=-1, keepdims=True)
    i2 = jnp.min(jnp.where(el2 == m2, lane, big), axis=-1, keepdims=True)
    z = jnp.sum(jnp.exp(el - m1), axis=-1, keepdims=True)
    p1 = 1.0 / z
    p2 = jnp.exp(m2 - m1) / z
    tot = p1 + p2
    w1 = p1 / tot * gw
    w2 = p2 / tot * gw
    ids = jnp.where(lane == 0, i1 - MOE_GROUPS, jnp.where(lane == 1, i2 - MOE_GROUPS, 0.0))
    id_ref[...] = ids.astype(jnp.int32)
    w_ref[...] = jnp.where(lane == 0, w1, jnp.where(lane == 1, w2, 0.0))


def _router_call(x, g, w_grp, b_grp, w_exp, b_exp):
    t, d = x.shape
    pad = LANES - MOE_GROUPS - N_EXP
    wr = jnp.concatenate([w_grp, w_exp, jnp.zeros((d, pad), F32)], axis=1)
    br = jnp.concatenate([b_grp, b_exp, jnp.zeros((pad,), F32)]).reshape(1, LANES)
    row = lambda i: (i, 0)
    fixed = lambda i: (0, 0)
    return pl.pallas_call(
        _router_kernel,
        grid=(t // ROW_TILE,),
        in_specs=[pl.BlockSpec((ROW_TILE, d), row), pl.BlockSpec((1, d), fixed),
                  pl.BlockSpec((d, LANES), fixed), pl.BlockSpec((1, LANES), fixed)],
        out_specs=[pl.BlockSpec((ROW_TILE, d), row), pl.BlockSpec((ROW_TILE, LANES), row),
                   pl.BlockSpec((ROW_TILE, LANES), row)],
        out_shape=[jax.ShapeDtypeStruct((t, d), F32), jax.ShapeDtypeStruct((t, LANES), jnp.int32),
                   jax.ShapeDtypeStruct((t, LANES), F32)],
        compiler_params=_params(("arbitrary",), 40),
        name="moe_router",
    )(x, g.reshape(1, d), wr, br)


def _moe_plan(e0, e1):
    t = e0.shape[0]
    n_tiles = 2 * t // MOE_TILE + N_EXP
    experts = jnp.arange(N_EXP, dtype=jnp.int32)[None, :]
    hot0 = (e0[:, None] == experts).astype(jnp.int32)
    hot1 = (e1[:, None] == experts).astype(jnp.int32)
    csum0 = jnp.cumsum(hot0, axis=0)
    csum1 = jnp.cumsum(hot1, axis=0)
    counts0 = csum0[-1]
    counts = counts0 + csum1[-1]
    tiles_e = (counts + MOE_TILE - 1) // MOE_TILE
    tiles_end = jnp.cumsum(tiles_e)
    row_start = (tiles_end - tiles_e) * MOE_TILE
    pos0 = jnp.sum(hot0 * (row_start[None, :] + csum0 - 1), axis=1)
    pos1 = jnp.sum(hot1 * (row_start[None, :] + counts0[None, :] + csum1 - 1), axis=1)
    pos = jnp.concatenate([pos0, pos1])
    n_valid = tiles_end[-1]
    tile_id = jnp.minimum(jnp.arange(n_tiles, dtype=jnp.int32), n_valid - 1)
    tile_expert = jnp.sum((tile_id[:, None] >= tiles_end[None, :]).astype(jnp.int32), axis=1)
    tile_expert = jnp.minimum(tile_expert, N_EXP - 1)
    changed = jnp.concatenate([jnp.ones((1,), bool), tile_expert[1:] != tile_expert[:-1]])
    first = jnp.logical_and(changed, jnp.arange(n_tiles) < n_valid).astype(jnp.int32)
    wslot = (jnp.cumsum(first) - 1) % 2
    next_start = tiles_end[tile_expert]
    next_expert = jnp.where(next_start < n_valid, tile_expert[jnp.minimum(next_start, n_tiles - 1)], -1)
    tile_meta = jnp.stack([tile_expert, first, wslot, next_expert]).astype(jnp.int32)
    pad_meta = jnp.stack([row_start + counts, tiles_e * MOE_TILE - counts]).astype(jnp.int32)
    return tile_meta, n_valid.reshape(1).astype(jnp.int32), pad_meta, pos.astype(jnp.int32)


def _dispatch_kernel(pos_ref, nv_ref, pad_ref, x_ref, xs_hbm, zeros, sem, zsem):
    i = pl.program_id(0)
    tm = x_ref.shape[0]
    n_tiles = xs_hbm.shape[0] // tm
    n_tok = pl.num_programs(0) * tm

    for r in range(tm):
        for kk in range(2):
            pltpu.make_async_copy(x_ref.at[pl.ds(r, 1)], xs_hbm.at[pl.ds(pos_ref[kk * n_tok + i * tm + r], 1)],
                                  sem).start(priority=kk)

    def pad_row_copy(e, j):
        return pltpu.make_async_copy(zeros.at[pl.ds(0, 1)], xs_hbm.at[pl.ds(pad_ref[0, e] + j, 1)], zsem)

    def tail_tile_copy(tile):
        return pltpu.make_async_copy(zeros, xs_hbm.at[pl.ds(tile * tm, tm)], zsem)

    def for_each_fill(fn):
        def per_expert(e, carry):
            lax.fori_loop(0, pad_ref[1, e], lambda j, c: (fn(pad_row_copy(e, j)), c)[1], 0)
            return carry
        lax.fori_loop(0, N_EXP, per_expert, 0)
        lax.fori_loop(nv_ref[0], n_tiles, lambda tile, c: (fn(tail_tile_copy(tile)), c)[1], 0)

    @pl.when(i == 0)
    def _():
        zeros[...] = jnp.zeros_like(zeros)
        for_each_fill(lambda c: c.start())

    for kk in range(2):
        pltpu.make_async_copy(x_ref, xs_hbm.at[pl.ds(0, tm)], sem).wait()

    @pl.when(i == pl.num_programs(0) - 1)
    def _():
        for_each_fill(lambda c: c.wait())


def _dispatch_call(xn, pos, n_valid, pad_meta, rows):
    t, d = xn.shape
    grid_spec = pltpu.PrefetchScalarGridSpec(
        num_scalar_prefetch=3,
        grid=(t // MOE_TILE,),
        in_specs=[pl.BlockSpec((MOE_TILE, d), lambda i, pos_ref, nv, pad: (i, 0))],
        out_specs=pl.BlockSpec(memory_space=pl.ANY),
        scratch_shapes=[pltpu.VMEM((MOE_TILE, d), F32), pltpu.SemaphoreType.DMA(()), pltpu.SemaphoreType.DMA(())],
    )
    return pl.pallas_call(
        _dispatch_kernel,
        grid_spec=grid_spec,
        out_shape=jax.ShapeDtypeStruct((rows, d), F32),
        compiler_params=_params(("arbitrary",), 32),
        name="moe_dispatch",
    )(pos, n_valid, pad_meta, xn)


def _moe_ffn_kernel(tm_ref, nv_ref, x_ref, wgu_hbm, wd_hbm, o_ref,
                    wgu_f, wd_f, wsems, wgu_b, wd_b, *, layer):
    i = pl.program_id(0)
    nv = nv_ref[0]
    expert, first, wslot, next_expert = tm_ref[0, i], tm_ref[1, i], tm_ref[2, i], tm_ref[3, i]

    def weight_copies(e, s):
        return (pltpu.make_async_copy(wgu_hbm.at[layer, e], wgu_f.at[s], wsems.at[0, s]),
                pltpu.make_async_copy(wd_hbm.at[layer, e], wd_f.at[s], wsems.at[1, s]))

    @pl.when(i == 0)
    def _():
        for c in weight_copies(expert, 0):
            c.start()

    @pl.when(first == 1)
    def _():
        for c in weight_copies(expert, wslot):
            c.wait()
        wgu_b[...] = wgu_f[wslot].astype(BF16)
        wd_b[...] = wd_f[wslot].astype(BF16)

        @pl.when(next_expert >= 0)
        def _():
            for c in weight_copies(next_expert, 1 - wslot):
                c.start()

    @pl.when(i < nv)
    def _():
        h = jnp.dot(x_ref[...].astype(BF16), wgu_b[...], preferred_element_type=F32)
        hg = h[:, :D_FF_E]
        hu = h[:, D_FF_E:]
        act = (hg * _sigmoid(hg) * hu).astype(BF16)
        o_ref[...] = jnp.dot(act, wd_b[...], preferred_element_type=F32)

    @pl.when(i >= nv)
    def _():
        o_ref[...] = jnp.zeros_like(o_ref)


def _moe_ffn_call(xs, w_gu, w_down, layer, tile_meta, n_valid):
    rows, d = xs.shape
    n_tiles = tile_meta.shape[1]
    grid_spec = pltpu.PrefetchScalarGridSpec(
        num_scalar_prefetch=2,
        grid=(n_tiles,),
        in_specs=[pl.BlockSpec((MOE_TILE, d), lambda i, tm, nv: (jnp.minimum(i, nv[0] - 1), 0)),
                  pl.BlockSpec(memory_space=pl.ANY), pl.BlockSpec(memory_space=pl.ANY)],
        out_specs=pl.BlockSpec((MOE_TILE, d), lambda i, tm, nv: (i, 0)),
        scratch_shapes=[
            pltpu.VMEM((2, d, 2 * D_FF_E), F32),
            pltpu.VMEM((2, D_FF_E, d), F32),
            pltpu.SemaphoreType.DMA((2, 2)),
            pltpu.VMEM((d, 2 * D_FF_E), BF16),
            pltpu.VMEM((D_FF_E, d), BF16),
        ],
    )
    return pl.pallas_call(
        functools.partial(_moe_ffn_kernel, layer=layer),
        grid_spec=grid_spec,
        out_shape=jax.ShapeDtypeStruct((rows, d), F32),
        compiler_params=_params(("arbitrary",), 48),
        name="moe_experts",
    )(tile_meta, n_valid, xs, w_gu, w_down)


def _combine_kernel(pos_ref, x_ref, w_ref, y_hbm, g_ref, *rest, emit_x, norm_tiles):
    outs = rest[:int(emit_x) + len(norm_tiles)]
    buf, sems = rest[int(emit_x) + len(norm_tiles):]
    i = pl.program_id(0)
    n = pl.num_programs(0)
    slot = lax.rem(i, 2)
    tm = x_ref.shape[0]

    def gather(tile, dst_slot):
        n_tok = n * tm
        for r in range(tm):
            for kk in range(2):
                pltpu.make_async_copy(y_hbm.at[pl.ds(pos_ref[kk * n_tok + tile * tm + r], 1)],
                                      buf.at[dst_slot, kk, pl.ds(r, 1)], sems.at[dst_slot]).start(priority=kk)

    @pl.when(i == 0)
    def _():
        gather(0, 0)

    for s in range(2):
        pl.when(jnp.logical_and(i + 1 < n, slot == s))(functools.partial(gather, i + 1, 1 - s))

    for kk in range(2):
        pltpu.make_async_copy(y_hbm.at[pl.ds(0, tm)], buf.at[slot, kk], sems.at[slot]).wait()
    xo = x_ref[...] + (w_ref[:, 0:1] * buf[slot, 0] + w_ref[:, 1:2] * buf[slot, 1])
    norm_refs = outs[int(emit_x):]
    if emit_x:
        outs[0][...] = xo

    def write_norm(p):
        norm_refs[p][...] = _rms(xo, g_ref[...]).astype(norm_refs[p].dtype)

    _for_each_part(i, norm_tiles, write_norm)


def _combine_call(x, ws, y_rows, pos, g, emit_x, norm_dtype, norm_rows):
    t, d = x.shape
    tm = MOE_TILE
    row = lambda i, pos_ref: (i, 0)
    out_specs, out_shape = [], []
    if emit_x:
        out_specs.append(pl.BlockSpec((tm, d), row))
        out_shape.append(jax.ShapeDtypeStruct((t, d), F32))
    norm_parts = [jax.ShapeDtypeStruct((r, d), norm_dtype) for r in norm_rows]
    out_specs += _part_specs(norm_parts, tm, d, row)
    out_shape += norm_parts
    grid_spec = pltpu.PrefetchScalarGridSpec(
        num_scalar_prefetch=1,
        grid=(t // tm,),
        in_specs=[pl.BlockSpec((tm, d), row), pl.BlockSpec((tm, LANES), row), pl.BlockSpec(memory_space=pl.ANY),
                  pl.BlockSpec((1, d), lambda i, pos_ref: (0, 0))],
        out_specs=out_specs,
        scratch_shapes=[pltpu.VMEM((2, 2, tm, d), F32), pltpu.SemaphoreType.DMA((2,))],
    )
    return pl.pallas_call(
        functools.partial(_combine_kernel, emit_x=emit_x, norm_tiles=_part_tiles(norm_parts, tm)),
        grid_spec=grid_spec,
        out_shape=out_shape,
        compiler_params=_params(("arbitrary",), 40),
        name="moe_combine",
    )(pos, x, ws, y_rows, g.reshape(1, d))


def _moe_layer(x, g_ffn, w_grp, b_grp, w_exp, b_exp, w_gu, w_down, layer, g_next, emit_x, norm_dtype, norm_rows):
    xn, ids, ws = _router_call(x, g_ffn, w_grp, b_grp, w_exp, b_exp)
    tile_meta, n_valid, pad_meta, pos = _moe_plan(ids[:, 0], ids[:, 1])
    xs = _dispatch_call(xn, pos, n_valid, pad_meta, tile_meta.shape[1] * MOE_TILE)
    y_rows = _moe_ffn_call(xs, w_gu, w_down, layer, tile_meta, n_valid)
    return _combine_call(x, ws, y_rows, pos, g_next, emit_x, norm_dtype, norm_rows)


def _s5_prep_kernel(are_ref, aim_ref, ldt_ref, bre_ref, bim_ref, lre_ref, lim_ref, bbr_ref, bbi_ref):
    are = are_ref[...]
    aim = aim_ref[...]
    dt = jnp.exp(ldt_ref[...])
    mag = jnp.exp(are * dt)
    lr = mag * jnp.cos(aim * dt)
    li = mag * jnp.sin(aim * dt)
    lre_ref[...] = lr
    lim_ref[...] = li
    den = are * are + aim * aim
    nr = lr - 1.0
    cr = (nr * are + li * aim) / den
    ci = (li * are - nr * aim) / den
    bre = bre_ref[...]
    bim = bim_ref[...]
    bbr_ref[...] = cr * bre - ci * bim
    bbi_ref[...] = cr * bim + ci * bre


def _s5_prep_call(a_re, a_im, log_dt, b_re, b_im):
    g, p, c = b_re.shape
    shp = (g, p * c)
    expand = lambda a: jnp.broadcast_to(a[:, :, None], (g, p, c)).reshape(shp)
    ldt = jnp.broadcast_to(log_dt[:, None], shp)
    outs = pl.pallas_call(
        _s5_prep_kernel,
        out_shape=[jax.ShapeDtypeStruct(shp, F32)] * 4,
        name="s5_discretize",
    )(expand(a_re), expand(a_im), ldt, b_re.reshape(shp), b_im.reshape(shp))
    lre, lim, bbr, bbi = outs
    return lre[:, ::c], lim[:, ::c], bbr, bbi


def _s5_compact(bbr, bbi, c_re, c_im):
    def b_part(bb):
        return bb.reshape(S5_FB, 16, S5_P, S5_GC).transpose(0, 1, 3, 2).reshape(S5_FB, 16 * S5_GC, S5_P)

    def c_part(cc):
        return cc.reshape(S5_FB, 16, S5_GC, S5_P).transpose(0, 3, 1, 2).reshape(S5_FB, S5_P, 16 * S5_GC)

    return b_part(bbr), b_part(bbi), c_part(c_re), c_part(c_im)


def _s5_build_mats(bbr_ref, bbi_ref, ccr_ref, cci_ref, bd_ref, cd_ref):
    nch, nst = 16 * S5_GC, 16 * S5_P
    iota = lambda shape, dim: lax.broadcasted_iota(jnp.int32, shape, dim)
    rep = ((iota((S5_P, nst), 1) & (S5_P - 1)) == iota((S5_P, nst), 0)).astype(BF16)
    rep_t = ((iota((nst, S5_P), 0) & (S5_P - 1)) == iota((nst, S5_P), 1)).astype(BF16)
    gc_bits, p_bits = S5_GC.bit_length() - 1, S5_P.bit_length() - 1
    diag = (iota((nch, nst), 0) >> gc_bits) == (iota((nch, nst), 1) >> p_bits)
    diag_t = (iota((nst, nch), 0) >> p_bits) == (iota((nst, nch), 1) >> gc_bits)
    for fb in range(S5_FB):
        for part, src in enumerate((bbr_ref, bbi_ref)):
            full = jnp.dot(src[fb].astype(BF16), rep, preferred_element_type=F32)
            bd_ref[fb, :, part * nst:(part + 1) * nst] = jnp.where(diag, full, 0.0).astype(BF16)
        for part, (src, sign) in enumerate(((ccr_ref, 1.0), (cci_ref, -1.0))):
            full = jnp.dot(rep_t, src[fb].astype(BF16), preferred_element_type=F32)
            cd_ref[fb, part * nst:(part + 1) * nst, :] = jnp.where(diag_t, sign * full, 0.0).astype(BF16)


def _s5_kernel(u_ref, bbr_ref, bbi_ref, ccr_ref, cci_ref, lre_ref, lim_ref, d_ref, h0r_ref, h0i_ref,
               z_ref, hr_ref, hi_ref, scr, hst, bd_ref, cd_ref, *, seg, carry):
    rows = u_ref.shape[0]
    n_seq = rows // seg
    ch = 16 * S5_GC
    half = SUBLANES * S5_PITCH
    blk = 2 * half

    @pl.when(pl.program_id(0) == 0)
    def _():
        _s5_build_mats(bbr_ref, bbi_ref, ccr_ref, cci_ref, bd_ref, cd_ref)

    for fb in range(S5_FB):
        bu = jnp.dot(u_ref[:, fb * ch:(fb + 1) * ch].astype(BF16), bd_ref[fb], preferred_element_type=F32)
        for c in range(2 * SUBLANES):
            r0 = fb * blk + c * S5_PITCH
            scr[r0:r0 + rows, :] = bu[:, c * LANES:(c + 1) * LANES]

    lre = [lre_ref[fb * SUBLANES:(fb + 1) * SUBLANES, :] for fb in range(S5_FB)]
    lim = [lim_ref[fb * SUBLANES:(fb + 1) * SUBLANES, :] for fb in range(S5_FB)]

    def scan(row0, state):
        def step(l, st):
            new = []
            for fb in range(S5_FB):
                hr, hi = st[2 * fb], st[2 * fb + 1]
                ire = pl.ds(fb * blk + row0 + l, SUBLANES, stride=S5_PITCH)
                iim = pl.ds(fb * blk + half + row0 + l, SUBLANES, stride=S5_PITCH)
                nr = lre[fb] * hr - lim[fb] * hi + scr[ire, :]
                ni = lre[fb] * hi + lim[fb] * hr + scr[iim, :]
                scr[ire, :] = nr
                scr[iim, :] = ni
                new += [nr, ni]
            return tuple(new)
        return lax.fori_loop(0, seg, step, state)

    def load_state(rref, iref, lead):
        st = []
        for fb in range(S5_FB):
            st += [rref[lead + (slice(fb * SUBLANES, (fb + 1) * SUBLANES), slice(None))],
                   iref[lead + (slice(fb * SUBLANES, (fb + 1) * SUBLANES), slice(None))]]
        return tuple(st)

    def store_state(st, rref, iref, lead):
        for fb in range(S5_FB):
            rref[lead + (slice(fb * SUBLANES, (fb + 1) * SUBLANES), slice(None))] = st[2 * fb]
            iref[lead + (slice(fb * SUBLANES, (fb + 1) * SUBLANES), slice(None))] = st[2 * fb + 1]

    if carry:
        @pl.when(pl.program_id(0) == 0)
        def _():
            hst[0] = h0r_ref[0]
            hst[1] = h0i_ref[0]
        hre_scr = hst.at[0]
        him_scr = hst.at[1]
        st = scan(0, load_state(hre_scr, him_scr, ()))
        store_state(st, hre_scr, him_scr, ())
        store_state(st, hr_ref, hi_ref, (0,))
    else:
        for s in range(n_seq):
            st = scan(s * seg, load_state(h0r_ref, h0i_ref, (s,)))
            store_state(st, hr_ref, hi_ref, (s,))

    for fb in range(S5_FB):
        hf = jnp.concatenate(
            [scr[fb * blk + c * S5_PITCH: fb * blk + c * S5_PITCH + rows, :] for c in range(2 * SUBLANES)],
            axis=1).astype(BF16)
        y = jnp.dot(hf, cd_ref[fb], preferred_element_type=F32)
        cs = slice(fb * ch, (fb + 1) * ch)
        yy = y + d_ref[:, cs] * u_ref[:, cs]
        z_ref[:, cs] = jax.nn.gelu(yy).astype(z_ref.dtype)


def _s5_call(u, mats, lre, lim, d_skip, h0r, h0i, row_block0, n_blocks, seg, carry):
    t, d = u.shape
    ch, nst = 16 * S5_GC, 16 * S5_P
    n_seq = S5_ROWS // seg
    ns_total = h0r.shape[0]
    st_idx = (lambda i: (0, 0, 0)) if carry else (lambda i: (i, 0, 0))
    fixed2 = lambda i: (0, 0)
    fixed3 = lambda i: (0, 0, 0)
    st_rows = S5_FB * SUBLANES
    return pl.pallas_call(
        functools.partial(_s5_kernel, seg=seg, carry=carry),
        grid=(n_blocks,),
        in_specs=[
            pl.BlockSpec((S5_ROWS, d), lambda i: (row_block0 + i, 0)),
            *[pl.BlockSpec(m.shape, fixed3) for m in mats],
            pl.BlockSpec(lre.shape, fixed2),
            pl.BlockSpec(lim.shape, fixed2),
            pl.BlockSpec((1, d), fixed2),
            pl.BlockSpec((n_seq, st_rows, LANES), st_idx),
            pl.BlockSpec((n_seq, st_rows, LANES), st_idx),
        ],
        out_specs=[
            pl.BlockSpec((S5_ROWS, d), lambda i: (i, 0)),
            pl.BlockSpec((n_seq, st_rows, LANES), st_idx),
            pl.BlockSpec((n_seq, st_rows, LANES), st_idx),
        ],
        out_shape=[
            jax.ShapeDtypeStruct((n_blocks * S5_ROWS, d), BF16),
            jax.ShapeDtypeStruct((ns_total, st_rows, LANES), F32),
            jax.ShapeDtypeStruct((ns_total, st_rows, LANES), F32),
        ],
        scratch_shapes=[
            pltpu.VMEM((2 * SUBLANES * S5_FB * S5_PITCH, LANES), F32),
            pltpu.VMEM((2, st_rows, LANES), F32),
            pltpu.VMEM((S5_FB, ch, 2 * nst), BF16),
            pltpu.VMEM((S5_FB, 2 * nst, ch), BF16),
        ],
        compiler_params=_params(("arbitrary",), 56),
        name="s5_scan",
    )(u, *mats, lre, lim, d_skip.reshape(1, d), h0r, h0i)


def kernel(x_prompt, x_sample, state_ret, state_s5_re, state_s5_im, norm_mix_g, norm_ffn_g, norm_final_g,
           ret_w_in, ret_gn_g, ret_w_out, s5_w_in, s5_a_re, s5_a_im, s5_log_dt, s5_b_re, s5_b_im,
           s5_c_re, s5_c_im, s5_d, s5_w_out, moe_w_grp, moe_b_grp, moe_w_exp, moe_b_exp, moe_w_gu, moe_w_down):
    bp, seq, d = x_prompt.shape
    db, dseq, _ = x_sample.shape
    assert d == D_MODEL and bp == 1
    assert seq % max(ROW_TILE, RET_ROWS) == 0 and seq % CHUNK == 0
    assert dseq < CHUNK and dseq % 16 == 0 and S5_ROWS % dseq == 0
    tp = bp * seq
    ts = db * dseq
    t = tp + ts
    assert ts % ROW_TILE == 0 and ts % S5_ROWS == 0

    x_parts = [x_prompt.reshape(tp, d), x_sample.reshape(ts, d)]

    half = RET_DK // 2
    freqs = ROPE_BASE ** (-jnp.arange(half, dtype=F32) / half)
    pos = jnp.concatenate([jnp.arange(seq, dtype=F32), jnp.tile(PAST_LEN + jnp.arange(dseq, dtype=F32), db)])
    ang = pos[:, None] * freqs[None, :]
    cos_t = jnp.cos(ang)
    sin_t = jnp.sin(ang)

    xn = _norm_call(x_parts, norm_mix_g[0], BF16)
    tn = 1024
    first_col = lambda j: 0
    qk = _mm_call([xn], ret_w_in[0], [0], tn, 2 * D_MODEL, [([cos_t], half, first_col), ([sin_t], half, first_col)],
                  BF16, functools.partial(_epi_rope, n_q_blocks=D_MODEL // tn), 40, "ret_qk_proj")
    vg = _mm_call([xn], ret_w_in[0], [2 * D_MODEL // tn], tn, 2 * RET_VW, [], BF16, _epi_plain, 40, "ret_vg_proj")
    og_p, ret_p = _ret_prompt_call(qk, vg, ret_gn_g[0], tp)
    og_s, ret_s = _ret_sample_call(qk, vg, ret_gn_g[0], state_ret[0], tp, dseq)
    tn = 512
    same_col = lambda j: j
    x1 = _mm_call([og_p, og_s], ret_w_out[0], [0], tn, d, [(x_parts, tn, same_col)], F32, _epi_residual,
                  40, "ret_out_proj")

    x2, xn = _moe_layer(x1, norm_ffn_g[0], moe_w_grp[0], moe_b_grp[0], moe_w_exp[0], moe_b_exp[0],
                        moe_w_gu, moe_w_down, 0, norm_mix_g[1], True, BF16, (t,))

    u = _mm_call([xn], s5_w_in[0], [0], 1024, d, [], F32, _epi_plain, 40, "s5_in_proj")
    lre, lim, bbr, bbi = _s5_prep_call(s5_a_re[0], s5_a_im[0], s5_log_dt[0], s5_b_re[0], s5_b_im[0])
    mats = _s5_compact(bbr, bbi, s5_c_re[0], s5_c_im[0])
    st_shape = (S5_FB * SUBLANES, LANES)
    lre = lre.reshape(st_shape)
    lim = lim.reshape(st_shape)
    zero_state = jnp.zeros((1,) + st_shape, F32)
    z_p, hpr, hpi = _s5_call(u, mats, lre, lim, s5_d[0], zero_state, zero_state,
                             0, tp // S5_ROWS, S5_ROWS, True)
    z_s, hsr, hsi = _s5_call(u, mats, lre, lim, s5_d[0],
                             state_s5_re[0].reshape((db,) + st_shape), state_s5_im[0].reshape((db,) + st_shape),
                             tp // S5_ROWS, ts // S5_ROWS, dseq, False)
    x3 = _mm_call([z_p, z_s], s5_w_out[0], [0, d // tn], tn, d, [([x2], tn, same_col)], F32, _epi_glu_residual,
                  40, "s5_out_proj")

    y_p, y_s = _moe_layer(x3, norm_ffn_g[1], moe_w_grp[1], moe_b_grp[1], moe_w_exp[1], moe_b_exp[1],
                          moe_w_gu, moe_w_down, 1, norm_final_g, False, F32, (tp, ts))

    s5_shape = (1, -1, S5_GROUPS, S5_P)
    return (y_p.reshape(bp, seq, d), y_s.reshape(db, dseq, d),
            ret_p.reshape(1, bp, RET_HEADS, RET_DK, RET_DV), ret_s.reshape((1,) + ret_s.shape),
            hpr.reshape(s5_shape), hpi.reshape(s5_shape), hsr.reshape(s5_shape), hsi.reshape(s5_shape))
```

```python
import functools
import math

import numpy as np
import jax
import jax.numpy as jnp
from jax import lax
from jax.experimental import pallas as pl
from jax.experimental.pallas import tpu as pltpu

F32 = jnp.float32
BF16 = jnp.bfloat16

D_MODEL = 2048
PAST_LEN = 2048
CHUNK = 64
RET_HEADS = 8
RET_DK = D_MODEL // RET_HEADS
RET_DV = 2 * RET_DK
RET_VW = RET_HEADS * RET_DV
ROPE_BASE = 10000.0
S5_GC = 16
S5_GROUPS = D_MODEL // S5_GC
S5_P = 64
MOE_GROUPS = 4
MOE_PER_GROUP = 8
N_EXP = MOE_GROUPS * MOE_PER_GROUP
D_FF_E = 512
EPS = 1e-6

LANES = 128
SUBLANES = 8
ROW_TILE = 512
MM_CHUNK = 256
RET_ROWS = 256
RET_HPS = 2
MOE_TILE = 256
S5_ROWS = 256
S5_FB = S5_GROUPS // 16
S5_PITCH = S5_ROWS + 8
MIB = 1024 * 1024


def _params(sem, vmem_mib):
    return pltpu.CompilerParams(dimension_semantics=sem, vmem_limit_bytes=vmem_mib * MIB)


def _rms(x, g):
    ms = jnp.mean(x * x, axis=-1, keepdims=True)
    return x * lax.rsqrt(ms + EPS) * g


def _sigmoid(x):
    return 1.0 / (1.0 + jnp.exp(-x))


def _part_specs(parts, rows, cols, rowcol):
    specs = []
    start = 0
    for p in parts:
        nt = p.shape[0] // rows

        def imap(*g, start=start, nt=nt):
            i, jc = rowcol(*g)
            return (jnp.clip(i - start, 0, nt - 1), jc)

        specs.append(pl.BlockSpec((rows, cols), imap))
        start += nt
    return specs


def _part_tiles(parts, rows):
    return tuple(p.shape[0] // rows for p in parts)


def _for_each_part(i, tiles, fn):
    if len(tiles) == 1:
        fn(0)
        return
    start = 0
    for p, nt in enumerate(tiles):
        pl.when(jnp.logical_and(i >= start, i < start + nt))(functools.partial(fn, p))
        start += nt


def _pick(refs, p):
    return refs[min(p, len(refs) - 1)]


def _norm_kernel(*refs, tiles):
    n = len(tiles)
    x_refs, g_ref, o_ref = refs[:n], refs[n], refs[n + 1]

    def run(p):
        o_ref[...] = _rms(x_refs[p][...], g_ref[...]).astype(o_ref.dtype)

    _for_each_part(pl.program_id(0), tiles, run)


def _norm_call(x_parts, g, out_dtype):
    d = x_parts[0].shape[1]
    t = sum(p.shape[0] for p in x_parts)
    return pl.pallas_call(
        functools.partial(_norm_kernel, tiles=_part_tiles(x_parts, ROW_TILE)),
        grid=(t // ROW_TILE,),
        in_specs=_part_specs(x_parts, ROW_TILE, d, lambda i: (i, 0)) + [pl.BlockSpec((1, d), lambda i: (0, 0))],
        out_specs=pl.BlockSpec((ROW_TILE, d), lambda i: (i, 0)),
        out_shape=jax.ShapeDtypeStruct((t, d), out_dtype),
        compiler_params=_params(("arbitrary",), 32),
        name="rmsnorm",
    )(*x_parts, g.reshape(1, d))


def _mm_kernel(*refs, tiles, n_x, n_w, n_extra, epi):
    x_refs = refs[:n_x]
    w_refs = refs[n_x:n_x + n_w]
    pos = n_x + n_w
    extra = []
    for n in n_extra:
        extra.append(refs[pos:pos + n])
        pos += n
    o_ref = refs[pos]
    wb = refs[pos + 1:]

    @pl.when(pl.program_id(1) == 0)
    def _():
        for w_ref, b in zip(w_refs, wb):
            b[...] = w_ref[...].astype(BF16)

    def run(p):
        x = _pick(x_refs, p)[...]
        for c in range(o_ref.shape[1] // MM_CHUNK):
            cs = slice(c * MM_CHUNK, (c + 1) * MM_CHUNK)
            accs = [jnp.dot(x, b[:, cs], preferred_element_type=F32) for b in wb]
            epi(accs, [_pick(e, p) for e in extra], o_ref, cs)

    _for_each_part(pl.program_id(1), tiles, run)


def _mm_call(x_parts, w, w_col_offsets, tn, n_out, extras, out_dtype, epi, vmem_mib, name):
    k = x_parts[0].shape[1]
    t = sum(p.shape[0] for p in x_parts)
    n_w = len(w_col_offsets)
    split = [parts for parts in [x_parts] + [e[0] for e in extras] if len(parts) > 1]
    tiles = _part_tiles(split[0], ROW_TILE) if split else (t // ROW_TILE,)
    assert all(_part_tiles(parts, ROW_TILE) == tiles for parts in split)
    in_specs = _part_specs(x_parts, ROW_TILE, k, lambda j, i: (i, 0))
    for off in w_col_offsets:
        in_specs.append(pl.BlockSpec((k, tn), lambda j, i, off=off: (0, j + off)))
    extra_arrays = []
    for parts, cols, colfn in extras:
        in_specs += _part_specs(parts, ROW_TILE, cols, lambda j, i, colfn=colfn: (i, colfn(j)))
        extra_arrays += list(parts)
    return pl.pallas_call(
        functools.partial(_mm_kernel, tiles=tiles, n_x=len(x_parts), n_w=n_w,
                          n_extra=tuple(len(e[0]) for e in extras), epi=epi),
        grid=(n_out // tn, t // ROW_TILE),
        in_specs=in_specs,
        out_specs=pl.BlockSpec((ROW_TILE, tn), lambda j, i: (i, j)),
        out_shape=jax.ShapeDtypeStruct((t, n_out), out_dtype),
        scratch_shapes=[pltpu.VMEM((k, tn), BF16) for _ in range(n_w)],
        compiler_params=_params(("arbitrary", "arbitrary"), vmem_mib),
        name=name,
    )(*x_parts, *([w] * n_w), *extra_arrays)


def _epi_plain(accs, extra, o_ref, cs):
    o_ref[:, cs] = accs[0].astype(o_ref.dtype)


def _epi_residual(accs, extra, o_ref, cs):
    o_ref[:, cs] = (extra[0][:, cs] + accs[0]).astype(o_ref.dtype)


def _epi_glu_residual(accs, extra, o_ref, cs):
    a, gt = accs
    o_ref[:, cs] = (extra[0][:, cs] + a * _sigmoid(gt)).astype(o_ref.dtype)


def _epi_rope(accs, extra, o_ref, cs, *, n_q_blocks):
    acc = accs[0]
    assert acc.shape[1] == RET_DK
    cos_ref, sin_ref = extra
    half = RET_DK // 2
    scale = jnp.where(pl.program_id(0) >= n_q_blocks, RET_DK ** -0.5, 1.0).astype(F32)
    c = cos_ref[...]
    s = sin_ref[...]
    t1 = acc[:, :half]
    t2 = acc[:, half:]
    o_ref[:, cs.start:cs.start + half] = ((t1 * c - t2 * s) * scale).astype(o_ref.dtype)
    o_ref[:, cs.start + half:cs.stop] = ((t1 * s + t2 * c) * scale).astype(o_ref.dtype)


def _ret_tables(rows, chunk):
    lg = jnp.log(jnp.asarray(1.0 - 2.0 ** (-5.0 - np.arange(RET_HEADS)), dtype=F32))[:, None, None]
    n = jnp.arange(rows, dtype=F32)
    ci = np.arange(rows) // chunk
    same_or_earlier = jnp.asarray(ci[None, :] <= ci[:, None])
    dmat = jnp.where(same_or_earlier[None], jnp.exp(jnp.abs(n[:, None] - n[None, :])[None] * lg), 0.0)
    dq = jnp.exp((n + 1.0)[None, :, None] * lg) * jnp.ones((1, 1, LANES), F32)
    dk = jnp.exp((rows - 1.0 - n)[None, :, None] * lg) * jnp.ones((1, 1, LANES), F32)
    gr = jnp.exp(rows * lg) * jnp.ones((1, 1, RET_DV), F32)
    return dmat, dq, dk, gr


def _ret_block(q, k, v, g, s_prev, dmat, dq, dk, gr, gn):
    s = lax.dot_general(q, k, (((1,), (1,)), ((), ())), preferred_element_type=F32) * dmat
    inter = jnp.dot(q, s_prev.astype(BF16), preferred_element_type=F32)
    o = jnp.dot(s.astype(BF16), v, preferred_element_type=F32) + jnp.concatenate([dq] * (RET_DV // LANES), axis=1) * inter
    kd = (k.astype(F32) * jnp.concatenate([dk] * (RET_DK // LANES), axis=1)).astype(BF16)
    s_new = s_prev * gr + lax.dot_general(kd, v, (((0,), (0,)), ((), ())), preferred_element_type=F32)
    mu = jnp.mean(o, axis=-1, keepdims=True)
    oc = o - mu
    var = jnp.mean(oc * oc, axis=-1, keepdims=True)
    on = oc * lax.rsqrt(var + EPS) * gn
    gf = g.astype(F32)
    return (gf * _sigmoid(gf) * on).astype(BF16), s_new


def _ret_prompt_kernel(q_ref, k_ref, v_ref, g_ref, dm_ref, dq_ref, dk_ref, gr_ref, gn_ref,
                       og_ref, st_ref, s_scr):
    @pl.when(pl.program_id(1) == 0)
    def _():
        s_scr[...] = jnp.zeros_like(s_scr)

    for hh in range(RET_HPS):
        ks = slice(hh * RET_DK, (hh + 1) * RET_DK)
        vs = slice(hh * RET_DV, (hh + 1) * RET_DV)
        og, s_new = _ret_block(q_ref[:, ks], k_ref[:, ks], v_ref[:, vs], g_ref[:, vs], s_scr[hh],
                               dm_ref[hh], dq_ref[hh], dk_ref[hh], gr_ref[hh], gn_ref[:, vs])
        og_ref[:, vs] = og
        s_scr[hh] = s_new
        st_ref[hh] = s_new


def _ret_prompt_call(qk, vg, gn, t_prompt):
    dm, dq, dk, gr = _ret_tables(RET_ROWS, CHUNK)
    hps = RET_HPS
    hb = RET_HEADS // hps
    return pl.pallas_call(
        _ret_prompt_kernel,
        grid=(hb, t_prompt // RET_ROWS),
        in_specs=[
            pl.BlockSpec((RET_ROWS, hps * RET_DK), lambda hh, c: (c, hh)),
            pl.BlockSpec((RET_ROWS, hps * RET_DK), lambda hh, c: (c, hb + hh)),
            pl.BlockSpec((RET_ROWS, hps * RET_DV), lambda hh, c: (c, hh)),
            pl.BlockSpec((RET_ROWS, hps * RET_DV), lambda hh, c: (c, hb + hh)),
            pl.BlockSpec((hps, RET_ROWS, RET_ROWS), lambda hh, c: (hh, 0, 0)),
            pl.BlockSpec((hps, RET_ROWS, LANES), lambda hh, c: (hh, 0, 0)),
            pl.BlockSpec((hps, RET_ROWS, LANES), lambda hh, c: (hh, 0, 0)),
            pl.BlockSpec((hps, 1, RET_DV), lambda hh, c: (hh, 0, 0)),
            pl.BlockSpec((1, hps * RET_DV), lambda hh, c: (0, hh)),
        ],
        out_specs=[
            pl.BlockSpec((RET_ROWS, hps * RET_DV), lambda hh, c: (c, hh)),
            pl.BlockSpec((hps, RET_DK, RET_DV), lambda hh, c: (hh, 0, 0)),
        ],
        out_shape=[
            jax.ShapeDtypeStruct((t_prompt, RET_VW), BF16),
            jax.ShapeDtypeStruct((RET_HEADS, RET_DK, RET_DV), F32),
        ],
        scratch_shapes=[pltpu.VMEM((hps, RET_DK, RET_DV), F32)],
        compiler_params=_params(("arbitrary", "arbitrary"), 40),
        name="retention_prompt",
    )(qk, qk, vg, vg, dm, dq, dk, gr, gn.reshape(1, RET_VW))


def _ret_sample_kernel(q_ref, k_ref, v_ref, g_ref, s0_ref, dm_ref, dq_ref, dk_ref, gr_ref, gn_ref,
                       og_ref, st_ref):
    for hh in range(RET_HEADS):
        ks = slice(hh * RET_DK, (hh + 1) * RET_DK)
        vs = slice(hh * RET_DV, (hh + 1) * RET_DV)
        og, s_new = _ret_block(q_ref[:, ks], k_ref[:, ks], v_ref[:, vs], g_ref[:, vs], s0_ref[0, hh],
                               dm_ref[hh], dq_ref[hh], dk_ref[hh], gr_ref[hh], gn_ref[:, vs])
        og_ref[:, vs] = og
        st_ref[0, hh] = s_new


def _ret_sample_call(qk, vg, gn, state, t_prompt, dec_seq):
    dm, dq, dk, gr = _ret_tables(dec_seq, dec_seq)
    b = state.shape[0]
    r0 = t_prompt // dec_seq
    const3 = lambda bb: (0, 0, 0)
    return pl.pallas_call(
        _ret_sample_kernel,
        grid=(b,),
        in_specs=[
            pl.BlockSpec((dec_seq, D_MODEL), lambda bb: (r0 + bb, 0)),
            pl.BlockSpec((dec_seq, D_MODEL), lambda bb: (r0 + bb, 1)),
            pl.BlockSpec((dec_seq, RET_VW), lambda bb: (r0 + bb, 0)),
            pl.BlockSpec((dec_seq, RET_VW), lambda bb: (r0 + bb, 1)),
            pl.BlockSpec((1, RET_HEADS, RET_DK, RET_DV), lambda bb: (bb, 0, 0, 0)),
            pl.BlockSpec(dm.shape, const3),
            pl.BlockSpec(dq.shape, const3),
            pl.BlockSpec(dk.shape, const3),
            pl.BlockSpec(gr.shape, const3),
            pl.BlockSpec((1, RET_VW), lambda bb: (0, 0)),
        ],
        out_specs=[
            pl.BlockSpec((dec_seq, RET_VW), lambda bb: (bb, 0)),
            pl.BlockSpec((1, RET_HEADS, RET_DK, RET_DV), lambda bb: (bb, 0, 0, 0)),
        ],
        out_shape=[
            jax.ShapeDtypeStruct((b * dec_seq, RET_VW), BF16),
            jax.ShapeDtypeStruct(state.shape, F32),
        ],
        compiler_params=_params(("arbitrary",), 40),
        name="retention_sample",
    )(qk, qk, vg, vg, state, dm, dq, dk, gr, gn.reshape(1, RET_VW))


def _router_kernel(x_ref, g_ref, wr_ref, br_ref, xn_ref, id_ref, w_ref):
    xn = _rms(x_ref[...], g_ref[...])
    xn_ref[...] = xn
    xh = xn.astype(BF16)
    xl = (xn - xh.astype(F32)).astype(BF16)
    w = wr_ref[...]
    wh = w.astype(BF16)
    wl = (w - wh.astype(F32)).astype(BF16)
    lg = (jnp.dot(xh, wh, preferred_element_type=F32) + jnp.dot(xh, wl, preferred_element_type=F32)
          + jnp.dot(xl, wh, preferred_element_type=F32)) + br_ref[...]
    lane = lax.broadcasted_iota(jnp.int32, lg.shape, 1).astype(F32)
    neg = jnp.float32(-jnp.inf)
    big = jnp.float32(LANES)
    gl = jnp.where(lane < MOE_GROUPS, lg, neg)
    gmax = jnp.max(gl, axis=-1, keepdims=True)
    gsel = jnp.min(jnp.where(gl == gmax, lane, big), axis=-1, keepdims=True)
    gw = 1.0 / jnp.sum(jnp.exp(gl - gmax), axis=-1, keepdims=True)
    lo = MOE_GROUPS + gsel * MOE_PER_GROUP
    el = jnp.where((lane >= lo) & (lane < lo + MOE_PER_GROUP), lg, neg)
    m1 = jnp.max(el, axis=-1, keepdims=True)
    i1 = jnp.min(jnp.where(el == m1, lane, big), axis=-1, keepdims=True)
    el2 = jnp.where(lane == i1, neg, el)
    m2 = jnp.max(el2, axis=-1, keepdims=True)
    i2 = jnp.min(jnp.where(el2 == m2, lane, big), axis=-1, keepdims=True)
    z = jnp.sum(jnp.exp(el - m1), axis=-1, keepdims=True)
    p1 = 1.0 / z
    p2 = jnp.exp(m2 - m1) / z
    tot = p1 + p2
    w1 = p1 / tot * gw
    w2 = p2 / tot * gw
    ids = jnp.where(lane == 0, i1 - MOE_GROUPS, jnp.where(lane == 1, i2 - MOE_GROUPS, 0.0))
    id_ref[...] = ids.astype(jnp.int32)
    w_ref[...] = jnp.where(lane == 0, w1, jnp.where(lane == 1, w2, 0.0))


def _router_call(x, g, w_grp, b_grp, w_exp, b_exp):
    t, d = x.shape
    pad = LANES - MOE_GROUPS - N_EXP
    wr = jnp.concatenate([w_grp, w_exp, jnp.zeros((d, pad), F32)], axis=1)
    br = jnp.concatenate([b_grp, b_exp, jnp.zeros((pad,), F32)]).reshape(1, LANES)
    row = lambda i: (i, 0)
    fixed = lambda i: (0, 0)
    return pl.pallas_call(
        _router_kernel,
        grid=(t // ROW_TILE,),
        in_specs=[pl.BlockSpec((ROW_TILE, d), row), pl.BlockSpec((1, d), fixed),
                  pl.BlockSpec((d, LANES), fixed), pl.BlockSpec((1, LANES), fixed)],
        out_specs=[pl.BlockSpec((ROW_TILE, d), row), pl.BlockSpec((ROW_TILE, LANES), row),
                   pl.BlockSpec((ROW_TILE, LANES), row)],
        out_shape=[jax.ShapeDtypeStruct((t, d), F32), jax.ShapeDtypeStruct((t, LANES), jnp.int32),
                   jax.ShapeDtypeStruct((t, LANES), F32)],
        compiler_params=_params(("arbitrary",), 40),
        name="moe_router",
    )(x, g.reshape(1, d), wr, br)


def _cumsum_rows(hot):
    t, n = hot.shape
    blk = LANES
    h3 = hot.reshape(t // blk, blk, n).astype(F32)
    tri = jnp.tril(jnp.ones((blk, blk), F32))
    inner = jnp.einsum("ij,bjn->bin", tri, h3).astype(jnp.int32)
    totals = inner[:, -1, :]
    offsets = jnp.cumsum(totals, axis=0) - totals
    return (inner + offsets[:, None, :]).reshape(t, n)


def _moe_plan(e0, e1):
    t = e0.shape[0]
    n_tiles = 2 * t // MOE_TILE + N_EXP
    experts = jnp.arange(N_EXP, dtype=jnp.int32)[None, :]
    hot0 = (e0[:, None] == experts).astype(jnp.int32)
    hot1 = (e1[:, None] == experts).astype(jnp.int32)
    csum0 = _cumsum_rows(hot0)
    csum1 = _cumsum_rows(hot1)
    counts0 = csum0[-1]
    counts = counts0 + csum1[-1]
    tiles_e = (counts + MOE_TILE - 1) // MOE_TILE
    tiles_end = jnp.cumsum(tiles_e)
    row_start = (tiles_end - tiles_e) * MOE_TILE
    pos0 = jnp.sum(hot0 * (row_start[None, :] + csum0 - 1), axis=1)
    pos1 = jnp.sum(hot1 * (row_start[None, :] + counts0[None, :] + csum1 - 1), axis=1)
    pos = jnp.concatenate([pos0, pos1])
    n_valid = tiles_end[-1]
    tile_id = jnp.minimum(jnp.arange(n_tiles, dtype=jnp.int32), n_valid - 1)
    tile_expert = jnp.sum((tile_id[:, None] >= tiles_end[None, :]).astype(jnp.int32), axis=1)
    tile_expert = jnp.minimum(tile_expert, N_EXP - 1)
    changed = jnp.concatenate([jnp.ones((1,), bool), tile_expert[1:] != tile_expert[:-1]])
    first = jnp.logical_and(changed, jnp.arange(n_tiles) < n_valid).astype(jnp.int32)
    wslot = (jnp.cumsum(first) - 1) % 2
    next_start = tiles_end[tile_expert]
    next_expert = jnp.where(next_start < n_valid, tile_expert[jnp.minimum(next_start, n_tiles - 1)], -1)
    tile_meta = jnp.stack([tile_expert, first, wslot, next_expert]).astype(jnp.int32)
    pad_meta = jnp.stack([row_start + counts, tiles_e * MOE_TILE - counts]).astype(jnp.int32)
    return tile_meta, n_valid.reshape(1).astype(jnp.int32), pad_meta, pos.astype(jnp.int32)


def _dispatch_kernel(pos_ref, nv_ref, pad_ref, x_ref, xs_hbm, zeros, sem, zsem):
    i = pl.program_id(0)
    tm = x_ref.shape[0]
    n_tiles = xs_hbm.shape[0] // MOE_TILE
    n_tok = pl.num_programs(0) * tm

    for r in range(tm):
        for kk in range(2):
            pltpu.make_async_copy(x_ref.at[pl.ds(r, 1)], xs_hbm.at[pl.ds(pos_ref[kk * n_tok + i * tm + r], 1)],
                                  sem).start(priority=kk)

    def zero_rows(start, n):
        return pltpu.make_async_copy(zeros.at[pl.ds(0, n)], xs_hbm.at[pl.ds(start, n)], zsem)

    block_sizes = [1 << b for b in range(MOE_TILE.bit_length() - 2, -1, -1)]

    def fill_expert(e, carry):
        start, length = pad_ref[0, e], pad_ref[1, e]
        head = jnp.bitwise_and(-start, SUBLANES - 1)
        lax.fori_loop(0, head, lambda j, c: (zero_rows(start + j, 1).start(), c)[1], 0)
        rest = length - head
        off = start + head
        for sz in block_sizes:
            if sz < SUBLANES:
                break
            take = jnp.bitwise_and(rest, sz) != 0
            pl.when(take)(lambda off=off, sz=sz: zero_rows(pl.multiple_of(off, SUBLANES), sz).start())
            off = off + jnp.where(take, sz, 0)
        return carry

    def drain_expert(e, carry):
        length = pad_ref[1, e]
        for sz in block_sizes:
            pl.when(jnp.bitwise_and(length, sz) != 0)(lambda sz=sz: zero_rows(0, sz).wait())
        return carry

    def tail_tile_copy(tile):
        return pltpu.make_async_copy(zeros, xs_hbm.at[pl.ds(tile * MOE_TILE, MOE_TILE)], zsem)

    @pl.when(i == 0)
    def _():
        zeros[...] = jnp.zeros_like(zeros)
        lax.fori_loop(0, N_EXP, fill_expert, 0)
        lax.fori_loop(nv_ref[0], n_tiles, lambda tile, c: (tail_tile_copy(tile).start(), c)[1], 0)

    for kk in range(2):
        pltpu.make_async_copy(x_ref, xs_hbm.at[pl.ds(0, tm)], sem).wait()

    @pl.when(i == pl.num_programs(0) - 1)
    def _():
        lax.fori_loop(0, N_EXP, drain_expert, 0)
        lax.fori_loop(nv_ref[0], n_tiles, lambda tile, c: (tail_tile_copy(tile).wait(), c)[1], 0)


def _dispatch_call(xn, pos, n_valid, pad_meta, rows):
    t, d = xn.shape
    grid_spec = pltpu.PrefetchScalarGridSpec(
        num_scalar_prefetch=3,
        grid=(t // ROW_TILE,),
        in_specs=[pl.BlockSpec((ROW_TILE, d), lambda i, pos_ref, nv, pad: (i, 0))],
        out_specs=pl.BlockSpec(memory_space=pl.ANY),
        scratch_shapes=[pltpu.VMEM((MOE_TILE, d), F32), pltpu.SemaphoreType.DMA(()), pltpu.SemaphoreType.DMA(())],
    )
    return pl.pallas_call(
        _dispatch_kernel,
        grid_spec=grid_spec,
        out_shape=jax.ShapeDtypeStruct((rows, d), F32),
        compiler_params=_params(("arbitrary",), 32),
        name="moe_dispatch",
    )(pos, n_valid, pad_meta, xn)


def _moe_ffn_kernel(tm_ref, nv_ref, x_ref, wgu_hbm, wd_hbm, o_ref,
                    wgu_f, wd_f, wsems, wgu_b, wd_b, *, layer):
    i = pl.program_id(0)
    nv = nv_ref[0]
    expert, first, wslot, next_expert = tm_ref[0, i], tm_ref[1, i], tm_ref[2, i], tm_ref[3, i]

    def weight_copies(e, s):
        return (pltpu.make_async_copy(wgu_hbm.at[layer, e], wgu_f.at[s], wsems.at[0, s]),
                pltpu.make_async_copy(wd_hbm.at[layer, e], wd_f.at[s], wsems.at[1, s]))

    @pl.when(i == 0)
    def _():
        for c in weight_copies(expert, 0):
            c.start()

    @pl.when(first == 1)
    def _():
        for c in weight_copies(expert, wslot):
            c.wait()
        wgu_b[...] = wgu_f[wslot].astype(BF16)
        wd_b[...] = wd_f[wslot].astype(BF16)

        @pl.when(next_expert >= 0)
        def _():
            for c in weight_copies(next_expert, 1 - wslot):
                c.start()

    @pl.when(i < nv)
    def _():
        h = jnp.dot(x_ref[...].astype(BF16), wgu_b[...], preferred_element_type=F32)
        hg = h[:, :D_FF_E]
        hu = h[:, D_FF_E:]
        act = (hg * _sigmoid(hg) * hu).astype(BF16)
        o_ref[...] = jnp.dot(act, wd_b[...], preferred_element_type=F32)

    @pl.when(i >= nv)
    def _():
        o_ref[...] = jnp.zeros_like(o_ref)


def _moe_ffn_call(xs, w_gu, w_down, layer, tile_meta, n_valid):
    rows, d = xs.shape
    n_tiles = tile_meta.shape[1]
    grid_spec = pltpu.PrefetchScalarGridSpec(
        num_scalar_prefetch=2,
        grid=(n_tiles,),
        in_specs=[pl.BlockSpec((MOE_TILE, d), lambda i, tm, nv: (jnp.minimum(i, nv[0] - 1), 0)),
                  pl.BlockSpec(memory_space=pl.ANY), pl.BlockSpec(memory_space=pl.ANY)],
        out_specs=pl.BlockSpec((MOE_TILE, d), lambda i, tm, nv: (i, 0)),
        scratch_shapes=[
            pltpu.VMEM((2, d, 2 * D_FF_E), F32),
            pltpu.VMEM((2, D_FF_E, d), F32),
            pltpu.SemaphoreType.DMA((2, 2)),
            pltpu.VMEM((d, 2 * D_FF_E), BF16),
            pltpu.VMEM((D_FF_E, d), BF16),
        ],
    )
    return pl.pallas_call(
        functools.partial(_moe_ffn_kernel, layer=layer),
        grid_spec=grid_spec,
        out_shape=jax.ShapeDtypeStruct((rows, d), F32),
        compiler_params=_params(("arbitrary",), 48),
        name="moe_experts",
    )(tile_meta, n_valid, xs, w_gu, w_down)


def _combine_kernel(pos_ref, x_ref, w_ref, y_hbm, g_ref, *rest, emit_x, norm_tiles):
    outs = rest[:int(emit_x) + len(norm_tiles)]
    buf, sems = rest[int(emit_x) + len(norm_tiles):]
    i = pl.program_id(0)
    n = pl.num_programs(0)
    slot = lax.rem(i, 2)
    tm = x_ref.shape[0]

    def gather(tile, dst_slot):
        n_tok = n * tm
        for r in range(tm):
            for kk in range(2):
                pltpu.make_async_copy(y_hbm.at[pl.ds(pos_ref[kk * n_tok + tile * tm + r], 1)],
                                      buf.at[dst_slot, kk, pl.ds(r, 1)], sems.at[dst_slot]).start(priority=kk)

    @pl.when(i == 0)
    def _():
        gather(0, 0)

    for s in range(2):
        pl.when(jnp.logical_and(i + 1 < n, slot == s))(functools.partial(gather, i + 1, 1 - s))

    for kk in range(2):
        pltpu.make_async_copy(y_hbm.at[pl.ds(0, tm)], buf.at[slot, kk], sems.at[slot]).wait()
    xo = x_ref[...] + (w_ref[:, 0:1] * buf[slot, 0] + w_ref[:, 1:2] * buf[slot, 1])
    norm_refs = outs[int(emit_x):]
    if emit_x:
        outs[0][...] = xo

    def write_norm(p):
        norm_refs[p][...] = _rms(xo, g_ref[...]).astype(norm_refs[p].dtype)

    _for_each_part(i, norm_tiles, write_norm)


def _combine_call(x, ws, y_rows, pos, g, emit_x, norm_dtype, norm_rows):
    t, d = x.shape
    tm = MOE_TILE
    row = lambda i, pos_ref: (i, 0)
    out_specs, out_shape = [], []
    if emit_x:
        out_specs.append(pl.BlockSpec((tm, d), row))
        out_shape.append(jax.ShapeDtypeStruct((t, d), F32))
    norm_parts = [jax.ShapeDtypeStruct((r, d), norm_dtype) for r in norm_rows]
    out_specs += _part_specs(norm_parts, tm, d, row)
    out_shape += norm_parts
    grid_spec = pltpu.PrefetchScalarGridSpec(
        num_scalar_prefetch=1,
        grid=(t // tm,),
        in_specs=[pl.BlockSpec((tm, d), row), pl.BlockSpec((tm, LANES), row), pl.BlockSpec(memory_space=pl.ANY),
                  pl.BlockSpec((1, d), lambda i, pos_ref: (0, 0))],
        out_specs=out_specs,
        scratch_shapes=[pltpu.VMEM((2, 2, tm, d), F32), pltpu.SemaphoreType.DMA((2,))],
    )
    return pl.pallas_call(
        functools.partial(_combine_kernel, emit_x=emit_x, norm_tiles=_part_tiles(norm_parts, tm)),
        grid_spec=grid_spec,
        out_shape=out_shape,
        compiler_params=_params(("arbitrary",), 40),
        name="moe_combine",
    )(pos, x, ws, y_rows, g.reshape(1, d))


def _moe_layer(x, g_ffn, w_grp, b_grp, w_exp, b_exp, w_gu, w_down, layer, g_next, emit_x, norm_dtype, norm_rows):
    xn, ids, ws = _router_call(x, g_ffn, w_grp, b_grp, w_exp, b_exp)
    tile_meta, n_valid, pad_meta, pos = _moe_plan(ids[:, 0], ids[:, 1])
    xs = _dispatch_call(xn, pos, n_valid, pad_meta, tile_meta.shape[1] * MOE_TILE)
    y_rows = _moe_ffn_call(xs, w_gu, w_down, layer, tile_meta, n_valid)
    return _combine_call(x, ws, y_rows, pos, g_next, emit_x, norm_dtype, norm_rows)


def _s5_prep_kernel(are_ref, aim_ref, ldt_ref, bre_ref, bim_ref, lre_ref, lim_ref, bbr_ref, bbi_ref):
    are = are_ref[...]
    aim = aim_ref[...]
    dt = jnp.exp(ldt_ref[...])
    mag = jnp.exp(are * dt)
    lr = mag * jnp.cos(aim * dt)
    li = mag * jnp.sin(aim * dt)
    lre_ref[...] = lr
    lim_ref[...] = li
    den = are * are + aim * aim
    nr = lr - 1.0
    cr = (nr * are + li * aim) / den
    ci = (li * are - nr * aim) / den
    bre = bre_ref[...]
    bim = bim_ref[...]
    bbr_ref[...] = cr * bre - ci * bim
    bbi_ref[...] = cr * bim + ci * bre


def _s5_prep_call(a_re, a_im, log_dt, b_re, b_im):
    g, p, c = b_re.shape
    shp = (g, p * c)
    expand = lambda a: jnp.broadcast_to(a[:, :, None], (g, p, c)).reshape(shp)
    ldt = jnp.broadcast_to(log_dt[:, None], shp)
    outs = pl.pallas_call(
        _s5_prep_kernel,
        out_shape=[jax.ShapeDtypeStruct(shp, F32)] * 4,
        name="s5_discretize",
    )(expand(a_re), expand(a_im), ldt, b_re.reshape(shp), b_im.reshape(shp))
    lre, lim, bbr, bbi = outs
    return lre[:, ::c], lim[:, ::c], bbr, bbi


def _s5_compact(bbr, bbi, c_re, c_im):
    def b_part(bb):
        return bb.reshape(S5_FB, 16, S5_P, S5_GC).transpose(0, 1, 3, 2).reshape(S5_FB, 16 * S5_GC, S5_P)

    def c_part(cc):
        return cc.reshape(S5_FB, 16, S5_GC, S5_P).transpose(0, 3, 1, 2).reshape(S5_FB, S5_P, 16 * S5_GC)

    return b_part(bbr), b_part(bbi), c_part(c_re), c_part(c_im)


def _s5_build_mats(bbr_ref, bbi_ref, ccr_ref, cci_ref, bd_ref, cd_ref):
    nch, nst = 16 * S5_GC, 16 * S5_P
    iota = lambda shape, dim: lax.broadcasted_iota(jnp.int32, shape, dim)
    rep = ((iota((S5_P, nst), 1) & (S5_P - 1)) == iota((S5_P, nst), 0)).astype(BF16)
    rep_t = ((iota((nst, S5_P), 0) & (S5_P - 1)) == iota((nst, S5_P), 1)).astype(BF16)
    gc_bits, p_bits = S5_GC.bit_length() - 1, S5_P.bit_length() - 1
    diag = (iota((nch, nst), 0) >> gc_bits) == (iota((nch, nst), 1) >> p_bits)
    diag_t = (iota((nst, nch), 0) >> p_bits) == (iota((nst, nch), 1) >> gc_bits)
    for fb in range(S5_FB):
        for part, src in enumerate((bbr_ref, bbi_ref)):
            full = jnp.dot(src[fb].astype(BF16), rep, preferred_element_type=F32)
            bd_ref[fb, :, part * nst:(part + 1) * nst] = jnp.where(diag, full, 0.0).astype(BF16)
        for part, (src, sign) in enumerate(((ccr_ref, 1.0), (cci_ref, -1.0))):
            full = jnp.dot(rep_t, src[fb].astype(BF16), preferred_element_type=F32)
            cd_ref[fb, part * nst:(part + 1) * nst, :] = jnp.where(diag_t, sign * full, 0.0).astype(BF16)


def _s5_kernel(u_ref, bbr_ref, bbi_ref, ccr_ref, cci_ref, lre_ref, lim_ref, d_ref, h0r_ref, h0i_ref,
               z_ref, hr_ref, hi_ref, scr, hst, bd_ref, cd_ref, *, seg, carry):
    rows = u_ref.shape[0]
    n_seq = rows // seg
    ch = 16 * S5_GC
    half = SUBLANES * S5_PITCH
    blk = 2 * half

    @pl.when(pl.program_id(0) == 0)
    def _():
        _s5_build_mats(bbr_ref, bbi_ref, ccr_ref, cci_ref, bd_ref, cd_ref)

    for fb in range(S5_FB):
        bu = jnp.dot(u_ref[:, fb * ch:(fb + 1) * ch].astype(BF16), bd_ref[fb], preferred_element_type=F32)
        for c in range(2 * SUBLANES):
            r0 = fb * blk + c * S5_PITCH
            scr[r0:r0 + rows, :] = bu[:, c * LANES:(c + 1) * LANES]

    lre = [lre_ref[fb * SUBLANES:(fb + 1) * SUBLANES, :] for fb in range(S5_FB)]
    lim = [lim_ref[fb * SUBLANES:(fb + 1) * SUBLANES, :] for fb in range(S5_FB)]

    def scan(row0, state):
        def step(l, st):
            new = []
            for fb in range(S5_FB):
                hr, hi = st[2 * fb], st[2 * fb + 1]
                ire = pl.ds(fb * blk + row0 + l, SUBLANES, stride=S5_PITCH)
                iim = pl.ds(fb * blk + half + row0 + l, SUBLANES, stride=S5_PITCH)
                nr = lre[fb] * hr - lim[fb] * hi + scr[ire, :]
                ni = lre[fb] * hi + lim[fb] * hr + scr[iim, :]
                scr[ire, :] = nr
                scr[iim, :] = ni
                new += [nr, ni]
            return tuple(new)
        return lax.fori_loop(0, seg, step, state)

    def load_state(rref, iref, lead):
        st = []
        for fb in range(S5_FB):
            st += [rref[lead + (slice(fb * SUBLANES, (fb + 1) * SUBLANES), slice(None))],
                   iref[lead + (slice(fb * SUBLANES, (fb + 1) * SUBLANES), slice(None))]]
        return tuple(st)

    def store_state(st, rref, iref, lead):
        for fb in range(S5_FB):
            rref[lead + (slice(fb * SUBLANES, (fb + 1) * SUBLANES), slice(None))] = st[2 * fb]
            iref[lead + (slice(fb * SUBLANES, (fb + 1) * SUBLANES), slice(None))] = st[2 * fb + 1]

    if carry:
        @pl.when(pl.program_id(0) == 0)
        def _():
            hst[0] = h0r_ref[0]
            hst[1] = h0i_ref[0]
        hre_scr = hst.at[0]
        him_scr = hst.at[1]
        st = scan(0, load_state(hre_scr, him_scr, ()))
        store_state(st, hre_scr, him_scr, ())
        store_state(st, hr_ref, hi_ref, (0,))
    else:
        for s in range(n_seq):
            st = scan(s * seg, load_state(h0r_ref, h0i_ref, (s,)))
            store_state(st, hr_ref, hi_ref, (s,))

    for fb in range(S5_FB):
        hf = jnp.concatenate(
            [scr[fb * blk + c * S5_PITCH: fb * blk + c * S5_PITCH + rows, :] for c in range(2 * SUBLANES)],
            axis=1).astype(BF16)
        y = jnp.dot(hf, cd_ref[fb], preferred_element_type=F32)
        cs = slice(fb * ch, (fb + 1) * ch)
        yy = y + d_ref[:, cs] * u_ref[:, cs]
        z_ref[:, cs] = jax.nn.gelu(yy).astype(z_ref.dtype)


def _s5_call(u, mats, lre, lim, d_skip, h0r, h0i, row_block0, n_blocks, seg, carry):
    t, d = u.shape
    ch, nst = 16 * S5_GC, 16 * S5_P
    n_seq = S5_ROWS // seg
    ns_total = h0r.shape[0]
    st_idx = (lambda i: (0, 0, 0)) if carry else (lambda i: (i, 0, 0))
    fixed2 = lambda i: (0, 0)
    fixed3 = lambda i: (0, 0, 0)
    st_rows = S5_FB * SUBLANES
    return pl.pallas_call(
        functools.partial(_s5_kernel, seg=seg, carry=carry),
        grid=(n_blocks,),
        in_specs=[
            pl.BlockSpec((S5_ROWS, d), lambda i: (row_block0 + i, 0)),
            *[pl.BlockSpec(m.shape, fixed3) for m in mats],
            pl.BlockSpec(lre.shape, fixed2),
            pl.BlockSpec(lim.shape, fixed2),
            pl.BlockSpec((1, d), fixed2),
            pl.BlockSpec((n_seq, st_rows, LANES), st_idx),
            pl.BlockSpec((n_seq, st_rows, LANES), st_idx),
        ],
        out_specs=[
            pl.BlockSpec((S5_ROWS, d), lambda i: (i, 0)),
            pl.BlockSpec((n_seq, st_rows, LANES), st_idx),
            pl.BlockSpec((n_seq, st_rows, LANES), st_idx),
        ],
        out_shape=[
            jax.ShapeDtypeStruct((n_blocks * S5_ROWS, d), BF16),
            jax.ShapeDtypeStruct((ns_total, st_rows, LANES), F32),
            jax.ShapeDtypeStruct((ns_total, st_rows, LANES), F32),
        ],
        scratch_shapes=[
            pltpu.VMEM((2 * SUBLANES * S5_FB * S5_PITCH, LANES), F32),
            pltpu.VMEM((2, st_rows, LANES), F32),
            pltpu.VMEM((S5_FB, ch, 2 * nst), BF16),
            pltpu.VMEM((S5_FB, 2 * nst, ch), BF16),
        ],
        compiler_params=_params(("arbitrary",), 56),
        name="s5_scan",
    )(u, *mats, lre, lim, d_skip.reshape(1, d), h0r, h0i)


def kernel(x_prompt, x_sample, state_ret, state_s5_re, state_s5_im, norm_mix_g, norm_ffn_g, norm_final_g,
           ret_w_in, ret_gn_g, ret_w_out, s5_w_in, s5_a_re, s5_a_im, s5_log_dt, s5_b_re, s5_b_im,
           s5_c_re, s5_c_im, s5_d, s5_w_out, moe_w_grp, moe_b_grp, moe_w_exp, moe_b_exp, moe_w_gu, moe_w_down):
    bp, seq, d = x_prompt.shape
    db, dseq, _ = x_sample.shape
    assert d == D_MODEL and bp == 1
    assert seq % max(ROW_TILE, RET_ROWS) == 0 and seq % CHUNK == 0
    assert dseq < CHUNK and dseq % 16 == 0 and S5_ROWS % dseq == 0
    tp = bp * seq
    ts = db * dseq
    t = tp + ts
    assert ts % ROW_TILE == 0 and ts % S5_ROWS == 0

    x_parts = [x_prompt.reshape(tp, d), x_sample.reshape(ts, d)]

    half = RET_DK // 2
    freqs = ROPE_BASE ** (-jnp.arange(half, dtype=F32) / half)
    pos = jnp.concatenate([jnp.arange(seq, dtype=F32), jnp.tile(PAST_LEN + jnp.arange(dseq, dtype=F32), db)])
    ang = pos[:, None] * freqs[None, :]
    cos_t = jnp.cos(ang)
    sin_t = jnp.sin(ang)

    xn = _norm_call(x_parts, norm_mix_g[0], BF16)
    tn = 1024
    first_col = lambda j: 0
    qk = _mm_call([xn], ret_w_in[0], [0], tn, 2 * D_MODEL, [([cos_t], half, first_col), ([sin_t], half, first_col)],
                  BF16, functools.partial(_epi_rope, n_q_blocks=D_MODEL // tn), 40, "ret_qk_proj")
    vg = _mm_call([xn], ret_w_in[0], [2 * D_MODEL // tn], tn, 2 * RET_VW, [], BF16, _epi_plain, 40, "ret_vg_proj")
    og_p, ret_p = _ret_prompt_call(qk, vg, ret_gn_g[0], tp)
    og_s, ret_s = _ret_sample_call(qk, vg, ret_gn_g[0], state_ret[0], tp, dseq)
    tn = 512
    same_col = lambda j: j
    x1 = _mm_call([og_p, og_s], ret_w_out[0], [0], tn, d, [(x_parts, tn, same_col)], F32, _epi_residual,
                  40, "ret_out_proj")

    x2, xn = _moe_layer(x1, norm_ffn_g[0], moe_w_grp[0], moe_b_grp[0], moe_w_exp[0], moe_b_exp[0],
                        moe_w_gu, moe_w_down, 0, norm_mix_g[1], True, BF16, (t,))

    u = _mm_call([xn], s5_w_in[0], [0], 1024, d, [], F32, _epi_plain, 40, "s5_in_proj")
    lre, lim, bbr, bbi = _s5_prep_call(s5_a_re[0], s5_a_im[0], s5_log_dt[0], s5_b_re[0], s5_b_im[0])
    mats = _s5_compact(bbr, bbi, s5_c_re[0], s5_c_im[0])
    st_shape = (S5_FB * SUBLANES, LANES)
    lre = lre.reshape(st_shape)
    lim = lim.reshape(st_shape)
    zero_state = jnp.zeros((1,) + st_shape, F32)
    z_p, hpr, hpi = _s5_call(u, mats, lre, lim, s5_d[0], zero_state, zero_state,
                             0, tp // S5_ROWS, S5_ROWS, True)
    z_s, hsr, hsi = _s5_call(u, mats, lre, lim, s5_d[0],
                             state_s5_re[0].reshape((db,) + st_shape), state_s5_im[0].reshape((db,) + st_shape),
                             tp // S5_ROWS, ts // S5_ROWS, dseq, False)
    x3 = _mm_call([z_p, z_s], s5_w_out[0], [0, d // tn], tn, d, [([x2], tn, same_col)], F32, _epi_glu_residual,
                  40, "s5_out_proj")

    y_p, y_s = _moe_layer(x3, norm_ffn_g[1], moe_w_grp[1], moe_b_grp[1], moe_w_exp[1], moe_b_exp[1],
                          moe_w_gu, moe_w_down, 1, norm_final_g, False, F32, (tp, ts))

    s5_shape = (1, -1, S5_GROUPS, S5_P)
    return (y_p.reshape(bp, seq, d), y_s.reshape(db, dseq, d),
            ret_p.reshape(1, bp, RET_HEADS, RET_DK, RET_DV), ret_s.reshape((1,) + ret_s.shape),
            hpr.reshape(s5_shape), hpi.reshape(s5_shape), hsr.reshape(s5_shape), hsi.reshape(s5_shape))
```

```python
import functools
import math

import numpy as np
import jax
import jax.numpy as jnp
from jax import lax
from jax.experimental import pallas as pl
from jax.experimental.pallas import tpu as pltpu

F32 = jnp.float32
BF16 = jnp.bfloat16

D_MODEL = 2048
PAST_LEN = 2048
CHUNK = 64
RET_HEADS = 8
RET_DK = D_MODEL // RET_HEADS
RET_DV = 2 * RET_DK
RET_VW = RET_HEADS * RET_DV
ROPE_BASE = 10000.0
S5_GC = 16
S5_GROUPS = D_MODEL // S5_GC
S5_P = 64
MOE_GROUPS = 4
MOE_PER_GROUP = 8
N_EXP = MOE_GROUPS * MOE_PER_GROUP
D_FF_E = 512
EPS = 1e-6

LANES = 128
SUBLANES = 8
ROW_TILE = 512
MM_CHUNK = 256
RET_ROWS = 256
RET_HPS = 2
MOE_TILE = 256
S5_ROWS = 128
S5_FB = S5_GROUPS // 16
S5_SCAN_FBS = 4
S5_PITCH = S5_ROWS + 8
MIB = 1024 * 1024


def _params(sem, vmem_mib):
    return pltpu.CompilerParams(dimension_semantics=sem, vmem_limit_bytes=vmem_mib * MIB)


def _rms(x, g):
    ms = jnp.mean(x * x, axis=-1, keepdims=True)
    return x * lax.rsqrt(ms + EPS) * g


def _sigmoid(x):
    return 1.0 / (1.0 + jnp.exp(-x))


def _part_specs(parts, rows, cols, rowcol):
    specs = []
    start = 0
    for p in parts:
        nt = p.shape[0] // rows

        def imap(*g, start=start, nt=nt):
            i, jc = rowcol(*g)
            return (jnp.clip(i - start, 0, nt - 1), jc)

        specs.append(pl.BlockSpec((rows, cols), imap))
        start += nt
    return specs


def _part_tiles(parts, rows):
    return tuple(p.shape[0] // rows for p in parts)


def _for_each_part(i, tiles, fn):
    if len(tiles) == 1:
        fn(0)
        return
    start = 0
    for p, nt in enumerate(tiles):
        pl.when(jnp.logical_and(i >= start, i < start + nt))(functools.partial(fn, p))
        start += nt


def _pick(refs, p):
    return refs[min(p, len(refs) - 1)]


def _norm_kernel(*refs, tiles):
    n = len(tiles)
    x_refs, g_ref, o_ref = refs[:n], refs[n], refs[n + 1]

    def run(p):
        o_ref[...] = _rms(x_refs[p][...], g_ref[...]).astype(o_ref.dtype)

    _for_each_part(pl.program_id(0), tiles, run)


def _norm_call(x_parts, g, out_dtype):
    d = x_parts[0].shape[1]
    t = sum(p.shape[0] for p in x_parts)
    return pl.pallas_call(
        functools.partial(_norm_kernel, tiles=_part_tiles(x_parts, ROW_TILE)),
        grid=(t // ROW_TILE,),
        in_specs=_part_specs(x_parts, ROW_TILE, d, lambda i: (i, 0)) + [pl.BlockSpec((1, d), lambda i: (0, 0))],
        out_specs=pl.BlockSpec((ROW_TILE, d), lambda i: (i, 0)),
        out_shape=jax.ShapeDtypeStruct((t, d), out_dtype),
        compiler_params=_params(("arbitrary",), 32),
        name="rmsnorm",
    )(*x_parts, g.reshape(1, d))


def _mm_kernel(*refs, tiles, n_x, n_w, n_extra, epi):
    x_refs = refs[:n_x]
    w_refs = refs[n_x:n_x + n_w]
    pos = n_x + n_w
    extra = []
    for n in n_extra:
        extra.append(refs[pos:pos + n])
        pos += n
    o_ref = refs[pos]
    wb = refs[pos + 1:]

    @pl.when(pl.program_id(1) == 0)
    def _():
        for w_ref, b in zip(w_refs, wb):
            b[...] = w_ref[...].astype(BF16)

    def run(p):
        x = _pick(x_refs, p)[...]
        for c in range(o_ref.shape[1] // MM_CHUNK):
            cs = slice(c * MM_CHUNK, (c + 1) * MM_CHUNK)
            accs = [jnp.dot(x, b[:, cs], preferred_element_type=F32) for b in wb]
            epi(accs, [_pick(e, p) for e in extra], o_ref, cs)

    _for_each_part(pl.program_id(1), tiles, run)


def _mm_call(x_parts, w, w_col_offsets, tn, n_out, extras, out_dtype, epi, vmem_mib, name):
    k = x_parts[0].shape[1]
    t = sum(p.shape[0] for p in x_parts)
    n_w = len(w_col_offsets)
    split = [parts for parts in [x_parts] + [e[0] for e in extras] if len(parts) > 1]
    tiles = _part_tiles(split[0], ROW_TILE) if split else (t // ROW_TILE,)
    assert all(_part_tiles(parts, ROW_TILE) == tiles for parts in split)
    in_specs = _part_specs(x_parts, ROW_TILE, k, lambda j, i: (i, 0))
    for off in w_col_offsets:
        in_specs.append(pl.BlockSpec((k, tn), lambda j, i, off=off: (0, j + off)))
    extra_arrays = []
    for parts, cols, colfn in extras:
        in_specs += _part_specs(parts, ROW_TILE, cols, lambda j, i, colfn=colfn: (i, colfn(j)))
        extra_arrays += list(parts)
    return pl.pallas_call(
        functools.partial(_mm_kernel, tiles=tiles, n_x=len(x_parts), n_w=n_w,
                          n_extra=tuple(len(e[0]) for e in extras), epi=epi),
        grid=(n_out // tn, t // ROW_TILE),
        in_specs=in_specs,
        out_specs=pl.BlockSpec((ROW_TILE, tn), lambda j, i: (i, j)),
        out_shape=jax.ShapeDtypeStruct((t, n_out), out_dtype),
        scratch_shapes=[pltpu.VMEM((k, tn), BF16) for _ in range(n_w)],
        compiler_params=_params(("arbitrary", "arbitrary"), vmem_mib),
        name=name,
    )(*x_parts, *([w] * n_w), *extra_arrays)


def _epi_plain(accs, extra, o_ref, cs):
    o_ref[:, cs] = accs[0].astype(o_ref.dtype)


def _epi_residual(accs, extra, o_ref, cs):
    o_ref[:, cs] = (extra[0][:, cs] + accs[0]).astype(o_ref.dtype)


def _epi_glu_residual(accs, extra, o_ref, cs):
    a, gt = accs
    o_ref[:, cs] = (extra[0][:, cs] + a * _sigmoid(gt)).astype(o_ref.dtype)


def _epi_rope(accs, extra, o_ref, cs, *, n_q_blocks):
    acc = accs[0]
    assert acc.shape[1] == RET_DK
    cos_ref, sin_ref = extra
    half = RET_DK // 2
    scale = jnp.where(pl.program_id(0) >= n_q_blocks, RET_DK ** -0.5, 1.0).astype(F32)
    c = cos_ref[...]
    s = sin_ref[...]
    t1 = acc[:, :half]
    t2 = acc[:, half:]
    o_ref[:, cs.start:cs.start + half] = ((t1 * c - t2 * s) * scale).astype(o_ref.dtype)
    o_ref[:, cs.start + half:cs.stop] = ((t1 * s + t2 * c) * scale).astype(o_ref.dtype)


def _ret_tables(rows, chunk):
    lg = jnp.log(jnp.asarray(1.0 - 2.0 ** (-5.0 - np.arange(RET_HEADS)), dtype=F32))[:, None, None]
    n = jnp.arange(rows, dtype=F32)
    ci = np.arange(rows) // chunk
    same_or_earlier = jnp.asarray(ci[None, :] <= ci[:, None])
    dmat = jnp.where(same_or_earlier[None], jnp.exp(jnp.abs(n[:, None] - n[None, :])[None] * lg), 0.0)
    dq = jnp.exp((n + 1.0)[None, :, None] * lg) * jnp.ones((1, 1, LANES), F32)
    dk = jnp.exp((rows - 1.0 - n)[None, :, None] * lg) * jnp.ones((1, 1, LANES), F32)
    gr = jnp.exp(rows * lg) * jnp.ones((1, 1, RET_DV), F32)
    return dmat, dq, dk, gr


def _ret_block(q, k, v, g, s_prev, dmat, dq, dk, gr, gn):
    s = lax.dot_general(q, k, (((1,), (1,)), ((), ())), preferred_element_type=F32) * dmat
    inter = jnp.dot(q, s_prev.astype(BF16), preferred_element_type=F32)
    o = jnp.dot(s.astype(BF16), v, preferred_element_type=F32) + jnp.concatenate([dq] * (RET_DV // LANES), axis=1) * inter
    kd = (k.astype(F32) * jnp.concatenate([dk] * (RET_DK // LANES), axis=1)).astype(BF16)
    s_new = s_prev * gr + lax.dot_general(kd, v, (((0,), (0,)), ((), ())), preferred_element_type=F32)
    mu = jnp.mean(o, axis=-1, keepdims=True)
    oc = o - mu
    var = jnp.mean(oc * oc, axis=-1, keepdims=True)
    on = oc * lax.rsqrt(var + EPS) * gn
    gf = g.astype(F32)
    return (gf * _sigmoid(gf) * on).astype(BF16), s_new


def _ret_prompt_kernel(q_ref, k_ref, v_ref, g_ref, dm_ref, dq_ref, dk_ref, gr_ref, gn_ref,
                       og_ref, st_ref, s_scr):
    @pl.when(pl.program_id(1) == 0)
    def _():
        s_scr[...] = jnp.zeros_like(s_scr)

    for hh in range(RET_HPS):
        ks = slice(hh * RET_DK, (hh + 1) * RET_DK)
        vs = slice(hh * RET_DV, (hh + 1) * RET_DV)
        og, s_new = _ret_block(q_ref[:, ks], k_ref[:, ks], v_ref[:, vs], g_ref[:, vs], s_scr[hh],
                               dm_ref[hh], dq_ref[hh], dk_ref[hh], gr_ref[hh], gn_ref[:, vs])
        og_ref[:, vs] = og
        s_scr[hh] = s_new
        st_ref[hh] = s_new


def _ret_prompt_call(qk, vg, gn, t_prompt):
    dm, dq, dk, gr = _ret_tables(RET_ROWS, CHUNK)
    hps = RET_HPS
    hb = RET_HEADS // hps
    return pl.pallas_call(
        _ret_prompt_kernel,
        grid=(hb, t_prompt // RET_ROWS),
        in_specs=[
            pl.BlockSpec((RET_ROWS, hps * RET_DK), lambda hh, c: (c, hh)),
            pl.BlockSpec((RET_ROWS, hps * RET_DK), lambda hh, c: (c, hb + hh)),
            pl.BlockSpec((RET_ROWS, hps * RET_DV), lambda hh, c: (c, hh)),
            pl.BlockSpec((RET_ROWS, hps * RET_DV), lambda hh, c: (c, hb + hh)),
            pl.BlockSpec((hps, RET_ROWS, RET_ROWS), lambda hh, c: (hh, 0, 0)),
            pl.BlockSpec((hps, RET_ROWS, LANES), lambda hh, c: (hh, 0, 0)),
            pl.BlockSpec((hps, RET_ROWS, LANES), lambda hh, c: (hh, 0, 0)),
            pl.BlockSpec((hps, 1, RET_DV), lambda hh, c: (hh, 0, 0)),
            pl.BlockSpec((1, hps * RET_DV), lambda hh, c: (0, hh)),
        ],
        out_specs=[
            pl.BlockSpec((RET_ROWS, hps * RET_DV), lambda hh, c: (c, hh)),
            pl.BlockSpec((hps, RET_DK, RET_DV), lambda hh, c: (hh, 0, 0)),
        ],
        out_shape=[
            jax.ShapeDtypeStruct((t_prompt, RET_VW), BF16),
            jax.ShapeDtypeStruct((RET_HEADS, RET_DK, RET_DV), F32),
        ],
        scratch_shapes=[pltpu.VMEM((hps, RET_DK, RET_DV), F32)],
        compiler_params=_params(("arbitrary", "arbitrary"), 40),
        name="retention_prompt",
    )(qk, qk, vg, vg, dm, dq, dk, gr, gn.reshape(1, RET_VW))


def _ret_sample_kernel(q_ref, k_ref, v_ref, g_ref, s0_ref, dm_ref, dq_ref, dk_ref, gr_ref, gn_ref,
                       og_ref, st_ref):
    for hh in range(RET_HEADS):
        ks = slice(hh * RET_DK, (hh + 1) * RET_DK)
        vs = slice(hh * RET_DV, (hh + 1) * RET_DV)
        og, s_new = _ret_block(q_ref[:, ks], k_ref[:, ks], v_ref[:, vs], g_ref[:, vs], s0_ref[0, hh],
                               dm_ref[hh], dq_ref[hh], dk_ref[hh], gr_ref[hh], gn_ref[:, vs])
        og_ref[:, vs] = og
        st_ref[0, hh] = s_new


def _ret_sample_call(qk, vg, gn, state, t_prompt, dec_seq):
    dm, dq, dk, gr = _ret_tables(dec_seq, dec_seq)
    b = state.shape[0]
    r0 = t_prompt // dec_seq
    const3 = lambda bb: (0, 0, 0)
    return pl.pallas_call(
        _ret_sample_kernel,
        grid=(b,),
        in_specs=[
            pl.BlockSpec((dec_seq, D_MODEL), lambda bb: (r0 + bb, 0)),
            pl.BlockSpec((dec_seq, D_MODEL), lambda bb: (r0 + bb, 1)),
            pl.BlockSpec((dec_seq, RET_VW), lambda bb: (r0 + bb, 0)),
            pl.BlockSpec((dec_seq, RET_VW), lambda bb: (r0 + bb, 1)),
            pl.BlockSpec((1, RET_HEADS, RET_DK, RET_DV), lambda bb: (bb, 0, 0, 0)),
            pl.BlockSpec(dm.shape, const3),
            pl.BlockSpec(dq.shape, const3),
            pl.BlockSpec(dk.shape, const3),
            pl.BlockSpec(gr.shape, const3),
            pl.BlockSpec((1, RET_VW), lambda bb: (0, 0)),
        ],
        out_specs=[
            pl.BlockSpec((dec_seq, RET_VW), lambda bb: (bb, 0)),
            pl.BlockSpec((1, RET_HEADS, RET_DK, RET_DV), lambda bb: (bb, 0, 0, 0)),
        ],
        out_shape=[
            jax.ShapeDtypeStruct((b * dec_seq, RET_VW), BF16),
            jax.ShapeDtypeStruct(state.shape, F32),
        ],
        compiler_params=_params(("arbitrary",), 40),
        name="retention_sample",
    )(qk, qk, vg, vg, state, dm, dq, dk, gr, gn.reshape(1, RET_VW))


def _router_kernel(x_ref, g_ref, wr_ref, br_ref, xn_ref, id_ref, w_ref):
    xn = _rms(x_ref[...], g_ref[...])
    xn_ref[...] = xn
    xh = xn.astype(BF16)
    xl = (xn - xh.astype(F32)).astype(BF16)
    w = wr_ref[...]
    wh = w.astype(BF16)
    wl = (w - wh.astype(F32)).astype(BF16)
    lg = (jnp.dot(xh, wh, preferred_element_type=F32) + jnp.dot(xh, wl, preferred_element_type=F32)
          + jnp.dot(xl, wh, preferred_element_type=F32)) + br_ref[...]
    lane = lax.broadcasted_iota(jnp.int32, lg.shape, 1).astype(F32)
    neg = jnp.float32(-jnp.inf)
    big = jnp.float32(LANES)
    gl = jnp.where(lane < MOE_GROUPS, lg, neg)
    gmax = jnp.max(gl, axis=-1, keepdims=True)
    gsel = jnp.min(jnp.where(gl == gmax, lane, big), axis=-1, keepdims=True)
    gw = 1.0 / jnp.sum(jnp.exp(gl - gmax), axis=-1, keepdims=True)
    lo = MOE_GROUPS + gsel * MOE_PER_GROUP
    el = jnp.where((lane >= lo) & (lane < lo + MOE_PER_GROUP), lg, neg)
    m1 = jnp.max(el, axis=-1, keepdims=True)
    i1 = jnp.min(jnp.where(el == m1, lane, big), axis=-1, keepdims=True)
    el2 = jnp.where(lane == i1, neg, el)
    m2 = jnp.max(el2, axis=-1, keepdims=True)
    i2 = jnp.min(jnp.where(el2 == m2, lane, big), axis=-1, keepdims=True)
    z = jnp.sum(jnp.exp(el - m1), axis=-1, keepdims=True)
    p1 = 1.0 / z
    p2 = jnp.exp(m2 - m1) / z
    tot = p1 + p2
    w1 = p1 / tot * gw
    w2 = p2 / tot * gw
    ids = jnp.where(lane == 0, i1 - MOE_GROUPS, jnp.where(lane == 1, i2 - MOE_GROUPS, 0.0))
    id_ref[...] = ids.astype(jnp.int32)
    w_ref[...] = jnp.where(lane == 0, w1, jnp.where(lane == 1, w2, 0.0))


def _router_call(x, g, w_grp, b_grp, w_exp, b_exp):
    t, d = x.shape
    pad = LANES - MOE_GROUPS - N_EXP
    wr = jnp.concatenate([w_grp, w_exp, jnp.zeros((d, pad), F32)], axis=1)
    br = jnp.concatenate([b_grp, b_exp, jnp.zeros((pad,), F32)]).reshape(1, LANES)
    row = lambda i: (i, 0)
    fixed = lambda i: (0, 0)
    return pl.pallas_call(
        _router_kernel,
        grid=(t // ROW_TILE,),
        in_specs=[pl.BlockSpec((ROW_TILE, d), row), pl.BlockSpec((1, d), fixed),
                  pl.BlockSpec((d, LANES), fixed), pl.BlockSpec((1, LANES), fixed)],
        out_specs=[pl.BlockSpec((ROW_TILE, d), row), pl.BlockSpec((ROW_TILE, LANES), row),
                   pl.BlockSpec((ROW_TILE, LANES), row)],
        out_shape=[jax.ShapeDtypeStruct((t, d), F32), jax.ShapeDtypeStruct((t, LANES), jnp.int32),
                   jax.ShapeDtypeStruct((t, LANES), F32)],
        compiler_params=_params(("arbitrary",), 40),
        name="moe_router",
    )(x, g.reshape(1, d), wr, br)


def _cumsum_rows(hot):
    t, n = hot.shape
    blk = LANES
    h3 = hot.reshape(t // blk, blk, n).astype(F32)
    tri = jnp.tril(jnp.ones((blk, blk), F32))
    inner = jnp.einsum("ij,bjn->bin", tri, h3).astype(jnp.int32)
    totals = inner[:, -1, :]
    offsets = jnp.cumsum(totals, axis=0) - totals
    return (inner + offsets[:, None, :]).reshape(t, n)


def _moe_plan(e0, e1):
    t = e0.shape[0]
    n_tiles = 2 * t // MOE_TILE + N_EXP
    experts = jnp.arange(N_EXP, dtype=jnp.int32)[None, :]
    hot0 = (e0[:, None] == experts).astype(jnp.int32)
    hot1 = (e1[:, None] == experts).astype(jnp.int32)
    csum0 = _cumsum_rows(hot0)
    csum1 = _cumsum_rows(hot1)
    counts0 = csum0[-1]
    counts = counts0 + csum1[-1]
    tiles_e = (counts + MOE_TILE - 1) // MOE_TILE
    tiles_end = jnp.cumsum(tiles_e)
    row_start = (tiles_end - tiles_e) * MOE_TILE
    pos0 = jnp.sum(hot0 * (row_start[None, :] + csum0 - 1), axis=1)
    pos1 = jnp.sum(hot1 * (row_start[None, :] + counts0[None, :] + csum1 - 1), axis=1)
    pos = jnp.concatenate([pos0, pos1])
    n_valid = tiles_end[-1]
    tile_id = jnp.minimum(jnp.arange(n_tiles, dtype=jnp.int32), n_valid - 1)
    tile_expert = jnp.sum((tile_id[:, None] >= tiles_end[None, :]).astype(jnp.int32), axis=1)
    tile_expert = jnp.minimum(tile_expert, N_EXP - 1)
    changed = jnp.concatenate([jnp.ones((1,), bool), tile_expert[1:] != tile_expert[:-1]])
    first = jnp.logical_and(changed, jnp.arange(n_tiles) < n_valid).astype(jnp.int32)
    wslot = (jnp.cumsum(first) - 1) % 2
    next_start = tiles_end[tile_expert]
    next_expert = jnp.where(next_start < n_valid, tile_expert[jnp.minimum(next_start, n_tiles - 1)], -1)
    tile_meta = jnp.stack([tile_expert, first, wslot, next_expert]).astype(jnp.int32)
    pad_meta = jnp.stack([row_start + counts, tiles_e * MOE_TILE - counts]).astype(jnp.int32)
    return tile_meta, n_valid.reshape(1).astype(jnp.int32), pad_meta, pos.astype(jnp.int32)


def _dispatch_kernel(pos_ref, nv_ref, pad_ref, x_ref, xs_hbm, zeros, sem, zsem):
    i = pl.program_id(0)
    tm = x_ref.shape[0]
    n_tiles = xs_hbm.shape[0] // MOE_TILE
    n_tok = pl.num_programs(0) * tm

    for r in range(tm):
        for kk in range(2):
            pltpu.make_async_copy(x_ref.at[pl.ds(r, 1)], xs_hbm.at[pl.ds(pos_ref[kk * n_tok + i * tm + r], 1)],
                                  sem).start(priority=kk)

    def zero_rows(start, n):
        return pltpu.make_async_copy(zeros.at[pl.ds(0, n)], xs_hbm.at[pl.ds(start, n)], zsem)

    block_sizes = [1 << b for b in range(MOE_TILE.bit_length() - 2, -1, -1)]

    def fill_expert(e, carry):
        start, length = pad_ref[0, e], pad_ref[1, e]
        head = jnp.bitwise_and(-start, SUBLANES - 1)
        lax.fori_loop(0, head, lambda j, c: (zero_rows(start + j, 1).start(), c)[1], 0)
        rest = length - head
        off = start + head
        for sz in block_sizes:
            if sz < SUBLANES:
                break
            take = jnp.bitwise_and(rest, sz) != 0
            pl.when(take)(lambda off=off, sz=sz: zero_rows(pl.multiple_of(off, SUBLANES), sz).start())
            off = off + jnp.where(take, sz, 0)
        return carry

    def drain_expert(e, carry):
        length = pad_ref[1, e]
        for sz in block_sizes:
            pl.when(jnp.bitwise_and(length, sz) != 0)(lambda sz=sz: zero_rows(0, sz).wait())
        return carry

    def tail_tile_copy(tile):
        return pltpu.make_async_copy(zeros, xs_hbm.at[pl.ds(tile * MOE_TILE, MOE_TILE)], zsem)

    @pl.when(i == 0)
    def _():
        zeros[...] = jnp.zeros_like(zeros)
        lax.fori_loop(0, N_EXP, fill_expert, 0)
        lax.fori_loop(nv_ref[0], n_tiles, lambda tile, c: (tail_tile_copy(tile).start(), c)[1], 0)

    for kk in range(2):
        pltpu.make_async_copy(x_ref, xs_hbm.at[pl.ds(0, tm)], sem).wait()

    @pl.when(i == pl.num_programs(0) - 1)
    def _():
        lax.fori_loop(0, N_EXP, drain_expert, 0)
        lax.fori_loop(nv_ref[0], n_tiles, lambda tile, c: (tail_tile_copy(tile).wait(), c)[1], 0)


def _dispatch_call(xn, pos, n_valid, pad_meta, rows):
    t, d = xn.shape
    grid_spec = pltpu.PrefetchScalarGridSpec(
        num_scalar_prefetch=3,
        grid=(t // ROW_TILE,),
        in_specs=[pl.BlockSpec((ROW_TILE, d), lambda i, pos_ref, nv, pad: (i, 0))],
        out_specs=pl.BlockSpec(memory_space=pl.ANY),
        scratch_shapes=[pltpu.VMEM((MOE_TILE, d), F32), pltpu.SemaphoreType.DMA(()), pltpu.SemaphoreType.DMA(())],
    )
    return pl.pallas_call(
        _dispatch_kernel,
        grid_spec=grid_spec,
        out_shape=jax.ShapeDtypeStruct((rows, d), F32),
        compiler_params=_params(("arbitrary",), 32),
        name="moe_dispatch",
    )(pos, n_valid, pad_meta, xn)


def _moe_ffn_kernel(tm_ref, nv_ref, x_ref, wgu_hbm, wd_hbm, o_ref,
                    wgu_f, wd_f, wsems, wgu_b, wd_b, *, layer):
    i = pl.program_id(0)
    nv = nv_ref[0]
    expert, first, wslot, next_expert = tm_ref[0, i], tm_ref[1, i], tm_ref[2, i], tm_ref[3, i]

    def weight_copies(e, s):
        return (pltpu.make_async_copy(wgu_hbm.at[layer, e], wgu_f.at[s], wsems.at[0, s]),
                pltpu.make_async_copy(wd_hbm.at[layer, e], wd_f.at[s], wsems.at[1, s]))

    @pl.when(i == 0)
    def _():
        for c in weight_copies(expert, 0):
            c.start()

    @pl.when(first == 1)
    def _():
        for c in weight_copies(expert, wslot):
            c.wait()
        wgu_b[...] = wgu_f[wslot].astype(BF16)
        wd_b[...] = wd_f[wslot].astype(BF16)

        @pl.when(next_expert >= 0)
        def _():
            for c in weight_copies(next_expert, 1 - wslot):
                c.start()

    @pl.when(i < nv)
    def _():
        h = jnp.dot(x_ref[...].astype(BF16), wgu_b[...], preferred_element_type=F32)
        hg = h[:, :D_FF_E]
        hu = h[:, D_FF_E:]
        act = (hg * _sigmoid(hg) * hu).astype(BF16)
        o_ref[...] = jnp.dot(act, wd_b[...], preferred_element_type=F32)

    @pl.when(i >= nv)
    def _():
        o_ref[...] = jnp.zeros_like(o_ref)


def _moe_ffn_call(xs, w_gu, w_down, layer, tile_meta, n_valid):
    rows, d = xs.shape
    n_tiles = tile_meta.shape[1]
    grid_spec = pltpu.PrefetchScalarGridSpec(
        num_scalar_prefetch=2,
        grid=(n_tiles,),
        in_specs=[pl.BlockSpec((MOE_TILE, d), lambda i, tm, nv: (jnp.minimum(i, nv[0] - 1), 0)),
                  pl.BlockSpec(memory_space=pl.ANY), pl.BlockSpec(memory_space=pl.ANY)],
        out_specs=pl.BlockSpec((MOE_TILE, d), lambda i, tm, nv: (i, 0)),
        scratch_shapes=[
            pltpu.VMEM((2, d, 2 * D_FF_E), F32),
            pltpu.VMEM((2, D_FF_E, d), F32),
            pltpu.SemaphoreType.DMA((2, 2)),
            pltpu.VMEM((d, 2 * D_FF_E), BF16),
            pltpu.VMEM((D_FF_E, d), BF16),
        ],
    )
    return pl.pallas_call(
        functools.partial(_moe_ffn_kernel, layer=layer),
        grid_spec=grid_spec,
        out_shape=jax.ShapeDtypeStruct((rows, d), F32),
        compiler_params=_params(("arbitrary",), 48),
        name="moe_experts",
    )(tile_meta, n_valid, xs, w_gu, w_down)


def _combine_kernel(pos_ref, x_ref, w_ref, y_hbm, g_ref, *rest, emit_x, norm_tiles):
    outs = rest[:int(emit_x) + len(norm_tiles)]
    buf, sems = rest[int(emit_x) + len(norm_tiles):]
    i = pl.program_id(0)
    n = pl.num_programs(0)
    slot = lax.rem(i, 2)
    tm = x_ref.shape[0]

    def gather(tile, dst_slot):
        n_tok = n * tm
        for r in range(tm):
            for kk in range(2):
                pltpu.make_async_copy(y_hbm.at[pl.ds(pos_ref[kk * n_tok + tile * tm + r], 1)],
                                      buf.at[dst_slot, kk, pl.ds(r, 1)], sems.at[dst_slot]).start(priority=kk)

    @pl.when(i == 0)
    def _():
        gather(0, 0)

    for s in range(2):
        pl.when(jnp.logical_and(i + 1 < n, slot == s))(functools.partial(gather, i + 1, 1 - s))

    for kk in range(2):
        pltpu.make_async_copy(y_hbm.at[pl.ds(0, tm)], buf.at[slot, kk], sems.at[slot]).wait()
    xo = x_ref[...] + (w_ref[:, 0:1] * buf[slot, 0] + w_ref[:, 1:2] * buf[slot, 1])
    norm_refs = outs[int(emit_x):]
    if emit_x:
        outs[0][...] = xo

    def write_norm(p):
        norm_refs[p][...] = _rms(xo, g_ref[...]).astype(norm_refs[p].dtype)

    _for_each_part(i, norm_tiles, write_norm)


def _combine_call(x, ws, y_rows, pos, g, emit_x, norm_dtype, norm_rows):
    t, d = x.shape
    tm = MOE_TILE
    row = lambda i, pos_ref: (i, 0)
    out_specs, out_shape = [], []
    if emit_x:
        out_specs.append(pl.BlockSpec((tm, d), row))
        out_shape.append(jax.ShapeDtypeStruct((t, d), F32))
    norm_parts = [jax.ShapeDtypeStruct((r, d), norm_dtype) for r in norm_rows]
    out_specs += _part_specs(norm_parts, tm, d, row)
    out_shape += norm_parts
    grid_spec = pltpu.PrefetchScalarGridSpec(
        num_scalar_prefetch=1,
        grid=(t // tm,),
        in_specs=[pl.BlockSpec((tm, d), row), pl.BlockSpec((tm, LANES), row), pl.BlockSpec(memory_space=pl.ANY),
                  pl.BlockSpec((1, d), lambda i, pos_ref: (0, 0))],
        out_specs=out_specs,
        scratch_shapes=[pltpu.VMEM((2, 2, tm, d), F32), pltpu.SemaphoreType.DMA((2,))],
    )
    return pl.pallas_call(
        functools.partial(_combine_kernel, emit_x=emit_x, norm_tiles=_part_tiles(norm_parts, tm)),
        grid_spec=grid_spec,
        out_shape=out_shape,
        compiler_params=_params(("arbitrary",), 40),
        name="moe_combine",
    )(pos, x, ws, y_rows, g.reshape(1, d))


def _moe_layer(x, g_ffn, w_grp, b_grp, w_exp, b_exp, w_gu, w_down, layer, g_next, emit_x, norm_dtype, norm_rows):
    xn, ids, ws = _router_call(x, g_ffn, w_grp, b_grp, w_exp, b_exp)
    tile_meta, n_valid, pad_meta, pos = _moe_plan(ids[:, 0], ids[:, 1])
    xs = _dispatch_call(xn, pos, n_valid, pad_meta, tile_meta.shape[1] * MOE_TILE)
    y_rows = _moe_ffn_call(xs, w_gu, w_down, layer, tile_meta, n_valid)
    return _combine_call(x, ws, y_rows, pos, g_next, emit_x, norm_dtype, norm_rows)


def _s5_prep_kernel(are_ref, aim_ref, ldt_ref, bre_ref, bim_ref, lre_ref, lim_ref, bbr_ref, bbi_ref):
    are = are_ref[...]
    aim = aim_ref[...]
    dt = jnp.exp(ldt_ref[...])
    mag = jnp.exp(are * dt)
    lr = mag * jnp.cos(aim * dt)
    li = mag * jnp.sin(aim * dt)
    lre_ref[...] = lr
    lim_ref[...] = li
    den = are * are + aim * aim
    nr = lr - 1.0
    cr = (nr * are + li * aim) / den
    ci = (li * are - nr * aim) / den
    bre = bre_ref[...]
    bim = bim_ref[...]
    bbr_ref[...] = cr * bre - ci * bim
    bbi_ref[...] = cr * bim + ci * bre


def _s5_prep_call(a_re, a_im, log_dt, b_re, b_im):
    g, p, c = b_re.shape
    shp = (g, p * c)
    expand = lambda a: jnp.broadcast_to(a[:, :, None], (g, p, c)).reshape(shp)
    ldt = jnp.broadcast_to(log_dt[:, None], shp)
    outs = pl.pallas_call(
        _s5_prep_kernel,
        out_shape=[jax.ShapeDtypeStruct(shp, F32)] * 4,
        name="s5_discretize",
    )(expand(a_re), expand(a_im), ldt, b_re.reshape(shp), b_im.reshape(shp))
    lre, lim, bbr, bbi = outs
    return lre[:, ::c], lim[:, ::c], bbr, bbi


def _s5_compact(bbr, bbi, c_re, c_im):
    def b_part(bb):
        return bb.reshape(S5_FB, 16, S5_P, S5_GC).transpose(0, 1, 3, 2).reshape(S5_FB, 16 * S5_GC, S5_P)

    def c_part(cc):
        return cc.reshape(S5_FB, 16, S5_GC, S5_P).transpose(0, 3, 1, 2).reshape(S5_FB, S5_P, 16 * S5_GC)

    return b_part(bbr), b_part(bbi), c_part(c_re), c_part(c_im)


def _s5_build_mats(bbr_ref, bbi_ref, ccr_ref, cci_ref, bd_ref, cd_ref):
    nch, nst = 16 * S5_GC, 16 * S5_P
    iota = lambda shape, dim: lax.broadcasted_iota(jnp.int32, shape, dim)
    rep = ((iota((S5_P, nst), 1) & (S5_P - 1)) == iota((S5_P, nst), 0)).astype(BF16)
    rep_t = ((iota((nst, S5_P), 0) & (S5_P - 1)) == iota((nst, S5_P), 1)).astype(BF16)
    gc_bits, p_bits = S5_GC.bit_length() - 1, S5_P.bit_length() - 1
    diag = (iota((nch, nst), 0) >> gc_bits) == (iota((nch, nst), 1) >> p_bits)
    diag_t = (iota((nst, nch), 0) >> p_bits) == (iota((nst, nch), 1) >> gc_bits)
    for fb in range(S5_FB):
        for part, src in enumerate((bbr_ref, bbi_ref)):
            full = jnp.dot(src[fb].astype(BF16), rep, preferred_element_type=F32)
            bd_ref[fb, :, part * nst:(part + 1) * nst] = jnp.where(diag, full, 0.0).astype(BF16)
        for part, (src, sign) in enumerate(((ccr_ref, 1.0), (cci_ref, -1.0))):
            full = jnp.dot(rep_t, src[fb].astype(BF16), preferred_element_type=F32)
            cd_ref[fb, part * nst:(part + 1) * nst, :] = jnp.where(diag_t, sign * full, 0.0).astype(BF16)


def _s5_kernel(u_ref, bbr_ref, bbi_ref, ccr_ref, cci_ref, lre_ref, lim_ref, d_ref, h0r_ref, h0i_ref,
               z_ref, hr_ref, hi_ref, hst, bd_ref, cd_ref, *scrs, seg, carry):
    rows = u_ref.shape[0]
    n_seq = rows // seg
    ch = 16 * S5_GC
    half = SUBLANES * S5_PITCH

    @pl.when(pl.program_id(0) == 0)
    def _():
        _s5_build_mats(bbr_ref, bbi_ref, ccr_ref, cci_ref, bd_ref, cd_ref)

    def inject(fb):
        bu = jnp.dot(u_ref[:, fb * ch:(fb + 1) * ch].astype(BF16), bd_ref[fb], preferred_element_type=F32)
        for c in range(2 * SUBLANES):
            scrs[fb][c * S5_PITCH:c * S5_PITCH + rows, :] = bu[:, c * LANES:(c + 1) * LANES]

    def state_rows(fb):
        return (slice(fb * SUBLANES, (fb + 1) * SUBLANES), slice(None))

    def scan(fbs):
        lam = [(lre_ref[state_rows(fb)], lim_ref[state_rows(fb)]) for fb in fbs]
        for s in range(n_seq):
            rref, iref, lead = (hst.at[0], hst.at[1], ()) if carry else (h0r_ref, h0i_ref, (s,))
            st = [(rref[lead + state_rows(fb)], iref[lead + state_rows(fb)]) for fb in fbs]
            for l in range(seg):
                for n, fb in enumerate(fbs):
                    (lr, li), (hr, hi) = lam[n], st[n]
                    ire = pl.ds(s * seg + l, SUBLANES, stride=S5_PITCH)
                    iim = pl.ds(half + s * seg + l, SUBLANES, stride=S5_PITCH)
                    nr = lr * hr - li * hi + scrs[fb][ire, :]
                    ni = lr * hi + li * hr + scrs[fb][iim, :]
                    scrs[fb][ire, :] = nr
                    scrs[fb][iim, :] = ni
                    st[n] = (nr, ni)
            for n, fb in enumerate(fbs):
                if carry:
                    hst[(0,) + state_rows(fb)] = st[n][0]
                    hst[(1,) + state_rows(fb)] = st[n][1]
                hr_ref[(s,) + state_rows(fb)] = st[n][0]
                hi_ref[(s,) + state_rows(fb)] = st[n][1]

    def project(fb):
        hf = jnp.concatenate(
            [scrs[fb][c * S5_PITCH:c * S5_PITCH + rows, :] for c in range(2 * SUBLANES)],
            axis=1).astype(BF16)
        y = jnp.dot(hf, cd_ref[fb], preferred_element_type=F32)
        cs = slice(fb * ch, (fb + 1) * ch)
        yy = y + d_ref[:, cs] * u_ref[:, cs]
        z_ref[:, cs] = jax.nn.gelu(yy).astype(z_ref.dtype)

    if carry:
        @pl.when(pl.program_id(0) == 0)
        def _():
            hst[0] = h0r_ref[0]
            hst[1] = h0i_ref[0]

    groups = [list(range(g, g + S5_SCAN_FBS)) for g in range(0, S5_FB, S5_SCAN_FBS)]
    for fb in groups[0]:
        inject(fb)
    for gi, fbs in enumerate(groups):
        if gi + 1 < len(groups):
            for fb in groups[gi + 1]:
                inject(fb)
        scan(fbs)
        for fb in fbs:
            project(fb)


def _s5_call(u, mats, lre, lim, d_skip, h0r, h0i, row_block0, n_blocks, seg, carry):
    t, d = u.shape
    ch, nst = 16 * S5_GC, 16 * S5_P
    n_seq = S5_ROWS // seg
    ns_total = h0r.shape[0]
    st_idx = (lambda i: (0, 0, 0)) if carry else (lambda i: (i, 0, 0))
    fixed2 = lambda i: (0, 0)
    fixed3 = lambda i: (0, 0, 0)
    st_rows = S5_FB * SUBLANES
    return pl.pallas_call(
        functools.partial(_s5_kernel, seg=seg, carry=carry),
        grid=(n_blocks,),
        in_specs=[
            pl.BlockSpec((S5_ROWS, d), lambda i: (row_block0 + i, 0)),
            *[pl.BlockSpec(m.shape, fixed3) for m in mats],
            pl.BlockSpec(lre.shape, fixed2),
            pl.BlockSpec(lim.shape, fixed2),
            pl.BlockSpec((1, d), fixed2),
            pl.BlockSpec((n_seq, st_rows, LANES), st_idx),
            pl.BlockSpec((n_seq, st_rows, LANES), st_idx),
        ],
        out_specs=[
            pl.BlockSpec((S5_ROWS, d), lambda i: (i, 0)),
            pl.BlockSpec((n_seq, st_rows, LANES), st_idx),
            pl.BlockSpec((n_seq, st_rows, LANES), st_idx),
        ],
        out_shape=[
            jax.ShapeDtypeStruct((n_blocks * S5_ROWS, d), BF16),
            jax.ShapeDtypeStruct((ns_total, st_rows, LANES), F32),
            jax.ShapeDtypeStruct((ns_total, st_rows, LANES), F32),
        ],
        scratch_shapes=[
            pltpu.VMEM((2, st_rows, LANES), F32),
            pltpu.VMEM((S5_FB, ch, 2 * nst), BF16),
            pltpu.VMEM((S5_FB, 2 * nst, ch), BF16),
        ] + [pltpu.VMEM((2 * SUBLANES * S5_PITCH, LANES), F32) for _ in range(S5_FB)],
        compiler_params=_params(("arbitrary",), 56),
        name="s5_scan",
    )(u, *mats, lre, lim, d_skip.reshape(1, d), h0r, h0i)


def kernel(x_prompt, x_sample, state_ret, state_s5_re, state_s5_im, norm_mix_g, norm_ffn_g, norm_final_g,
           ret_w_in, ret_gn_g, ret_w_out, s5_w_in, s5_a_re, s5_a_im, s5_log_dt, s5_b_re, s5_b_im,
           s5_c_re, s5_c_im, s5_d, s5_w_out, moe_w_grp, moe_b_grp, moe_w_exp, moe_b_exp, moe_w_gu, moe_w_down):
    bp, seq, d = x_prompt.shape
    db, dseq, _ = x_sample.shape
    assert d == D_MODEL and bp == 1
    assert seq % max(ROW_TILE, RET_ROWS) == 0 and seq % CHUNK == 0
    assert dseq < CHUNK and dseq % 16 == 0 and S5_ROWS % dseq == 0
    tp = bp * seq
    ts = db * dseq
    t = tp + ts
    assert ts % ROW_TILE == 0 and ts % S5_ROWS == 0

    x_parts = [x_prompt.reshape(tp, d), x_sample.reshape(ts, d)]

    half = RET_DK // 2
    freqs = ROPE_BASE ** (-jnp.arange(half, dtype=F32) / half)
    pos = jnp.concatenate([jnp.arange(seq, dtype=F32), jnp.tile(PAST_LEN + jnp.arange(dseq, dtype=F32), db)])
    ang = pos[:, None] * freqs[None, :]
    cos_t = jnp.cos(ang)
    sin_t = jnp.sin(ang)

    xn = _norm_call(x_parts, norm_mix_g[0], BF16)
    tn = 1024
    first_col = lambda j: 0
    qk = _mm_call([xn], ret_w_in[0], [0], tn, 2 * D_MODEL, [([cos_t], half, first_col), ([sin_t], half, first_col)],
                  BF16, functools.partial(_epi_rope, n_q_blocks=D_MODEL // tn), 40, "ret_qk_proj")
    vg = _mm_call([xn], ret_w_in[0], [2 * D_MODEL // tn], tn, 2 * RET_VW, [], BF16, _epi_plain, 40, "ret_vg_proj")
    og_p, ret_p = _ret_prompt_call(qk, vg, ret_gn_g[0], tp)
    og_s, ret_s = _ret_sample_call(qk, vg, ret_gn_g[0], state_ret[0], tp, dseq)
    tn = 512
    same_col = lambda j: j
    x1 = _mm_call([og_p, og_s], ret_w_out[0], [0], tn, d, [(x_parts, tn, same_col)], F32, _epi_residual,
                  40, "ret_out_proj")

    x2, xn = _moe_layer(x1, norm_ffn_g[0], moe_w_grp[0], moe_b_grp[0], moe_w_exp[0], moe_b_exp[0],
                        moe_w_gu, moe_w_down, 0, norm_mix_g[1], True, BF16, (t,))

    u = _mm_call([xn], s5_w_in[0], [0], 1024, d, [], F32, _epi_plain, 40, "s5_in_proj")
    lre, lim, bbr, bbi = _s5_prep_call(s5_a_re[0], s5_a_im[0], s5_log_dt[0], s5_b_re[0], s5_b_im[0])
    mats = _s5_compact(bbr, bbi, s5_c_re[0], s5_c_im[0])
    st_shape = (S5_FB * SUBLANES, LANES)
    lre = lre.reshape(st_shape)
    lim = lim.reshape(st_shape)
    zero_state = jnp.zeros((1,) + st_shape, F32)
    z_p, hpr, hpi = _s5_call(u, mats, lre, lim, s5_d[0], zero_state, zero_state,
                             0, tp // S5_ROWS, S5_ROWS, True)
    z_s, hsr, hsi = _s5_call(u, mats, lre, lim, s5_d[0],
                             state_s5_re[0].reshape((db,) + st_shape), state_s5_im[0].reshape((db,) + st_shape),
                             tp // S5_ROWS, ts // S5_ROWS, dseq, False)
    x3 = _mm_call([z_p, z_s], s5_w_out[0], [0, d // tn], tn, d, [([x2], tn, same_col)], F32, _epi_glu_residual,
                  40, "s5_out_proj")

    y_p, y_s = _moe_layer(x3, norm_ffn_g[1], moe_w_grp[1], moe_b_grp[1], moe_w_exp[1], moe_b_exp[1],
                          moe_w_gu, moe_w_down, 1, norm_final_g, False, F32, (tp, ts))

    s5_shape = (1, -1, S5_GROUPS, S5_P)
    return (y_p.reshape(bp, seq, d), y_s.reshape(db, dseq, d),
            ret_p.reshape(1, bp, RET_HEADS, RET_DK, RET_DV), ret_s.reshape((1,) + ret_s.shape),
            hpr.reshape(s5_shape), hpi.reshape(s5_shape), hsr.reshape(s5_shape), hsi.reshape(s5_shape))
```

```python
import functools
import math

import numpy as np
import jax
import jax.numpy as jnp
from jax import lax
from jax.experimental import pallas as pl
from jax.experimental.pallas import tpu as pltpu

F32 = jnp.float32
BF16 = jnp.bfloat16

D_MODEL = 2048
PAST_LEN = 2048
CHUNK = 64
RET_HEADS = 8
RET_DK = D_MODEL // RET_HEADS
RET_DV = 2 * RET_DK
RET_VW = RET_HEADS * RET_DV
ROPE_BASE = 10000.0
S5_GC = 16
S5_GROUPS = D_MODEL // S5_GC
S5_P = 64
MOE_GROUPS = 4
MOE_PER_GROUP = 8
N_EXP = MOE_GROUPS * MOE_PER_GROUP
D_FF_E = 512
EPS = 1e-6

LANES = 128
SUBLANES = 8
ROW_TILE = 512
MM_CHUNK = 256
RET_ROWS = 256
RET_HPS = 2
MOE_TILE = 256
S5_ROWS = 256
S5_FB = S5_GROUPS // 16
S5_PITCH = S5_ROWS + 4
MIB = 1024 * 1024


def _params(sem, vmem_mib):
    return pltpu.CompilerParams(dimension_semantics=sem, vmem_limit_bytes=vmem_mib * MIB)


def _rms(x, g):
    ms = jnp.mean(x * x, axis=-1, keepdims=True)
    return x * lax.rsqrt(ms + EPS) * g


def _sigmoid(x):
    return 1.0 / (1.0 + jnp.exp(-x))


def _part_specs(parts, rows, cols, rowcol):
    specs = []
    start = 0
    for p in parts:
        nt = p.shape[0] // rows

        def imap(*g, start=start, nt=nt):
            i, jc = rowcol(*g)
            return (jnp.clip(i - start, 0, nt - 1), jc)

        specs.append(pl.BlockSpec((rows, cols), imap))
        start += nt
    return specs


def _part_tiles(parts, rows):
    return tuple(p.shape[0] // rows for p in parts)


def _for_each_part(i, tiles, fn):
    if len(tiles) == 1:
        fn(0)
        return
    start = 0
    for p, nt in enumerate(tiles):
        pl.when(jnp.logical_and(i >= start, i < start + nt))(functools.partial(fn, p))
        start += nt


def _pick(refs, p):
    return refs[min(p, len(refs) - 1)]


def _norm_kernel(*refs, tiles):
    n = len(tiles)
    x_refs, g_ref, o_ref = refs[:n], refs[n], refs[n + 1]

    def run(p):
        o_ref[...] = _rms(x_refs[p][...], g_ref[...]).astype(o_ref.dtype)

    _for_each_part(pl.program_id(0), tiles, run)


def _norm_call(x_parts, g, out_dtype):
    d = x_parts[0].shape[1]
    t = sum(p.shape[0] for p in x_parts)
    return pl.pallas_call(
        functools.partial(_norm_kernel, tiles=_part_tiles(x_parts, ROW_TILE)),
        grid=(t // ROW_TILE,),
        in_specs=_part_specs(x_parts, ROW_TILE, d, lambda i: (i, 0)) + [pl.BlockSpec((1, d), lambda i: (0, 0))],
        out_specs=pl.BlockSpec((ROW_TILE, d), lambda i: (i, 0)),
        out_shape=jax.ShapeDtypeStruct((t, d), out_dtype),
        compiler_params=_params(("arbitrary",), 32),
        name="rmsnorm",
    )(*x_parts, g.reshape(1, d))


def _mm_kernel(*refs, tiles, n_x, n_w, n_extra, epi):
    x_refs = refs[:n_x]
    w_refs = refs[n_x:n_x + n_w]
    pos = n_x + n_w
    extra = []
    for n in n_extra:
        extra.append(refs[pos:pos + n])
        pos += n
    o_ref = refs[pos]
    wb = refs[pos + 1:]

    @pl.when(pl.program_id(1) == 0)
    def _():
        for w_ref, b in zip(w_refs, wb):
            b[...] = w_ref[...].astype(BF16)

    def run(p):
        x = _pick(x_refs, p)[...]
        for c in range(o_ref.shape[1] // MM_CHUNK):
            cs = slice(c * MM_CHUNK, (c + 1) * MM_CHUNK)
            accs = [jnp.dot(x, b[:, cs], preferred_element_type=F32) for b in wb]
            epi(accs, [_pick(e, p) for e in extra], o_ref, cs)

    _for_each_part(pl.program_id(1), tiles, run)


def _mm_call(x_parts, w, w_col_offsets, tn, n_out, extras, out_dtype, epi, vmem_mib, name):
    k = x_parts[0].shape[1]
    t = sum(p.shape[0] for p in x_parts)
    n_w = len(w_col_offsets)
    split = [parts for parts in [x_parts] + [e[0] for e in extras] if len(parts) > 1]
    tiles = _part_tiles(split[0], ROW_TILE) if split else (t // ROW_TILE,)
    assert all(_part_tiles(parts, ROW_TILE) == tiles for parts in split)
    in_specs = _part_specs(x_parts, ROW_TILE, k, lambda j, i: (i, 0))
    for off in w_col_offsets:
        in_specs.append(pl.BlockSpec((k, tn), lambda j, i, off=off: (0, j + off)))
    extra_arrays = []
    for parts, cols, colfn in extras:
        in_specs += _part_specs(parts, ROW_TILE, cols, lambda j, i, colfn=colfn: (i, colfn(j)))
        extra_arrays += list(parts)
    return pl.pallas_call(
        functools.partial(_mm_kernel, tiles=tiles, n_x=len(x_parts), n_w=n_w,
                          n_extra=tuple(len(e[0]) for e in extras), epi=epi),
        grid=(n_out // tn, t // ROW_TILE),
        in_specs=in_specs,
        out_specs=pl.BlockSpec((ROW_TILE, tn), lambda j, i: (i, j)),
        out_shape=jax.ShapeDtypeStruct((t, n_out), out_dtype),
        scratch_shapes=[pltpu.VMEM((k, tn), BF16) for _ in range(n_w)],
        compiler_params=_params(("arbitrary", "arbitrary"), vmem_mib),
        name=name,
    )(*x_parts, *([w] * n_w), *extra_arrays)


def _epi_plain(accs, extra, o_ref, cs):
    o_ref[:, cs] = accs[0].astype(o_ref.dtype)


def _epi_residual(accs, extra, o_ref, cs):
    o_ref[:, cs] = (extra[0][:, cs] + accs[0]).astype(o_ref.dtype)


def _epi_glu_residual(accs, extra, o_ref, cs):
    a, gt = accs
    o_ref[:, cs] = (extra[0][:, cs] + a * _sigmoid(gt)).astype(o_ref.dtype)


def _epi_rope(accs, extra, o_ref, cs, *, n_q_blocks):
    acc = accs[0]
    assert acc.shape[1] == RET_DK
    cos_ref, sin_ref = extra
    half = RET_DK // 2
    scale = jnp.where(pl.program_id(0) >= n_q_blocks, RET_DK ** -0.5, 1.0).astype(F32)
    c = cos_ref[...]
    s = sin_ref[...]
    t1 = acc[:, :half]
    t2 = acc[:, half:]
    o_ref[:, cs.start:cs.start + half] = ((t1 * c - t2 * s) * scale).astype(o_ref.dtype)
    o_ref[:, cs.start + half:cs.stop] = ((t1 * s + t2 * c) * scale).astype(o_ref.dtype)


def _ret_tables(rows, chunk):
    lg = jnp.log(jnp.asarray(1.0 - 2.0 ** (-5.0 - np.arange(RET_HEADS)), dtype=F32))[:, None, None]
    n = jnp.arange(rows, dtype=F32)
    ci = np.arange(rows) // chunk
    same_or_earlier = jnp.asarray(ci[None, :] <= ci[:, None])
    dmat = jnp.where(same_or_earlier[None], jnp.exp(jnp.abs(n[:, None] - n[None, :])[None] * lg), 0.0)
    dq = jnp.exp((n + 1.0)[None, :, None] * lg) * jnp.ones((1, 1, LANES), F32)
    dk = jnp.exp((rows - 1.0 - n)[None, :, None] * lg) * jnp.ones((1, 1, LANES), F32)
    gr = jnp.exp(rows * lg) * jnp.ones((1, 1, RET_DV), F32)
    return dmat, dq, dk, gr


def _ret_block(q, k, v, g, s_prev, dmat, dq, dk, gr, gn):
    s = lax.dot_general(q, k, (((1,), (1,)), ((), ())), preferred_element_type=F32) * dmat
    inter = jnp.dot(q, s_prev.astype(BF16), preferred_element_type=F32)
    o = jnp.dot(s.astype(BF16), v, preferred_element_type=F32) + jnp.concatenate([dq] * (RET_DV // LANES), axis=1) * inter
    kd = (k.astype(F32) * jnp.concatenate([dk] * (RET_DK // LANES), axis=1)).astype(BF16)
    s_new = s_prev * gr + lax.dot_general(kd, v, (((0,), (0,)), ((), ())), preferred_element_type=F32)
    mu = jnp.mean(o, axis=-1, keepdims=True)
    oc = o - mu
    var = jnp.mean(oc * oc, axis=-1, keepdims=True)
    on = oc * lax.rsqrt(var + EPS) * gn
    gf = g.astype(F32)
    return (gf * _sigmoid(gf) * on).astype(BF16), s_new


def _ret_prompt_kernel(q_ref, k_ref, v_ref, g_ref, dm_ref, dq_ref, dk_ref, gr_ref, gn_ref,
                       og_ref, st_ref, s_scr):
    @pl.when(pl.program_id(1) == 0)
    def _():
        s_scr[...] = jnp.zeros_like(s_scr)

    for hh in range(RET_HPS):
        ks = slice(hh * RET_DK, (hh + 1) * RET_DK)
        vs = slice(hh * RET_DV, (hh + 1) * RET_DV)
        og, s_new = _ret_block(q_ref[:, ks], k_ref[:, ks], v_ref[:, vs], g_ref[:, vs], s_scr[hh],
                               dm_ref[hh], dq_ref[hh], dk_ref[hh], gr_ref[hh], gn_ref[:, vs])
        og_ref[:, vs] = og
        s_scr[hh] = s_new
        st_ref[hh] = s_new


def _ret_prompt_call(qk, vg, gn, t_prompt):
    dm, dq, dk, gr = _ret_tables(RET_ROWS, CHUNK)
    hps = RET_HPS
    hb = RET_HEADS // hps
    return pl.pallas_call(
        _ret_prompt_kernel,
        grid=(hb, t_prompt // RET_ROWS),
        in_specs=[
            pl.BlockSpec((RET_ROWS, hps * RET_DK), lambda hh, c: (c, hh)),
            pl.BlockSpec((RET_ROWS, hps * RET_DK), lambda hh, c: (c, hb + hh)),
            pl.BlockSpec((RET_ROWS, hps * RET_DV), lambda hh, c: (c, hh)),
            pl.BlockSpec((RET_ROWS, hps * RET_DV), lambda hh, c: (c, hb + hh)),
            pl.BlockSpec((hps, RET_ROWS, RET_ROWS), lambda hh, c: (hh, 0, 0)),
            pl.BlockSpec((hps, RET_ROWS, LANES), lambda hh, c: (hh, 0, 0)),
            pl.BlockSpec((hps, RET_ROWS, LANES), lambda hh, c: (hh, 0, 0)),
            pl.BlockSpec((hps, 1, RET_DV), lambda hh, c: (hh, 0, 0)),
            pl.BlockSpec((1, hps * RET_DV), lambda hh, c: (0, hh)),
        ],
        out_specs=[
            pl.BlockSpec((RET_ROWS, hps * RET_DV), lambda hh, c: (c, hh)),
            pl.BlockSpec((hps, RET_DK, RET_DV), lambda hh, c: (hh, 0, 0)),
        ],
        out_shape=[
            jax.ShapeDtypeStruct((t_prompt, RET_VW), BF16),
            jax.ShapeDtypeStruct((RET_HEADS, RET_DK, RET_DV), F32),
        ],
        scratch_shapes=[pltpu.VMEM((hps, RET_DK, RET_DV), F32)],
        compiler_params=_params(("arbitrary", "arbitrary"), 40),
        name="retention_prompt",
    )(qk, qk, vg, vg, dm, dq, dk, gr, gn.reshape(1, RET_VW))


def _ret_sample_kernel(q_ref, k_ref, v_ref, g_ref, s0_ref, dm_ref, dq_ref, dk_ref, gr_ref, gn_ref,
                       og_ref, st_ref):
    for hh in range(RET_HEADS):
        ks = slice(hh * RET_DK, (hh + 1) * RET_DK)
        vs = slice(hh * RET_DV, (hh + 1) * RET_DV)
        og, s_new = _ret_block(q_ref[:, ks], k_ref[:, ks], v_ref[:, vs], g_ref[:, vs], s0_ref[0, hh],
                               dm_ref[hh], dq_ref[hh], dk_ref[hh], gr_ref[hh], gn_ref[:, vs])
        og_ref[:, vs] = og
        st_ref[0, hh] = s_new


def _ret_sample_call(qk, vg, gn, state, t_prompt, dec_seq):
    dm, dq, dk, gr = _ret_tables(dec_seq, dec_seq)
    b = state.shape[0]
    r0 = t_prompt // dec_seq
    const3 = lambda bb: (0, 0, 0)
    return pl.pallas_call(
        _ret_sample_kernel,
        grid=(b,),
        in_specs=[
            pl.BlockSpec((dec_seq, D_MODEL), lambda bb: (r0 + bb, 0)),
            pl.BlockSpec((dec_seq, D_MODEL), lambda bb: (r0 + bb, 1)),
            pl.BlockSpec((dec_seq, RET_VW), lambda bb: (r0 + bb, 0)),
            pl.BlockSpec((dec_seq, RET_VW), lambda bb: (r0 + bb, 1)),
            pl.BlockSpec((1, RET_HEADS, RET_DK, RET_DV), lambda bb: (bb, 0, 0, 0)),
            pl.BlockSpec(dm.shape, const3),
            pl.BlockSpec(dq.shape, const3),
            pl.BlockSpec(dk.shape, const3),
            pl.BlockSpec(gr.shape, const3),
            pl.BlockSpec((1, RET_VW), lambda bb: (0, 0)),
        ],
        out_specs=[
            pl.BlockSpec((dec_seq, RET_VW), lambda bb: (bb, 0)),
            pl.BlockSpec((1, RET_HEADS, RET_DK, RET_DV), lambda bb: (bb, 0, 0, 0)),
        ],
        out_shape=[
            jax.ShapeDtypeStruct((b * dec_seq, RET_VW), BF16),
            jax.ShapeDtypeStruct(state.shape, F32),
        ],
        compiler_params=_params(("arbitrary",), 40),
        name="retention_sample",
    )(qk, qk, vg, vg, state, dm, dq, dk, gr, gn.reshape(1, RET_VW))


def _router_kernel(x_ref, g_ref, wr_ref, br_ref, xn_ref, id_ref, w_ref):
    xn = _rms(x_ref[...], g_ref[...])
    xn_ref[...] = xn
    xh = xn.astype(BF16)
    xl = (xn - xh.astype(F32)).astype(BF16)
    w = wr_ref[...]
    wh = w.astype(BF16)
    wl = (w - wh.astype(F32)).astype(BF16)
    lg = (jnp.dot(xh, wh, preferred_element_type=F32) + jnp.dot(xh, wl, preferred_element_type=F32)
          + jnp.dot(xl, wh, preferred_element_type=F32)) + br_ref[...]
    lane = lax.broadcasted_iota(jnp.int32, lg.shape, 1).astype(F32)
    neg = jnp.float32(-jnp.inf)
    big = jnp.float32(LANES)
    gl = jnp.where(lane < MOE_GROUPS, lg, neg)
    gmax = jnp.max(gl, axis=-1, keepdims=True)
    gsel = jnp.min(jnp.where(gl == gmax, lane, big), axis=-1, keepdims=True)
    gw = 1.0 / jnp.sum(jnp.exp(gl - gmax), axis=-1, keepdims=True)
    lo = MOE_GROUPS + gsel * MOE_PER_GROUP
    el = jnp.where((lane >= lo) & (lane < lo + MOE_PER_GROUP), lg, neg)
    m1 = jnp.max(el, axis=-1, keepdims=True)
    i1 = jnp.min(jnp.where(el == m1, lane, big), axis=-1, keepdims=True)
    el2 = jnp.where(lane == i1, neg, el)
    m2 = jnp.max(el2, axis=-1, keepdims=True)
    i2 = jnp.min(jnp.where(el2 == m2, lane, big), axis=-1, keepdims=True)
    z = jnp.sum(jnp.exp(el - m1), axis=-1, keepdims=True)
    p1 = 1.0 / z
    p2 = jnp.exp(m2 - m1) / z
    tot = p1 + p2
    w1 = p1 / tot * gw
    w2 = p2 / tot * gw
    ids = jnp.where(lane == 0, i1 - MOE_GROUPS, jnp.where(lane == 1, i2 - MOE_GROUPS, 0.0))
    id_ref[...] = ids.astype(jnp.int32)
    w_ref[...] = jnp.where(lane == 0, w1, jnp.where(lane == 1, w2, 0.0))


def _router_call(x, g, w_grp, b_grp, w_exp, b_exp):
    t, d = x.shape
    pad = LANES - MOE_GROUPS - N_EXP
    wr = jnp.concatenate([w_grp, w_exp, jnp.zeros((d, pad), F32)], axis=1)
    br = jnp.concatenate([b_grp, b_exp, jnp.zeros((pad,), F32)]).reshape(1, LANES)
    row = lambda i: (i, 0)
    fixed = lambda i: (0, 0)
    return pl.pallas_call(
        _router_kernel,
        grid=(t // ROW_TILE,),
        in_specs=[pl.BlockSpec((ROW_TILE, d), row), pl.BlockSpec((1, d), fixed),
                  pl.BlockSpec((d, LANES), fixed), pl.BlockSpec((1, LANES), fixed)],
        out_specs=[pl.BlockSpec((ROW_TILE, d), row), pl.BlockSpec((ROW_TILE, LANES), row),
                   pl.BlockSpec((ROW_TILE, LANES), row)],
        out_shape=[jax.ShapeDtypeStruct((t, d), F32), jax.ShapeDtypeStruct((t, LANES), jnp.int32),
                   jax.ShapeDtypeStruct((t, LANES), F32)],
        compiler_params=_params(("arbitrary",), 40),
        name="moe_router",
    )(x, g.reshape(1, d), wr, br)


def _cumsum_rows(hot):
    t, n = hot.shape
    blk = LANES
    h3 = hot.reshape(t // blk, blk, n).astype(F32)
    tri = jnp.tril(jnp.ones((blk, blk), F32))
    inner = jnp.einsum("ij,bjn->bin", tri, h3).astype(jnp.int32)
    totals = inner[:, -1, :]
    offsets = jnp.cumsum(totals, axis=0) - totals
    return (inner + offsets[:, None, :]).reshape(t, n)


def _moe_plan(e0, e1):
    t = e0.shape[0]
    n_tiles = 2 * t // MOE_TILE + N_EXP
    experts = jnp.arange(N_EXP, dtype=jnp.int32)[None, :]
    hot0 = (e0[:, None] == experts).astype(jnp.int32)
    hot1 = (e1[:, None] == experts).astype(jnp.int32)
    csum0 = _cumsum_rows(hot0)
    csum1 = _cumsum_rows(hot1)
    counts0 = csum0[-1]
    counts = counts0 + csum1[-1]
    tiles_e = (counts + MOE_TILE - 1) // MOE_TILE
    tiles_end = jnp.cumsum(tiles_e)
    row_start = (tiles_end - tiles_e) * MOE_TILE
    pos0 = jnp.sum(hot0 * (row_start[None, :] + csum0 - 1), axis=1)
    pos1 = jnp.sum(hot1 * (row_start[None, :] + counts0[None, :] + csum1 - 1), axis=1)
    pos = jnp.concatenate([pos0, pos1])
    n_valid = tiles_end[-1]
    tile_id = jnp.minimum(jnp.arange(n_tiles, dtype=jnp.int32), n_valid - 1)
    tile_expert = jnp.sum((tile_id[:, None] >= tiles_end[None, :]).astype(jnp.int32), axis=1)
    tile_expert = jnp.minimum(tile_expert, N_EXP - 1)
    changed = jnp.concatenate([jnp.ones((1,), bool), tile_expert[1:] != tile_expert[:-1]])
    first = jnp.logical_and(changed, jnp.arange(n_tiles) < n_valid).astype(jnp.int32)
    wslot = (jnp.cumsum(first) - 1) % 2
    next_start = tiles_end[tile_expert]
    next_expert = jnp.where(next_start < n_valid, tile_expert[jnp.minimum(next_start, n_tiles - 1)], -1)
    tile_meta = jnp.stack([tile_expert, first, wslot, next_expert]).astype(jnp.int32)
    pad_meta = jnp.stack([row_start + counts, tiles_e * MOE_TILE - counts]).astype(jnp.int32)
    return tile_meta, n_valid.reshape(1).astype(jnp.int32), pad_meta, pos.astype(jnp.int32)


def _dispatch_kernel(pos_ref, nv_ref, pad_ref, x_ref, xs_hbm, zeros, sem, zsem):
    i = pl.program_id(0)
    tm = x_ref.shape[0]
    n_tiles = xs_hbm.shape[0] // MOE_TILE
    n_tok = pl.num_programs(0) * tm

    for r in range(tm):
        for kk in range(2):
            pltpu.make_async_copy(x_ref.at[pl.ds(r, 1)], xs_hbm.at[pl.ds(pos_ref[kk * n_tok + i * tm + r], 1)],
                                  sem).start(priority=kk)

    def zero_rows(start, n):
        return pltpu.make_async_copy(zeros.at[pl.ds(0, n)], xs_hbm.at[pl.ds(start, n)], zsem)

    block_sizes = [1 << b for b in range(MOE_TILE.bit_length() - 2, -1, -1)]

    def fill_expert(e, carry):
        start, length = pad_ref[0, e], pad_ref[1, e]
        head = jnp.bitwise_and(-start, SUBLANES - 1)
        lax.fori_loop(0, head, lambda j, c: (zero_rows(start + j, 1).start(), c)[1], 0)
        rest = length - head
        off = start + head
        for sz in block_sizes:
            if sz < SUBLANES:
                break
            take = jnp.bitwise_and(rest, sz) != 0
            pl.when(take)(lambda off=off, sz=sz: zero_rows(pl.multiple_of(off, SUBLANES), sz).start())
            off = off + jnp.where(take, sz, 0)
        return carry

    def drain_expert(e, carry):
        length = pad_ref[1, e]
        for sz in block_sizes:
            pl.when(jnp.bitwise_and(length, sz) != 0)(lambda sz=sz: zero_rows(0, sz).wait())
        return carry

    def tail_tile_copy(tile):
        return pltpu.make_async_copy(zeros, xs_hbm.at[pl.ds(tile * MOE_TILE, MOE_TILE)], zsem)

    @pl.when(i == 0)
    def _():
        zeros[...] = jnp.zeros_like(zeros)
        lax.fori_loop(0, N_EXP, fill_expert, 0)
        lax.fori_loop(nv_ref[0], n_tiles, lambda tile, c: (tail_tile_copy(tile).start(), c)[1], 0)

    for kk in range(2):
        pltpu.make_async_copy(x_ref, xs_hbm.at[pl.ds(0, tm)], sem).wait()

    @pl.when(i == pl.num_programs(0) - 1)
    def _():
        lax.fori_loop(0, N_EXP, drain_expert, 0)
        lax.fori_loop(nv_ref[0], n_tiles, lambda tile, c: (tail_tile_copy(tile).wait(), c)[1], 0)


def _dispatch_call(xn, pos, n_valid, pad_meta, rows):
    t, d = xn.shape
    grid_spec = pltpu.PrefetchScalarGridSpec(
        num_scalar_prefetch=3,
        grid=(t // ROW_TILE,),
        in_specs=[pl.BlockSpec((ROW_TILE, d), lambda i, pos_ref, nv, pad: (i, 0))],
        out_specs=pl.BlockSpec(memory_space=pl.ANY),
        scratch_shapes=[pltpu.VMEM((MOE_TILE, d), F32), pltpu.SemaphoreType.DMA(()), pltpu.SemaphoreType.DMA(())],
    )
    return pl.pallas_call(
        _dispatch_kernel,
        grid_spec=grid_spec,
        out_shape=jax.ShapeDtypeStruct((rows, d), F32),
        compiler_params=_params(("arbitrary",), 32),
        name="moe_dispatch",
    )(pos, n_valid, pad_meta, xn)


def _moe_ffn_kernel(tm_ref, nv_ref, x_ref, wgu_hbm, wd_hbm, o_ref,
                    wgu_f, wd_f, wsems, wgu_b, wd_b, *, layer):
    i = pl.program_id(0)
    nv = nv_ref[0]
    expert, first, wslot, next_expert = tm_ref[0, i], tm_ref[1, i], tm_ref[2, i], tm_ref[3, i]

    def weight_copies(e, s):
        return (pltpu.make_async_copy(wgu_hbm.at[layer, e], wgu_f.at[s], wsems.at[0, s]),
                pltpu.make_async_copy(wd_hbm.at[layer, e], wd_f.at[s], wsems.at[1, s]))

    @pl.when(i == 0)
    def _():
        for c in weight_copies(expert, 0):
            c.start()

    @pl.when(first == 1)
    def _():
        for c in weight_copies(expert, wslot):
            c.wait()
        wgu_b[...] = wgu_f[wslot].astype(BF16)
        wd_b[...] = wd_f[wslot].astype(BF16)

        @pl.when(next_expert >= 0)
        def _():
            for c in weight_copies(next_expert, 1 - wslot):
                c.start()

    @pl.when(i < nv)
    def _():
        h = jnp.dot(x_ref[...].astype(BF16), wgu_b[...], preferred_element_type=F32)
        hg = h[:, :D_FF_E]
        hu = h[:, D_FF_E:]
        act = (hg * _sigmoid(hg) * hu).astype(BF16)
        o_ref[...] = jnp.dot(act, wd_b[...], preferred_element_type=F32)

    @pl.when(i >= nv)
    def _():
        o_ref[...] = jnp.zeros_like(o_ref)


def _moe_ffn_call(xs, w_gu, w_down, layer, tile_meta, n_valid):
    rows, d = xs.shape
    n_tiles = tile_meta.shape[1]
    grid_spec = pltpu.PrefetchScalarGridSpec(
        num_scalar_prefetch=2,
        grid=(n_tiles,),
        in_specs=[pl.BlockSpec((MOE_TILE, d), lambda i, tm, nv: (jnp.minimum(i, nv[0] - 1), 0)),
                  pl.BlockSpec(memory_space=pl.ANY), pl.BlockSpec(memory_space=pl.ANY)],
        out_specs=pl.BlockSpec((MOE_TILE, d), lambda i, tm, nv: (i, 0)),
        scratch_shapes=[
            pltpu.VMEM((2, d, 2 * D_FF_E), F32),
            pltpu.VMEM((2, D_FF_E, d), F32),
            pltpu.SemaphoreType.DMA((2, 2)),
            pltpu.VMEM((d, 2 * D_FF_E), BF16),
            pltpu.VMEM((D_FF_E, d), BF16),
        ],
    )
    return pl.pallas_call(
        functools.partial(_moe_ffn_kernel, layer=layer),
        grid_spec=grid_spec,
        out_shape=jax.ShapeDtypeStruct((rows, d), F32),
        compiler_params=_params(("arbitrary",), 48),
        name="moe_experts",
    )(tile_meta, n_valid, xs, w_gu, w_down)


def _combine_kernel(pos_ref, x_ref, w_ref, y_hbm, g_ref, *rest, emit_x, norm_tiles):
    outs = rest[:int(emit_x) + len(norm_tiles)]
    buf, sems = rest[int(emit_x) + len(norm_tiles):]
    i = pl.program_id(0)
    n = pl.num_programs(0)
    slot = lax.rem(i, 2)
    tm = x_ref.shape[0]

    def gather(tile, dst_slot):
        n_tok = n * tm
        for r in range(tm):
            for kk in range(2):
                pltpu.make_async_copy(y_hbm.at[pl.ds(pos_ref[kk * n_tok + tile * tm + r], 1)],
                                      buf.at[dst_slot, kk, pl.ds(r, 1)], sems.at[dst_slot]).start(priority=kk)

    @pl.when(i == 0)
    def _():
        gather(0, 0)

    for s in range(2):
        pl.when(jnp.logical_and(i + 1 < n, slot == s))(functools.partial(gather, i + 1, 1 - s))

    for kk in range(2):
        pltpu.make_async_copy(y_hbm.at[pl.ds(0, tm)], buf.at[slot, kk], sems.at[slot]).wait()
    xo = x_ref[...] + (w_ref[:, 0:1] * buf[slot, 0] + w_ref[:, 1:2] * buf[slot, 1])
    norm_refs = outs[int(emit_x):]
    if emit_x:
        outs[0][...] = xo

    def write_norm(p):
        norm_refs[p][...] = _rms(xo, g_ref[...]).astype(norm_refs[p].dtype)

    _for_each_part(i, norm_tiles, write_norm)


def _combine_call(x, ws, y_rows, pos, g, emit_x, norm_dtype, norm_rows):
    t, d = x.shape
    tm = MOE_TILE
    row = lambda i, pos_ref: (i, 0)
    out_specs, out_shape = [], []
    if emit_x:
        out_specs.append(pl.BlockSpec((tm, d), row))
        out_shape.append(jax.ShapeDtypeStruct((t, d), F32))
    norm_parts = [jax.ShapeDtypeStruct((r, d), norm_dtype) for r in norm_rows]
    out_specs += _part_specs(norm_parts, tm, d, row)
    out_shape += norm_parts
    grid_spec = pltpu.PrefetchScalarGridSpec(
        num_scalar_prefetch=1,
        grid=(t // tm,),
        in_specs=[pl.BlockSpec((tm, d), row), pl.BlockSpec((tm, LANES), row), pl.BlockSpec(memory_space=pl.ANY),
                  pl.BlockSpec((1, d), lambda i, pos_ref: (0, 0))],
        out_specs=out_specs,
        scratch_shapes=[pltpu.VMEM((2, 2, tm, d), F32), pltpu.SemaphoreType.DMA((2,))],
    )
    return pl.pallas_call(
        functools.partial(_combine_kernel, emit_x=emit_x, norm_tiles=_part_tiles(norm_parts, tm)),
        grid_spec=grid_spec,
        out_shape=out_shape,
        compiler_params=_params(("arbitrary",), 40),
        name="moe_combine",
    )(pos, x, ws, y_rows, g.reshape(1, d))


def _moe_layer(x, g_ffn, w_grp, b_grp, w_exp, b_exp, w_gu, w_down, layer, g_next, emit_x, norm_dtype, norm_rows):
    xn, ids, ws = _router_call(x, g_ffn, w_grp, b_grp, w_exp, b_exp)
    tile_meta, n_valid, pad_meta, pos = _moe_plan(ids[:, 0], ids[:, 1])
    xs = _dispatch_call(xn, pos, n_valid, pad_meta, tile_meta.shape[1] * MOE_TILE)
    y_rows = _moe_ffn_call(xs, w_gu, w_down, layer, tile_meta, n_valid)
    return _combine_call(x, ws, y_rows, pos, g_next, emit_x, norm_dtype, norm_rows)


def _s5_prep_kernel(are_ref, aim_ref, ldt_ref, bre_ref, bim_ref, lre_ref, lim_ref, bbr_ref, bbi_ref):
    are = are_ref[...]
    aim = aim_ref[...]
    dt = jnp.exp(ldt_ref[...])
    mag = jnp.exp(are * dt)
    lr = mag * jnp.cos(aim * dt)
    li = mag * jnp.sin(aim * dt)
    lre_ref[...] = lr
    lim_ref[...] = li
    den = are * are + aim * aim
    nr = lr - 1.0
    cr = (nr * are + li * aim) / den
    ci = (li * are - nr * aim) / den
    bre = bre_ref[...]
    bim = bim_ref[...]
    bbr_ref[...] = cr * bre - ci * bim
    bbi_ref[...] = cr * bim + ci * bre


def _s5_prep_call(a_re, a_im, log_dt, b_re, b_im):
    g, p, c = b_re.shape
    shp = (g, p * c)
    expand = lambda a: jnp.broadcast_to(a[:, :, None], (g, p, c)).reshape(shp)
    ldt = jnp.broadcast_to(log_dt[:, None], shp)
    outs = pl.pallas_call(
        _s5_prep_kernel,
        out_shape=[jax.ShapeDtypeStruct(shp, F32)] * 4,
        name="s5_discretize",
    )(expand(a_re), expand(a_im), ldt, b_re.reshape(shp), b_im.reshape(shp))
    lre, lim, bbr, bbi = outs
    return lre[:, ::c], lim[:, ::c], bbr, bbi


def _s5_compact(bbr, bbi, c_re, c_im):
    def b_part(bb):
        return bb.reshape(S5_FB, 16, S5_P, S5_GC).transpose(0, 1, 3, 2).reshape(S5_FB, 16 * S5_GC, S5_P)

    def c_part(cc):
        return cc.reshape(S5_FB, 16, S5_GC, S5_P).transpose(0, 3, 1, 2).reshape(S5_FB, S5_P, 16 * S5_GC)

    return b_part(bbr), b_part(bbi), c_part(c_re), c_part(c_im)


def _s5_build_mats(bbr_ref, bbi_ref, ccr_ref, cci_ref, bd_ref, cd_ref):
    nch, nst = 16 * S5_GC, 16 * S5_P
    iota = lambda shape, dim: lax.broadcasted_iota(jnp.int32, shape, dim)
    rep = ((iota((S5_P, nst), 1) & (S5_P - 1)) == iota((S5_P, nst), 0)).astype(BF16)
    rep_t = ((iota((nst, S5_P), 0) & (S5_P - 1)) == iota((nst, S5_P), 1)).astype(BF16)
    gc_bits, p_bits = S5_GC.bit_length() - 1, S5_P.bit_length() - 1
    diag = (iota((nch, nst), 0) >> gc_bits) == (iota((nch, nst), 1) >> p_bits)
    diag_t = (iota((nst, nch), 0) >> p_bits) == (iota((nst, nch), 1) >> gc_bits)
    for fb in range(S5_FB):
        for part, src in enumerate((bbr_ref, bbi_ref)):
            full = jnp.dot(src[fb].astype(BF16), rep, preferred_element_type=F32)
            bd_ref[fb, :, part * nst:(part + 1) * nst] = jnp.where(diag, full, 0.0).astype(BF16)
        for part, (src, sign) in enumerate(((ccr_ref, 1.0), (cci_ref, -1.0))):
            full = jnp.dot(rep_t, src[fb].astype(BF16), preferred_element_type=F32)
            cd_ref[fb, part * nst:(part + 1) * nst, :] = jnp.where(diag_t, sign * full, 0.0).astype(BF16)


def _s5_kernel(u_ref, bbr_ref, bbi_ref, ccr_ref, cci_ref, lre_ref, lim_ref, d_ref, h0r_ref, h0i_ref,
               z_ref, hr_ref, hi_ref, scr, hst, bd_ref, cd_ref, *, seg, carry):
    rows = u_ref.shape[0]
    n_seq = rows // seg
    ch = 16 * S5_GC
    half = SUBLANES * S5_PITCH
    blk = 2 * half

    @pl.when(pl.program_id(0) == 0)
    def _():
        _s5_build_mats(bbr_ref, bbi_ref, ccr_ref, cci_ref, bd_ref, cd_ref)

    for fb in range(S5_FB):
        bu = jnp.dot(u_ref[:, fb * ch:(fb + 1) * ch].astype(BF16), bd_ref[fb], preferred_element_type=F32)
        for c in range(2 * SUBLANES):
            r0 = fb * blk + c * S5_PITCH
            scr[r0:r0 + rows, :] = bu[:, c * LANES:(c + 1) * LANES]

    lre = [lre_ref[fb * SUBLANES:(fb + 1) * SUBLANES, :] for fb in range(S5_FB)]
    lim = [lim_ref[fb * SUBLANES:(fb + 1) * SUBLANES, :] for fb in range(S5_FB)]

    def scan(row0, state):
        def step(l, st):
            new = []
            for fb in range(S5_FB):
                hr, hi = st[2 * fb], st[2 * fb + 1]
                ire = pl.ds(fb * blk + row0 + l, SUBLANES, stride=S5_PITCH)
                iim = pl.ds(fb * blk + half + row0 + l, SUBLANES, stride=S5_PITCH)
                nr = lre[fb] * hr - lim[fb] * hi + scr[ire, :]
                ni = lre[fb] * hi + lim[fb] * hr + scr[iim, :]
                scr[ire, :] = nr
                scr[iim, :] = ni
                new += [nr, ni]
            return tuple(new)
        return lax.fori_loop(0, seg, step, state, unroll=4)

    def load_state(rref, iref, lead):
        st = []
        for fb in range(S5_FB):
            st += [rref[lead + (slice(fb * SUBLANES, (fb + 1) * SUBLANES), slice(None))],
                   iref[lead + (slice(fb * SUBLANES, (fb + 1) * SUBLANES), slice(None))]]
        return tuple(st)

    def store_state(st, rref, iref, lead):
        for fb in range(S5_FB):
            rref[lead + (slice(fb * SUBLANES, (fb + 1) * SUBLANES), slice(None))] = st[2 * fb]
            iref[lead + (slice(fb * SUBLANES, (fb + 1) * SUBLANES), slice(None))] = st[2 * fb + 1]

    if carry:
        @pl.when(pl.program_id(0) == 0)
        def _():
            hst[0] = h0r_ref[0]
            hst[1] = h0i_ref[0]
        hre_scr = hst.at[0]
        him_scr = hst.at[1]
        st = scan(0, load_state(hre_scr, him_scr, ()))
        store_state(st, hre_scr, him_scr, ())
        store_state(st, hr_ref, hi_ref, (0,))
    else:
        for s in range(n_seq):
            st = scan(s * seg, load_state(h0r_ref, h0i_ref, (s,)))
            store_state(st, hr_ref, hi_ref, (s,))

    for fb in range(S5_FB):
        hf = jnp.concatenate(
            [scr[fb * blk + c * S5_PITCH: fb * blk + c * S5_PITCH + rows, :] for c in range(2 * SUBLANES)],
            axis=1).astype(BF16)
        y = jnp.dot(hf, cd_ref[fb], preferred_element_type=F32)
        cs = slice(fb * ch, (fb + 1) * ch)
        yy = y + d_ref[:, cs] * u_ref[:, cs]
        z_ref[:, cs] = jax.nn.gelu(yy).astype(z_ref.dtype)


def _s5_call(u, mats, lre, lim, d_skip, h0r, h0i, row_block0, n_blocks, seg, carry):
    t, d = u.shape
    ch, nst = 16 * S5_GC, 16 * S5_P
    n_seq = S5_ROWS // seg
    ns_total = h0r.shape[0]
    st_idx = (lambda i: (0, 0, 0)) if carry else (lambda i: (i, 0, 0))
    fixed2 = lambda i: (0, 0)
    fixed3 = lambda i: (0, 0, 0)
    st_rows = S5_FB * SUBLANES
    return pl.pallas_call(
        functools.partial(_s5_kernel, seg=seg, carry=carry),
        grid=(n_blocks,),
        in_specs=[
            pl.BlockSpec((S5_ROWS, d), lambda i: (row_block0 + i, 0)),
            *[pl.BlockSpec(m.shape, fixed3) for m in mats],
            pl.BlockSpec(lre.shape, fixed2),
            pl.BlockSpec(lim.shape, fixed2),
            pl.BlockSpec((1, d), fixed2),
            pl.BlockSpec((n_seq, st_rows, LANES), st_idx),
            pl.BlockSpec((n_seq, st_rows, LANES), st_idx),
        ],
        out_specs=[
            pl.BlockSpec((S5_ROWS, d), lambda i: (i, 0)),
            pl.BlockSpec((n_seq, st_rows, LANES), st_idx),
            pl.BlockSpec((n_seq, st_rows, LANES), st_idx),
        ],
        out_shape=[
            jax.ShapeDtypeStruct((n_blocks * S5_ROWS, d), BF16),
            jax.ShapeDtypeStruct((ns_total, st_rows, LANES), F32),
            jax.ShapeDtypeStruct((ns_total, st_rows, LANES), F32),
        ],
        scratch_shapes=[
            pltpu.VMEM((2 * SUBLANES * S5_FB * S5_PITCH, LANES), F32),
            pltpu.VMEM((2, st_rows, LANES), F32),
            pltpu.VMEM((S5_FB, ch, 2 * nst), BF16),
            pltpu.VMEM((S5_FB, 2 * nst, ch), BF16),
        ],
        compiler_params=_params(("arbitrary",), 56),
        name="s5_scan",
    )(u, *mats, lre, lim, d_skip.reshape(1, d), h0r, h0i)


def kernel(x_prompt, x_sample, state_ret, state_s5_re, state_s5_im, norm_mix_g, norm_ffn_g, norm_final_g,
           ret_w_in, ret_gn_g, ret_w_out, s5_w_in, s5_a_re, s5_a_im, s5_log_dt, s5_b_re, s5_b_im,
           s5_c_re, s5_c_im, s5_d, s5_w_out, moe_w_grp, moe_b_grp, moe_w_exp, moe_b_exp, moe_w_gu, moe_w_down):
    bp, seq, d = x_prompt.shape
    db, dseq, _ = x_sample.shape
    assert d == D_MODEL and bp == 1
    assert seq % max(ROW_TILE, RET_ROWS) == 0 and seq % CHUNK == 0
    assert dseq < CHUNK and dseq % 16 == 0 and S5_ROWS % dseq == 0
    tp = bp * seq
    ts = db * dseq
    t = tp + ts
    assert ts % ROW_TILE == 0 and ts % S5_ROWS == 0

    x_parts = [x_prompt.reshape(tp, d), x_sample.reshape(ts, d)]

    half = RET_DK // 2
    freqs = ROPE_BASE ** (-jnp.arange(half, dtype=F32) / half)
    pos = jnp.concatenate([jnp.arange(seq, dtype=F32), jnp.tile(PAST_LEN + jnp.arange(dseq, dtype=F32), db)])
    ang = pos[:, None] * freqs[None, :]
    cos_t = jnp.cos(ang)
    sin_t = jnp.sin(ang)

    xn = _norm_call(x_parts, norm_mix_g[0], BF16)
    tn = 1024
    first_col = lambda j: 0
    qk = _mm_call([xn], ret_w_in[0], [0], tn, 2 * D_MODEL, [([cos_t], half, first_col), ([sin_t], half, first_col)],
                  BF16, functools.partial(_epi_rope, n_q_blocks=D_MODEL // tn), 40, "ret_qk_proj")
    vg = _mm_call([xn], ret_w_in[0], [2 * D_MODEL // tn], tn, 2 * RET_VW, [], BF16, _epi_plain, 40, "ret_vg_proj")
    og_p, ret_p = _ret_prompt_call(qk, vg, ret_gn_g[0], tp)
    og_s, ret_s = _ret_sample_call(qk, vg, ret_gn_g[0], state_ret[0], tp, dseq)
    tn = 512
    same_col = lambda j: j
    x1 = _mm_call([og_p, og_s], ret_w_out[0], [0], tn, d, [(x_parts, tn, same_col)], F32, _epi_residual,
                  40, "ret_out_proj")

    x2, xn = _moe_layer(x1, norm_ffn_g[0], moe_w_grp[0], moe_b_grp[0], moe_w_exp[0], moe_b_exp[0],
                        moe_w_gu, moe_w_down, 0, norm_mix_g[1], True, BF16, (t,))

    u = _mm_call([xn], s5_w_in[0], [0], 1024, d, [], F32, _epi_plain, 40, "s5_in_proj")
    lre, lim, bbr, bbi = _s5_prep_call(s5_a_re[0], s5_a_im[0], s5_log_dt[0], s5_b_re[0], s5_b_im[0])
    mats = _s5_compact(bbr, bbi, s5_c_re[0], s5_c_im[0])
    st_shape = (S5_FB * SUBLANES, LANES)
    lre = lre.reshape(st_shape)
    lim = lim.reshape(st_shape)
    zero_state = jnp.zeros((1,) + st_shape, F32)
    z_p, hpr, hpi = _s5_call(u, mats, lre, lim, s5_d[0], zero_state, zero_state,
                             0, tp // S5_ROWS, S5_ROWS, True)
    z_s, hsr, hsi = _s5_call(u, mats, lre, lim, s5_d[0],
                             state_s5_re[0].reshape((db,) + st_shape), state_s5_im[0].reshape((db,) + st_shape),
                             tp // S5_ROWS, ts // S5_ROWS, dseq, False)
    x3 = _mm_call([z_p, z_s], s5_w_out[0], [0, d // tn], tn, d, [([x2], tn, same_col)], F32, _epi_glu_residual,
                  40, "s5_out_proj")

    y_p, y_s = _moe_layer(x3, norm_ffn_g[1], moe_w_grp[1], moe_b_grp[1], moe_w_exp[1], moe_b_exp[1],
                          moe_w_gu, moe_w_down, 1, norm_final_g, False, F32, (tp, ts))

    s5_shape = (1, -1, S5_GROUPS, S5_P)
    return (y_p.reshape(bp, seq, d), y_s.reshape(db, dseq, d),
            ret_p.reshape(1, bp, RET_HEADS, RET_DK, RET_DV), ret_s.reshape((1,) + ret_s.shape),
            hpr.reshape(s5_shape), hpi.reshape(s5_shape), hsr.reshape(s5_shape), hsi.reshape(s5_shape))
```

```python
import functools
import math

import numpy as np
import jax
import jax.numpy as jnp
from jax import lax
from jax.experimental import pallas as pl
from jax.experimental.pallas import tpu as pltpu

F32 = jnp.float32
BF16 = jnp.bfloat16

D_MODEL = 2048
PAST_LEN = 2048
CHUNK = 64
RET_HEADS = 8
RET_DK = D_MODEL // RET_HEADS
RET_DV = 2 * RET_DK
RET_VW = RET_HEADS * RET_DV
ROPE_BASE = 10000.0
S5_GC = 16
S5_GROUPS = D_MODEL // S5_GC
S5_P = 64
MOE_GROUPS = 4
MOE_PER_GROUP = 8
N_EXP = MOE_GROUPS * MOE_PER_GROUP
D_FF_E = 512
EPS = 1e-6

LANES = 128
SUBLANES = 8
ROW_TILE = 512
MM_CHUNK = 256
RET_ROWS = 256
RET_HPS = 4
MOE_TILE = 256
S5_ROWS = 256
S5_FB = S5_GROUPS // 16
S5_PITCH = S5_ROWS + 4
MIB = 1024 * 1024


def _params(sem, vmem_mib):
    return pltpu.CompilerParams(dimension_semantics=sem, vmem_limit_bytes=vmem_mib * MIB)


def _rms(x, g):
    ms = jnp.mean(x * x, axis=-1, keepdims=True)
    return x * lax.rsqrt(ms + EPS) * g


def _sigmoid(x):
    return 1.0 / (1.0 + jnp.exp(-x))


def _part_specs(parts, rows, cols, rowcol):
    specs = []
    start = 0
    for p in parts:
        nt = p.shape[0] // rows

        def imap(*g, start=start, nt=nt):
            i, jc = rowcol(*g)
            return (jnp.clip(i - start, 0, nt - 1), jc)

        specs.append(pl.BlockSpec((rows, cols), imap))
        start += nt
    return specs


def _part_tiles(parts, rows):
    return tuple(p.shape[0] // rows for p in parts)


def _for_each_part(i, tiles, fn):
    if len(tiles) == 1:
        fn(0)
        return
    start = 0
    for p, nt in enumerate(tiles):
        pl.when(jnp.logical_and(i >= start, i < start + nt))(functools.partial(fn, p))
        start += nt


def _pick(refs, p):
    return refs[min(p, len(refs) - 1)]


def _norm_kernel(*refs, tiles):
    n = len(tiles)
    x_refs, g_ref, o_ref = refs[:n], refs[n], refs[n + 1]

    def run(p):
        o_ref[...] = _rms(x_refs[p][...], g_ref[...]).astype(o_ref.dtype)

    _for_each_part(pl.program_id(0), tiles, run)


def _norm_call(x_parts, g, out_dtype):
    d = x_parts[0].shape[1]
    t = sum(p.shape[0] for p in x_parts)
    return pl.pallas_call(
        functools.partial(_norm_kernel, tiles=_part_tiles(x_parts, ROW_TILE)),
        grid=(t // ROW_TILE,),
        in_specs=_part_specs(x_parts, ROW_TILE, d, lambda i: (i, 0)) + [pl.BlockSpec((1, d), lambda i: (0, 0))],
        out_specs=pl.BlockSpec((ROW_TILE, d), lambda i: (i, 0)),
        out_shape=jax.ShapeDtypeStruct((t, d), out_dtype),
        compiler_params=_params(("arbitrary",), 32),
        name="rmsnorm",
    )(*x_parts, g.reshape(1, d))


def _mm_kernel(*refs, tiles, n_x, n_w, n_extra, epi):
    x_refs = refs[:n_x]
    w_refs = refs[n_x:n_x + n_w]
    pos = n_x + n_w
    extra = []
    for n in n_extra:
        extra.append(refs[pos:pos + n])
        pos += n
    o_ref = refs[pos]
    wb = refs[pos + 1:]

    @pl.when(pl.program_id(1) == 0)
    def _():
        for w_ref, b in zip(w_refs, wb):
            b[...] = w_ref[...].astype(BF16)

    def run(p):
        x = _pick(x_refs, p)[...]
        for c in range(o_ref.shape[1] // MM_CHUNK):
            cs = slice(c * MM_CHUNK, (c + 1) * MM_CHUNK)
            accs = [jnp.dot(x, b[:, cs], preferred_element_type=F32) for b in wb]
            epi(accs, [_pick(e, p) for e in extra], o_ref, cs)

    _for_each_part(pl.program_id(1), tiles, run)


def _mm_call(x_parts, w, w_col_offsets, tn, n_out, extras, out_dtype, epi, vmem_mib, name):
    k = x_parts[0].shape[1]
    t = sum(p.shape[0] for p in x_parts)
    n_w = len(w_col_offsets)
    split = [parts for parts in [x_parts] + [e[0] for e in extras] if len(parts) > 1]
    tiles = _part_tiles(split[0], ROW_TILE) if split else (t // ROW_TILE,)
    assert all(_part_tiles(parts, ROW_TILE) == tiles for parts in split)
    in_specs = _part_specs(x_parts, ROW_TILE, k, lambda j, i: (i, 0))
    for off in w_col_offsets:
        in_specs.append(pl.BlockSpec((k, tn), lambda j, i, off=off: (0, j + off)))
    extra_arrays = []
    for parts, cols, colfn in extras:
        in_specs += _part_specs(parts, ROW_TILE, cols, lambda j, i, colfn=colfn: (i, colfn(j)))
        extra_arrays += list(parts)
    return pl.pallas_call(
        functools.partial(_mm_kernel, tiles=tiles, n_x=len(x_parts), n_w=n_w,
                          n_extra=tuple(len(e[0]) for e in extras), epi=epi),
        grid=(n_out // tn, t // ROW_TILE),
        in_specs=in_specs,
        out_specs=pl.BlockSpec((ROW_TILE, tn), lambda j, i: (i, j)),
        out_shape=jax.ShapeDtypeStruct((t, n_out), out_dtype),
        scratch_shapes=[pltpu.VMEM((k, tn), BF16) for _ in range(n_w)],
        compiler_params=_params(("arbitrary", "arbitrary"), vmem_mib),
        name=name,
    )(*x_parts, *([w] * n_w), *extra_arrays)


def _epi_plain(accs, extra, o_ref, cs):
    o_ref[:, cs] = accs[0].astype(o_ref.dtype)


def _epi_residual(accs, extra, o_ref, cs):
    o_ref[:, cs] = (extra[0][:, cs] + accs[0]).astype(o_ref.dtype)


def _epi_glu_residual(accs, extra, o_ref, cs):
    a, gt = accs
    o_ref[:, cs] = (extra[0][:, cs] + a * _sigmoid(gt)).astype(o_ref.dtype)


def _epi_rope(accs, extra, o_ref, cs, *, n_q_blocks):
    acc = accs[0]
    assert acc.shape[1] == RET_DK
    cos_ref, sin_ref = extra
    half = RET_DK // 2
    scale = jnp.where(pl.program_id(0) >= n_q_blocks, RET_DK ** -0.5, 1.0).astype(F32)
    c = cos_ref[...]
    s = sin_ref[...]
    t1 = acc[:, :half]
    t2 = acc[:, half:]
    o_ref[:, cs.start:cs.start + half] = ((t1 * c - t2 * s) * scale).astype(o_ref.dtype)
    o_ref[:, cs.start + half:cs.stop] = ((t1 * s + t2 * c) * scale).astype(o_ref.dtype)


def _ret_tables(rows, chunk):
    lg = jnp.log(jnp.asarray(1.0 - 2.0 ** (-5.0 - np.arange(RET_HEADS)), dtype=F32))[:, None, None]
    n = jnp.arange(rows, dtype=F32)
    ci = np.arange(rows) // chunk
    same_or_earlier = jnp.asarray(ci[None, :] <= ci[:, None])
    dmat = jnp.where(same_or_earlier[None], jnp.exp(jnp.abs(n[:, None] - n[None, :])[None] * lg), 0.0)
    dq = jnp.exp((n + 1.0)[None, :, None] * lg) * jnp.ones((1, 1, LANES), F32)
    dk = jnp.exp((rows - 1.0 - n)[None, :, None] * lg) * jnp.ones((1, 1, LANES), F32)
    gr = jnp.exp(rows * lg) * jnp.ones((1, 1, RET_DV), F32)
    return dmat, dq, dk, gr


def _ret_block(q, k, v, g, s_prev, dmat, dq, dk, gr, gn):
    s = lax.dot_general(q, k, (((1,), (1,)), ((), ())), preferred_element_type=F32) * dmat
    inter = jnp.dot(q, s_prev.astype(BF16), preferred_element_type=F32)
    o = jnp.dot(s.astype(BF16), v, preferred_element_type=F32) + jnp.concatenate([dq] * (RET_DV // LANES), axis=1) * inter
    kd = (k.astype(F32) * jnp.concatenate([dk] * (RET_DK // LANES), axis=1)).astype(BF16)
    s_new = s_prev * gr + lax.dot_general(kd, v, (((0,), (0,)), ((), ())), preferred_element_type=F32)
    mu = jnp.mean(o, axis=-1, keepdims=True)
    oc = o - mu
    var = jnp.mean(oc * oc, axis=-1, keepdims=True)
    on = oc * lax.rsqrt(var + EPS) * gn
    gf = g.astype(F32)
    return (gf * _sigmoid(gf) * on).astype(BF16), s_new


def _ret_prompt_kernel(q_ref, k_ref, v_ref, g_ref, dm_ref, dq_ref, dk_ref, gr_ref, gn_ref,
                       og_ref, st_ref, s_scr):
    @pl.when(pl.program_id(1) == 0)
    def _():
        s_scr[...] = jnp.zeros_like(s_scr)

    for hh in range(RET_HPS):
        ks = slice(hh * RET_DK, (hh + 1) * RET_DK)
        vs = slice(hh * RET_DV, (hh + 1) * RET_DV)
        og, s_new = _ret_block(q_ref[:, ks], k_ref[:, ks], v_ref[:, vs], g_ref[:, vs], s_scr[hh],
                               dm_ref[hh], dq_ref[hh], dk_ref[hh], gr_ref[hh], gn_ref[:, vs])
        og_ref[:, vs] = og
        s_scr[hh] = s_new
        st_ref[hh] = s_new


def _ret_prompt_call(qk, vg, gn, t_prompt):
    dm, dq, dk, gr = _ret_tables(RET_ROWS, CHUNK)
    hps = RET_HPS
    hb = RET_HEADS // hps
    return pl.pallas_call(
        _ret_prompt_kernel,
        grid=(hb, t_prompt // RET_ROWS),
        in_specs=[
            pl.BlockSpec((RET_ROWS, hps * RET_DK), lambda hh, c: (c, hh)),
            pl.BlockSpec((RET_ROWS, hps * RET_DK), lambda hh, c: (c, hb + hh)),
            pl.BlockSpec((RET_ROWS, hps * RET_DV), lambda hh, c: (c, hh)),
            pl.BlockSpec((RET_ROWS, hps * RET_DV), lambda hh, c: (c, hb + hh)),
            pl.BlockSpec((hps, RET_ROWS, RET_ROWS), lambda hh, c: (hh, 0, 0)),
            pl.BlockSpec((hps, RET_ROWS, LANES), lambda hh, c: (hh, 0, 0)),
            pl.BlockSpec((hps, RET_ROWS, LANES), lambda hh, c: (hh, 0, 0)),
            pl.BlockSpec((hps, 1, RET_DV), lambda hh, c: (hh, 0, 0)),
            pl.BlockSpec((1, hps * RET_DV), lambda hh, c: (0, hh)),
        ],
        out_specs=[
            pl.BlockSpec((RET_ROWS, hps * RET_DV), lambda hh, c: (c, hh)),
            pl.BlockSpec((hps, RET_DK, RET_DV), lambda hh, c: (hh, 0, 0)),
        ],
        out_shape=[
            jax.ShapeDtypeStruct((t_prompt, RET_VW), BF16),
            jax.ShapeDtypeStruct((RET_HEADS, RET_DK, RET_DV), F32),
        ],
        scratch_shapes=[pltpu.VMEM((hps, RET_DK, RET_DV), F32)],
        compiler_params=_params(("arbitrary", "arbitrary"), 40),
        name="retention_prompt",
    )(qk, qk, vg, vg, dm, dq, dk, gr, gn.reshape(1, RET_VW))


def _ret_sample_kernel(q_ref, k_ref, v_ref, g_ref, s0_ref, dm_ref, dq_ref, dk_ref, gr_ref, gn_ref,
                       og_ref, st_ref):
    for hh in range(RET_HEADS):
        ks = slice(hh * RET_DK, (hh + 1) * RET_DK)
        vs = slice(hh * RET_DV, (hh + 1) * RET_DV)
        og, s_new = _ret_block(q_ref[:, ks], k_ref[:, ks], v_ref[:, vs], g_ref[:, vs], s0_ref[0, hh],
                               dm_ref[hh], dq_ref[hh], dk_ref[hh], gr_ref[hh], gn_ref[:, vs])
        og_ref[:, vs] = og
        st_ref[0, hh] = s_new


def _ret_sample_call(qk, vg, gn, state, t_prompt, dec_seq):
    dm, dq, dk, gr = _ret_tables(dec_seq, dec_seq)
    b = state.shape[0]
    r0 = t_prompt // dec_seq
    const3 = lambda bb: (0, 0, 0)
    return pl.pallas_call(
        _ret_sample_kernel,
        grid=(b,),
        in_specs=[
            pl.BlockSpec((dec_seq, D_MODEL), lambda bb: (r0 + bb, 0)),
            pl.BlockSpec((dec_seq, D_MODEL), lambda bb: (r0 + bb, 1)),
            pl.BlockSpec((dec_seq, RET_VW), lambda bb: (r0 + bb, 0)),
            pl.BlockSpec((dec_seq, RET_VW), lambda bb: (r0 + bb, 1)),
            pl.BlockSpec((1, RET_HEADS, RET_DK, RET_DV), lambda bb: (bb, 0, 0, 0)),
            pl.BlockSpec(dm.shape, const3),
            pl.BlockSpec(dq.shape, const3),
            pl.BlockSpec(dk.shape, const3),
            pl.BlockSpec(gr.shape, const3),
            pl.BlockSpec((1, RET_VW), lambda bb: (0, 0)),
        ],
        out_specs=[
            pl.BlockSpec((dec_seq, RET_VW), lambda bb: (bb, 0)),
            pl.BlockSpec((1, RET_HEADS, RET_DK, RET_DV), lambda bb: (bb, 0, 0, 0)),
        ],
        out_shape=[
            jax.ShapeDtypeStruct((b * dec_seq, RET_VW), BF16),
            jax.ShapeDtypeStruct(state.shape, F32),
        ],
        compiler_params=_params(("arbitrary",), 40),
        name="retention_sample",
    )(qk, qk, vg, vg, state, dm, dq, dk, gr, gn.reshape(1, RET_VW))


def _router_kernel(x_ref, g_ref, wr_ref, br_ref, id_ref, w_ref):
    xn = _rms(x_ref[...], g_ref[...])
    xh = xn.astype(BF16)
    xl = (xn - xh.astype(F32)).astype(BF16)
    w = wr_ref[...]
    wh = w.astype(BF16)
    wl = (w - wh.astype(F32)).astype(BF16)
    lg = (jnp.dot(xh, wh, preferred_element_type=F32) + jnp.dot(xh, wl, preferred_element_type=F32)
          + jnp.dot(xl, wh, preferred_element_type=F32)) + br_ref[...]
    lane = lax.broadcasted_iota(jnp.int32, lg.shape, 1).astype(F32)
    neg = jnp.float32(-jnp.inf)
    big = jnp.float32(LANES)
    gl = jnp.where(lane < MOE_GROUPS, lg, neg)
    gmax = jnp.max(gl, axis=-1, keepdims=True)
    gsel = jnp.min(jnp.where(gl == gmax, lane, big), axis=-1, keepdims=True)
    gw = 1.0 / jnp.sum(jnp.exp(gl - gmax), axis=-1, keepdims=True)
    lo = MOE_GROUPS + gsel * MOE_PER_GROUP
    el = jnp.where((lane >= lo) & (lane < lo + MOE_PER_GROUP), lg, neg)
    m1 = jnp.max(el, axis=-1, keepdims=True)
    i1 = jnp.min(jnp.where(el == m1, lane, big), axis=-1, keepdims=True)
    el2 = jnp.where(lane == i1, neg, el)
    m2 = jnp.max(el2, axis=-1, keepdims=True)
    i2 = jnp.min(jnp.where(el2 == m2, lane, big), axis=-1, keepdims=True)
    z = jnp.sum(jnp.exp(el - m1), axis=-1, keepdims=True)
    p1 = 1.0 / z
    p2 = jnp.exp(m2 - m1) / z
    tot = p1 + p2
    w1 = p1 / tot * gw
    w2 = p2 / tot * gw
    ids = jnp.where(lane == 0, i1 - MOE_GROUPS, jnp.where(lane == 1, i2 - MOE_GROUPS, 0.0))
    id_ref[...] = ids.astype(jnp.int32)
    w_ref[...] = jnp.where(lane == 0, w1, jnp.where(lane == 1, w2, 0.0))


def _router_call(x, g, w_grp, b_grp, w_exp, b_exp):
    t, d = x.shape
    pad = LANES - MOE_GROUPS - N_EXP
    wr = jnp.concatenate([w_grp, w_exp, jnp.zeros((d, pad), F32)], axis=1)
    br = jnp.concatenate([b_grp, b_exp, jnp.zeros((pad,), F32)]).reshape(1, LANES)
    row = lambda i: (i, 0)
    fixed = lambda i: (0, 0)
    return pl.pallas_call(
        _router_kernel,
        grid=(t // ROW_TILE,),
        in_specs=[pl.BlockSpec((ROW_TILE, d), row), pl.BlockSpec((1, d), fixed),
                  pl.BlockSpec((d, LANES), fixed), pl.BlockSpec((1, LANES), fixed)],
        out_specs=[pl.BlockSpec((ROW_TILE, LANES), row), pl.BlockSpec((ROW_TILE, LANES), row)],
        out_shape=[jax.ShapeDtypeStruct((t, LANES), jnp.int32), jax.ShapeDtypeStruct((t, LANES), F32)],
        compiler_params=_params(("arbitrary",), 40),
        name="moe_router",
    )(x, g.reshape(1, d), wr, br)


def _cumsum_rows(hot):
    t, n = hot.shape
    blk = LANES
    h3 = hot.reshape(t // blk, blk, n).astype(F32)
    tri = jnp.tril(jnp.ones((blk, blk), F32))
    inner = jnp.einsum("ij,bjn->bin", tri, h3).astype(jnp.int32)
    totals = inner[:, -1, :]
    offsets = jnp.cumsum(totals, axis=0) - totals
    return (inner + offsets[:, None, :]).reshape(t, n)


def _moe_plan(e0, e1):
    t = e0.shape[0]
    n_tiles = 2 * t // MOE_TILE + N_EXP
    experts = jnp.arange(N_EXP, dtype=jnp.int32)[None, :]
    hot0 = (e0[:, None] == experts).astype(jnp.int32)
    hot1 = (e1[:, None] == experts).astype(jnp.int32)
    csum0 = _cumsum_rows(hot0)
    csum1 = _cumsum_rows(hot1)
    counts0 = csum0[-1]
    counts = counts0 + csum1[-1]
    tiles_e = (counts + MOE_TILE - 1) // MOE_TILE
    tiles_end = jnp.cumsum(tiles_e)
    row_start = (tiles_end - tiles_e) * MOE_TILE
    pos0 = jnp.sum(hot0 * (row_start[None, :] + csum0 - 1), axis=1)
    pos1 = jnp.sum(hot1 * (row_start[None, :] + counts0[None, :] + csum1 - 1), axis=1)
    pos = jnp.concatenate([pos0, pos1])
    n_valid = tiles_end[-1]
    tile_id = jnp.minimum(jnp.arange(n_tiles, dtype=jnp.int32), n_valid - 1)
    tile_expert = jnp.sum((tile_id[:, None] >= tiles_end[None, :]).astype(jnp.int32), axis=1)
    tile_expert = jnp.minimum(tile_expert, N_EXP - 1)
    changed = jnp.concatenate([jnp.ones((1,), bool), tile_expert[1:] != tile_expert[:-1]])
    first = jnp.logical_and(changed, jnp.arange(n_tiles) < n_valid).astype(jnp.int32)
    wslot = (jnp.cumsum(first) - 1) % 2
    next_start = tiles_end[tile_expert]
    next_expert = jnp.where(next_start < n_valid, tile_expert[jnp.minimum(next_start, n_tiles - 1)], -1)
    tile_meta = jnp.stack([tile_expert, first, wslot, next_expert]).astype(jnp.int32)
    pad_meta = jnp.stack([row_start + counts, tiles_e * MOE_TILE - counts]).astype(jnp.int32)
    return tile_meta, n_valid.reshape(1).astype(jnp.int32), pad_meta, pos.astype(jnp.int32)


def _dispatch_kernel(pos_ref, nv_ref, pad_ref, x_ref, g_ref, xs_hbm, xnbuf, zeros, sems, zsem):
    i = pl.program_id(0)
    n = pl.num_programs(0)
    tm = x_ref.shape[0]
    n_tiles = xs_hbm.shape[0] // MOE_TILE
    n_tok = n * tm
    slot = lax.rem(i, 2)

    def wait_rows(s):
        for kk in range(2):
            pltpu.make_async_copy(xnbuf.at[s], xs_hbm.at[pl.ds(0, tm)], sems.at[s]).wait()

    pl.when(i >= 2)(lambda: wait_rows(slot))
    xnbuf[slot] = _rms(x_ref[...], g_ref[...])
    for r in range(tm):
        for kk in range(2):
            pltpu.make_async_copy(xnbuf.at[slot, pl.ds(r, 1)],
                                  xs_hbm.at[pl.ds(pos_ref[kk * n_tok + i * tm + r], 1)],
                                  sems.at[slot]).start(priority=kk)

    def zero_rows(start, n):
        return pltpu.make_async_copy(zeros.at[pl.ds(0, n)], xs_hbm.at[pl.ds(start, n)], zsem)

    block_sizes = [1 << b for b in range(MOE_TILE.bit_length() - 2, -1, -1)]

    def fill_expert(e, carry):
        start, length = pad_ref[0, e], pad_ref[1, e]
        head = jnp.bitwise_and(-start, SUBLANES - 1)
        lax.fori_loop(0, head, lambda j, c: (zero_rows(start + j, 1).start(), c)[1], 0)
        rest = length - head
        off = start + head
        for sz in block_sizes:
            if sz < SUBLANES:
                break
            take = jnp.bitwise_and(rest, sz) != 0
            pl.when(take)(lambda off=off, sz=sz: zero_rows(pl.multiple_of(off, SUBLANES), sz).start())
            off = off + jnp.where(take, sz, 0)
        return carry

    def drain_expert(e, carry):
        length = pad_ref[1, e]
        for sz in block_sizes:
            pl.when(jnp.bitwise_and(length, sz) != 0)(lambda sz=sz: zero_rows(0, sz).wait())
        return carry

    def tail_tile_copy(tile):
        return pltpu.make_async_copy(zeros, xs_hbm.at[pl.ds(tile * MOE_TILE, MOE_TILE)], zsem)

    @pl.when(i == 0)
    def _():
        zeros[...] = jnp.zeros_like(zeros)
        lax.fori_loop(0, N_EXP, fill_expert, 0)
        lax.fori_loop(nv_ref[0], n_tiles, lambda tile, c: (tail_tile_copy(tile).start(), c)[1], 0)

    @pl.when(i == n - 1)
    def _():
        pl.when(n >= 2)(lambda: wait_rows(1 - slot))
        wait_rows(slot)
        lax.fori_loop(0, N_EXP, drain_expert, 0)
        lax.fori_loop(nv_ref[0], n_tiles, lambda tile, c: (tail_tile_copy(tile).wait(), c)[1], 0)


def _dispatch_call(x, g, pos, n_valid, pad_meta, rows):
    t, d = x.shape
    grid_spec = pltpu.PrefetchScalarGridSpec(
        num_scalar_prefetch=3,
        grid=(t // ROW_TILE,),
        in_specs=[pl.BlockSpec((ROW_TILE, d), lambda i, pos_ref, nv, pad: (i, 0)),
                  pl.BlockSpec((1, d), lambda i, pos_ref, nv, pad: (0, 0))],
        out_specs=pl.BlockSpec(memory_space=pl.ANY),
        scratch_shapes=[pltpu.VMEM((2, ROW_TILE, d), F32), pltpu.VMEM((MOE_TILE, d), F32),
                        pltpu.SemaphoreType.DMA((2,)), pltpu.SemaphoreType.DMA(())],
    )
    return pl.pallas_call(
        _dispatch_kernel,
        grid_spec=grid_spec,
        out_shape=jax.ShapeDtypeStruct((rows, d), F32),
        compiler_params=_params(("arbitrary",), 32),
        name="moe_dispatch",
    )(pos, n_valid, pad_meta, x, g.reshape(1, d))


def _moe_ffn_kernel(tm_ref, nv_ref, x_ref, wgu_hbm, wd_hbm, o_ref,
                    wgu_f, wd_f, wsems, wgu_b, wd_b, *, layer):
    i = pl.program_id(0)
    nv = nv_ref[0]
    expert, first, wslot, next_expert = tm_ref[0, i], tm_ref[1, i], tm_ref[2, i], tm_ref[3, i]

    def weight_copies(e, s):
        return (pltpu.make_async_copy(wgu_hbm.at[layer, e], wgu_f.at[s], wsems.at[0, s]),
                pltpu.make_async_copy(wd_hbm.at[layer, e], wd_f.at[s], wsems.at[1, s]))

    @pl.when(i == 0)
    def _():
        for c in weight_copies(expert, 0):
            c.start()

    @pl.when(first == 1)
    def _():
        for c in weight_copies(expert, wslot):
            c.wait()
        wgu_b[...] = wgu_f[wslot].astype(BF16)
        wd_b[...] = wd_f[wslot].astype(BF16)

        @pl.when(next_expert >= 0)
        def _():
            for c in weight_copies(next_expert, 1 - wslot):
                c.start()

    @pl.when(i < nv)
    def _():
        h = jnp.dot(x_ref[...].astype(BF16), wgu_b[...], preferred_element_type=F32)
        hg = h[:, :D_FF_E]
        hu = h[:, D_FF_E:]
        act = (hg * _sigmoid(hg) * hu).astype(BF16)
        o_ref[...] = jnp.dot(act, wd_b[...], preferred_element_type=F32)

    @pl.when(i >= nv)
    def _():
        o_ref[...] = jnp.zeros_like(o_ref)


def _moe_ffn_call(xs, w_gu, w_down, layer, tile_meta, n_valid):
    rows, d = xs.shape
    n_tiles = tile_meta.shape[1]
    grid_spec = pltpu.PrefetchScalarGridSpec(
        num_scalar_prefetch=2,
        grid=(n_tiles,),
        in_specs=[pl.BlockSpec((MOE_TILE, d), lambda i, tm, nv: (jnp.minimum(i, nv[0] - 1), 0)),
                  pl.BlockSpec(memory_space=pl.ANY), pl.BlockSpec(memory_space=pl.ANY)],
        out_specs=pl.BlockSpec((MOE_TILE, d), lambda i, tm, nv: (i, 0)),
        scratch_shapes=[
            pltpu.VMEM((2, d, 2 * D_FF_E), F32),
            pltpu.VMEM((2, D_FF_E, d), F32),
            pltpu.SemaphoreType.DMA((2, 2)),
            pltpu.VMEM((d, 2 * D_FF_E), BF16),
            pltpu.VMEM((D_FF_E, d), BF16),
        ],
    )
    return pl.pallas_call(
        functools.partial(_moe_ffn_kernel, layer=layer),
        grid_spec=grid_spec,
        out_shape=jax.ShapeDtypeStruct((rows, d), F32),
        compiler_params=_params(("arbitrary",), 48),
        name="moe_experts",
    )(tile_meta, n_valid, xs, w_gu, w_down)


def _combine_kernel(pos_ref, x_ref, w_ref, y_hbm, g_ref, *rest, emit_x, norm_tiles):
    outs = rest[:int(emit_x) + len(norm_tiles)]
    buf, sems = rest[int(emit_x) + len(norm_tiles):]
    i = pl.program_id(0)
    n = pl.num_programs(0)
    slot = lax.rem(i, 2)
    tm = x_ref.shape[0]

    def gather(tile, dst_slot):
        n_tok = n * tm
        for r in range(tm):
            for kk in range(2):
                pltpu.make_async_copy(y_hbm.at[pl.ds(pos_ref[kk * n_tok + tile * tm + r], 1)],
                                      buf.at[dst_slot, kk, pl.ds(r, 1)], sems.at[dst_slot]).start(priority=kk)

    @pl.when(i == 0)
    def _():
        gather(0, 0)

    for s in range(2):
        pl.when(jnp.logical_and(i + 1 < n, slot == s))(functools.partial(gather, i + 1, 1 - s))

    for kk in range(2):
        pltpu.make_async_copy(y_hbm.at[pl.ds(0, tm)], buf.at[slot, kk], sems.at[slot]).wait()
    xo = x_ref[...] + (w_ref[:, 0:1] * buf[slot, 0] + w_ref[:, 1:2] * buf[slot, 1])
    norm_refs = outs[int(emit_x):]
    if emit_x:
        outs[0][...] = xo

    def write_norm(p):
        norm_refs[p][...] = _rms(xo, g_ref[...]).astype(norm_refs[p].dtype)

    _for_each_part(i, norm_tiles, write_norm)


def _combine_call(x, ws, y_rows, pos, g, emit_x, norm_dtype, norm_rows):
    t, d = x.shape
    tm = MOE_TILE
    row = lambda i, pos_ref: (i, 0)
    out_specs, out_shape = [], []
    if emit_x:
        out_specs.append(pl.BlockSpec((tm, d), row))
        out_shape.append(jax.ShapeDtypeStruct((t, d), F32))
    norm_parts = [jax.ShapeDtypeStruct((r, d), norm_dtype) for r in norm_rows]
    out_specs += _part_specs(norm_parts, tm, d, row)
    out_shape += norm_parts
    grid_spec = pltpu.PrefetchScalarGridSpec(
        num_scalar_prefetch=1,
        grid=(t // tm,),
        in_specs=[pl.BlockSpec((tm, d), row), pl.BlockSpec((tm, LANES), row), pl.BlockSpec(memory_space=pl.ANY),
                  pl.BlockSpec((1, d), lambda i, pos_ref: (0, 0))],
        out_specs=out_specs,
        scratch_shapes=[pltpu.VMEM((2, 2, tm, d), F32), pltpu.SemaphoreType.DMA((2,))],
    )
    return pl.pallas_call(
        functools.partial(_combine_kernel, emit_x=emit_x, norm_tiles=_part_tiles(norm_parts, tm)),
        grid_spec=grid_spec,
        out_shape=out_shape,
        compiler_params=_params(("arbitrary",), 40),
        name="moe_combine",
    )(pos, x, ws, y_rows, g.reshape(1, d))


def _moe_layer(x, g_ffn, w_grp, b_grp, w_exp, b_exp, w_gu, w_down, layer, g_next, emit_x, norm_dtype, norm_rows):
    ids, ws = _router_call(x, g_ffn, w_grp, b_grp, w_exp, b_exp)
    tile_meta, n_valid, pad_meta, pos = _moe_plan(ids[:, 0], ids[:, 1])
    xs = _dispatch_call(x, g_ffn, pos, n_valid, pad_meta, tile_meta.shape[1] * MOE_TILE)
    y_rows = _moe_ffn_call(xs, w_gu, w_down, layer, tile_meta, n_valid)
    return _combine_call(x, ws, y_rows, pos, g_next, emit_x, norm_dtype, norm_rows)


def _s5_prep_kernel(are_ref, aim_ref, ldt_ref, bre_ref, bim_ref, lre_ref, lim_ref, bbr_ref, bbi_ref):
    are = are_ref[...]
    aim = aim_ref[...]
    dt = jnp.exp(ldt_ref[...])
    mag = jnp.exp(are * dt)
    lr = mag * jnp.cos(aim * dt)
    li = mag * jnp.sin(aim * dt)
    lre_ref[...] = lr
    lim_ref[...] = li
    den = are * are + aim * aim
    nr = lr - 1.0
    cr = (nr * are + li * aim) / den
    ci = (li * are - nr * aim) / den
    bre = bre_ref[...]
    bim = bim_ref[...]
    bbr_ref[...] = cr * bre - ci * bim
    bbi_ref[...] = cr * bim + ci * bre


def _s5_prep_call(a_re, a_im, log_dt, b_re, b_im):
    g, p, c = b_re.shape
    shp = (g, p * c)
    expand = lambda a: jnp.broadcast_to(a[:, :, None], (g, p, c)).reshape(shp)
    ldt = jnp.broadcast_to(log_dt[:, None], shp)
    outs = pl.pallas_call(
        _s5_prep_kernel,
        out_shape=[jax.ShapeDtypeStruct(shp, F32)] * 4,
        name="s5_discretize",
    )(expand(a_re), expand(a_im), ldt, b_re.reshape(shp), b_im.reshape(shp))
    lre, lim, bbr, bbi = outs
    return lre[:, ::c], lim[:, ::c], bbr, bbi


def _s5_compact(bbr, bbi, c_re, c_im):
    def b_part(bb):
        return bb.reshape(S5_FB, 16, S5_P, S5_GC).transpose(0, 1, 3, 2).reshape(S5_FB, 16 * S5_GC, S5_P)

    def c_part(cc):
        return cc.reshape(S5_FB, 16, S5_GC, S5_P).transpose(0, 3, 1, 2).reshape(S5_FB, S5_P, 16 * S5_GC)

    return b_part(bbr), b_part(bbi), c_part(c_re), c_part(c_im)


def _s5_build_mats(bbr_ref, bbi_ref, ccr_ref, cci_ref, bd_ref, cd_ref):
    nch, nst = 16 * S5_GC, 16 * S5_P
    iota = lambda shape, dim: lax.broadcasted_iota(jnp.int32, shape, dim)
    rep = ((iota((S5_P, nst), 1) & (S5_P - 1)) == iota((S5_P, nst), 0)).astype(BF16)
    rep_t = ((iota((nst, S5_P), 0) & (S5_P - 1)) == iota((nst, S5_P), 1)).astype(BF16)
    gc_bits, p_bits = S5_GC.bit_length() - 1, S5_P.bit_length() - 1
    diag = (iota((nch, nst), 0) >> gc_bits) == (iota((nch, nst), 1) >> p_bits)
    diag_t = (iota((nst, nch), 0) >> p_bits) == (iota((nst, nch), 1) >> gc_bits)
    for fb in range(S5_FB):
        for part, src in enumerate((bbr_ref, bbi_ref)):
            full = jnp.dot(src[fb].astype(BF16), rep, preferred_element_type=F32)
            bd_ref[fb, :, part * nst:(part + 1) * nst] = jnp.where(diag, full, 0.0).astype(BF16)
        for part, (src, sign) in enumerate(((ccr_ref, 1.0), (cci_ref, -1.0))):
            full = jnp.dot(rep_t, src[fb].astype(BF16), preferred_element_type=F32)
            cd_ref[fb, part * nst:(part + 1) * nst, :] = jnp.where(diag_t, sign * full, 0.0).astype(BF16)


def _s5_kernel(u_ref, bbr_ref, bbi_ref, ccr_ref, cci_ref, lre_ref, lim_ref, d_ref, h0r_ref, h0i_ref,
               z_ref, hr_ref, hi_ref, scr, hst, bd_ref, cd_ref, *, seg, carry):
    rows = u_ref.shape[0]
    n_seq = rows // seg
    ch = 16 * S5_GC
    half = SUBLANES * S5_PITCH
    blk = 2 * half

    @pl.when(pl.program_id(0) == 0)
    def _():
        _s5_build_mats(bbr_ref, bbi_ref, ccr_ref, cci_ref, bd_ref, cd_ref)

    for fb in range(S5_FB):
        bu = jnp.dot(u_ref[:, fb * ch:(fb + 1) * ch].astype(BF16), bd_ref[fb], preferred_element_type=F32)
        for c in range(2 * SUBLANES):
            r0 = fb * blk + c * S5_PITCH
            scr[r0:r0 + rows, :] = bu[:, c * LANES:(c + 1) * LANES]

    lre = [lre_ref[fb * SUBLANES:(fb + 1) * SUBLANES, :] for fb in range(S5_FB)]
    lim = [lim_ref[fb * SUBLANES:(fb + 1) * SUBLANES, :] for fb in range(S5_FB)]

    def scan(row0, state):
        def step(l, st):
            new = []
            for fb in range(S5_FB):
                hr, hi = st[2 * fb], st[2 * fb + 1]
                ire = pl.ds(fb * blk + row0 + l, SUBLANES, stride=S5_PITCH)
                iim = pl.ds(fb * blk + half + row0 + l, SUBLANES, stride=S5_PITCH)
                nr = lre[fb] * hr - lim[fb] * hi + scr[ire, :]
                ni = lre[fb] * hi + lim[fb] * hr + scr[iim, :]
                scr[ire, :] = nr
                scr[iim, :] = ni
                new += [nr, ni]
            return tuple(new)
        return lax.fori_loop(0, seg, step, state, unroll=8)

    def load_state(rref, iref, lead):
        st = []
        for fb in range(S5_FB):
            st += [rref[lead + (slice(fb * SUBLANES, (fb + 1) * SUBLANES), slice(None))],
                   iref[lead + (slice(fb * SUBLANES, (fb + 1) * SUBLANES), slice(None))]]
        return tuple(st)

    def store_state(st, rref, iref, lead):
        for fb in range(S5_FB):
            rref[lead + (slice(fb * SUBLANES, (fb + 1) * SUBLANES), slice(None))] = st[2 * fb]
            iref[lead + (slice(fb * SUBLANES, (fb + 1) * SUBLANES), slice(None))] = st[2 * fb + 1]

    if carry:
        @pl.when(pl.program_id(0) == 0)
        def _():
            hst[0] = h0r_ref[0]
            hst[1] = h0i_ref[0]
        hre_scr = hst.at[0]
        him_scr = hst.at[1]
        st = scan(0, load_state(hre_scr, him_scr, ()))
        store_state(st, hre_scr, him_scr, ())
        store_state(st, hr_ref, hi_ref, (0,))
    else:
        for s in range(n_seq):
            st = scan(s * seg, load_state(h0r_ref, h0i_ref, (s,)))
            store_state(st, hr_ref, hi_ref, (s,))

    for fb in range(S5_FB):
        hf = jnp.concatenate(
            [scr[fb * blk + c * S5_PITCH: fb * blk + c * S5_PITCH + rows, :] for c in range(2 * SUBLANES)],
            axis=1).astype(BF16)
        y = jnp.dot(hf, cd_ref[fb], preferred_element_type=F32)
        cs = slice(fb * ch, (fb + 1) * ch)
        yy = y + d_ref[:, cs] * u_ref[:, cs]
        z_ref[:, cs] = jax.nn.gelu(yy).astype(z_ref.dtype)


def _s5_call(u, mats, lre, lim, d_skip, h0r, h0i, row_block0, n_blocks, seg, carry):
    t, d = u.shape
    ch, nst = 16 * S5_GC, 16 * S5_P
    n_seq = S5_ROWS // seg
    ns_total = h0r.shape[0]
    st_idx = (lambda i: (0, 0, 0)) if carry else (lambda i: (i, 0, 0))
    fixed2 = lambda i: (0, 0)
    fixed3 = lambda i: (0, 0, 0)
    st_rows = S5_FB * SUBLANES
    return pl.pallas_call(
        functools.partial(_s5_kernel, seg=seg, carry=carry),
        grid=(n_blocks,),
        in_specs=[
            pl.BlockSpec((S5_ROWS, d), lambda i: (row_block0 + i, 0)),
            *[pl.BlockSpec(m.shape, fixed3) for m in mats],
            pl.BlockSpec(lre.shape, fixed2),
            pl.BlockSpec(lim.shape, fixed2),
            pl.BlockSpec((1, d), fixed2),
            pl.BlockSpec((n_seq, st_rows, LANES), st_idx),
            pl.BlockSpec((n_seq, st_rows, LANES), st_idx),
        ],
        out_specs=[
            pl.BlockSpec((S5_ROWS, d), lambda i: (i, 0)),
            pl.BlockSpec((n_seq, st_rows, LANES), st_idx),
            pl.BlockSpec((n_seq, st_rows, LANES), st_idx),
        ],
        out_shape=[
            jax.ShapeDtypeStruct((n_blocks * S5_ROWS, d), BF16),
            jax.ShapeDtypeStruct((ns_total, st_rows, LANES), F32),
            jax.ShapeDtypeStruct((ns_total, st_rows, LANES), F32),
        ],
        scratch_shapes=[
            pltpu.VMEM((2 * SUBLANES * S5_FB * S5_PITCH, LANES), F32),
            pltpu.VMEM((2, st_rows, LANES), F32),
            pltpu.VMEM((S5_FB, ch, 2 * nst), BF16),
            pltpu.VMEM((S5_FB, 2 * nst, ch), BF16),
        ],
        compiler_params=_params(("arbitrary",), 56),
        name="s5_scan",
    )(u, *mats, lre, lim, d_skip.reshape(1, d), h0r, h0i)


def kernel(x_prompt, x_sample, state_ret, state_s5_re, state_s5_im, norm_mix_g, norm_ffn_g, norm_final_g,
           ret_w_in, ret_gn_g, ret_w_out, s5_w_in, s5_a_re, s5_a_im, s5_log_dt, s5_b_re, s5_b_im,
           s5_c_re, s5_c_im, s5_d, s5_w_out, moe_w_grp, moe_b_grp, moe_w_exp, moe_b_exp, moe_w_gu, moe_w_down):
    bp, seq, d = x_prompt.shape
    db, dseq, _ = x_sample.shape
    assert d == D_MODEL and bp == 1
    assert seq % max(ROW_TILE, RET_ROWS) == 0 and seq % CHUNK == 0
    assert dseq < CHUNK and dseq % 16 == 0 and S5_ROWS % dseq == 0
    tp = bp * seq
    ts = db * dseq
    t = tp + ts
    assert ts % ROW_TILE == 0 and ts % S5_ROWS == 0

    x_parts = [x_prompt.reshape(tp, d), x_sample.reshape(ts, d)]

    half = RET_DK // 2
    freqs = ROPE_BASE ** (-jnp.arange(half, dtype=F32) / half)
    pos = jnp.concatenate([jnp.arange(seq, dtype=F32), jnp.tile(PAST_LEN + jnp.arange(dseq, dtype=F32), db)])
    ang = pos[:, None] * freqs[None, :]
    cos_t = jnp.cos(ang)
    sin_t = jnp.sin(ang)

    xn = _norm_call(x_parts, norm_mix_g[0], BF16)
    tn = 1024
    first_col = lambda j: 0
    qk = _mm_call([xn], ret_w_in[0], [0], tn, 2 * D_MODEL, [([cos_t], half, first_col), ([sin_t], half, first_col)],
                  BF16, functools.partial(_epi_rope, n_q_blocks=D_MODEL // tn), 40, "ret_qk_proj")
    vg = _mm_call([xn], ret_w_in[0], [2 * D_MODEL // tn], tn, 2 * RET_VW, [], BF16, _epi_plain, 40, "ret_vg_proj")
    og_p, ret_p = _ret_prompt_call(qk, vg, ret_gn_g[0], tp)
    og_s, ret_s = _ret_sample_call(qk, vg, ret_gn_g[0], state_ret[0], tp, dseq)
    tn = 512
    same_col = lambda j: j
    x1 = _mm_call([og_p, og_s], ret_w_out[0], [0], tn, d, [(x_parts, tn, same_col)], F32, _epi_residual,
                  40, "ret_out_proj")

    x2, xn = _moe_layer(x1, norm_ffn_g[0], moe_w_grp[0], moe_b_grp[0], moe_w_exp[0], moe_b_exp[0],
                        moe_w_gu, moe_w_down, 0, norm_mix_g[1], True, BF16, (t,))

    u = _mm_call([xn], s5_w_in[0], [0], 1024, d, [], F32, _epi_plain, 40, "s5_in_proj")
    lre, lim, bbr, bbi = _s5_prep_call(s5_a_re[0], s5_a_im[0], s5_log_dt[0], s5_b_re[0], s5_b_im[0])
    mats = _s5_compact(bbr, bbi, s5_c_re[0], s5_c_im[0])
    st_shape = (S5_FB * SUBLANES, LANES)
    lre = lre.reshape(st_shape)
    lim = lim.reshape(st_shape)
    zero_state = jnp.zeros((1,) + st_shape, F32)
    z_p, hpr, hpi = _s5_call(u, mats, lre, lim, s5_d[0], zero_state, zero_state,
                             0, tp // S5_ROWS, S5_ROWS, True)
    z_s, hsr, hsi = _s5_call(u, mats, lre, lim, s5_d[0],
                             state_s5_re[0].reshape((db,) + st_shape), state_s5_im[0].reshape((db,) + st_shape),
                             tp // S5_ROWS, ts // S5_ROWS, dseq, False)
    x3 = _mm_call([z_p, z_s], s5_w_out[0], [0, d // tn], tn, d, [([x2], tn, same_col)], F32, _epi_glu_residual,
                  40, "s5_out_proj")

    y_p, y_s = _moe_layer(x3, norm_ffn_g[1], moe_w_grp[1], moe_b_grp[1], moe_w_exp[1], moe_b_exp[1],
                          moe_w_gu, moe_w_down, 1, norm_final_g, False, F32, (tp, ts))

    s5_shape = (1, -1, S5_GROUPS, S5_P)
    return (y_p.reshape(bp, seq, d), y_s.reshape(db, dseq, d),
            ret_p.reshape(1, bp, RET_HEADS, RET_DK, RET_DV), ret_s.reshape((1,) + ret_s.shape),
            hpr.reshape(s5_shape), hpi.reshape(s5_shape), hsr.reshape(s5_shape), hsi.reshape(s5_shape))
```

```python
import functools
import math

import numpy as np
import jax
import jax.numpy as jnp
from jax import lax
from jax.experimental import pallas as pl
from jax.experimental.pallas import tpu as pltpu

F32 = jnp.float32
BF16 = jnp.bfloat16

D_MODEL = 2048
PAST_LEN = 2048
CHUNK = 64
RET_HEADS = 8
RET_DK = D_MODEL // RET_HEADS
RET_DV = 2 * RET_DK
RET_VW = RET_HEADS * RET_DV
ROPE_BASE = 10000.0
S5_GC = 16
S5_GROUPS = D_MODEL // S5_GC
S5_P = 64
MOE_GROUPS = 4
MOE_PER_GROUP = 8
N_EXP = MOE_GROUPS * MOE_PER_GROUP
D_FF_E = 512
EPS = 1e-6

LANES = 128
SUBLANES = 8
ROW_TILE = 512
MM_CHUNK = 256
RET_ROWS = 256
RET_HPS = 4
MOE_TILE = 256
S5_ROWS = 256
S5_FB = S5_GROUPS // 16
S5_PITCH = S5_ROWS + 4
MIB = 1024 * 1024


def _params(sem, vmem_mib):
    return pltpu.CompilerParams(dimension_semantics=sem, vmem_limit_bytes=vmem_mib * MIB)


def _rms(x, g):
    ms = jnp.mean(x * x, axis=-1, keepdims=True)
    return x * lax.rsqrt(ms + EPS) * g


def _sigmoid(x):
    return 1.0 / (1.0 + jnp.exp(-x))


def _part_specs(parts, rows, cols, rowcol):
    specs = []
    start = 0
    for p in parts:
        nt = p.shape[0] // rows

        def imap(*g, start=start, nt=nt):
            i, jc = rowcol(*g)
            return (jnp.clip(i - start, 0, nt - 1), jc)

        specs.append(pl.BlockSpec((rows, cols), imap))
        start += nt
    return specs


def _part_tiles(parts, rows):
    return tuple(p.shape[0] // rows for p in parts)


def _for_each_part(i, tiles, fn):
    if len(tiles) == 1:
        fn(0)
        return
    start = 0
    for p, nt in enumerate(tiles):
        pl.when(jnp.logical_and(i >= start, i < start + nt))(functools.partial(fn, p))
        start += nt


def _pick(refs, p):
    return refs[min(p, len(refs) - 1)]


def _norm_kernel(*refs, tiles):
    n = len(tiles)
    x_refs, g_ref, o_ref = refs[:n], refs[n], refs[n + 1]

    def run(p):
        o_ref[...] = _rms(x_refs[p][...], g_ref[...]).astype(o_ref.dtype)

    _for_each_part(pl.program_id(0), tiles, run)


def _norm_call(x_parts, g, out_dtype):
    d = x_parts[0].shape[1]
    t = sum(p.shape[0] for p in x_parts)
    return pl.pallas_call(
        functools.partial(_norm_kernel, tiles=_part_tiles(x_parts, ROW_TILE)),
        grid=(t // ROW_TILE,),
        in_specs=_part_specs(x_parts, ROW_TILE, d, lambda i: (i, 0)) + [pl.BlockSpec((1, d), lambda i: (0, 0))],
        out_specs=pl.BlockSpec((ROW_TILE, d), lambda i: (i, 0)),
        out_shape=jax.ShapeDtypeStruct((t, d), out_dtype),
        compiler_params=_params(("arbitrary",), 32),
        name="rmsnorm",
    )(*x_parts, g.reshape(1, d))


def _mm_kernel(*refs, tiles, n_x, n_w, n_extra, epi):
    x_refs = refs[:n_x]
    w_refs = refs[n_x:n_x + n_w]
    pos = n_x + n_w
    extra = []
    for n in n_extra:
        extra.append(refs[pos:pos + n])
        pos += n
    o_ref = refs[pos]
    wb = refs[pos + 1:]

    @pl.when(pl.program_id(1) == 0)
    def _():
        for w_ref, b in zip(w_refs, wb):
            b[...] = w_ref[...].astype(BF16)

    def run(p):
        x = _pick(x_refs, p)[...]
        for c in range(o_ref.shape[1] // MM_CHUNK):
            cs = slice(c * MM_CHUNK, (c + 1) * MM_CHUNK)
            accs = [jnp.dot(x, b[:, cs], preferred_element_type=F32) for b in wb]
            epi(accs, [_pick(e, p) for e in extra], o_ref, cs)

    _for_each_part(pl.program_id(1), tiles, run)


def _mm_call(x_parts, w, w_col_offsets, tn, n_out, extras, out_dtype, epi, vmem_mib, name):
    k = x_parts[0].shape[1]
    t = sum(p.shape[0] for p in x_parts)
    n_w = len(w_col_offsets)
    split = [parts for parts in [x_parts] + [e[0] for e in extras] if len(parts) > 1]
    tiles = _part_tiles(split[0], ROW_TILE) if split else (t // ROW_TILE,)
    assert all(_part_tiles(parts, ROW_TILE) == tiles for parts in split)
    in_specs = _part_specs(x_parts, ROW_TILE, k, lambda j, i: (i, 0))
    for off in w_col_offsets:
        in_specs.append(pl.BlockSpec((k, tn), lambda j, i, off=off: (0, j + off)))
    extra_arrays = []
    for parts, cols, colfn in extras:
        in_specs += _part_specs(parts, ROW_TILE, cols, lambda j, i, colfn=colfn: (i, colfn(j)))
        extra_arrays += list(parts)
    return pl.pallas_call(
        functools.partial(_mm_kernel, tiles=tiles, n_x=len(x_parts), n_w=n_w,
                          n_extra=tuple(len(e[0]) for e in extras), epi=epi),
        grid=(n_out // tn, t // ROW_TILE),
        in_specs=in_specs,
        out_specs=pl.BlockSpec((ROW_TILE, tn), lambda j, i: (i, j)),
        out_shape=jax.ShapeDtypeStruct((t, n_out), out_dtype),
        scratch_shapes=[pltpu.VMEM((k, tn), BF16) for _ in range(n_w)],
        compiler_params=_params(("arbitrary", "arbitrary"), vmem_mib),
        name=name,
    )(*x_parts, *([w] * n_w), *extra_arrays)


def _epi_plain(accs, extra, o_ref, cs):
    o_ref[:, cs] = accs[0].astype(o_ref.dtype)


def _epi_residual(accs, extra, o_ref, cs):
    o_ref[:, cs] = (extra[0][:, cs] + accs[0]).astype(o_ref.dtype)


def _epi_glu_residual(accs, extra, o_ref, cs):
    a, gt = accs
    o_ref[:, cs] = (extra[0][:, cs] + a * _sigmoid(gt)).astype(o_ref.dtype)


def _epi_rope(accs, extra, o_ref, cs, *, n_q_blocks):
    acc = accs[0]
    assert acc.shape[1] == RET_DK
    cos_ref, sin_ref = extra
    half = RET_DK // 2
    scale = jnp.where(pl.program_id(0) >= n_q_blocks, RET_DK ** -0.5, 1.0).astype(F32)
    c = cos_ref[...]
    s = sin_ref[...]
    t1 = acc[:, :half]
    t2 = acc[:, half:]
    o_ref[:, cs.start:cs.start + half] = ((t1 * c - t2 * s) * scale).astype(o_ref.dtype)
    o_ref[:, cs.start + half:cs.stop] = ((t1 * s + t2 * c) * scale).astype(o_ref.dtype)


def _ret_tables(rows, chunk):
    lg = jnp.log(jnp.asarray(1.0 - 2.0 ** (-5.0 - np.arange(RET_HEADS)), dtype=F32))[:, None, None]
    n = jnp.arange(rows, dtype=F32)
    ci = np.arange(rows) // chunk
    same_or_earlier = jnp.asarray(ci[None, :] <= ci[:, None])
    dmat = jnp.where(same_or_earlier[None], jnp.exp(jnp.abs(n[:, None] - n[None, :])[None] * lg), 0.0)
    dq = jnp.exp((n + 1.0)[None, :, None] * lg) * jnp.ones((1, 1, LANES), F32)
    dk = jnp.exp((rows - 1.0 - n)[None, :, None] * lg) * jnp.ones((1, 1, LANES), F32)
    gr = jnp.exp(rows * lg) * jnp.ones((1, 1, RET_DV), F32)
    return dmat, dq, dk, gr


def _ret_block(q, k, v, g, s_prev, dmat, dq, dk, gr, gn):
    s = lax.dot_general(q, k, (((1,), (1,)), ((), ())), preferred_element_type=F32) * dmat
    inter = jnp.dot(q, s_prev.astype(BF16), preferred_element_type=F32)
    o = jnp.dot(s.astype(BF16), v, preferred_element_type=F32) + jnp.concatenate([dq] * (RET_DV // LANES), axis=1) * inter
    kd = (k.astype(F32) * jnp.concatenate([dk] * (RET_DK // LANES), axis=1)).astype(BF16)
    s_new = s_prev * gr + lax.dot_general(kd, v, (((0,), (0,)), ((), ())), preferred_element_type=F32)
    mu = jnp.mean(o, axis=-1, keepdims=True)
    oc = o - mu
    var = jnp.mean(oc * oc, axis=-1, keepdims=True)
    on = oc * lax.rsqrt(var + EPS) * gn
    gf = g.astype(F32)
    return (gf * _sigmoid(gf) * on).astype(BF16), s_new


def _ret_prompt_kernel(q_ref, k_ref, v_ref, g_ref, dm_ref, dq_ref, dk_ref, gr_ref, gn_ref,
                       og_ref, st_ref, s_scr):
    @pl.when(pl.program_id(1) == 0)
    def _():
        s_scr[...] = jnp.zeros_like(s_scr)

    for hh in range(RET_HPS):
        ks = slice(hh * RET_DK, (hh + 1) * RET_DK)
        vs = slice(hh * RET_DV, (hh + 1) * RET_DV)
        og, s_new = _ret_block(q_ref[:, ks], k_ref[:, ks], v_ref[:, vs], g_ref[:, vs], s_scr[hh],
                               dm_ref[hh], dq_ref[hh], dk_ref[hh], gr_ref[hh], gn_ref[:, vs])
        og_ref[:, vs] = og
        s_scr[hh] = s_new
        st_ref[hh] = s_new


def _ret_prompt_call(qk, vg, gn, t_prompt):
    dm, dq, dk, gr = _ret_tables(RET_ROWS, CHUNK)
    hps = RET_HPS
    hb = RET_HEADS // hps
    return pl.pallas_call(
        _ret_prompt_kernel,
        grid=(hb, t_prompt // RET_ROWS),
        in_specs=[
            pl.BlockSpec((RET_ROWS, hps * RET_DK), lambda hh, c: (c, hh)),
            pl.BlockSpec((RET_ROWS, hps * RET_DK), lambda hh, c: (c, hb + hh)),
            pl.BlockSpec((RET_ROWS, hps * RET_DV), lambda hh, c: (c, hh)),
            pl.BlockSpec((RET_ROWS, hps * RET_DV), lambda hh, c: (c, hb + hh)),
            pl.BlockSpec((hps, RET_ROWS, RET_ROWS), lambda hh, c: (hh, 0, 0)),
            pl.BlockSpec((hps, RET_ROWS, LANES), lambda hh, c: (hh, 0, 0)),
            pl.BlockSpec((hps, RET_ROWS, LANES), lambda hh, c: (hh, 0, 0)),
            pl.BlockSpec((hps, 1, RET_DV), lambda hh, c: (hh, 0, 0)),
            pl.BlockSpec((1, hps * RET_DV), lambda hh, c: (0, hh)),
        ],
        out_specs=[
            pl.BlockSpec((RET_ROWS, hps * RET_DV), lambda hh, c: (c, hh)),
            pl.BlockSpec((hps, RET_DK, RET_DV), lambda hh, c: (hh, 0, 0)),
        ],
        out_shape=[
            jax.ShapeDtypeStruct((t_prompt, RET_VW), BF16),
            jax.ShapeDtypeStruct((RET_HEADS, RET_DK, RET_DV), F32),
        ],
        scratch_shapes=[pltpu.VMEM((hps, RET_DK, RET_DV), F32)],
        compiler_params=_params(("arbitrary", "arbitrary"), 40),
        name="retention_prompt",
    )(qk, qk, vg, vg, dm, dq, dk, gr, gn.reshape(1, RET_VW))


def _ret_sample_kernel(q_ref, k_ref, v_ref, g_ref, s0_ref, dm_ref, dq_ref, dk_ref, gr_ref, gn_ref,
                       og_ref, st_ref):
    for hh in range(RET_HEADS):
        ks = slice(hh * RET_DK, (hh + 1) * RET_DK)
        vs = slice(hh * RET_DV, (hh + 1) * RET_DV)
        og, s_new = _ret_block(q_ref[:, ks], k_ref[:, ks], v_ref[:, vs], g_ref[:, vs], s0_ref[0, hh],
                               dm_ref[hh], dq_ref[hh], dk_ref[hh], gr_ref[hh], gn_ref[:, vs])
        og_ref[:, vs] = og
        st_ref[0, hh] = s_new


def _ret_sample_call(qk, vg, gn, state, t_prompt, dec_seq):
    dm, dq, dk, gr = _ret_tables(dec_seq, dec_seq)
    b = state.shape[0]
    r0 = t_prompt // dec_seq
    const3 = lambda bb: (0, 0, 0)
    return pl.pallas_call(
        _ret_sample_kernel,
        grid=(b,),
        in_specs=[
            pl.BlockSpec((dec_seq, D_MODEL), lambda bb: (r0 + bb, 0)),
            pl.BlockSpec((dec_seq, D_MODEL), lambda bb: (r0 + bb, 1)),
            pl.BlockSpec((dec_seq, RET_VW), lambda bb: (r0 + bb, 0)),
            pl.BlockSpec((dec_seq, RET_VW), lambda bb: (r0 + bb, 1)),
            pl.BlockSpec((1, RET_HEADS, RET_DK, RET_DV), lambda bb: (bb, 0, 0, 0)),
            pl.BlockSpec(dm.shape, const3),
            pl.BlockSpec(dq.shape, const3),
            pl.BlockSpec(dk.shape, const3),
            pl.BlockSpec(gr.shape, const3),
            pl.BlockSpec((1, RET_VW), lambda bb: (0, 0)),
        ],
        out_specs=[
            pl.BlockSpec((dec_seq, RET_VW), lambda bb: (bb, 0)),
            pl.BlockSpec((1, RET_HEADS, RET_DK, RET_DV), lambda bb: (bb, 0, 0, 0)),
        ],
        out_shape=[
            jax.ShapeDtypeStruct((b * dec_seq, RET_VW), BF16),
            jax.ShapeDtypeStruct(state.shape, F32),
        ],
        compiler_params=_params(("arbitrary",), 40),
        name="retention_sample",
    )(qk, qk, vg, vg, state, dm, dq, dk, gr, gn.reshape(1, RET_VW))


def _router_kernel(x_ref, g_ref, wr_ref, br_ref, id_ref, w_ref):
    xn = _rms(x_ref[...], g_ref[...])
    xh = xn.astype(BF16)
    xl = (xn - xh.astype(F32)).astype(BF16)
    w = wr_ref[...]
    wh = w.astype(BF16)
    wl = (w - wh.astype(F32)).astype(BF16)
    hi = jnp.dot(xh, jnp.concatenate([wh, wl], axis=1), preferred_element_type=F32)
    lg = (hi[:, :LANES] + hi[:, LANES:] + jnp.dot(xl, wh, preferred_element_type=F32)) + br_ref[...]
    lane = lax.broadcasted_iota(jnp.int32, lg.shape, 1).astype(F32)
    neg = jnp.float32(-jnp.inf)
    big = jnp.float32(LANES)
    gl = jnp.where(lane < MOE_GROUPS, lg, neg)
    gmax = jnp.max(gl, axis=-1, keepdims=True)
    gsel = jnp.min(jnp.where(gl == gmax, lane, big), axis=-1, keepdims=True)
    gw = 1.0 / jnp.sum(jnp.exp(gl - gmax), axis=-1, keepdims=True)
    lo = MOE_GROUPS + gsel * MOE_PER_GROUP
    el = jnp.where((lane >= lo) & (lane < lo + MOE_PER_GROUP), lg, neg)
    m1 = jnp.max(el, axis=-1, keepdims=True)
    i1 = jnp.min(jnp.where(el == m1, lane, big), axis=-1, keepdims=True)
    el2 = jnp.where(lane == i1, neg, el)
    m2 = jnp.max(el2, axis=-1, keepdims=True)
    i2 = jnp.min(jnp.where(el2 == m2, lane, big), axis=-1, keepdims=True)
    z = jnp.sum(jnp.exp(el - m1), axis=-1, keepdims=True)
    p1 = 1.0 / z
    p2 = jnp.exp(m2 - m1) / z
    tot = p1 + p2
    w1 = p1 / tot * gw
    w2 = p2 / tot * gw
    ids = jnp.where(lane == 0, i1 - MOE_GROUPS, jnp.where(lane == 1, i2 - MOE_GROUPS, 0.0))
    id_ref[...] = ids.astype(jnp.int32)
    w_ref[...] = jnp.where(lane == 0, w1, jnp.where(lane == 1, w2, 0.0))


def _router_call(x, g, w_grp, b_grp, w_exp, b_exp):
    t, d = x.shape
    pad = LANES - MOE_GROUPS - N_EXP
    wr = jnp.concatenate([w_grp, w_exp, jnp.zeros((d, pad), F32)], axis=1)
    br = jnp.concatenate([b_grp, b_exp, jnp.zeros((pad,), F32)]).reshape(1, LANES)
    row = lambda i: (i, 0)
    fixed = lambda i: (0, 0)
    return pl.pallas_call(
        _router_kernel,
        grid=(t // ROW_TILE,),
        in_specs=[pl.BlockSpec((ROW_TILE, d), row), pl.BlockSpec((1, d), fixed),
                  pl.BlockSpec((d, LANES), fixed), pl.BlockSpec((1, LANES), fixed)],
        out_specs=[pl.BlockSpec((ROW_TILE, LANES), row), pl.BlockSpec((ROW_TILE, LANES), row)],
        out_shape=[jax.ShapeDtypeStruct((t, LANES), jnp.int32), jax.ShapeDtypeStruct((t, LANES), F32)],
        compiler_params=_params(("arbitrary",), 40),
        name="moe_router",
    )(x, g.reshape(1, d), wr, br)


def _cumsum_rows(hot):
    t, n = hot.shape
    blk = LANES
    h3 = hot.reshape(t // blk, blk, n).astype(F32)
    tri = jnp.tril(jnp.ones((blk, blk), F32))
    inner = jnp.einsum("ij,bjn->bin", tri, h3).astype(jnp.int32)
    totals = inner[:, -1, :]
    offsets = jnp.cumsum(totals, axis=0) - totals
    return (inner + offsets[:, None, :]).reshape(t, n)


def _moe_plan(e0, e1):
    t = e0.shape[0]
    n_tiles = 2 * t // MOE_TILE + N_EXP
    experts = jnp.arange(N_EXP, dtype=jnp.int32)[None, :]
    hot0 = (e0[:, None] == experts).astype(jnp.int32)
    hot1 = (e1[:, None] == experts).astype(jnp.int32)
    csum0 = _cumsum_rows(hot0)
    csum1 = _cumsum_rows(hot1)
    counts0 = csum0[-1]
    counts = counts0 + csum1[-1]
    tiles_e = (counts + MOE_TILE - 1) // MOE_TILE
    tiles_end = jnp.cumsum(tiles_e)
    row_start = (tiles_end - tiles_e) * MOE_TILE
    pos0 = jnp.sum(hot0 * (row_start[None, :] + csum0 - 1), axis=1)
    pos1 = jnp.sum(hot1 * (row_start[None, :] + counts0[None, :] + csum1 - 1), axis=1)
    pos = jnp.concatenate([pos0, pos1])
    n_valid = tiles_end[-1]
    tile_id = jnp.minimum(jnp.arange(n_tiles, dtype=jnp.int32), n_valid - 1)
    tile_expert = jnp.sum((tile_id[:, None] >= tiles_end[None, :]).astype(jnp.int32), axis=1)
    tile_expert = jnp.minimum(tile_expert, N_EXP - 1)
    changed = jnp.concatenate([jnp.ones((1,), bool), tile_expert[1:] != tile_expert[:-1]])
    first = jnp.logical_and(changed, jnp.arange(n_tiles) < n_valid).astype(jnp.int32)
    wslot = (jnp.cumsum(first) - 1) % 2
    next_start = tiles_end[tile_expert]
    next_expert = jnp.where(next_start < n_valid, tile_expert[jnp.minimum(next_start, n_tiles - 1)], -1)
    tile_meta = jnp.stack([tile_expert, first, wslot, next_expert]).astype(jnp.int32)
    pad_meta = jnp.stack([row_start + counts, tiles_e * MOE_TILE - counts]).astype(jnp.int32)
    return tile_meta, n_valid.reshape(1).astype(jnp.int32), pad_meta, pos.astype(jnp.int32)


def _dispatch_kernel(pos_ref, nv_ref, pad_ref, x_ref, g_ref, xs_hbm, xnbuf, zeros, sems, zsem):
    i = pl.program_id(0)
    n = pl.num_programs(0)
    tm = x_ref.shape[0]
    n_tiles = xs_hbm.shape[0] // MOE_TILE
    n_tok = n * tm
    slot = lax.rem(i, 2)

    def wait_rows(s):
        for kk in range(2):
            pltpu.make_async_copy(xnbuf.at[s], xs_hbm.at[pl.ds(0, tm)], sems.at[s]).wait()

    pl.when(i >= 2)(lambda: wait_rows(slot))
    xnbuf[slot] = _rms(x_ref[...], g_ref[...])
    for r in range(tm):
        for kk in range(2):
            pltpu.make_async_copy(xnbuf.at[slot, pl.ds(r, 1)],
                                  xs_hbm.at[pl.ds(pos_ref[kk * n_tok + i * tm + r], 1)],
                                  sems.at[slot]).start(priority=kk)

    def zero_rows(start, n):
        return pltpu.make_async_copy(zeros.at[pl.ds(0, n)], xs_hbm.at[pl.ds(start, n)], zsem)

    block_sizes = [1 << b for b in range(MOE_TILE.bit_length() - 2, -1, -1)]

    def fill_expert(e, carry):
        start, length = pad_ref[0, e], pad_ref[1, e]
        head = jnp.bitwise_and(-start, SUBLANES - 1)
        lax.fori_loop(0, head, lambda j, c: (zero_rows(start + j, 1).start(), c)[1], 0)
        rest = length - head
        off = start + head
        for sz in block_sizes:
            if sz < SUBLANES:
                break
            take = jnp.bitwise_and(rest, sz) != 0
            pl.when(take)(lambda off=off, sz=sz: zero_rows(pl.multiple_of(off, SUBLANES), sz).start())
            off = off + jnp.where(take, sz, 0)
        return carry

    def drain_expert(e, carry):
        length = pad_ref[1, e]
        for sz in block_sizes:
            pl.when(jnp.bitwise_and(length, sz) != 0)(lambda sz=sz: zero_rows(0, sz).wait())
        return carry

    def tail_tile_copy(tile):
        return pltpu.make_async_copy(zeros, xs_hbm.at[pl.ds(tile * MOE_TILE, MOE_TILE)], zsem)

    @pl.when(i == 0)
    def _():
        zeros[...] = jnp.zeros_like(zeros)
        lax.fori_loop(0, N_EXP, fill_expert, 0)
        lax.fori_loop(nv_ref[0], n_tiles, lambda tile, c: (tail_tile_copy(tile).start(), c)[1], 0)

    @pl.when(i == n - 1)
    def _():
        pl.when(n >= 2)(lambda: wait_rows(1 - slot))
        wait_rows(slot)
        lax.fori_loop(0, N_EXP, drain_expert, 0)
        lax.fori_loop(nv_ref[0], n_tiles, lambda tile, c: (tail_tile_copy(tile).wait(), c)[1], 0)


def _dispatch_call(x, g, pos, n_valid, pad_meta, rows):
    t, d = x.shape
    grid_spec = pltpu.PrefetchScalarGridSpec(
        num_scalar_prefetch=3,
        grid=(t // ROW_TILE,),
        in_specs=[pl.BlockSpec((ROW_TILE, d), lambda i, pos_ref, nv, pad: (i, 0)),
                  pl.BlockSpec((1, d), lambda i, pos_ref, nv, pad: (0, 0))],
        out_specs=pl.BlockSpec(memory_space=pl.ANY),
        scratch_shapes=[pltpu.VMEM((2, ROW_TILE, d), F32), pltpu.VMEM((MOE_TILE, d), F32),
                        pltpu.SemaphoreType.DMA((2,)), pltpu.SemaphoreType.DMA(())],
    )
    return pl.pallas_call(
        _dispatch_kernel,
        grid_spec=grid_spec,
        out_shape=jax.ShapeDtypeStruct((rows, d), F32),
        compiler_params=_params(("arbitrary",), 32),
        name="moe_dispatch",
    )(pos, n_valid, pad_meta, x, g.reshape(1, d))


def _moe_ffn_kernel(tm_ref, nv_ref, x_ref, wgu_hbm, wd_hbm, o_ref,
                    wgu_f, wd_f, wsems, wgu_b, wd_b, *, layer):
    i = pl.program_id(0)
    nv = nv_ref[0]
    expert, first, wslot, next_expert = tm_ref[0, i], tm_ref[1, i], tm_ref[2, i], tm_ref[3, i]

    def weight_copies(e, s):
        return (pltpu.make_async_copy(wgu_hbm.at[layer, e], wgu_f.at[s], wsems.at[0, s]),
                pltpu.make_async_copy(wd_hbm.at[layer, e], wd_f.at[s], wsems.at[1, s]))

    @pl.when(i == 0)
    def _():
        for c in weight_copies(expert, 0):
            c.start()

    @pl.when(first == 1)
    def _():
        @pl.when(next_expert >= 0)
        def _():
            for c in weight_copies(next_expert, 1 - wslot):
                c.start()

        for c in weight_copies(expert, wslot):
            c.wait()
        wgu_b[...] = wgu_f[wslot].astype(BF16)
        wd_b[...] = wd_f[wslot].astype(BF16)

    @pl.when(i < nv)
    def _():
        h = jnp.dot(x_ref[...].astype(BF16), wgu_b[...], preferred_element_type=F32)
        hg = h[:, :D_FF_E]
        hu = h[:, D_FF_E:]
        act = (hg * _sigmoid(hg) * hu).astype(BF16)
        o_ref[...] = jnp.dot(act, wd_b[...], preferred_element_type=F32)

    @pl.when(i >= nv)
    def _():
        o_ref[...] = jnp.zeros_like(o_ref)


def _moe_ffn_call(xs, w_gu, w_down, layer, tile_meta, n_valid):
    rows, d = xs.shape
    n_tiles = tile_meta.shape[1]
    grid_spec = pltpu.PrefetchScalarGridSpec(
        num_scalar_prefetch=2,
        grid=(n_tiles,),
        in_specs=[pl.BlockSpec((MOE_TILE, d), lambda i, tm, nv: (jnp.minimum(i, nv[0] - 1), 0)),
                  pl.BlockSpec(memory_space=pl.ANY), pl.BlockSpec(memory_space=pl.ANY)],
        out_specs=pl.BlockSpec((MOE_TILE, d), lambda i, tm, nv: (i, 0)),
        scratch_shapes=[
            pltpu.VMEM((2, d, 2 * D_FF_E), F32),
            pltpu.VMEM((2, D_FF_E, d), F32),
            pltpu.SemaphoreType.DMA((2, 2)),
            pltpu.VMEM((d, 2 * D_FF_E), BF16),
            pltpu.VMEM((D_FF_E, d), BF16),
        ],
    )
    return pl.pallas_call(
        functools.partial(_moe_ffn_kernel, layer=layer),
        grid_spec=grid_spec,
        out_shape=jax.ShapeDtypeStruct((rows, d), F32),
        compiler_params=_params(("arbitrary",), 48),
        name="moe_experts",
    )(tile_meta, n_valid, xs, w_gu, w_down)


def _combine_kernel(pos_ref, x_ref, w_ref, y_hbm, g_ref, *rest, emit_x, norm_tiles):
    outs = rest[:int(emit_x) + len(norm_tiles)]
    buf, sems = rest[int(emit_x) + len(norm_tiles):]
    i = pl.program_id(0)
    n = pl.num_programs(0)
    slot = lax.rem(i, 2)
    tm = x_ref.shape[0]

    def gather(tile, dst_slot):
        n_tok = n * tm
        for r in range(tm):
            for kk in range(2):
                pltpu.make_async_copy(y_hbm.at[pl.ds(pos_ref[kk * n_tok + tile * tm + r], 1)],
                                      buf.at[dst_slot, kk, pl.ds(r, 1)], sems.at[dst_slot]).start(priority=kk)

    @pl.when(i == 0)
    def _():
        gather(0, 0)

    for s in range(2):
        pl.when(jnp.logical_and(i + 1 < n, slot == s))(functools.partial(gather, i + 1, 1 - s))

    for kk in range(2):
        pltpu.make_async_copy(y_hbm.at[pl.ds(0, tm)], buf.at[slot, kk], sems.at[slot]).wait()
    xo = x_ref[...] + (w_ref[:, 0:1] * buf[slot, 0] + w_ref[:, 1:2] * buf[slot, 1])
    norm_refs = outs[int(emit_x):]
    if emit_x:
        outs[0][...] = xo

    def write_norm(p):
        norm_refs[p][...] = _rms(xo, g_ref[...]).astype(norm_refs[p].dtype)

    _for_each_part(i, norm_tiles, write_norm)


def _combine_call(x, ws, y_rows, pos, g, emit_x, norm_dtype, norm_rows):
    t, d = x.shape
    tm = MOE_TILE
    row = lambda i, pos_ref: (i, 0)
    out_specs, out_shape = [], []
    if emit_x:
        out_specs.append(pl.BlockSpec((tm, d), row))
        out_shape.append(jax.ShapeDtypeStruct((t, d), F32))
    norm_parts = [jax.ShapeDtypeStruct((r, d), norm_dtype) for r in norm_rows]
    out_specs += _part_specs(norm_parts, tm, d, row)
    out_shape += norm_parts
    grid_spec = pltpu.PrefetchScalarGridSpec(
        num_scalar_prefetch=1,
        grid=(t // tm,),
        in_specs=[pl.BlockSpec((tm, d), row), pl.BlockSpec((tm, LANES), row), pl.BlockSpec(memory_space=pl.ANY),
                  pl.BlockSpec((1, d), lambda i, pos_ref: (0, 0))],
        out_specs=out_specs,
        scratch_shapes=[pltpu.VMEM((2, 2, tm, d), F32), pltpu.SemaphoreType.DMA((2,))],
    )
    return pl.pallas_call(
        functools.partial(_combine_kernel, emit_x=emit_x, norm_tiles=_part_tiles(norm_parts, tm)),
        grid_spec=grid_spec,
        out_shape=out_shape,
        compiler_params=_params(("arbitrary",), 40),
        name="moe_combine",
    )(pos, x, ws, y_rows, g.reshape(1, d))


def _moe_layer(x, g_ffn, w_grp, b_grp, w_exp, b_exp, w_gu, w_down, layer, g_next, emit_x, norm_dtype, norm_rows):
    ids, ws = _router_call(x, g_ffn, w_grp, b_grp, w_exp, b_exp)
    tile_meta, n_valid, pad_meta, pos = _moe_plan(ids[:, 0], ids[:, 1])
    xs = _dispatch_call(x, g_ffn, pos, n_valid, pad_meta, tile_meta.shape[1] * MOE_TILE)
    y_rows = _moe_ffn_call(xs, w_gu, w_down, layer, tile_meta, n_valid)
    return _combine_call(x, ws, y_rows, pos, g_next, emit_x, norm_dtype, norm_rows)


def _s5_prep_kernel(are_ref, aim_ref, ldt_ref, bre_ref, bim_ref, lre_ref, lim_ref, bbr_ref, bbi_ref):
    are = are_ref[...]
    aim = aim_ref[...]
    dt = jnp.exp(ldt_ref[...])
    mag = jnp.exp(are * dt)
    lr = mag * jnp.cos(aim * dt)
    li = mag * jnp.sin(aim * dt)
    lre_ref[...] = lr
    lim_ref[...] = li
    den = are * are + aim * aim
    nr = lr - 1.0
    cr = (nr * are + li * aim) / den
    ci = (li * are - nr * aim) / den
    bre = bre_ref[...]
    bim = bim_ref[...]
    bbr_ref[...] = cr * bre - ci * bim
    bbi_ref[...] = cr * bim + ci * bre


def _s5_prep_call(a_re, a_im, log_dt, b_re, b_im):
    g, p, c = b_re.shape
    shp = (g, p * c)
    expand = lambda a: jnp.broadcast_to(a[:, :, None], (g, p, c)).reshape(shp)
    ldt = jnp.broadcast_to(log_dt[:, None], shp)
    outs = pl.pallas_call(
        _s5_prep_kernel,
        out_shape=[jax.ShapeDtypeStruct(shp, F32)] * 4,
        name="s5_discretize",
    )(expand(a_re), expand(a_im), ldt, b_re.reshape(shp), b_im.reshape(shp))
    lre, lim, bbr, bbi = outs
    return lre[:, ::c], lim[:, ::c], bbr, bbi


def _s5_compact(bbr, bbi, c_re, c_im):
    def b_part(bb):
        return bb.reshape(S5_FB, 16, S5_P, S5_GC).transpose(0, 1, 3, 2).reshape(S5_FB, 16 * S5_GC, S5_P)

    def c_part(cc):
        return cc.reshape(S5_FB, 16, S5_GC, S5_P).transpose(0, 3, 1, 2).reshape(S5_FB, S5_P, 16 * S5_GC)

    return b_part(bbr), b_part(bbi), c_part(c_re), c_part(c_im)


def _s5_build_mats(bbr_ref, bbi_ref, ccr_ref, cci_ref, bd_ref, cd_ref):
    nch, nst = 16 * S5_GC, 16 * S5_P
    iota = lambda shape, dim: lax.broadcasted_iota(jnp.int32, shape, dim)
    rep = ((iota((S5_P, nst), 1) & (S5_P - 1)) == iota((S5_P, nst), 0)).astype(BF16)
    rep_t = ((iota((nst, S5_P), 0) & (S5_P - 1)) == iota((nst, S5_P), 1)).astype(BF16)
    gc_bits, p_bits = S5_GC.bit_length() - 1, S5_P.bit_length() - 1
    diag = (iota((nch, nst), 0) >> gc_bits) == (iota((nch, nst), 1) >> p_bits)
    diag_t = (iota((nst, nch), 0) >> p_bits) == (iota((nst, nch), 1) >> gc_bits)
    for fb in range(S5_FB):
        for part, src in enumerate((bbr_ref, bbi_ref)):
            full = jnp.dot(src[fb].astype(BF16), rep, preferred_element_type=F32)
            bd_ref[fb, :, part * nst:(part + 1) * nst] = jnp.where(diag, full, 0.0).astype(BF16)
        for part, (src, sign) in enumerate(((ccr_ref, 1.0), (cci_ref, -1.0))):
            full = jnp.dot(rep_t, src[fb].astype(BF16), preferred_element_type=F32)
            cd_ref[fb, part * nst:(part + 1) * nst, :] = jnp.where(diag_t, sign * full, 0.0).astype(BF16)


def _s5_kernel(u_ref, bbr_ref, bbi_ref, ccr_ref, cci_ref, lre_ref, lim_ref, d_ref, h0r_ref, h0i_ref,
               z_ref, hr_ref, hi_ref, scr, hst, bd_ref, cd_ref, *, seg, carry):
    rows = u_ref.shape[0]
    n_seq = rows // seg
    ch = 16 * S5_GC
    half = SUBLANES * S5_PITCH
    blk = 2 * half

    @pl.when(pl.program_id(0) == 0)
    def _():
        _s5_build_mats(bbr_ref, bbi_ref, ccr_ref, cci_ref, bd_ref, cd_ref)

    for fb in range(S5_FB):
        bu = jnp.dot(u_ref[:, fb * ch:(fb + 1) * ch].astype(BF16), bd_ref[fb], preferred_element_type=F32)
        for c in range(2 * SUBLANES):
            r0 = fb * blk + c * S5_PITCH
            scr[r0:r0 + rows, :] = bu[:, c * LANES:(c + 1) * LANES]

    lre = [lre_ref[fb * SUBLANES:(fb + 1) * SUBLANES, :] for fb in range(S5_FB)]
    lim = [lim_ref[fb * SUBLANES:(fb + 1) * SUBLANES, :] for fb in range(S5_FB)]

    def scan(row0, state):
        def step(l, st):
            new = []
            for fb in range(S5_FB):
                hr, hi = st[2 * fb], st[2 * fb + 1]
                ire = pl.ds(fb * blk + row0 + l, SUBLANES, stride=S5_PITCH)
                iim = pl.ds(fb * blk + half + row0 + l, SUBLANES, stride=S5_PITCH)
                nr = lre[fb] * hr - lim[fb] * hi + scr[ire, :]
                ni = lre[fb] * hi + lim[fb] * hr + scr[iim, :]
                scr[ire, :] = nr
                scr[iim, :] = ni
                new += [nr, ni]
            return tuple(new)
        return lax.fori_loop(0, seg, step, state, unroll=8)

    def load_state(rref, iref, lead):
        st = []
        for fb in range(S5_FB):
            st += [rref[lead + (slice(fb * SUBLANES, (fb + 1) * SUBLANES), slice(None))],
                   iref[lead + (slice(fb * SUBLANES, (fb + 1) * SUBLANES), slice(None))]]
        return tuple(st)

    def store_state(st, rref, iref, lead):
        for fb in range(S5_FB):
            rref[lead + (slice(fb * SUBLANES, (fb + 1) * SUBLANES), slice(None))] = st[2 * fb]
            iref[lead + (slice(fb * SUBLANES, (fb + 1) * SUBLANES), slice(None))] = st[2 * fb + 1]

    if carry:
        @pl.when(pl.program_id(0) == 0)
        def _():
            hst[0] = h0r_ref[0]
            hst[1] = h0i_ref[0]
        hre_scr = hst.at[0]
        him_scr = hst.at[1]
        st = scan(0, load_state(hre_scr, him_scr, ()))
        store_state(st, hre_scr, him_scr, ())
        store_state(st, hr_ref, hi_ref, (0,))
    else:
        for s in range(n_seq):
            st = scan(s * seg, load_state(h0r_ref, h0i_ref, (s,)))
            store_state(st, hr_ref, hi_ref, (s,))

    for fb in range(S5_FB):
        hf = jnp.concatenate(
            [scr[fb * blk + c * S5_PITCH: fb * blk + c * S5_PITCH + rows, :] for c in range(2 * SUBLANES)],
            axis=1).astype(BF16)
        y = jnp.dot(hf, cd_ref[fb], preferred_element_type=F32)
        cs = slice(fb * ch, (fb + 1) * ch)
        yy = y + d_ref[:, cs] * u_ref[:, cs]
        z_ref[:, cs] = jax.nn.gelu(yy).astype(z_ref.dtype)


def _s5_call(u, mats, lre, lim, d_skip, h0r, h0i, row_block0, n_blocks, seg, carry):
    t, d = u.shape
    ch, nst = 16 * S5_GC, 16 * S5_P
    n_seq = S5_ROWS // seg
    ns_total = h0r.shape[0]
    st_idx = (lambda i: (0, 0, 0)) if carry else (lambda i: (i, 0, 0))
    fixed2 = lambda i: (0, 0)
    fixed3 = lambda i: (0, 0, 0)
    st_rows = S5_FB * SUBLANES
    return pl.pallas_call(
        functools.partial(_s5_kernel, seg=seg, carry=carry),
        grid=(n_blocks,),
        in_specs=[
            pl.BlockSpec((S5_ROWS, d), lambda i: (row_block0 + i, 0)),
            *[pl.BlockSpec(m.shape, fixed3) for m in mats],
            pl.BlockSpec(lre.shape, fixed2),
            pl.BlockSpec(lim.shape, fixed2),
            pl.BlockSpec((1, d), fixed2),
            pl.BlockSpec((n_seq, st_rows, LANES), st_idx),
            pl.BlockSpec((n_seq, st_rows, LANES), st_idx),
        ],
        out_specs=[
            pl.BlockSpec((S5_ROWS, d), lambda i: (i, 0)),
            pl.BlockSpec((n_seq, st_rows, LANES), st_idx),
            pl.BlockSpec((n_seq, st_rows, LANES), st_idx),
        ],
        out_shape=[
            jax.ShapeDtypeStruct((n_blocks * S5_ROWS, d), BF16),
            jax.ShapeDtypeStruct((ns_total, st_rows, LANES), F32),
            jax.ShapeDtypeStruct((ns_total, st_rows, LANES), F32),
        ],
        scratch_shapes=[
            pltpu.VMEM((2 * SUBLANES * S5_FB * S5_PITCH, LANES), F32),
            pltpu.VMEM((2, st_rows, LANES), F32),
            pltpu.VMEM((S5_FB, ch, 2 * nst), BF16),
            pltpu.VMEM((S5_FB, 2 * nst, ch), BF16),
        ],
        compiler_params=_params(("arbitrary",), 56),
        name="s5_scan",
    )(u, *mats, lre, lim, d_skip.reshape(1, d), h0r, h0i)


def kernel(x_prompt, x_sample, state_ret, state_s5_re, state_s5_im, norm_mix_g, norm_ffn_g, norm_final_g,
           ret_w_in, ret_gn_g, ret_w_out, s5_w_in, s5_a_re, s5_a_im, s5_log_dt, s5_b_re, s5_b_im,
           s5_c_re, s5_c_im, s5_d, s5_w_out, moe_w_grp, moe_b_grp, moe_w_exp, moe_b_exp, moe_w_gu, moe_w_down):
    bp, seq, d = x_prompt.shape
    db, dseq, _ = x_sample.shape
    assert d == D_MODEL and bp == 1
    assert seq % max(ROW_TILE, RET_ROWS) == 0 and seq % CHUNK == 0
    assert dseq < CHUNK and dseq % 16 == 0 and S5_ROWS % dseq == 0
    tp = bp * seq
    ts = db * dseq
    t = tp + ts
    assert ts % ROW_TILE == 0 and ts % S5_ROWS == 0

    x_parts = [x_prompt.reshape(tp, d), x_sample.reshape(ts, d)]

    half = RET_DK // 2
    freqs = ROPE_BASE ** (-jnp.arange(half, dtype=F32) / half)
    pos = jnp.concatenate([jnp.arange(seq, dtype=F32), jnp.tile(PAST_LEN + jnp.arange(dseq, dtype=F32), db)])
    ang = pos[:, None] * freqs[None, :]
    cos_t = jnp.cos(ang)
    sin_t = jnp.sin(ang)

    xn = _norm_call(x_parts, norm_mix_g[0], BF16)
    tn = 1024
    first_col = lambda j: 0
    qk = _mm_call([xn], ret_w_in[0], [0], tn, 2 * D_MODEL, [([cos_t], half, first_col), ([sin_t], half, first_col)],
                  BF16, functools.partial(_epi_rope, n_q_blocks=D_MODEL // tn), 40, "ret_qk_proj")
    vg = _mm_call([xn], ret_w_in[0], [2 * D_MODEL // tn], tn, 2 * RET_VW, [], BF16, _epi_plain, 40, "ret_vg_proj")
    og_p, ret_p = _ret_prompt_call(qk, vg, ret_gn_g[0], tp)
    og_s, ret_s = _ret_sample_call(qk, vg, ret_gn_g[0], state_ret[0], tp, dseq)
    tn = 512
    same_col = lambda j: j
    x1 = _mm_call([og_p, og_s], ret_w_out[0], [0], tn, d, [(x_parts, tn, same_col)], F32, _epi_residual,
                  40, "ret_out_proj")

    x2, xn = _moe_layer(x1, norm_ffn_g[0], moe_w_grp[0], moe_b_grp[0], moe_w_exp[0], moe_b_exp[0],
                        moe_w_gu, moe_w_down, 0, norm_mix_g[1], True, BF16, (t,))

    u = _mm_call([xn], s5_w_in[0], [0], 1024, d, [], F32, _epi_plain, 40, "s5_in_proj")
    lre, lim, bbr, bbi = _s5_prep_call(s5_a_re[0], s5_a_im[0], s5_log_dt[0], s5_b_re[0], s5_b_im[0])
    mats = _s5_compact(bbr, bbi, s5_c_re[0], s5_c_im[0])
    st_shape = (S5_FB * SUBLANES, LANES)
    lre = lre.reshape(st_shape)
    lim = lim.reshape(st_shape)
    zero_state = jnp.zeros((1,) + st_shape, F32)
    z_p, hpr, hpi = _s5_call(u, mats, lre, lim, s5_d[0], zero_state, zero_state,
                             0, tp // S5_ROWS, S5_ROWS, True)
    z_s, hsr, hsi = _s5_call(u, mats, lre, lim, s5_d[0],
                             state_s5_re[0].reshape((db,) + st_shape), state_s5_im[0].reshape((db,) + st_shape),
                             tp // S5_ROWS, ts // S5_ROWS, dseq, False)
    x3 = _mm_call([z_p, z_s], s5_w_out[0], [0, d // tn], tn, d, [([x2], tn, same_col)], F32, _epi_glu_residual,
                  40, "s5_out_proj")

    y_p, y_s = _moe_layer(x3, norm_ffn_g[1], moe_w_grp[1], moe_b_grp[1], moe_w_exp[1], moe_b_exp[1],
                          moe_w_gu, moe_w_down, 1, norm_final_g, False, F32, (tp, ts))

    s5_shape = (1, -1, S5_GROUPS, S5_P)
    return (y_p.reshape(bp, seq, d), y_s.reshape(db, dseq, d),
            ret_p.reshape(1, bp, RET_HEADS, RET_DK, RET_DV), ret_s.reshape((1,) + ret_s.shape),
            hpr.reshape(s5_shape), hpi.reshape(s5_shape), hsr.reshape(s5_shape), hsi.reshape(s5_shape))
```

```python
import functools
import math

import numpy as np
import jax
import jax.numpy as jnp
from jax import lax
from jax.experimental import pallas as pl
from jax.experimental.pallas import tpu as pltpu

F32 = jnp.float32
BF16 = jnp.bfloat16

D_MODEL = 2048
PAST_LEN = 2048
CHUNK = 64
RET_HEADS = 8
RET_DK = D_MODEL // RET_HEADS
RET_DV = 2 * RET_DK
RET_VW = RET_HEADS * RET_DV
ROPE_BASE = 10000.0
S5_GC = 16
S5_GROUPS = D_MODEL // S5_GC
S5_P = 64
MOE_GROUPS = 4
MOE_PER_GROUP = 8
N_EXP = MOE_GROUPS * MOE_PER_GROUP
D_FF_E = 512
EPS = 1e-6

LANES = 128
SUBLANES = 8
ROW_TILE = 512
MM_CHUNK = 256
RET_ROWS = 256
RET_HPS = 4
MOE_TILE = 256
S5_ROWS = 256
S5_FB = S5_GROUPS // 16
S5_PITCH = S5_ROWS + 4
MIB = 1024 * 1024


def _params(sem, vmem_mib):
    return pltpu.CompilerParams(dimension_semantics=sem, vmem_limit_bytes=vmem_mib * MIB)


def _rms(x, g):
    ms = jnp.mean(x * x, axis=-1, keepdims=True)
    return x * lax.rsqrt(ms + EPS) * g


def _sigmoid(x):
    return 1.0 / (1.0 + jnp.exp(-x))


def _part_specs(parts, rows, cols, rowcol):
    specs = []
    start = 0
    for p in parts:
        nt = p.shape[0] // rows

        def imap(*g, start=start, nt=nt):
            i, jc = rowcol(*g)
            return (jnp.clip(i - start, 0, nt - 1), jc)

        specs.append(pl.BlockSpec((rows, cols), imap))
        start += nt
    return specs


def _part_tiles(parts, rows):
    return tuple(p.shape[0] // rows for p in parts)


def _for_each_part(i, tiles, fn):
    if len(tiles) == 1:
        fn(0)
        return
    start = 0
    for p, nt in enumerate(tiles):
        pl.when(jnp.logical_and(i >= start, i < start + nt))(functools.partial(fn, p))
        start += nt


def _pick(refs, p):
    return refs[min(p, len(refs) - 1)]


def _norm_kernel(*refs, tiles):
    n = len(tiles)
    x_refs, g_ref, o_ref = refs[:n], refs[n], refs[n + 1]

    def run(p):
        o_ref[...] = _rms(x_refs[p][...], g_ref[...]).astype(o_ref.dtype)

    _for_each_part(pl.program_id(0), tiles, run)


def _norm_call(x_parts, g, out_dtype):
    d = x_parts[0].shape[1]
    t = sum(p.shape[0] for p in x_parts)
    return pl.pallas_call(
        functools.partial(_norm_kernel, tiles=_part_tiles(x_parts, ROW_TILE)),
        grid=(t // ROW_TILE,),
        in_specs=_part_specs(x_parts, ROW_TILE, d, lambda i: (i, 0)) + [pl.BlockSpec((1, d), lambda i: (0, 0))],
        out_specs=pl.BlockSpec((ROW_TILE, d), lambda i: (i, 0)),
        out_shape=jax.ShapeDtypeStruct((t, d), out_dtype),
        compiler_params=_params(("arbitrary",), 32),
        name="rmsnorm",
    )(*x_parts, g.reshape(1, d))


def _mm_kernel(*refs, tiles, n_x, n_w, n_extra, epi):
    x_refs = refs[:n_x]
    w_refs = refs[n_x:n_x + n_w]
    pos = n_x + n_w
    extra = []
    for n in n_extra:
        extra.append(refs[pos:pos + n])
        pos += n
    o_ref = refs[pos]
    wb = refs[pos + 1:]

    @pl.when(pl.program_id(1) == 0)
    def _():
        for w_ref, b in zip(w_refs, wb):
            b[...] = w_ref[...].astype(BF16)

    def run(p):
        x = _pick(x_refs, p)[...]
        for c in range(o_ref.shape[1] // MM_CHUNK):
            cs = slice(c * MM_CHUNK, (c + 1) * MM_CHUNK)
            accs = [jnp.dot(x, b[:, cs], preferred_element_type=F32) for b in wb]
            epi(accs, [_pick(e, p) for e in extra], o_ref, cs)

    _for_each_part(pl.program_id(1), tiles, run)


def _mm_call(x_parts, w, w_col_offsets, tn, n_out, extras, out_dtype, epi, vmem_mib, name):
    k = x_parts[0].shape[1]
    t = sum(p.shape[0] for p in x_parts)
    n_w = len(w_col_offsets)
    split = [parts for parts in [x_parts] + [e[0] for e in extras] if len(parts) > 1]
    tiles = _part_tiles(split[0], ROW_TILE) if split else (t // ROW_TILE,)
    assert all(_part_tiles(parts, ROW_TILE) == tiles for parts in split)
    in_specs = _part_specs(x_parts, ROW_TILE, k, lambda j, i: (i, 0))
    for off in w_col_offsets:
        in_specs.append(pl.BlockSpec((k, tn), lambda j, i, off=off: (0, j + off)))
    extra_arrays = []
    for parts, cols, colfn in extras:
        in_specs += _part_specs(parts, ROW_TILE, cols, lambda j, i, colfn=colfn: (i, colfn(j)))
        extra_arrays += list(parts)
    return pl.pallas_call(
        functools.partial(_mm_kernel, tiles=tiles, n_x=len(x_parts), n_w=n_w,
                          n_extra=tuple(len(e[0]) for e in extras), epi=epi),
        grid=(n_out // tn, t // ROW_TILE),
        in_specs=in_specs,
        out_specs=pl.BlockSpec((ROW_TILE, tn), lambda j, i: (i, j)),
        out_shape=jax.ShapeDtypeStruct((t, n_out), out_dtype),
        scratch_shapes=[pltpu.VMEM((k, tn), BF16) for _ in range(n_w)],
        compiler_params=_params(("arbitrary", "arbitrary"), vmem_mib),
        name=name,
    )(*x_parts, *([w] * n_w), *extra_arrays)


def _epi_plain(accs, extra, o_ref, cs):
    o_ref[:, cs] = accs[0].astype(o_ref.dtype)


def _epi_residual(accs, extra, o_ref, cs):
    o_ref[:, cs] = (extra[0][:, cs] + accs[0]).astype(o_ref.dtype)


def _epi_glu_residual(accs, extra, o_ref, cs):
    a, gt = accs
    o_ref[:, cs] = (extra[0][:, cs] + a * _sigmoid(gt)).astype(o_ref.dtype)


def _epi_rope(accs, extra, o_ref, cs, *, n_q_blocks):
    acc = accs[0]
    assert acc.shape[1] == RET_DK
    cos_ref, sin_ref = extra
    half = RET_DK // 2
    scale = jnp.where(pl.program_id(0) >= n_q_blocks, RET_DK ** -0.5, 1.0).astype(F32)
    c = cos_ref[...]
    s = sin_ref[...]
    t1 = acc[:, :half]
    t2 = acc[:, half:]
    o_ref[:, cs.start:cs.start + half] = ((t1 * c - t2 * s) * scale).astype(o_ref.dtype)
    o_ref[:, cs.start + half:cs.stop] = ((t1 * s + t2 * c) * scale).astype(o_ref.dtype)


def _ret_tables(rows, chunk):
    lg = jnp.log(jnp.asarray(1.0 - 2.0 ** (-5.0 - np.arange(RET_HEADS)), dtype=F32))[:, None, None]
    n = jnp.arange(rows, dtype=F32)
    ci = np.arange(rows) // chunk
    same_or_earlier = jnp.asarray(ci[None, :] <= ci[:, None])
    dmat = jnp.where(same_or_earlier[None], jnp.exp(jnp.abs(n[:, None] - n[None, :])[None] * lg), 0.0)
    dq = jnp.exp((n + 1.0)[None, :, None] * lg) * jnp.ones((1, 1, LANES), F32)
    dk = jnp.exp((rows - 1.0 - n)[None, :, None] * lg) * jnp.ones((1, 1, LANES), F32)
    gr = jnp.exp(rows * lg) * jnp.ones((1, 1, RET_DV), F32)
    return dmat, dq, dk, gr


def _ret_block(q, k, v, g, s_prev, dmat, dq, dk, gr, gn):
    s = lax.dot_general(q, k, (((1,), (1,)), ((), ())), preferred_element_type=F32) * dmat
    inter = jnp.dot(q, s_prev.astype(BF16), preferred_element_type=F32)
    o = jnp.dot(s.astype(BF16), v, preferred_element_type=F32) + jnp.concatenate([dq] * (RET_DV // LANES), axis=1) * inter
    kd = (k.astype(F32) * jnp.concatenate([dk] * (RET_DK // LANES), axis=1)).astype(BF16)
    s_new = s_prev * gr + lax.dot_general(kd, v, (((0,), (0,)), ((), ())), preferred_element_type=F32)
    mu = jnp.mean(o, axis=-1, keepdims=True)
    oc = o - mu
    var = jnp.mean(oc * oc, axis=-1, keepdims=True)
    on = oc * lax.rsqrt(var + EPS) * gn
    gf = g.astype(F32)
    return (gf * _sigmoid(gf) * on).astype(BF16), s_new


def _ret_prompt_kernel(q_ref, k_ref, v_ref, g_ref, dm_ref, dq_ref, dk_ref, gr_ref, gn_ref,
                       og_ref, st_ref, s_scr):
    @pl.when(pl.program_id(1) == 0)
    def _():
        s_scr[...] = jnp.zeros_like(s_scr)

    for hh in range(RET_HPS):
        ks = slice(hh * RET_DK, (hh + 1) * RET_DK)
        vs = slice(hh * RET_DV, (hh + 1) * RET_DV)
        og, s_new = _ret_block(q_ref[:, ks], k_ref[:, ks], v_ref[:, vs], g_ref[:, vs], s_scr[hh],
                               dm_ref[hh], dq_ref[hh], dk_ref[hh], gr_ref[hh], gn_ref[:, vs])
        og_ref[:, vs] = og
        s_scr[hh] = s_new
        st_ref[hh] = s_new


def _ret_prompt_call(qk, vg, gn, t_prompt):
    dm, dq, dk, gr = _ret_tables(RET_ROWS, CHUNK)
    hps = RET_HPS
    hb = RET_HEADS // hps
    return pl.pallas_call(
        _ret_prompt_kernel,
        grid=(hb, t_prompt // RET_ROWS),
        in_specs=[
            pl.BlockSpec((RET_ROWS, hps * RET_DK), lambda hh, c: (c, hh)),
            pl.BlockSpec((RET_ROWS, hps * RET_DK), lambda hh, c: (c, hb + hh)),
            pl.BlockSpec((RET_ROWS, hps * RET_DV), lambda hh, c: (c, hh)),
            pl.BlockSpec((RET_ROWS, hps * RET_DV), lambda hh, c: (c, hb + hh)),
            pl.BlockSpec((hps, RET_ROWS, RET_ROWS), lambda hh, c: (hh, 0, 0)),
            pl.BlockSpec((hps, RET_ROWS, LANES), lambda hh, c: (hh, 0, 0)),
            pl.BlockSpec((hps, RET_ROWS, LANES), lambda hh, c: (hh, 0, 0)),
            pl.BlockSpec((hps, 1, RET_DV), lambda hh, c: (hh, 0, 0)),
            pl.BlockSpec((1, hps * RET_DV), lambda hh, c: (0, hh)),
        ],
        out_specs=[
            pl.BlockSpec((RET_ROWS, hps * RET_DV), lambda hh, c: (c, hh)),
            pl.BlockSpec((hps, RET_DK, RET_DV), lambda hh, c: (hh, 0, 0)),
        ],
        out_shape=[
            jax.ShapeDtypeStruct((t_prompt, RET_VW), BF16),
            jax.ShapeDtypeStruct((RET_HEADS, RET_DK, RET_DV), F32),
        ],
        scratch_shapes=[pltpu.VMEM((hps, RET_DK, RET_DV), F32)],
        compiler_params=_params(("arbitrary", "arbitrary"), 40),
        name="retention_prompt",
    )(qk, qk, vg, vg, dm, dq, dk, gr, gn.reshape(1, RET_VW))


def _ret_sample_kernel(q_ref, k_ref, v_ref, g_ref, s0_ref, dm_ref, dq_ref, dk_ref, gr_ref, gn_ref,
                       og_ref, st_ref):
    for hh in range(RET_HEADS):
        ks = slice(hh * RET_DK, (hh + 1) * RET_DK)
        vs = slice(hh * RET_DV, (hh + 1) * RET_DV)
        og, s_new = _ret_block(q_ref[:, ks], k_ref[:, ks], v_ref[:, vs], g_ref[:, vs], s0_ref[0, hh],
                               dm_ref[hh], dq_ref[hh], dk_ref[hh], gr_ref[hh], gn_ref[:, vs])
        og_ref[:, vs] = og
        st_ref[0, hh] = s_new


def _ret_sample_call(qk, vg, gn, state, t_prompt, dec_seq):
    dm, dq, dk, gr = _ret_tables(dec_seq, dec_seq)
    b = state.shape[0]
    r0 = t_prompt // dec_seq
    const3 = lambda bb: (0, 0, 0)
    return pl.pallas_call(
        _ret_sample_kernel,
        grid=(b,),
        in_specs=[
            pl.BlockSpec((dec_seq, D_MODEL), lambda bb: (r0 + bb, 0)),
            pl.BlockSpec((dec_seq, D_MODEL), lambda bb: (r0 + bb, 1)),
            pl.BlockSpec((dec_seq, RET_VW), lambda bb: (r0 + bb, 0)),
            pl.BlockSpec((dec_seq, RET_VW), lambda bb: (r0 + bb, 1)),
            pl.BlockSpec((1, RET_HEADS, RET_DK, RET_DV), lambda bb: (bb, 0, 0, 0)),
            pl.BlockSpec(dm.shape, const3),
            pl.BlockSpec(dq.shape, const3),
            pl.BlockSpec(dk.shape, const3),
            pl.BlockSpec(gr.shape, const3),
            pl.BlockSpec((1, RET_VW), lambda bb: (0, 0)),
        ],
        out_specs=[
            pl.BlockSpec((dec_seq, RET_VW), lambda bb: (bb, 0)),
            pl.BlockSpec((1, RET_HEADS, RET_DK, RET_DV), lambda bb: (bb, 0, 0, 0)),
        ],
        out_shape=[
            jax.ShapeDtypeStruct((b * dec_seq, RET_VW), BF16),
            jax.ShapeDtypeStruct(state.shape, F32),
        ],
        compiler_params=_params(("arbitrary",), 40),
        name="retention_sample",
    )(qk, qk, vg, vg, state, dm, dq, dk, gr, gn.reshape(1, RET_VW))


def _router_kernel(x_ref, g_ref, wr_ref, br_ref, id_ref, w_ref):
    xn = _rms(x_ref[...], g_ref[...])
    xh = xn.astype(BF16)
    xl = (xn - xh.astype(F32)).astype(BF16)
    w = wr_ref[...]
    wh = w.astype(BF16)
    wl = (w - wh.astype(F32)).astype(BF16)
    hi = jnp.dot(xh, jnp.concatenate([wh, wl], axis=1), preferred_element_type=F32)
    lg = (hi[:, :LANES] + hi[:, LANES:] + jnp.dot(xl, wh, preferred_element_type=F32)) + br_ref[...]
    lane = lax.broadcasted_iota(jnp.int32, lg.shape, 1).astype(F32)
    neg = jnp.float32(-jnp.inf)
    big = jnp.float32(LANES)
    gl = jnp.where(lane < MOE_GROUPS, lg, neg)
    gmax = jnp.max(gl, axis=-1, keepdims=True)
    gsel = jnp.min(jnp.where(gl == gmax, lane, big), axis=-1, keepdims=True)
    gw = 1.0 / jnp.sum(jnp.exp(gl - gmax), axis=-1, keepdims=True)
    lo = MOE_GROUPS + gsel * MOE_PER_GROUP
    el = jnp.where((lane >= lo) & (lane < lo + MOE_PER_GROUP), lg, neg)
    m1 = jnp.max(el, axis=-1, keepdims=True)
    i1 = jnp.min(jnp.where(el == m1, lane, big), axis=-1, keepdims=True)
    el2 = jnp.where(lane == i1, neg, el)
    m2 = jnp.max(el2, axis=-1, keepdims=True)
    i2 = jnp.min(jnp.where(el2 == m2, lane, big), axis=-1, keepdims=True)
    z = jnp.sum(jnp.exp(el - m1), axis=-1, keepdims=True)
    p1 = 1.0 / z
    p2 = jnp.exp(m2 - m1) / z
    tot = p1 + p2
    w1 = p1 / tot * gw
    w2 = p2 / tot * gw
    ids = jnp.where(lane == 0, i1 - MOE_GROUPS, jnp.where(lane == 1, i2 - MOE_GROUPS, 0.0))
    id_ref[...] = ids.astype(jnp.int32)
    w_ref[...] = jnp.where(lane == 0, w1, jnp.where(lane == 1, w2, 0.0))


def _router_call(x, g, w_grp, b_grp, w_exp, b_exp):
    t, d = x.shape
    pad = LANES - MOE_GROUPS - N_EXP
    wr = jnp.concatenate([w_grp, w_exp, jnp.zeros((d, pad), F32)], axis=1)
    br = jnp.concatenate([b_grp, b_exp, jnp.zeros((pad,), F32)]).reshape(1, LANES)
    row = lambda i: (i, 0)
    fixed = lambda i: (0, 0)
    return pl.pallas_call(
        _router_kernel,
        grid=(t // ROW_TILE,),
        in_specs=[pl.BlockSpec((ROW_TILE, d), row), pl.BlockSpec((1, d), fixed),
                  pl.BlockSpec((d, LANES), fixed), pl.BlockSpec((1, LANES), fixed)],
        out_specs=[pl.BlockSpec((ROW_TILE, LANES), row), pl.BlockSpec((ROW_TILE, LANES), row)],
        out_shape=[jax.ShapeDtypeStruct((t, LANES), jnp.int32), jax.ShapeDtypeStruct((t, LANES), F32)],
        compiler_params=_params(("arbitrary",), 40),
        name="moe_router",
    )(x, g.reshape(1, d), wr, br)


def _cumsum_rows(hot):
    t, n = hot.shape
    blk = LANES
    h3 = hot.reshape(t // blk, blk, n).astype(F32)
    tri = jnp.tril(jnp.ones((blk, blk), F32))
    inner = jnp.einsum("ij,bjn->bin", tri, h3).astype(jnp.int32)
    totals = inner[:, -1, :]
    offsets = jnp.cumsum(totals, axis=0) - totals
    return (inner + offsets[:, None, :]).reshape(t, n)


def _moe_plan(e0, e1):
    t = e0.shape[0]
    n_tiles = 2 * t // MOE_TILE + N_EXP
    experts = jnp.arange(N_EXP, dtype=jnp.int32)[None, :]
    hot0 = (e0[:, None] == experts).astype(jnp.int32)
    hot1 = (e1[:, None] == experts).astype(jnp.int32)
    csum0 = _cumsum_rows(hot0)
    csum1 = _cumsum_rows(hot1)
    counts0 = csum0[-1]
    counts = counts0 + csum1[-1]
    tiles_e = (counts + MOE_TILE - 1) // MOE_TILE
    tiles_end = jnp.cumsum(tiles_e)
    row_start = (tiles_end - tiles_e) * MOE_TILE
    pos0 = jnp.sum(hot0 * (row_start[None, :] + csum0 - 1), axis=1)
    pos1 = jnp.sum(hot1 * (row_start[None, :] + counts0[None, :] + csum1 - 1), axis=1)
    pos = jnp.concatenate([pos0, pos1])
    n_valid = tiles_end[-1]
    tile_id = jnp.minimum(jnp.arange(n_tiles, dtype=jnp.int32), n_valid - 1)
    tile_expert = jnp.sum((tile_id[:, None] >= tiles_end[None, :]).astype(jnp.int32), axis=1)
    tile_expert = jnp.minimum(tile_expert, N_EXP - 1)
    changed = jnp.concatenate([jnp.ones((1,), bool), tile_expert[1:] != tile_expert[:-1]])
    first = jnp.logical_and(changed, jnp.arange(n_tiles) < n_valid).astype(jnp.int32)
    wslot = (jnp.cumsum(first) - 1) % 2
    next_start = tiles_end[tile_expert]
    next_expert = jnp.where(next_start < n_valid, tile_expert[jnp.minimum(next_start, n_tiles - 1)], -1)
    tile_meta = jnp.stack([tile_expert, first, wslot, next_expert]).astype(jnp.int32)
    pad_meta = jnp.stack([row_start + counts, tiles_e * MOE_TILE - counts]).astype(jnp.int32)
    return tile_meta, n_valid.reshape(1).astype(jnp.int32), pad_meta, pos.astype(jnp.int32)


def _dispatch_kernel(pos_ref, nv_ref, pad_ref, x_ref, g_ref, xs_hbm, xnbuf, zeros, sems, zsem):
    i = pl.program_id(0)
    n = pl.num_programs(0)
    tm = x_ref.shape[0]
    n_tiles = xs_hbm.shape[0] // MOE_TILE
    n_tok = n * tm
    slot = lax.rem(i, 2)

    def wait_rows(s):
        for kk in range(2):
            pltpu.make_async_copy(xnbuf.at[s], xs_hbm.at[pl.ds(0, tm)], sems.at[s]).wait()

    pl.when(i >= 2)(lambda: wait_rows(slot))
    xnbuf[slot] = _rms(x_ref[...], g_ref[...])
    for r in range(tm):
        for kk in range(2):
            pltpu.make_async_copy(xnbuf.at[slot, pl.ds(r, 1)],
                                  xs_hbm.at[pl.ds(pos_ref[kk * n_tok + i * tm + r], 1)],
                                  sems.at[slot]).start(priority=kk)

    def zero_rows(start, n):
        return pltpu.make_async_copy(zeros.at[pl.ds(0, n)], xs_hbm.at[pl.ds(start, n)], zsem)

    block_sizes = [1 << b for b in range(MOE_TILE.bit_length() - 2, -1, -1)]

    def fill_expert(e, carry):
        start, length = pad_ref[0, e], pad_ref[1, e]
        head = jnp.bitwise_and(-start, SUBLANES - 1)
        lax.fori_loop(0, head, lambda j, c: (zero_rows(start + j, 1).start(), c)[1], 0)
        rest = length - head
        off = start + head
        for sz in block_sizes:
            if sz < SUBLANES:
                break
            take = jnp.bitwise_and(rest, sz) != 0
            pl.when(take)(lambda off=off, sz=sz: zero_rows(pl.multiple_of(off, SUBLANES), sz).start())
            off = off + jnp.where(take, sz, 0)
        return carry

    def drain_expert(e, carry):
        length = pad_ref[1, e]
        for sz in block_sizes:
            pl.when(jnp.bitwise_and(length, sz) != 0)(lambda sz=sz: zero_rows(0, sz).wait())
        return carry

    def tail_tile_copy(tile):
        return pltpu.make_async_copy(zeros, xs_hbm.at[pl.ds(tile * MOE_TILE, MOE_TILE)], zsem)

    @pl.when(i == 0)
    def _():
        zeros[...] = jnp.zeros_like(zeros)
        lax.fori_loop(0, N_EXP, fill_expert, 0)
        lax.fori_loop(nv_ref[0], n_tiles, lambda tile, c: (tail_tile_copy(tile).start(), c)[1], 0)

    @pl.when(i == n - 1)
    def _():
        pl.when(n >= 2)(lambda: wait_rows(1 - slot))
        wait_rows(slot)
        lax.fori_loop(0, N_EXP, drain_expert, 0)
        lax.fori_loop(nv_ref[0], n_tiles, lambda tile, c: (tail_tile_copy(tile).wait(), c)[1], 0)


def _dispatch_call(x, g, pos, n_valid, pad_meta, rows):
    t, d = x.shape
    grid_spec = pltpu.PrefetchScalarGridSpec(
        num_scalar_prefetch=3,
        grid=(t // ROW_TILE,),
        in_specs=[pl.BlockSpec((ROW_TILE, d), lambda i, pos_ref, nv, pad: (i, 0)),
                  pl.BlockSpec((1, d), lambda i, pos_ref, nv, pad: (0, 0))],
        out_specs=pl.BlockSpec(memory_space=pl.ANY),
        scratch_shapes=[pltpu.VMEM((2, ROW_TILE, d), F32), pltpu.VMEM((MOE_TILE, d), F32),
                        pltpu.SemaphoreType.DMA((2,)), pltpu.SemaphoreType.DMA(())],
    )
    return pl.pallas_call(
        _dispatch_kernel,
        grid_spec=grid_spec,
        out_shape=jax.ShapeDtypeStruct((rows, d), F32),
        compiler_params=_params(("arbitrary",), 32),
        name="moe_dispatch",
    )(pos, n_valid, pad_meta, x, g.reshape(1, d))


def _moe_ffn_kernel(tm_ref, nv_ref, x_ref, wgu_hbm, wd_hbm, o_ref,
                    wgu_f, wd_f, wsems, wgu_b, wd_b, *, layer):
    i = pl.program_id(0)
    nv = nv_ref[0]
    expert, first, wslot, next_expert = tm_ref[0, i], tm_ref[1, i], tm_ref[2, i], tm_ref[3, i]

    def weight_copies(e, s):
        return (pltpu.make_async_copy(wgu_hbm.at[layer, e], wgu_f.at[s], wsems.at[0, s]),
                pltpu.make_async_copy(wd_hbm.at[layer, e], wd_f.at[s], wsems.at[1, s]))

    @pl.when(i == 0)
    def _():
        for c in weight_copies(expert, 0):
            c.start()

    @pl.when(first == 1)
    def _():
        @pl.when(next_expert >= 0)
        def _():
            for c in weight_copies(next_expert, 1 - wslot):
                c.start()

        for c in weight_copies(expert, wslot):
            c.wait()
        wgu_b[...] = wgu_f[wslot].astype(BF16)
        wd_b[...] = wd_f[wslot].astype(BF16)

    @pl.when(i < nv)
    def _():
        h = jnp.dot(x_ref[...].astype(BF16), wgu_b[...], preferred_element_type=F32)
        hg = h[:, :D_FF_E]
        hu = h[:, D_FF_E:]
        act = (hg * _sigmoid(hg) * hu).astype(BF16)
        o_ref[...] = jnp.dot(act, wd_b[...], preferred_element_type=F32)

    @pl.when(i >= nv)
    def _():
        o_ref[...] = jnp.zeros_like(o_ref)


def _moe_ffn_call(xs, w_gu, w_down, layer, tile_meta, n_valid):
    rows, d = xs.shape
    n_tiles = tile_meta.shape[1]
    grid_spec = pltpu.PrefetchScalarGridSpec(
        num_scalar_prefetch=2,
        grid=(n_tiles,),
        in_specs=[pl.BlockSpec((MOE_TILE, d), lambda i, tm, nv: (jnp.minimum(i, nv[0] - 1), 0)),
                  pl.BlockSpec(memory_space=pl.ANY), pl.BlockSpec(memory_space=pl.ANY)],
        out_specs=pl.BlockSpec((MOE_TILE, d), lambda i, tm, nv: (i, 0)),
        scratch_shapes=[
            pltpu.VMEM((2, d, 2 * D_FF_E), F32),
            pltpu.VMEM((2, D_FF_E, d), F32),
            pltpu.SemaphoreType.DMA((2, 2)),
            pltpu.VMEM((d, 2 * D_FF_E), BF16),
            pltpu.VMEM((D_FF_E, d), BF16),
        ],
    )
    return pl.pallas_call(
        functools.partial(_moe_ffn_kernel, layer=layer),
        grid_spec=grid_spec,
        out_shape=jax.ShapeDtypeStruct((rows, d), F32),
        compiler_params=_params(("arbitrary",), 48),
        name="moe_experts",
    )(tile_meta, n_valid, xs, w_gu, w_down)


def _combine_kernel(pos_ref, x_ref, w_ref, y_hbm, g_ref, *rest, emit_x, norm_tiles):
    outs = rest[:int(emit_x) + len(norm_tiles)]
    buf, sems = rest[int(emit_x) + len(norm_tiles):]
    i = pl.program_id(0)
    n = pl.num_programs(0)
    slot = lax.rem(i, 2)
    tm = x_ref.shape[0]

    def gather(tile, dst_slot):
        n_tok = n * tm
        for r in range(tm):
            for kk in range(2):
                pltpu.make_async_copy(y_hbm.at[pl.ds(pos_ref[kk * n_tok + tile * tm + r], 1)],
                                      buf.at[dst_slot, kk, pl.ds(r, 1)], sems.at[dst_slot]).start(priority=kk)

    @pl.when(i == 0)
    def _():
        gather(0, 0)

    for s in range(2):
        pl.when(jnp.logical_and(i + 1 < n, slot == s))(functools.partial(gather, i + 1, 1 - s))

    for kk in range(2):
        pltpu.make_async_copy(y_hbm.at[pl.ds(0, tm)], buf.at[slot, kk], sems.at[slot]).wait()
    xo = x_ref[...] + (w_ref[:, 0:1] * buf[slot, 0] + w_ref[:, 1:2] * buf[slot, 1])
    norm_refs = outs[int(emit_x):]
    if emit_x:
        outs[0][...] = xo

    def write_norm(p):
        norm_refs[p][...] = _rms(xo, g_ref[...]).astype(norm_refs[p].dtype)

    _for_each_part(i, norm_tiles, write_norm)


def _combine_call(x, ws, y_rows, pos, g, emit_x, norm_dtype, norm_rows):
    t, d = x.shape
    tm = MOE_TILE
    row = lambda i, pos_ref: (i, 0)
    out_specs, out_shape = [], []
    if emit_x:
        out_specs.append(pl.BlockSpec((tm, d), row))
        out_shape.append(jax.ShapeDtypeStruct((t, d), F32))
    norm_parts = [jax.ShapeDtypeStruct((r, d), norm_dtype) for r in norm_rows]
    out_specs += _part_specs(norm_parts, tm, d, row)
    out_shape += norm_parts
    grid_spec = pltpu.PrefetchScalarGridSpec(
        num_scalar_prefetch=1,
        grid=(t // tm,),
        in_specs=[pl.BlockSpec((tm, d), row), pl.BlockSpec((tm, LANES), row), pl.BlockSpec(memory_space=pl.ANY),
                  pl.BlockSpec((1, d), lambda i, pos_ref: (0, 0))],
        out_specs=out_specs,
        scratch_shapes=[pltpu.VMEM((2, 2, tm, d), F32), pltpu.SemaphoreType.DMA((2,))],
    )
    return pl.pallas_call(
        functools.partial(_combine_kernel, emit_x=emit_x, norm_tiles=_part_tiles(norm_parts, tm)),
        grid_spec=grid_spec,
        out_shape=out_shape,
        compiler_params=_params(("arbitrary",), 40),
        name="moe_combine",
    )(pos, x, ws, y_rows, g.reshape(1, d))


def _moe_layer(x, g_ffn, w_grp, b_grp, w_exp, b_exp, w_gu, w_down, layer, g_next, emit_x, norm_dtype, norm_rows):
    ids, ws = _router_call(x, g_ffn, w_grp, b_grp, w_exp, b_exp)
    tile_meta, n_valid, pad_meta, pos = _moe_plan(ids[:, 0], ids[:, 1])
    xs = _dispatch_call(x, g_ffn, pos, n_valid, pad_meta, tile_meta.shape[1] * MOE_TILE)
    y_rows = _moe_ffn_call(xs, w_gu, w_down, layer, tile_meta, n_valid)
    return _combine_call(x, ws, y_rows, pos, g_next, emit_x, norm_dtype, norm_rows)


def _s5_prep_kernel(are_ref, aim_ref, ldt_ref, bre_ref, bim_ref, lre_ref, lim_ref, bbr_ref, bbi_ref):
    are = are_ref[...]
    aim = aim_ref[...]
    dt = jnp.exp(ldt_ref[...])
    mag = jnp.exp(are * dt)
    lr = mag * jnp.cos(aim * dt)
    li = mag * jnp.sin(aim * dt)
    lre_ref[...] = lr
    lim_ref[...] = li
    den = are * are + aim * aim
    nr = lr - 1.0
    cr = (nr * are + li * aim) / den
    ci = (li * are - nr * aim) / den
    bre = bre_ref[...]
    bim = bim_ref[...]
    bbr_ref[...] = cr * bre - ci * bim
    bbi_ref[...] = cr * bim + ci * bre


def _s5_prep_call(a_re, a_im, log_dt, b_re, b_im):
    g, p, c = b_re.shape
    shp = (g, p * c)
    expand = lambda a: jnp.broadcast_to(a[:, :, None], (g, p, c)).reshape(shp)
    ldt = jnp.broadcast_to(log_dt[:, None], shp)
    outs = pl.pallas_call(
        _s5_prep_kernel,
        out_shape=[jax.ShapeDtypeStruct(shp, F32)] * 4,
        name="s5_discretize",
    )(expand(a_re), expand(a_im), ldt, b_re.reshape(shp), b_im.reshape(shp))
    lre, lim, bbr, bbi = outs
    return lre[:, ::c], lim[:, ::c], bbr, bbi


def _s5_compact(bbr, bbi, c_re, c_im):
    def b_part(bb):
        return bb.reshape(S5_FB, 16, S5_P, S5_GC).transpose(0, 1, 3, 2).reshape(S5_FB, 16 * S5_GC, S5_P)

    def c_part(cc):
        return cc.reshape(S5_FB, 16, S5_GC, S5_P).transpose(0, 3, 1, 2).reshape(S5_FB, S5_P, 16 * S5_GC)

    return b_part(bbr), b_part(bbi), c_part(c_re), c_part(c_im)


def _s5_build_mats(bbr_ref, bbi_ref, ccr_ref, cci_ref, bd_ref, cd_ref):
    nch, nst = 16 * S5_GC, 16 * S5_P
    iota = lambda shape, dim: lax.broadcasted_iota(jnp.int32, shape, dim)
    rep = ((iota((S5_P, nst), 1) & (S5_P - 1)) == iota((S5_P, nst), 0)).astype(BF16)
    rep_t = ((iota((nst, S5_P), 0) & (S5_P - 1)) == iota((nst, S5_P), 1)).astype(BF16)
    gc_bits, p_bits = S5_GC.bit_length() - 1, S5_P.bit_length() - 1
    diag = (iota((nch, nst), 0) >> gc_bits) == (iota((nch, nst), 1) >> p_bits)
    diag_t = (iota((nst, nch), 0) >> p_bits) == (iota((nst, nch), 1) >> gc_bits)
    for fb in range(S5_FB):
        for part, src in enumerate((bbr_ref, bbi_ref)):
            full = jnp.dot(src[fb].astype(BF16), rep, preferred_element_type=F32)
            bd_ref[fb, :, part * nst:(part + 1) * nst] = jnp.where(diag, full, 0.0).astype(BF16)
        for part, (src, sign) in enumerate(((ccr_ref, 1.0), (cci_ref, -1.0))):
            full = jnp.dot(rep_t, src[fb].astype(BF16), preferred_element_type=F32)
            cd_ref[fb, part * nst:(part + 1) * nst, :] = jnp.where(diag_t, sign * full, 0.0).astype(BF16)


def _s5_kernel(u_ref, bbr_ref, bbi_ref, ccr_ref, cci_ref, lre_ref, lim_ref, d_ref, h0r_ref, h0i_ref,
               z_ref, hr_ref, hi_ref, scr, hst, bd_ref, cd_ref, *, seg, carry):
    rows = u_ref.shape[0]
    n_seq = rows // seg
    ch = 16 * S5_GC
    half = SUBLANES * S5_PITCH
    blk = 2 * half

    @pl.when(pl.program_id(0) == 0)
    def _():
        _s5_build_mats(bbr_ref, bbi_ref, ccr_ref, cci_ref, bd_ref, cd_ref)

    for fb in range(S5_FB):
        bu = jnp.dot(u_ref[:, fb * ch:(fb + 1) * ch].astype(BF16), bd_ref[fb], preferred_element_type=F32)
        for c in range(2 * SUBLANES):
            r0 = fb * blk + c * S5_PITCH
            scr[r0:r0 + rows, :] = bu[:, c * LANES:(c + 1) * LANES]

    lre = [lre_ref[fb * SUBLANES:(fb + 1) * SUBLANES, :] for fb in range(S5_FB)]
    lim = [lim_ref[fb * SUBLANES:(fb + 1) * SUBLANES, :] for fb in range(S5_FB)]

    def scan(row0, state):
        def step(l, st):
            new = []
            for fb in range(S5_FB):
                hr, hi = st[2 * fb], st[2 * fb + 1]
                ire = pl.ds(fb * blk + row0 + l, SUBLANES, stride=S5_PITCH)
                iim = pl.ds(fb * blk + half + row0 + l, SUBLANES, stride=S5_PITCH)
                nr = lre[fb] * hr - lim[fb] * hi + scr[ire, :]
                ni = lre[fb] * hi + lim[fb] * hr + scr[iim, :]
                scr[ire, :] = nr
                scr[iim, :] = ni
                new += [nr, ni]
            return tuple(new)
        return lax.fori_loop(0, seg, step, state, unroll=8)

    def load_state(rref, iref, lead):
        st = []
        for fb in range(S5_FB):
            st += [rref[lead + (slice(fb * SUBLANES, (fb + 1) * SUBLANES), slice(None))],
                   iref[lead + (slice(fb * SUBLANES, (fb + 1) * SUBLANES), slice(None))]]
        return tuple(st)

    def store_state(st, rref, iref, lead):
        for fb in range(S5_FB):
            rref[lead + (slice(fb * SUBLANES, (fb + 1) * SUBLANES), slice(None))] = st[2 * fb]
            iref[lead + (slice(fb * SUBLANES, (fb + 1) * SUBLANES), slice(None))] = st[2 * fb + 1]

    if carry:
        @pl.when(pl.program_id(0) == 0)
        def _():
            hst[0] = h0r_ref[0]
            hst[1] = h0i_ref[0]
        hre_scr = hst.at[0]
        him_scr = hst.at[1]
        st = scan(0, load_state(hre_scr, him_scr, ()))
        store_state(st, hre_scr, him_scr, ())
        store_state(st, hr_ref, hi_ref, (0,))
    else:
        for s in range(n_seq):
            st = scan(s * seg, load_state(h0r_ref, h0i_ref, (s,)))
            store_state(st, hr_ref, hi_ref, (s,))

    for fb in range(S5_FB):
        hf = jnp.concatenate(
            [scr[fb * blk + c * S5_PITCH: fb * blk + c * S5_PITCH + rows, :] for c in range(2 * SUBLANES)],
            axis=1).astype(BF16)
        y = jnp.dot(hf, cd_ref[fb], preferred_element_type=F32)
        cs = slice(fb * ch, (fb + 1) * ch)
        yy = y + d_ref[:, cs] * u_ref[:, cs]
        z_ref[:, cs] = jax.nn.gelu(yy).astype(z_ref.dtype)


def _s5_call(u, mats, lre, lim, d_skip, h0r, h0i, row_block0, n_blocks, seg, carry):
    t, d = u.shape
    ch, nst = 16 * S5_GC, 16 * S5_P
    n_seq = S5_ROWS // seg
    ns_total = h0r.shape[0]
    st_idx = (lambda i: (0, 0, 0)) if carry else (lambda i: (i, 0, 0))
    fixed2 = lambda i: (0, 0)
    fixed3 = lambda i: (0, 0, 0)
    st_rows = S5_FB * SUBLANES
    return pl.pallas_call(
        functools.partial(_s5_kernel, seg=seg, carry=carry),
        grid=(n_blocks,),
        in_specs=[
            pl.BlockSpec((S5_ROWS, d), lambda i: (row_block0 + i, 0)),
            *[pl.BlockSpec(m.shape, fixed3) for m in mats],
            pl.BlockSpec(lre.shape, fixed2),
            pl.BlockSpec(lim.shape, fixed2),
            pl.BlockSpec((1, d), fixed2),
            pl.BlockSpec((n_seq, st_rows, LANES), st_idx),
            pl.BlockSpec((n_seq, st_rows, LANES), st_idx),
        ],
        out_specs=[
            pl.BlockSpec((S5_ROWS, d), lambda i: (i, 0)),
            pl.BlockSpec((n_seq, st_rows, LANES), st_idx),
            pl.BlockSpec((n_seq, st_rows, LANES), st_idx),
        ],
        out_shape=[
            jax.ShapeDtypeStruct((n_blocks * S5_ROWS, d), BF16),
            jax.ShapeDtypeStruct((ns_total, st_rows, LANES), F32),
            jax.ShapeDtypeStruct((ns_total, st_rows, LANES), F32),
        ],
        scratch_shapes=[
            pltpu.VMEM((2 * SUBLANES * S5_FB * S5_PITCH, LANES), F32),
            pltpu.VMEM((2, st_rows, LANES), F32),
            pltpu.VMEM((S5_FB, ch, 2 * nst), BF16),
            pltpu.VMEM((S5_FB, 2 * nst, ch), BF16),
        ],
        compiler_params=_params(("arbitrary",), 56),
        name="s5_scan",
    )(u, *mats, lre, lim, d_skip.reshape(1, d), h0r, h0i)


def kernel(x_prompt, x_sample, state_ret, state_s5_re, state_s5_im, norm_mix_g, norm_ffn_g, norm_final_g,
           ret_w_in, ret_gn_g, ret_w_out, s5_w_in, s5_a_re, s5_a_im, s5_log_dt, s5_b_re, s5_b_im,
           s5_c_re, s5_c_im, s5_d, s5_w_out, moe_w_grp, moe_b_grp, moe_w_exp, moe_b_exp, moe_w_gu, moe_w_down):
    bp, seq, d = x_prompt.shape
    db, dseq, _ = x_sample.shape
    assert d == D_MODEL and bp == 1
    assert seq % max(ROW_TILE, RET_ROWS) == 0 and seq % CHUNK == 0
    assert dseq < CHUNK and dseq % 16 == 0 and S5_ROWS % dseq == 0
    tp = bp * seq
    ts = db * dseq
    t = tp + ts
    assert ts % ROW_TILE == 0 and ts % S5_ROWS == 0

    x_parts = [x_prompt.reshape(tp, d), x_sample.reshape(ts, d)]

    half = RET_DK // 2
    freqs = ROPE_BASE ** (-jnp.arange(half, dtype=F32) / half)
    pos = jnp.concatenate([jnp.arange(seq, dtype=F32), jnp.tile(PAST_LEN + jnp.arange(dseq, dtype=F32), db)])
    ang = pos[:, None] * freqs[None, :]
    cos_t = jnp.cos(ang)
    sin_t = jnp.sin(ang)

    xn = _norm_call(x_parts, norm_mix_g[0], BF16)
    tn = 2048
    first_col = lambda j: 0
    qk = _mm_call([xn], ret_w_in[0], [0], tn, 2 * D_MODEL, [([cos_t], half, first_col), ([sin_t], half, first_col)],
                  BF16, functools.partial(_epi_rope, n_q_blocks=D_MODEL // tn), 56, "ret_qk_proj")
    vg = _mm_call([xn], ret_w_in[0], [2 * D_MODEL // tn], tn, 2 * RET_VW, [], BF16, _epi_plain, 56, "ret_vg_proj")
    og_p, ret_p = _ret_prompt_call(qk, vg, ret_gn_g[0], tp)
    og_s, ret_s = _ret_sample_call(qk, vg, ret_gn_g[0], state_ret[0], tp, dseq)
    tn = 512
    same_col = lambda j: j
    x1 = _mm_call([og_p, og_s], ret_w_out[0], [0], tn, d, [(x_parts, tn, same_col)], F32, _epi_residual,
                  40, "ret_out_proj")

    x2, xn = _moe_layer(x1, norm_ffn_g[0], moe_w_grp[0], moe_b_grp[0], moe_w_exp[0], moe_b_exp[0],
                        moe_w_gu, moe_w_down, 0, norm_mix_g[1], True, BF16, (t,))

    u = _mm_call([xn], s5_w_in[0], [0], 2048, d, [], F32, _epi_plain, 56, "s5_in_proj")
    lre, lim, bbr, bbi = _s5_prep_call(s5_a_re[0], s5_a_im[0], s5_log_dt[0], s5_b_re[0], s5_b_im[0])
    mats = _s5_compact(bbr, bbi, s5_c_re[0], s5_c_im[0])
    st_shape = (S5_FB * SUBLANES, LANES)
    lre = lre.reshape(st_shape)
    lim = lim.reshape(st_shape)
    zero_state = jnp.zeros((1,) + st_shape, F32)
    z_p, hpr, hpi = _s5_call(u, mats, lre, lim, s5_d[0], zero_state, zero_state,
                             0, tp // S5_ROWS, S5_ROWS, True)
    z_s, hsr, hsi = _s5_call(u, mats, lre, lim, s5_d[0],
                             state_s5_re[0].reshape((db,) + st_shape), state_s5_im[0].reshape((db,) + st_shape),
                             tp // S5_ROWS, ts // S5_ROWS, dseq, False)
    x3 = _mm_call([z_p, z_s], s5_w_out[0], [0, d // tn], tn, d, [([x2], tn, same_col)], F32, _epi_glu_residual,
                  40, "s5_out_proj")

    y_p, y_s = _moe_layer(x3, norm_ffn_g[1], moe_w_grp[1], moe_b_grp[1], moe_w_exp[1], moe_b_exp[1],
                          moe_w_gu, moe_w_down, 1, norm_final_g, False, F32, (tp, ts))

    s5_shape = (1, -1, S5_GROUPS, S5_P)
    return (y_p.reshape(bp, seq, d), y_s.reshape(db, dseq, d),
            ret_p.reshape(1, bp, RET_HEADS, RET_DK, RET_DV), ret_s.reshape((1,) + ret_s.shape),
            hpr.reshape(s5_shape), hpi.reshape(s5_shape), hsr.reshape(s5_shape), hsi.reshape(s5_shape))
```

```python
import functools
import math

import numpy as np
import jax
import jax.numpy as jnp
from jax import lax
from jax.experimental import pallas as pl
from jax.experimental.pallas import tpu as pltpu

F32 = jnp.float32
BF16 = jnp.bfloat16

D_MODEL = 2048
PAST_LEN = 2048
CHUNK = 64
RET_HEADS = 8
RET_DK = D_MODEL // RET_HEADS
RET_DV = 2 * RET_DK
RET_VW = RET_HEADS * RET_DV
ROPE_BASE = 10000.0
S5_GC = 16
S5_GROUPS = D_MODEL // S5_GC
S5_P = 64
MOE_GROUPS = 4
MOE_PER_GROUP = 8
N_EXP = MOE_GROUPS * MOE_PER_GROUP
D_FF_E = 512
EPS = 1e-6

LANES = 128
SUBLANES = 8
ROW_TILE = 512
MM_CHUNK = 256
RET_ROWS = 256
RET_HPS = 4
MOE_TILE = 256
S5_ROWS = 256
S5_FB = S5_GROUPS // 16
S5_PITCH = S5_ROWS + 4
MIB = 1024 * 1024


def _params(sem, vmem_mib):
    return pltpu.CompilerParams(dimension_semantics=sem, vmem_limit_bytes=vmem_mib * MIB)


def _rms(x, g):
    ms = jnp.mean(x * x, axis=-1, keepdims=True)
    return x * lax.rsqrt(ms + EPS) * g


def _sigmoid(x):
    return 1.0 / (1.0 + jnp.exp(-x))


def _part_specs(parts, rows, cols, rowcol):
    specs = []
    start = 0
    for p in parts:
        nt = p.shape[0] // rows

        def imap(*g, start=start, nt=nt):
            i, jc = rowcol(*g)
            return (jnp.clip(i - start, 0, nt - 1), jc)

        specs.append(pl.BlockSpec((rows, cols), imap))
        start += nt
    return specs


def _part_tiles(parts, rows):
    return tuple(p.shape[0] // rows for p in parts)


def _for_each_part(i, tiles, fn):
    if len(tiles) == 1:
        fn(0)
        return
    start = 0
    for p, nt in enumerate(tiles):
        pl.when(jnp.logical_and(i >= start, i < start + nt))(functools.partial(fn, p))
        start += nt


def _pick(refs, p):
    return refs[min(p, len(refs) - 1)]


def _norm_kernel(*refs, tiles):
    n = len(tiles)
    x_refs, g_ref, o_ref = refs[:n], refs[n], refs[n + 1]

    def run(p):
        o_ref[...] = _rms(x_refs[p][...], g_ref[...]).astype(o_ref.dtype)

    _for_each_part(pl.program_id(0), tiles, run)


def _norm_call(x_parts, g, out_dtype):
    d = x_parts[0].shape[1]
    t = sum(p.shape[0] for p in x_parts)
    return pl.pallas_call(
        functools.partial(_norm_kernel, tiles=_part_tiles(x_parts, ROW_TILE)),
        grid=(t // ROW_TILE,),
        in_specs=_part_specs(x_parts, ROW_TILE, d, lambda i: (i, 0)) + [pl.BlockSpec((1, d), lambda i: (0, 0))],
        out_specs=pl.BlockSpec((ROW_TILE, d), lambda i: (i, 0)),
        out_shape=jax.ShapeDtypeStruct((t, d), out_dtype),
        compiler_params=_params(("arbitrary",), 32),
        name="rmsnorm",
    )(*x_parts, g.reshape(1, d))


def _mm_kernel(*refs, tiles, n_x, n_w, n_extra, epi):
    x_refs = refs[:n_x]
    w_refs = refs[n_x:n_x + n_w]
    pos = n_x + n_w
    extra = []
    for n in n_extra:
        extra.append(refs[pos:pos + n])
        pos += n
    o_ref = refs[pos]
    wb = refs[pos + 1:]

    @pl.when(pl.program_id(1) == 0)
    def _():
        for w_ref, b in zip(w_refs, wb):
            b[...] = w_ref[...].astype(BF16)

    def run(p):
        x = _pick(x_refs, p)[...]
        for c in range(o_ref.shape[1] // MM_CHUNK):
            cs = slice(c * MM_CHUNK, (c + 1) * MM_CHUNK)
            accs = [jnp.dot(x, b[:, cs], preferred_element_type=F32) for b in wb]
            epi(accs, [_pick(e, p) for e in extra], o_ref, cs)

    _for_each_part(pl.program_id(1), tiles, run)


def _mm_call(x_parts, w, w_col_offsets, tn, n_out, extras, out_dtype, epi, vmem_mib, name):
    k = x_parts[0].shape[1]
    t = sum(p.shape[0] for p in x_parts)
    n_w = len(w_col_offsets)
    split = [parts for parts in [x_parts] + [e[0] for e in extras] if len(parts) > 1]
    tiles = _part_tiles(split[0], ROW_TILE) if split else (t // ROW_TILE,)
    assert all(_part_tiles(parts, ROW_TILE) == tiles for parts in split)
    in_specs = _part_specs(x_parts, ROW_TILE, k, lambda j, i: (i, 0))
    for off in w_col_offsets:
        in_specs.append(pl.BlockSpec((k, tn), lambda j, i, off=off: (0, j + off)))
    extra_arrays = []
    for parts, cols, colfn in extras:
        in_specs += _part_specs(parts, ROW_TILE, cols, lambda j, i, colfn=colfn: (i, colfn(j)))
        extra_arrays += list(parts)
    return pl.pallas_call(
        functools.partial(_mm_kernel, tiles=tiles, n_x=len(x_parts), n_w=n_w,
                          n_extra=tuple(len(e[0]) for e in extras), epi=epi),
        grid=(n_out // tn, t // ROW_TILE),
        in_specs=in_specs,
        out_specs=pl.BlockSpec((ROW_TILE, tn), lambda j, i: (i, j)),
        out_shape=jax.ShapeDtypeStruct((t, n_out), out_dtype),
        scratch_shapes=[pltpu.VMEM((k, tn), BF16) for _ in range(n_w)],
        compiler_params=_params(("arbitrary", "arbitrary"), vmem_mib),
        name=name,
    )(*x_parts, *([w] * n_w), *extra_arrays)


def _epi_plain(accs, extra, o_ref, cs):
    o_ref[:, cs] = accs[0].astype(o_ref.dtype)


def _epi_residual(accs, extra, o_ref, cs):
    o_ref[:, cs] = (extra[0][:, cs] + accs[0]).astype(o_ref.dtype)


def _epi_glu_residual(accs, extra, o_ref, cs):
    a, gt = accs
    o_ref[:, cs] = (extra[0][:, cs] + a * _sigmoid(gt)).astype(o_ref.dtype)


def _epi_rope(accs, extra, o_ref, cs, *, n_q_blocks):
    acc = accs[0]
    assert acc.shape[1] == RET_DK
    cos_ref, sin_ref = extra
    half = RET_DK // 2
    scale = jnp.where(pl.program_id(0) >= n_q_blocks, RET_DK ** -0.5, 1.0).astype(F32)
    c = cos_ref[...]
    s = sin_ref[...]
    t1 = acc[:, :half]
    t2 = acc[:, half:]
    o_ref[:, cs.start:cs.start + half] = ((t1 * c - t2 * s) * scale).astype(o_ref.dtype)
    o_ref[:, cs.start + half:cs.stop] = ((t1 * s + t2 * c) * scale).astype(o_ref.dtype)


def _ret_tables(rows, chunk):
    lg = jnp.log(jnp.asarray(1.0 - 2.0 ** (-5.0 - np.arange(RET_HEADS)), dtype=F32))[:, None, None]
    n = jnp.arange(rows, dtype=F32)
    ci = np.arange(rows) // chunk
    same_or_earlier = jnp.asarray(ci[None, :] <= ci[:, None])
    dmat = jnp.where(same_or_earlier[None], jnp.exp(jnp.abs(n[:, None] - n[None, :])[None] * lg), 0.0)
    dq = jnp.exp((n + 1.0)[None, :, None] * lg) * jnp.ones((1, 1, LANES), F32)
    dk = jnp.exp((rows - 1.0 - n)[None, :, None] * lg) * jnp.ones((1, 1, LANES), F32)
    gr = jnp.exp(rows * lg) * jnp.ones((1, 1, RET_DV), F32)
    return dmat, dq, dk, gr


def _ret_block(q, k, v, g, s_prev, dmat, dq, dk, gr, gn):
    s = lax.dot_general(q, k, (((1,), (1,)), ((), ())), preferred_element_type=F32) * dmat
    inter = jnp.dot(q, s_prev.astype(BF16), preferred_element_type=F32)
    o = jnp.dot(s.astype(BF16), v, preferred_element_type=F32) + jnp.concatenate([dq] * (RET_DV // LANES), axis=1) * inter
    kd = (k.astype(F32) * jnp.concatenate([dk] * (RET_DK // LANES), axis=1)).astype(BF16)
    s_new = s_prev * gr + lax.dot_general(kd, v, (((0,), (0,)), ((), ())), preferred_element_type=F32)
    mu = jnp.mean(o, axis=-1, keepdims=True)
    oc = o - mu
    var = jnp.mean(oc * oc, axis=-1, keepdims=True)
    on = oc * lax.rsqrt(var + EPS) * gn
    gf = g.astype(F32)
    return (gf * _sigmoid(gf) * on).astype(BF16), s_new


def _ret_prompt_kernel(q_ref, k_ref, v_ref, g_ref, dm_ref, dq_ref, dk_ref, gr_ref, gn_ref,
                       og_ref, st_ref, s_scr):
    @pl.when(pl.program_id(1) == 0)
    def _():
        s_scr[...] = jnp.zeros_like(s_scr)

    for hh in range(RET_HPS):
        ks = slice(hh * RET_DK, (hh + 1) * RET_DK)
        vs = slice(hh * RET_DV, (hh + 1) * RET_DV)
        og, s_new = _ret_block(q_ref[:, ks], k_ref[:, ks], v_ref[:, vs], g_ref[:, vs], s_scr[hh],
                               dm_ref[hh], dq_ref[hh], dk_ref[hh], gr_ref[hh], gn_ref[:, vs])
        og_ref[:, vs] = og
        s_scr[hh] = s_new
        st_ref[hh] = s_new


def _ret_prompt_call(qk, vg, gn, t_prompt):
    dm, dq, dk, gr = _ret_tables(RET_ROWS, CHUNK)
    hps = RET_HPS
    hb = RET_HEADS // hps
    return pl.pallas_call(
        _ret_prompt_kernel,
        grid=(hb, t_prompt // RET_ROWS),
        in_specs=[
            pl.BlockSpec((RET_ROWS, hps * RET_DK), lambda hh, c: (c, hh)),
            pl.BlockSpec((RET_ROWS, hps * RET_DK), lambda hh, c: (c, hb + hh)),
            pl.BlockSpec((RET_ROWS, hps * RET_DV), lambda hh, c: (c, hh)),
            pl.BlockSpec((RET_ROWS, hps * RET_DV), lambda hh, c: (c, hb + hh)),
            pl.BlockSpec((hps, RET_ROWS, RET_ROWS), lambda hh, c: (hh, 0, 0)),
            pl.BlockSpec((hps, RET_ROWS, LANES), lambda hh, c: (hh, 0, 0)),
            pl.BlockSpec((hps, RET_ROWS, LANES), lambda hh, c: (hh, 0, 0)),
            pl.BlockSpec((hps, 1, RET_DV), lambda hh, c: (hh, 0, 0)),
            pl.BlockSpec((1, hps * RET_DV), lambda hh, c: (0, hh)),
        ],
        out_specs=[
            pl.BlockSpec((RET_ROWS, hps * RET_DV), lambda hh, c: (c, hh)),
            pl.BlockSpec((hps, RET_DK, RET_DV), lambda hh, c: (hh, 0, 0)),
        ],
        out_shape=[
            jax.ShapeDtypeStruct((t_prompt, RET_VW), BF16),
            jax.ShapeDtypeStruct((RET_HEADS, RET_DK, RET_DV), F32),
        ],
        scratch_shapes=[pltpu.VMEM((hps, RET_DK, RET_DV), F32)],
        compiler_params=_params(("arbitrary", "arbitrary"), 40),
        name="retention_prompt",
    )(qk, qk, vg, vg, dm, dq, dk, gr, gn.reshape(1, RET_VW))


def _ret_sample_kernel(q_ref, k_ref, v_ref, g_ref, s0_ref, dm_ref, dq_ref, dk_ref, gr_ref, gn_ref,
                       og_ref, st_ref):
    for hh in range(RET_HEADS):
        ks = slice(hh * RET_DK, (hh + 1) * RET_DK)
        vs = slice(hh * RET_DV, (hh + 1) * RET_DV)
        og, s_new = _ret_block(q_ref[:, ks], k_ref[:, ks], v_ref[:, vs], g_ref[:, vs], s0_ref[0, hh],
                               dm_ref[hh], dq_ref[hh], dk_ref[hh], gr_ref[hh], gn_ref[:, vs])
        og_ref[:, vs] = og
        st_ref[0, hh] = s_new


def _ret_sample_call(qk, vg, gn, state, t_prompt, dec_seq):
    dm, dq, dk, gr = _ret_tables(dec_seq, dec_seq)
    b = state.shape[0]
    r0 = t_prompt // dec_seq
    const3 = lambda bb: (0, 0, 0)
    return pl.pallas_call(
        _ret_sample_kernel,
        grid=(b,),
        in_specs=[
            pl.BlockSpec((dec_seq, D_MODEL), lambda bb: (r0 + bb, 0)),
            pl.BlockSpec((dec_seq, D_MODEL), lambda bb: (r0 + bb, 1)),
            pl.BlockSpec((dec_seq, RET_VW), lambda bb: (r0 + bb, 0)),
            pl.BlockSpec((dec_seq, RET_VW), lambda bb: (r0 + bb, 1)),
            pl.BlockSpec((1, RET_HEADS, RET_DK, RET_DV), lambda bb: (bb, 0, 0, 0)),
            pl.BlockSpec(dm.shape, const3),
            pl.BlockSpec(dq.shape, const3),
            pl.BlockSpec(dk.shape, const3),
            pl.BlockSpec(gr.shape, const3),
            pl.BlockSpec((1, RET_VW), lambda bb: (0, 0)),
        ],
        out_specs=[
            pl.BlockSpec((dec_seq, RET_VW), lambda bb: (bb, 0)),
            pl.BlockSpec((1, RET_HEADS, RET_DK, RET_DV), lambda bb: (bb, 0, 0, 0)),
        ],
        out_shape=[
            jax.ShapeDtypeStruct((b * dec_seq, RET_VW), BF16),
            jax.ShapeDtypeStruct(state.shape, F32),
        ],
        compiler_params=_params(("arbitrary",), 40),
        name="retention_sample",
    )(qk, qk, vg, vg, state, dm, dq, dk, gr, gn.reshape(1, RET_VW))


def _router_kernel(x_ref, g_ref, wr_ref, br_ref, id_ref, w_ref):
    xn = _rms(x_ref[...], g_ref[...])
    xh = xn.astype(BF16)
    xl = (xn - xh.astype(F32)).astype(BF16)
    w = wr_ref[...]
    wh = w.astype(BF16)
    wl = (w - wh.astype(F32)).astype(BF16)
    hi = jnp.dot(xh, jnp.concatenate([wh, wl], axis=1), preferred_element_type=F32)
    lg = (hi[:, :LANES] + hi[:, LANES:] + jnp.dot(xl, wh, preferred_element_type=F32)) + br_ref[...]
    lane = lax.broadcasted_iota(jnp.int32, lg.shape, 1).astype(F32)
    neg = jnp.float32(-jnp.inf)
    big = jnp.float32(LANES)
    gl = jnp.where(lane < MOE_GROUPS, lg, neg)
    gmax = jnp.max(gl, axis=-1, keepdims=True)
    gsel = jnp.min(jnp.where(gl == gmax, lane, big), axis=-1, keepdims=True)
    gw = 1.0 / jnp.sum(jnp.exp(gl - gmax), axis=-1, keepdims=True)
    lo = MOE_GROUPS + gsel * MOE_PER_GROUP
    el = jnp.where((lane >= lo) & (lane < lo + MOE_PER_GROUP), lg, neg)
    m1 = jnp.max(el, axis=-1, keepdims=True)
    i1 = jnp.min(jnp.where(el == m1, lane, big), axis=-1, keepdims=True)
    el2 = jnp.where(lane == i1, neg, el)
    m2 = jnp.max(el2, axis=-1, keepdims=True)
    i2 = jnp.min(jnp.where(el2 == m2, lane, big), axis=-1, keepdims=True)
    z = jnp.sum(jnp.exp(el - m1), axis=-1, keepdims=True)
    p1 = 1.0 / z
    p2 = jnp.exp(m2 - m1) / z
    tot = p1 + p2
    w1 = p1 / tot * gw
    w2 = p2 / tot * gw
    ids = jnp.where(lane == 0, i1 - MOE_GROUPS, jnp.where(lane == 1, i2 - MOE_GROUPS, 0.0))
    id_ref[...] = ids.astype(jnp.int32)
    w_ref[...] = jnp.where(lane == 0, w1, jnp.where(lane == 1, w2, 0.0))


def _router_call(x, g, w_grp, b_grp, w_exp, b_exp):
    t, d = x.shape
    pad = LANES - MOE_GROUPS - N_EXP
    wr = jnp.concatenate([w_grp, w_exp, jnp.zeros((d, pad), F32)], axis=1)
    br = jnp.concatenate([b_grp, b_exp, jnp.zeros((pad,), F32)]).reshape(1, LANES)
    row = lambda i: (i, 0)
    fixed = lambda i: (0, 0)
    return pl.pallas_call(
        _router_kernel,
        grid=(t // ROW_TILE,),
        in_specs=[pl.BlockSpec((ROW_TILE, d), row), pl.BlockSpec((1, d), fixed),
                  pl.BlockSpec((d, LANES), fixed), pl.BlockSpec((1, LANES), fixed)],
        out_specs=[pl.BlockSpec((ROW_TILE, LANES), row), pl.BlockSpec((ROW_TILE, LANES), row)],
        out_shape=[jax.ShapeDtypeStruct((t, LANES), jnp.int32), jax.ShapeDtypeStruct((t, LANES), F32)],
        compiler_params=_params(("arbitrary",), 40),
        name="moe_router",
    )(x, g.reshape(1, d), wr, br)


def _cumsum_rows(hot):
    t, n = hot.shape
    blk = LANES
    h3 = hot.reshape(t // blk, blk, n).astype(F32)
    tri = jnp.tril(jnp.ones((blk, blk), F32))
    inner = jnp.einsum("ij,bjn->bin", tri, h3).astype(jnp.int32)
    totals = inner[:, -1, :]
    offsets = jnp.cumsum(totals, axis=0) - totals
    return (inner + offsets[:, None, :]).reshape(t, n)


def _moe_plan(e0, e1):
    t = e0.shape[0]
    n_tiles = 2 * t // MOE_TILE + N_EXP
    experts = jnp.arange(N_EXP, dtype=jnp.int32)[None, :]
    hot0 = (e0[:, None] == experts).astype(jnp.int32)
    hot1 = (e1[:, None] == experts).astype(jnp.int32)
    csum0 = _cumsum_rows(hot0)
    csum1 = _cumsum_rows(hot1)
    counts0 = csum0[-1]
    counts = counts0 + csum1[-1]
    tiles_e = (counts + MOE_TILE - 1) // MOE_TILE
    tiles_end = jnp.cumsum(tiles_e)
    row_start = (tiles_end - tiles_e) * MOE_TILE
    pos0 = jnp.sum(hot0 * (row_start[None, :] + csum0 - 1), axis=1)
    pos1 = jnp.sum(hot1 * (row_start[None, :] + counts0[None, :] + csum1 - 1), axis=1)
    pos = jnp.concatenate([pos0, pos1])
    n_valid = tiles_end[-1]
    tile_id = jnp.minimum(jnp.arange(n_tiles, dtype=jnp.int32), n_valid - 1)
    tile_expert = jnp.sum((tile_id[:, None] >= tiles_end[None, :]).astype(jnp.int32), axis=1)
    tile_expert = jnp.minimum(tile_expert, N_EXP - 1)
    changed = jnp.concatenate([jnp.ones((1,), bool), tile_expert[1:] != tile_expert[:-1]])
    first = jnp.logical_and(changed, jnp.arange(n_tiles) < n_valid).astype(jnp.int32)
    wslot = (jnp.cumsum(first) - 1) % 2
    next_start = tiles_end[tile_expert]
    next_expert = jnp.where(next_start < n_valid, tile_expert[jnp.minimum(next_start, n_tiles - 1)], -1)
    tile_meta = jnp.stack([tile_expert, first, wslot, next_expert]).astype(jnp.int32)
    pad_meta = jnp.stack([row_start + counts, tiles_e * MOE_TILE - counts]).astype(jnp.int32)
    return tile_meta, n_valid.reshape(1).astype(jnp.int32), pad_meta, pos.astype(jnp.int32)


def _dispatch_kernel(pos_ref, nv_ref, pad_ref, x_ref, g_ref, xs_hbm, xnbuf, zeros, sems, zsem):
    i = pl.program_id(0)
    n = pl.num_programs(0)
    tm = x_ref.shape[0]
    n_tiles = xs_hbm.shape[0] // MOE_TILE
    n_tok = n * tm
    slot = lax.rem(i, 2)

    def wait_rows(s):
        for kk in range(2):
            pltpu.make_async_copy(xnbuf.at[s], xs_hbm.at[pl.ds(0, tm)], sems.at[s]).wait()

    pl.when(i >= 2)(lambda: wait_rows(slot))
    xnbuf[slot] = _rms(x_ref[...], g_ref[...])
    for r in range(tm):
        for kk in range(2):
            pltpu.make_async_copy(xnbuf.at[slot, pl.ds(r, 1)],
                                  xs_hbm.at[pl.ds(pos_ref[kk * n_tok + i * tm + r], 1)],
                                  sems.at[slot]).start(priority=kk)

    def zero_rows(start, n):
        return pltpu.make_async_copy(zeros.at[pl.ds(0, n)], xs_hbm.at[pl.ds(start, n)], zsem)

    block_sizes = [1 << b for b in range(MOE_TILE.bit_length() - 2, -1, -1)]

    def fill_expert(e, carry):
        start, length = pad_ref[0, e], pad_ref[1, e]
        head = jnp.bitwise_and(-start, SUBLANES - 1)
        lax.fori_loop(0, head, lambda j, c: (zero_rows(start + j, 1).start(), c)[1], 0)
        rest = length - head
        off = start + head
        for sz in block_sizes:
            if sz < SUBLANES:
                break
            take = jnp.bitwise_and(rest, sz) != 0
            pl.when(take)(lambda off=off, sz=sz: zero_rows(pl.multiple_of(off, SUBLANES), sz).start())
            off = off + jnp.where(take, sz, 0)
        return carry

    def drain_expert(e, carry):
        length = pad_ref[1, e]
        for sz in block_sizes:
            pl.when(jnp.bitwise_and(length, sz) != 0)(lambda sz=sz: zero_rows(0, sz).wait())
        return carry

    def tail_tile_copy(tile):
        return pltpu.make_async_copy(zeros, xs_hbm.at[pl.ds(tile * MOE_TILE, MOE_TILE)], zsem)

    @pl.when(i == 0)
    def _():
        zeros[...] = jnp.zeros_like(zeros)
        lax.fori_loop(0, N_EXP, fill_expert, 0)
        lax.fori_loop(nv_ref[0], n_tiles, lambda tile, c: (tail_tile_copy(tile).start(), c)[1], 0)

    @pl.when(i == n - 1)
    def _():
        pl.when(n >= 2)(lambda: wait_rows(1 - slot))
        wait_rows(slot)
        lax.fori_loop(0, N_EXP, drain_expert, 0)
        lax.fori_loop(nv_ref[0], n_tiles, lambda tile, c: (tail_tile_copy(tile).wait(), c)[1], 0)


def _dispatch_call(x, g, pos, n_valid, pad_meta, rows):
    t, d = x.shape
    grid_spec = pltpu.PrefetchScalarGridSpec(
        num_scalar_prefetch=3,
        grid=(t // ROW_TILE,),
        in_specs=[pl.BlockSpec((ROW_TILE, d), lambda i, pos_ref, nv, pad: (i, 0)),
                  pl.BlockSpec((1, d), lambda i, pos_ref, nv, pad: (0, 0))],
        out_specs=pl.BlockSpec(memory_space=pl.ANY),
        scratch_shapes=[pltpu.VMEM((2, ROW_TILE, d), F32), pltpu.VMEM((MOE_TILE, d), F32),
                        pltpu.SemaphoreType.DMA((2,)), pltpu.SemaphoreType.DMA(())],
    )
    return pl.pallas_call(
        _dispatch_kernel,
        grid_spec=grid_spec,
        out_shape=jax.ShapeDtypeStruct((rows, d), F32),
        compiler_params=_params(("arbitrary",), 32),
        name="moe_dispatch",
    )(pos, n_valid, pad_meta, x, g.reshape(1, d))


def _moe_ffn_kernel(tm_ref, nv_ref, x_ref, wgu_hbm, wd_hbm, o_ref,
                    wgu_f, wd_f, wsems, wgu_b, wd_b, *, layer):
    i = pl.program_id(0)
    nv = nv_ref[0]
    expert, first, wslot, next_expert = tm_ref[0, i], tm_ref[1, i], tm_ref[2, i], tm_ref[3, i]

    def weight_copies(e, s):
        return (pltpu.make_async_copy(wgu_hbm.at[layer, e], wgu_f.at[s], wsems.at[0, s]),
                pltpu.make_async_copy(wd_hbm.at[layer, e], wd_f.at[s], wsems.at[1, s]))

    @pl.when(i == 0)
    def _():
        for c in weight_copies(expert, 0):
            c.start()

    @pl.when(first == 1)
    def _():
        @pl.when(next_expert >= 0)
        def _():
            for c in weight_copies(next_expert, 1 - wslot):
                c.start()

        for c in weight_copies(expert, wslot):
            c.wait()
        wgu_b[...] = wgu_f[wslot].astype(BF16)
        wd_b[...] = wd_f[wslot].astype(BF16)

    @pl.when(i < nv)
    def _():
        h = jnp.dot(x_ref[...].astype(BF16), wgu_b[...], preferred_element_type=F32)
        hg = h[:, :D_FF_E]
        hu = h[:, D_FF_E:]
        act = (hg * _sigmoid(hg) * hu).astype(BF16)
        o_ref[...] = jnp.dot(act, wd_b[...], preferred_element_type=F32)


def _moe_ffn_call(xs, w_gu, w_down, layer, tile_meta, n_valid):
    rows, d = xs.shape
    n_tiles = tile_meta.shape[1]
    valid_tile = lambda i, tm, nv: (jnp.minimum(i, nv[0] - 1), 0)
    grid_spec = pltpu.PrefetchScalarGridSpec(
        num_scalar_prefetch=2,
        grid=(n_tiles,),
        in_specs=[pl.BlockSpec((MOE_TILE, d), valid_tile),
                  pl.BlockSpec(memory_space=pl.ANY), pl.BlockSpec(memory_space=pl.ANY)],
        out_specs=pl.BlockSpec((MOE_TILE, d), valid_tile),
        scratch_shapes=[
            pltpu.VMEM((2, d, 2 * D_FF_E), F32),
            pltpu.VMEM((2, D_FF_E, d), F32),
            pltpu.SemaphoreType.DMA((2, 2)),
            pltpu.VMEM((d, 2 * D_FF_E), BF16),
            pltpu.VMEM((D_FF_E, d), BF16),
        ],
    )
    return pl.pallas_call(
        functools.partial(_moe_ffn_kernel, layer=layer),
        grid_spec=grid_spec,
        out_shape=jax.ShapeDtypeStruct((rows, d), F32),
        input_output_aliases={2: 0},
        compiler_params=_params(("arbitrary",), 48),
        name="moe_experts",
    )(tile_meta, n_valid, xs, w_gu, w_down)


def _combine_kernel(pos_ref, x_ref, w_ref, y_hbm, g_ref, *rest, emit_x, norm_tiles):
    outs = rest[:int(emit_x) + len(norm_tiles)]
    buf, sems = rest[int(emit_x) + len(norm_tiles):]
    i = pl.program_id(0)
    n = pl.num_programs(0)
    slot = lax.rem(i, 2)
    tm = x_ref.shape[0]

    def gather(tile, dst_slot):
        n_tok = n * tm
        for r in range(tm):
            for kk in range(2):
                pltpu.make_async_copy(y_hbm.at[pl.ds(pos_ref[kk * n_tok + tile * tm + r], 1)],
                                      buf.at[dst_slot, kk, pl.ds(r, 1)], sems.at[dst_slot]).start(priority=kk)

    @pl.when(i == 0)
    def _():
        gather(0, 0)

    for s in range(2):
        pl.when(jnp.logical_and(i + 1 < n, slot == s))(functools.partial(gather, i + 1, 1 - s))

    for kk in range(2):
        pltpu.make_async_copy(y_hbm.at[pl.ds(0, tm)], buf.at[slot, kk], sems.at[slot]).wait()
    xo = x_ref[...] + (w_ref[:, 0:1] * buf[slot, 0] + w_ref[:, 1:2] * buf[slot, 1])
    norm_refs = outs[int(emit_x):]
    if emit_x:
        outs[0][...] = xo

    def write_norm(p):
        norm_refs[p][...] = _rms(xo, g_ref[...]).astype(norm_refs[p].dtype)

    _for_each_part(i, norm_tiles, write_norm)


def _combine_call(x, ws, y_rows, pos, g, emit_x, norm_dtype, norm_rows):
    t, d = x.shape
    tm = MOE_TILE
    row = lambda i, pos_ref: (i, 0)
    out_specs, out_shape = [], []
    if emit_x:
        out_specs.append(pl.BlockSpec((tm, d), row))
        out_shape.append(jax.ShapeDtypeStruct((t, d), F32))
    norm_parts = [jax.ShapeDtypeStruct((r, d), norm_dtype) for r in norm_rows]
    out_specs += _part_specs(norm_parts, tm, d, row)
    out_shape += norm_parts
    grid_spec = pltpu.PrefetchScalarGridSpec(
        num_scalar_prefetch=1,
        grid=(t // tm,),
        in_specs=[pl.BlockSpec((tm, d), row), pl.BlockSpec((tm, LANES), row), pl.BlockSpec(memory_space=pl.ANY),
                  pl.BlockSpec((1, d), lambda i, pos_ref: (0, 0))],
        out_specs=out_specs,
        scratch_shapes=[pltpu.VMEM((2, 2, tm, d), F32), pltpu.SemaphoreType.DMA((2,))],
    )
    return pl.pallas_call(
        functools.partial(_combine_kernel, emit_x=emit_x, norm_tiles=_part_tiles(norm_parts, tm)),
        grid_spec=grid_spec,
        out_shape=out_shape,
        compiler_params=_params(("arbitrary",), 40),
        name="moe_combine",
    )(pos, x, ws, y_rows, g.reshape(1, d))


def _moe_layer(x, g_ffn, w_grp, b_grp, w_exp, b_exp, w_gu, w_down, layer, g_next, emit_x, norm_dtype, norm_rows):
    ids, ws = _router_call(x, g_ffn, w_grp, b_grp, w_exp, b_exp)
    tile_meta, n_valid, pad_meta, pos = _moe_plan(ids[:, 0], ids[:, 1])
    xs = _dispatch_call(x, g_ffn, pos, n_valid, pad_meta, tile_meta.shape[1] * MOE_TILE)
    y_rows = _moe_ffn_call(xs, w_gu, w_down, layer, tile_meta, n_valid)
    return _combine_call(x, ws, y_rows, pos, g_next, emit_x, norm_dtype, norm_rows)


def _s5_prep_kernel(are_ref, aim_ref, ldt_ref, bre_ref, bim_ref, lre_ref, lim_ref, bbr_ref, bbi_ref):
    are = are_ref[...]
    aim = aim_ref[...]
    dt = jnp.exp(ldt_ref[...])
    mag = jnp.exp(are * dt)
    lr = mag * jnp.cos(aim * dt)
    li = mag * jnp.sin(aim * dt)
    lre_ref[...] = lr
    lim_ref[...] = li
    den = are * are + aim * aim
    nr = lr - 1.0
    cr = (nr * are + li * aim) / den
    ci = (li * are - nr * aim) / den
    bre = bre_ref[...]
    bim = bim_ref[...]
    bbr_ref[...] = cr * bre - ci * bim
    bbi_ref[...] = cr * bim + ci * bre


def _s5_prep_call(a_re, a_im, log_dt, b_re, b_im):
    g, p, c = b_re.shape
    shp = (g, p * c)
    expand = lambda a: jnp.broadcast_to(a[:, :, None], (g, p, c)).reshape(shp)
    ldt = jnp.broadcast_to(log_dt[:, None], shp)
    outs = pl.pallas_call(
        _s5_prep_kernel,
        out_shape=[jax.ShapeDtypeStruct(shp, F32)] * 4,
        name="s5_discretize",
    )(expand(a_re), expand(a_im), ldt, b_re.reshape(shp), b_im.reshape(shp))
    lre, lim, bbr, bbi = outs
    return lre[:, ::c], lim[:, ::c], bbr, bbi


def _s5_compact(bbr, bbi, c_re, c_im):
    def b_part(bb):
        return bb.reshape(S5_FB, 16, S5_P, S5_GC).transpose(0, 1, 3, 2).reshape(S5_FB, 16 * S5_GC, S5_P)

    def c_part(cc):
        return cc.reshape(S5_FB, 16, S5_GC, S5_P).transpose(0, 3, 1, 2).reshape(S5_FB, S5_P, 16 * S5_GC)

    return b_part(bbr), b_part(bbi), c_part(c_re), c_part(c_im)


def _s5_build_mats(bbr_ref, bbi_ref, ccr_ref, cci_ref, bd_ref, cd_ref):
    nch, nst = 16 * S5_GC, 16 * S5_P
    iota = lambda shape, dim: lax.broadcasted_iota(jnp.int32, shape, dim)
    rep = ((iota((S5_P, nst), 1) & (S5_P - 1)) == iota((S5_P, nst), 0)).astype(BF16)
    rep_t = ((iota((nst, S5_P), 0) & (S5_P - 1)) == iota((nst, S5_P), 1)).astype(BF16)
    gc_bits, p_bits = S5_GC.bit_length() - 1, S5_P.bit_length() - 1
    diag = (iota((nch, nst), 0) >> gc_bits) == (iota((nch, nst), 1) >> p_bits)
    diag_t = (iota((nst, nch), 0) >> p_bits) == (iota((nst, nch), 1) >> gc_bits)
    for fb in range(S5_FB):
        for part, src in enumerate((bbr_ref, bbi_ref)):
            full = jnp.dot(src[fb].astype(BF16), rep, preferred_element_type=F32)
            bd_ref[fb, :, part * nst:(part + 1) * nst] = jnp.where(diag, full, 0.0).astype(BF16)
        for part, (src, sign) in enumerate(((ccr_ref, 1.0), (cci_ref, -1.0))):
            full = jnp.dot(rep_t, src[fb].astype(BF16), preferred_element_type=F32)
            cd_ref[fb, part * nst:(part + 1) * nst, :] = jnp.where(diag_t, sign * full, 0.0).astype(BF16)


def _s5_kernel(u_ref, bbr_ref, bbi_ref, ccr_ref, cci_ref, lre_ref, lim_ref, d_ref, h0r_ref, h0i_ref,
               z_ref, hr_ref, hi_ref, scr, hst, bd_ref, cd_ref, *, seg, carry):
    rows = u_ref.shape[0]
    n_seq = rows // seg
    ch = 16 * S5_GC
    half = SUBLANES * S5_PITCH
    blk = 2 * half

    @pl.when(pl.program_id(0) == 0)
    def _():
        _s5_build_mats(bbr_ref, bbi_ref, ccr_ref, cci_ref, bd_ref, cd_ref)

    for fb in range(S5_FB):
        bu = jnp.dot(u_ref[:, fb * ch:(fb + 1) * ch].astype(BF16), bd_ref[fb], preferred_element_type=F32)
        for c in range(2 * SUBLANES):
            r0 = fb * blk + c * S5_PITCH
            scr[r0:r0 + rows, :] = bu[:, c * LANES:(c + 1) * LANES]

    lre = [lre_ref[fb * SUBLANES:(fb + 1) * SUBLANES, :] for fb in range(S5_FB)]
    lim = [lim_ref[fb * SUBLANES:(fb + 1) * SUBLANES, :] for fb in range(S5_FB)]

    def scan(row0, state):
        def step(l, st):
            new = []
            for fb in range(S5_FB):
                hr, hi = st[2 * fb], st[2 * fb + 1]
                ire = pl.ds(fb * blk + row0 + l, SUBLANES, stride=S5_PITCH)
                iim = pl.ds(fb * blk + half + row0 + l, SUBLANES, stride=S5_PITCH)
                nr = lre[fb] * hr - lim[fb] * hi + scr[ire, :]
                ni = lre[fb] * hi + lim[fb] * hr + scr[iim, :]
                scr[ire, :] = nr
                scr[iim, :] = ni
                new += [nr, ni]
            return tuple(new)
        return lax.fori_loop(0, seg, step, state, unroll=8)

    def load_state(rref, iref, lead):
        st = []
        for fb in range(S5_FB):
            st += [rref[lead + (slice(fb * SUBLANES, (fb + 1) * SUBLANES), slice(None))],
                   iref[lead + (slice(fb * SUBLANES, (fb + 1) * SUBLANES), slice(None))]]
        return tuple(st)

    def store_state(st, rref, iref, lead):
        for fb in range(S5_FB):
            rref[lead + (slice(fb * SUBLANES, (fb + 1) * SUBLANES), slice(None))] = st[2 * fb]
            iref[lead + (slice(fb * SUBLANES, (fb + 1) * SUBLANES), slice(None))] = st[2 * fb + 1]

    if carry:
        @pl.when(pl.program_id(0) == 0)
        def _():
            hst[0] = h0r_ref[0]
            hst[1] = h0i_ref[0]
        hre_scr = hst.at[0]
        him_scr = hst.at[1]
        st = scan(0, load_state(hre_scr, him_scr, ()))
        store_state(st, hre_scr, him_scr, ())
        store_state(st, hr_ref, hi_ref, (0,))
    else:
        for s in range(n_seq):
            st = scan(s * seg, load_state(h0r_ref, h0i_ref, (s,)))
            store_state(st, hr_ref, hi_ref, (s,))

    for fb in range(S5_FB):
        hf = jnp.concatenate(
            [scr[fb * blk + c * S5_PITCH: fb * blk + c * S5_PITCH + rows, :] for c in range(2 * SUBLANES)],
            axis=1).astype(BF16)
        y = jnp.dot(hf, cd_ref[fb], preferred_element_type=F32)
        cs = slice(fb * ch, (fb + 1) * ch)
        yy = y + d_ref[:, cs] * u_ref[:, cs]
        z_ref[:, cs] = jax.nn.gelu(yy).astype(z_ref.dtype)


def _s5_call(u, mats, lre, lim, d_skip, h0r, h0i, row_block0, n_blocks, seg, carry):
    t, d = u.shape
    ch, nst = 16 * S5_GC, 16 * S5_P
    n_seq = S5_ROWS // seg
    ns_total = h0r.shape[0]
    st_idx = (lambda i: (0, 0, 0)) if carry else (lambda i: (i, 0, 0))
    fixed2 = lambda i: (0, 0)
    fixed3 = lambda i: (0, 0, 0)
    st_rows = S5_FB * SUBLANES
    return pl.pallas_call(
        functools.partial(_s5_kernel, seg=seg, carry=carry),
        grid=(n_blocks,),
        in_specs=[
            pl.BlockSpec((S5_ROWS, d), lambda i: (row_block0 + i, 0)),
            *[pl.BlockSpec(m.shape, fixed3) for m in mats],
            pl.BlockSpec(lre.shape, fixed2),
            pl.BlockSpec(lim.shape, fixed2),
            pl.BlockSpec((1, d), fixed2),
            pl.BlockSpec((n_seq, st_rows, LANES), st_idx),
            pl.BlockSpec((n_seq, st_rows, LANES), st_idx),
        ],
        out_specs=[
            pl.BlockSpec((S5_ROWS, d), lambda i: (i, 0)),
            pl.BlockSpec((n_seq, st_rows, LANES), st_idx),
            pl.BlockSpec((n_seq, st_rows, LANES), st_idx),
        ],
        out_shape=[
            jax.ShapeDtypeStruct((n_blocks * S5_ROWS, d), BF16),
            jax.ShapeDtypeStruct((ns_total, st_rows, LANES), F32),
            jax.ShapeDtypeStruct((ns_total, st_rows, LANES), F32),
        ],
        scratch_shapes=[
            pltpu.VMEM((2 * SUBLANES * S5_FB * S5_PITCH, LANES), F32),
            pltpu.VMEM((2, st_rows, LANES), F32),
            pltpu.VMEM((S5_FB, ch, 2 * nst), BF16),
            pltpu.VMEM((S5_FB, 2 * nst, ch), BF16),
        ],
        compiler_params=_params(("arbitrary",), 56),
        name="s5_scan",
    )(u, *mats, lre, lim, d_skip.reshape(1, d), h0r, h0i)


def kernel(x_prompt, x_sample, state_ret, state_s5_re, state_s5_im, norm_mix_g, norm_ffn_g, norm_final_g,
           ret_w_in, ret_gn_g, ret_w_out, s5_w_in, s5_a_re, s5_a_im, s5_log_dt, s5_b_re, s5_b_im,
           s5_c_re, s5_c_im, s5_d, s5_w_out, moe_w_grp, moe_b_grp, moe_w_exp, moe_b_exp, moe_w_gu, moe_w_down):
    bp, seq, d = x_prompt.shape
    db, dseq, _ = x_sample.shape
    assert d == D_MODEL and bp == 1
    assert seq % max(ROW_TILE, RET_ROWS) == 0 and seq % CHUNK == 0
    assert dseq < CHUNK and dseq % 16 == 0 and S5_ROWS % dseq == 0
    tp = bp * seq
    ts = db * dseq
    t = tp + ts
    assert ts % ROW_TILE == 0 and ts % S5_ROWS == 0

    x_parts = [x_prompt.reshape(tp, d), x_sample.reshape(ts, d)]

    half = RET_DK // 2
    freqs = ROPE_BASE ** (-jnp.arange(half, dtype=F32) / half)
    pos = jnp.concatenate([jnp.arange(seq, dtype=F32), jnp.tile(PAST_LEN + jnp.arange(dseq, dtype=F32), db)])
    ang = pos[:, None] * freqs[None, :]
    cos_t = jnp.cos(ang)
    sin_t = jnp.sin(ang)

    xn = _norm_call(x_parts, norm_mix_g[0], BF16)
    tn = 2048
    first_col = lambda j: 0
    qk = _mm_call([xn], ret_w_in[0], [0], tn, 2 * D_MODEL, [([cos_t], half, first_col), ([sin_t], half, first_col)],
                  BF16, functools.partial(_epi_rope, n_q_blocks=D_MODEL // tn), 56, "ret_qk_proj")
    vg = _mm_call([xn], ret_w_in[0], [2 * D_MODEL // tn], tn, 2 * RET_VW, [], BF16, _epi_plain, 56, "ret_vg_proj")
    og_p, ret_p = _ret_prompt_call(qk, vg, ret_gn_g[0], tp)
    og_s, ret_s = _ret_sample_call(qk, vg, ret_gn_g[0], state_ret[0], tp, dseq)
    tn = 512
    same_col = lambda j: j
    x1 = _mm_call([og_p, og_s], ret_w_out[0], [0], tn, d, [(x_parts, tn, same_col)], F32, _epi_residual,
                  40, "ret_out_proj")

    x2, xn = _moe_layer(x1, norm_ffn_g[0], moe_w_grp[0], moe_b_grp[0], moe_w_exp[0], moe_b_exp[0],
                        moe_w_gu, moe_w_down, 0, norm_mix_g[1], True, BF16, (t,))

    u = _mm_call([xn], s5_w_in[0], [0], 2048, d, [], F32, _epi_plain, 56, "s5_in_proj")
    lre, lim, bbr, bbi = _s5_prep_call(s5_a_re[0], s5_a_im[0], s5_log_dt[0], s5_b_re[0], s5_b_im[0])
    mats = _s5_compact(bbr, bbi, s5_c_re[0], s5_c_im[0])
    st_shape = (S5_FB * SUBLANES, LANES)
    lre = lre.reshape(st_shape)
    lim = lim.reshape(st_shape)
    zero_state = jnp.zeros((1,) + st_shape, F32)
    z_p, hpr, hpi = _s5_call(u, mats, lre, lim, s5_d[0], zero_state, zero_state,
                             0, tp // S5_ROWS, S5_ROWS, True)
    z_s, hsr, hsi = _s5_call(u, mats, lre, lim, s5_d[0],
                             state_s5_re[0].reshape((db,) + st_shape), state_s5_im[0].reshape((db,) + st_shape),
                             tp // S5_ROWS, ts // S5_ROWS, dseq, False)
    x3 = _mm_call([z_p, z_s], s5_w_out[0], [0, d // tn], tn, d, [([x2], tn, same_col)], F32, _epi_glu_residual,
                  40, "s5_out_proj")

    y_p, y_s = _moe_layer(x3, norm_ffn_g[1], moe_w_grp[1], moe_b_grp[1], moe_w_exp[1], moe_b_exp[1],
                          moe_w_gu, moe_w_down, 1, norm_final_g, False, F32, (tp, ts))

    s5_shape = (1, -1, S5_GROUPS, S5_P)
    return (y_p.reshape(bp, seq, d), y_s.reshape(db, dseq, d),
            ret_p.reshape(1, bp, RET_HEADS, RET_DK, RET_DV), ret_s.reshape((1,) + ret_s.shape),
            hpr.reshape(s5_shape), hpi.reshape(s5_shape), hsr.reshape(s5_shape), hsi.reshape(s5_shape))
```

```python
import functools
import math

import numpy as np
import jax
import jax.numpy as jnp
from jax import lax
from jax.experimental import pallas as pl
from jax.experimental.pallas import tpu as pltpu

F32 = jnp.float32
BF16 = jnp.bfloat16

D_MODEL = 2048
PAST_LEN = 2048
CHUNK = 64
RET_HEADS = 8
RET_DK = D_MODEL // RET_HEADS
RET_DV = 2 * RET_DK
RET_VW = RET_HEADS * RET_DV
ROPE_BASE = 10000.0
S5_GC = 16
S5_GROUPS = D_MODEL // S5_GC
S5_P = 64
MOE_GROUPS = 4
MOE_PER_GROUP = 8
N_EXP = MOE_GROUPS * MOE_PER_GROUP
D_FF_E = 512
EPS = 1e-6

LANES = 128
SUBLANES = 8
ROW_TILE = 512
MM_CHUNK = 256
RET_ROWS = 256
RET_HPS = 4
MOE_TILE = 256
S5_ROWS = 256
S5_FB = S5_GROUPS // 16
S5_PITCH = S5_ROWS + 4
MIB = 1024 * 1024


def _params(sem, vmem_mib):
    return pltpu.CompilerParams(dimension_semantics=sem, vmem_limit_bytes=vmem_mib * MIB)


def _rms(x, g):
    ms = jnp.mean(x * x, axis=-1, keepdims=True)
    return x * lax.rsqrt(ms + EPS) * g


def _sigmoid(x):
    return 1.0 / (1.0 + jnp.exp(-x))


def _part_specs(parts, rows, cols, rowcol):
    specs = []
    start = 0
    for p in parts:
        nt = p.shape[0] // rows

        def imap(*g, start=start, nt=nt):
            i, jc = rowcol(*g)
            return (jnp.clip(i - start, 0, nt - 1), jc)

        specs.append(pl.BlockSpec((rows, cols), imap))
        start += nt
    return specs


def _part_tiles(parts, rows):
    return tuple(p.shape[0] // rows for p in parts)


def _for_each_part(i, tiles, fn):
    if len(tiles) == 1:
        fn(0)
        return
    start = 0
    for p, nt in enumerate(tiles):
        pl.when(jnp.logical_and(i >= start, i < start + nt))(functools.partial(fn, p))
        start += nt


def _pick(refs, p):
    return refs[min(p, len(refs) - 1)]


def _norm_kernel(*refs, tiles):
    n = len(tiles)
    x_refs, g_ref, o_ref = refs[:n], refs[n], refs[n + 1]

    def run(p):
        o_ref[...] = _rms(x_refs[p][...], g_ref[...]).astype(o_ref.dtype)

    _for_each_part(pl.program_id(0), tiles, run)


def _norm_call(x_parts, g, out_dtype):
    d = x_parts[0].shape[1]
    t = sum(p.shape[0] for p in x_parts)
    return pl.pallas_call(
        functools.partial(_norm_kernel, tiles=_part_tiles(x_parts, ROW_TILE)),
        grid=(t // ROW_TILE,),
        in_specs=_part_specs(x_parts, ROW_TILE, d, lambda i: (i, 0)) + [pl.BlockSpec((1, d), lambda i: (0, 0))],
        out_specs=pl.BlockSpec((ROW_TILE, d), lambda i: (i, 0)),
        out_shape=jax.ShapeDtypeStruct((t, d), out_dtype),
        compiler_params=_params(("arbitrary",), 32),
        name="rmsnorm",
    )(*x_parts, g.reshape(1, d))


def _mm_kernel(*refs, tiles, n_x, n_w, n_extra, epi):
    x_refs = refs[:n_x]
    w_refs = refs[n_x:n_x + n_w]
    pos = n_x + n_w
    extra = []
    for n in n_extra:
        extra.append(refs[pos:pos + n])
        pos += n
    o_ref = refs[pos]
    wb = refs[pos + 1:]

    @pl.when(pl.program_id(1) == 0)
    def _():
        for w_ref, b in zip(w_refs, wb):
            b[...] = w_ref[...].astype(BF16)

    def run(p):
        x = _pick(x_refs, p)[...]
        for c in range(o_ref.shape[1] // MM_CHUNK):
            cs = slice(c * MM_CHUNK, (c + 1) * MM_CHUNK)
            accs = [jnp.dot(x, b[:, cs], preferred_element_type=F32) for b in wb]
            epi(accs, [_pick(e, p) for e in extra], o_ref, cs)

    _for_each_part(pl.program_id(1), tiles, run)


def _mm_call(x_parts, w, w_col_offsets, tn, n_out, extras, out_dtype, epi, vmem_mib, name):
    k = x_parts[0].shape[1]
    t = sum(p.shape[0] for p in x_parts)
    n_w = len(w_col_offsets)
    split = [parts for parts in [x_parts] + [e[0] for e in extras] if len(parts) > 1]
    tiles = _part_tiles(split[0], ROW_TILE) if split else (t // ROW_TILE,)
    assert all(_part_tiles(parts, ROW_TILE) == tiles for parts in split)
    in_specs = _part_specs(x_parts, ROW_TILE, k, lambda j, i: (i, 0))
    for off in w_col_offsets:
        in_specs.append(pl.BlockSpec((k, tn), lambda j, i, off=off: (0, j + off)))
    extra_arrays = []
    for parts, cols, colfn in extras:
        in_specs += _part_specs(parts, ROW_TILE, cols, lambda j, i, colfn=colfn: (i, colfn(j)))
        extra_arrays += list(parts)
    return pl.pallas_call(
        functools.partial(_mm_kernel, tiles=tiles, n_x=len(x_parts), n_w=n_w,
                          n_extra=tuple(len(e[0]) for e in extras), epi=epi),
        grid=(n_out // tn, t // ROW_TILE),
        in_specs=in_specs,
        out_specs=pl.BlockSpec((ROW_TILE, tn), lambda j, i: (i, j)),
        out_shape=jax.ShapeDtypeStruct((t, n_out), out_dtype),
        scratch_shapes=[pltpu.VMEM((k, tn), BF16) for _ in range(n_w)],
        compiler_params=_params(("arbitrary", "arbitrary"), vmem_mib),
        name=name,
    )(*x_parts, *([w] * n_w), *extra_arrays)


def _mm_resident_kernel(*refs, tiles, n_x, n_w, n_extra, epi):
    x_refs = refs[:n_x]
    w_refs = refs[n_x:n_x + n_w]
    pos = n_x + n_w
    extra = []
    for n in n_extra:
        extra.append(refs[pos:pos + n])
        pos += n
    o_ref = refs[pos]
    wb = refs[pos + 1:]
    i = pl.program_id(0)
    j = pl.program_id(1)

    @pl.when(i == 0)
    def _():
        for w_ref, b in zip(w_refs, wb):
            b[j] = w_ref[...].astype(BF16)

    def run(p):
        x = _pick(x_refs, p)[...]
        for c in range(o_ref.shape[1] // MM_CHUNK):
            cs = slice(c * MM_CHUNK, (c + 1) * MM_CHUNK)
            accs = [jnp.dot(x, b[j, :, cs], preferred_element_type=F32) for b in wb]
            epi(accs, [_pick(e, p) for e in extra], o_ref, cs)

    _for_each_part(i, tiles, run)


def _mm_resident_call(x_parts, w, w_col_offsets, tn, n_out, extras, out_dtype, epi, vmem_mib, name):
    k = x_parts[0].shape[1]
    t = sum(p.shape[0] for p in x_parts)
    n_w = len(w_col_offsets)
    nj = n_out // tn
    split = [parts for parts in [x_parts] + [e[0] for e in extras] if len(parts) > 1]
    tiles = _part_tiles(split[0], ROW_TILE) if split else (t // ROW_TILE,)
    assert all(_part_tiles(parts, ROW_TILE) == tiles for parts in split)
    in_specs = _part_specs(x_parts, ROW_TILE, k, lambda i, j: (i, 0))
    for off in w_col_offsets:
        in_specs.append(pl.BlockSpec((k, tn), lambda i, j, off=off: (0, jnp.where(i == 0, j, nj - 1) + off)))
    extra_arrays = []
    for parts, cols, colfn in extras:
        in_specs += _part_specs(parts, ROW_TILE, cols, lambda i, j, colfn=colfn: (i, colfn(j)))
        extra_arrays += list(parts)
    return pl.pallas_call(
        functools.partial(_mm_resident_kernel, tiles=tiles, n_x=len(x_parts), n_w=n_w,
                          n_extra=tuple(len(e[0]) for e in extras), epi=epi),
        grid=(t // ROW_TILE, nj),
        in_specs=in_specs,
        out_specs=pl.BlockSpec((ROW_TILE, tn), lambda i, j: (i, j)),
        out_shape=jax.ShapeDtypeStruct((t, n_out), out_dtype),
        scratch_shapes=[pltpu.VMEM((nj, k, tn), BF16) for _ in range(n_w)],
        compiler_params=_params(("arbitrary", "arbitrary"), vmem_mib),
        name=name,
    )(*x_parts, *([w] * n_w), *extra_arrays)


def _epi_plain(accs, extra, o_ref, cs):
    o_ref[:, cs] = accs[0].astype(o_ref.dtype)


def _epi_residual(accs, extra, o_ref, cs):
    o_ref[:, cs] = (extra[0][:, cs] + accs[0]).astype(o_ref.dtype)


def _epi_glu_residual(accs, extra, o_ref, cs):
    a, gt = accs
    o_ref[:, cs] = (extra[0][:, cs] + a * _sigmoid(gt)).astype(o_ref.dtype)


def _epi_rope(accs, extra, o_ref, cs, *, n_q_blocks):
    acc = accs[0]
    assert acc.shape[1] == RET_DK
    cos_ref, sin_ref = extra
    half = RET_DK // 2
    scale = jnp.where(pl.program_id(0) >= n_q_blocks, RET_DK ** -0.5, 1.0).astype(F32)
    c = cos_ref[...]
    s = sin_ref[...]
    t1 = acc[:, :half]
    t2 = acc[:, half:]
    o_ref[:, cs.start:cs.start + half] = ((t1 * c - t2 * s) * scale).astype(o_ref.dtype)
    o_ref[:, cs.start + half:cs.stop] = ((t1 * s + t2 * c) * scale).astype(o_ref.dtype)


def _ret_tables(rows, chunk):
    lg = jnp.log(jnp.asarray(1.0 - 2.0 ** (-5.0 - np.arange(RET_HEADS)), dtype=F32))[:, None, None]
    n = jnp.arange(rows, dtype=F32)
    ci = np.arange(rows) // chunk
    same_or_earlier = jnp.asarray(ci[None, :] <= ci[:, None])
    dmat = jnp.where(same_or_earlier[None], jnp.exp(jnp.abs(n[:, None] - n[None, :])[None] * lg), 0.0)
    dq = jnp.exp((n + 1.0)[None, :, None] * lg) * jnp.ones((1, 1, LANES), F32)
    dk = jnp.exp((rows - 1.0 - n)[None, :, None] * lg) * jnp.ones((1, 1, LANES), F32)
    gr = jnp.exp(rows * lg) * jnp.ones((1, 1, RET_DV), F32)
    return dmat, dq, dk, gr


def _ret_block(q, k, v, g, s_prev, dmat, dq, dk, gr, gn):
    s = lax.dot_general(q, k, (((1,), (1,)), ((), ())), preferred_element_type=F32) * dmat
    inter = jnp.dot(q, s_prev.astype(BF16), preferred_element_type=F32)
    o = jnp.dot(s.astype(BF16), v, preferred_element_type=F32) + jnp.concatenate([dq] * (RET_DV // LANES), axis=1) * inter
    kd = (k.astype(F32) * jnp.concatenate([dk] * (RET_DK // LANES), axis=1)).astype(BF16)
    s_new = s_prev * gr + lax.dot_general(kd, v, (((0,), (0,)), ((), ())), preferred_element_type=F32)
    mu = jnp.mean(o, axis=-1, keepdims=True)
    oc = o - mu
    var = jnp.mean(oc * oc, axis=-1, keepdims=True)
    on = oc * lax.rsqrt(var + EPS) * gn
    gf = g.astype(F32)
    return (gf * _sigmoid(gf) * on).astype(BF16), s_new


def _ret_prompt_kernel(q_ref, k_ref, v_ref, g_ref, dm_ref, dq_ref, dk_ref, gr_ref, gn_ref,
                       og_ref, st_ref, s_scr):
    @pl.when(pl.program_id(1) == 0)
    def _():
        s_scr[...] = jnp.zeros_like(s_scr)

    for hh in range(RET_HPS):
        ks = slice(hh * RET_DK, (hh + 1) * RET_DK)
        vs = slice(hh * RET_DV, (hh + 1) * RET_DV)
        og, s_new = _ret_block(q_ref[:, ks], k_ref[:, ks], v_ref[:, vs], g_ref[:, vs], s_scr[hh],
                               dm_ref[hh], dq_ref[hh], dk_ref[hh], gr_ref[hh], gn_ref[:, vs])
        og_ref[:, vs] = og
        s_scr[hh] = s_new
        st_ref[hh] = s_new


def _ret_prompt_call(qk, vg, gn, t_prompt):
    dm, dq, dk, gr = _ret_tables(RET_ROWS, CHUNK)
    hps = RET_HPS
    hb = RET_HEADS // hps
    return pl.pallas_call(
        _ret_prompt_kernel,
        grid=(hb, t_prompt // RET_ROWS),
        in_specs=[
            pl.BlockSpec((RET_ROWS, hps * RET_DK), lambda hh, c: (c, hh)),
            pl.BlockSpec((RET_ROWS, hps * RET_DK), lambda hh, c: (c, hb + hh)),
            pl.BlockSpec((RET_ROWS, hps * RET_DV), lambda hh, c: (c, hh)),
            pl.BlockSpec((RET_ROWS, hps * RET_DV), lambda hh, c: (c, hb + hh)),
            pl.BlockSpec((hps, RET_ROWS, RET_ROWS), lambda hh, c: (hh, 0, 0)),
            pl.BlockSpec((hps, RET_ROWS, LANES), lambda hh, c: (hh, 0, 0)),
            pl.BlockSpec((hps, RET_ROWS, LANES), lambda hh, c: (hh, 0, 0)),
            pl.BlockSpec((hps, 1, RET_DV), lambda hh, c: (hh, 0, 0)),
            pl.BlockSpec((1, hps * RET_DV), lambda hh, c: (0, hh)),
        ],
        out_specs=[
            pl.BlockSpec((RET_ROWS, hps * RET_DV), lambda hh, c: (c, hh)),
            pl.BlockSpec((hps, RET_DK, RET_DV), lambda hh, c: (hh, 0, 0)),
        ],
        out_shape=[
            jax.ShapeDtypeStruct((t_prompt, RET_VW), BF16),
            jax.ShapeDtypeStruct((RET_HEADS, RET_DK, RET_DV), F32),
        ],
        scratch_shapes=[pltpu.VMEM((hps, RET_DK, RET_DV), F32)],
        compiler_params=_params(("arbitrary", "arbitrary"), 40),
        name="retention_prompt",
    )(qk, qk, vg, vg, dm, dq, dk, gr, gn.reshape(1, RET_VW))


def _ret_sample_kernel(q_ref, k_ref, v_ref, g_ref, s0_ref, dm_ref, dq_ref, dk_ref, gr_ref, gn_ref,
                       og_ref, st_ref):
    for hh in range(RET_HEADS):
        ks = slice(hh * RET_DK, (hh + 1) * RET_DK)
        vs = slice(hh * RET_DV, (hh + 1) * RET_DV)
        og, s_new = _ret_block(q_ref[:, ks], k_ref[:, ks], v_ref[:, vs], g_ref[:, vs], s0_ref[0, hh],
                               dm_ref[hh], dq_ref[hh], dk_ref[hh], gr_ref[hh], gn_ref[:, vs])
        og_ref[:, vs] = og
        st_ref[0, hh] = s_new


def _ret_sample_call(qk, vg, gn, state, t_prompt, dec_seq):
    dm, dq, dk, gr = _ret_tables(dec_seq, dec_seq)
    b = state.shape[0]
    r0 = t_prompt // dec_seq
    const3 = lambda bb: (0, 0, 0)
    return pl.pallas_call(
        _ret_sample_kernel,
        grid=(b,),
        in_specs=[
            pl.BlockSpec((dec_seq, D_MODEL), lambda bb: (r0 + bb, 0)),
            pl.BlockSpec((dec_seq, D_MODEL), lambda bb: (r0 + bb, 1)),
            pl.BlockSpec((dec_seq, RET_VW), lambda bb: (r0 + bb, 0)),
            pl.BlockSpec((dec_seq, RET_VW), lambda bb: (r0 + bb, 1)),
            pl.BlockSpec((1, RET_HEADS, RET_DK, RET_DV), lambda bb: (bb, 0, 0, 0)),
            pl.BlockSpec(dm.shape, const3),
            pl.BlockSpec(dq.shape, const3),
            pl.BlockSpec(dk.shape, const3),
            pl.BlockSpec(gr.shape, const3),
            pl.BlockSpec((1, RET_VW), lambda bb: (0, 0)),
        ],
        out_specs=[
            pl.BlockSpec((dec_seq, RET_VW), lambda bb: (bb, 0)),
            pl.BlockSpec((1, RET_HEADS, RET_DK, RET_DV), lambda bb: (bb, 0, 0, 0)),
        ],
        out_shape=[
            jax.ShapeDtypeStruct((b * dec_seq, RET_VW), BF16),
            jax.ShapeDtypeStruct(state.shape, F32),
        ],
        compiler_params=_params(("arbitrary",), 40),
        name="retention_sample",
    )(qk, qk, vg, vg, state, dm, dq, dk, gr, gn.reshape(1, RET_VW))


def _router_kernel(x_ref, g_ref, wr_ref, br_ref, id_ref, w_ref):
    xn = _rms(x_ref[...], g_ref[...])
    xh = xn.astype(BF16)
    xl = (xn - xh.astype(F32)).astype(BF16)
    w = wr_ref[...]
    wh = w.astype(BF16)
    wl = (w - wh.astype(F32)).astype(BF16)
    hi = jnp.dot(xh, jnp.concatenate([wh, wl], axis=1), preferred_element_type=F32)
    lg = (hi[:, :LANES] + hi[:, LANES:] + jnp.dot(xl, wh, preferred_element_type=F32)) + br_ref[...]
    lane = lax.broadcasted_iota(jnp.int32, lg.shape, 1).astype(F32)
    neg = jnp.float32(-jnp.inf)
    big = jnp.float32(LANES)
    gl = jnp.where(lane < MOE_GROUPS, lg, neg)
    gmax = jnp.max(gl, axis=-1, keepdims=True)
    gsel = jnp.min(jnp.where(gl == gmax, lane, big), axis=-1, keepdims=True)
    gw = 1.0 / jnp.sum(jnp.exp(gl - gmax), axis=-1, keepdims=True)
    lo = MOE_GROUPS + gsel * MOE_PER_GROUP
    el = jnp.where((lane >= lo) & (lane < lo + MOE_PER_GROUP), lg, neg)
    m1 = jnp.max(el, axis=-1, keepdims=True)
    i1 = jnp.min(jnp.where(el == m1, lane, big), axis=-1, keepdims=True)
    el2 = jnp.where(lane == i1, neg, el)
    m2 = jnp.max(el2, axis=-1, keepdims=True)
    i2 = jnp.min(jnp.where(el2 == m2, lane, big), axis=-1, keepdims=True)
    z = jnp.sum(jnp.exp(el - m1), axis=-1, keepdims=True)
    p1 = 1.0 / z
    p2 = jnp.exp(m2 - m1) / z
    tot = p1 + p2
    w1 = p1 / tot * gw
    w2 = p2 / tot * gw
    ids = jnp.where(lane == 0, i1 - MOE_GROUPS, jnp.where(lane == 1, i2 - MOE_GROUPS, 0.0))
    id_ref[...] = ids.astype(jnp.int32)
    w_ref[...] = jnp.where(lane == 0, w1, jnp.where(lane == 1, w2, 0.0))


def _router_call(x, g, w_grp, b_grp, w_exp, b_exp):
    t, d = x.shape
    pad = LANES - MOE_GROUPS - N_EXP
    wr = jnp.concatenate([w_grp, w_exp, jnp.zeros((d, pad), F32)], axis=1)
    br = jnp.concatenate([b_grp, b_exp, jnp.zeros((pad,), F32)]).reshape(1, LANES)
    row = lambda i: (i, 0)
    fixed = lambda i: (0, 0)
    return pl.pallas_call(
        _router_kernel,
        grid=(t // ROW_TILE,),
        in_specs=[pl.BlockSpec((ROW_TILE, d), row), pl.BlockSpec((1, d), fixed),
                  pl.BlockSpec((d, LANES), fixed), pl.BlockSpec((1, LANES), fixed)],
        out_specs=[pl.BlockSpec((ROW_TILE, LANES), row), pl.BlockSpec((ROW_TILE, LANES), row)],
        out_shape=[jax.ShapeDtypeStruct((t, LANES), jnp.int32), jax.ShapeDtypeStruct((t, LANES), F32)],
        compiler_params=_params(("arbitrary",), 40),
        name="moe_router",
    )(x, g.reshape(1, d), wr, br)


def _cumsum_rows(hot):
    t, n = hot.shape
    blk = LANES
    h3 = hot.reshape(t // blk, blk, n).astype(F32)
    tri = jnp.tril(jnp.ones((blk, blk), F32))
    inner = jnp.einsum("ij,bjn->bin", tri, h3).astype(jnp.int32)
    totals = inner[:, -1, :]
    offsets = jnp.cumsum(totals, axis=0) - totals
    return (inner + offsets[:, None, :]).reshape(t, n)


def _moe_plan(e0, e1):
    t = e0.shape[0]
    n_tiles = 2 * t // MOE_TILE + N_EXP
    experts = jnp.arange(N_EXP, dtype=jnp.int32)[None, :]
    hot0 = (e0[:, None] == experts).astype(jnp.int32)
    hot1 = (e1[:, None] == experts).astype(jnp.int32)
    csum0 = _cumsum_rows(hot0)
    csum1 = _cumsum_rows(hot1)
    counts0 = csum0[-1]
    counts = counts0 + csum1[-1]
    tiles_e = (counts + MOE_TILE - 1) // MOE_TILE
    tiles_end = jnp.cumsum(tiles_e)
    row_start = (tiles_end - tiles_e) * MOE_TILE
    pos0 = jnp.sum(hot0 * (row_start[None, :] + csum0 - 1), axis=1)
    pos1 = jnp.sum(hot1 * (row_start[None, :] + counts0[None, :] + csum1 - 1), axis=1)
    pos = jnp.concatenate([pos0, pos1])
    n_valid = tiles_end[-1]
    tile_id = jnp.minimum(jnp.arange(n_tiles, dtype=jnp.int32), n_valid - 1)
    tile_expert = jnp.sum((tile_id[:, None] >= tiles_end[None, :]).astype(jnp.int32), axis=1)
    tile_expert = jnp.minimum(tile_expert, N_EXP - 1)
    changed = jnp.concatenate([jnp.ones((1,), bool), tile_expert[1:] != tile_expert[:-1]])
    first = jnp.logical_and(changed, jnp.arange(n_tiles) < n_valid).astype(jnp.int32)
    wslot = (jnp.cumsum(first) - 1) % 2
    next_start = tiles_end[tile_expert]
    next_expert = jnp.where(next_start < n_valid, tile_expert[jnp.minimum(next_start, n_tiles - 1)], -1)
    tile_meta = jnp.stack([tile_expert, first, wslot, next_expert]).astype(jnp.int32)
    pad_meta = jnp.stack([row_start + counts, tiles_e * MOE_TILE - counts]).astype(jnp.int32)
    return tile_meta, n_valid.reshape(1).astype(jnp.int32), pad_meta, pos.astype(jnp.int32)


def _dispatch_kernel(pos_ref, nv_ref, pad_ref, x_ref, g_ref, xs_hbm, xnbuf, zeros, sems, zsem):
    i = pl.program_id(0)
    n = pl.num_programs(0)
    tm = x_ref.shape[0]
    n_tiles = xs_hbm.shape[0] // MOE_TILE
    n_tok = n * tm
    slot = lax.rem(i, 2)

    def wait_rows(s):
        for kk in range(2):
            pltpu.make_async_copy(xnbuf.at[s], xs_hbm.at[pl.ds(0, tm)], sems.at[s]).wait()

    pl.when(i >= 2)(lambda: wait_rows(slot))
    xnbuf[slot] = _rms(x_ref[...], g_ref[...])
    for r in range(tm):
        for kk in range(2):
            pltpu.make_async_copy(xnbuf.at[slot, pl.ds(r, 1)],
                                  xs_hbm.at[pl.ds(pos_ref[kk * n_tok + i * tm + r], 1)],
                                  sems.at[slot]).start(priority=kk)

    def zero_rows(start, n):
        return pltpu.make_async_copy(zeros.at[pl.ds(0, n)], xs_hbm.at[pl.ds(start, n)], zsem)

    block_sizes = [1 << b for b in range(MOE_TILE.bit_length() - 2, -1, -1)]

    def fill_expert(e, carry):
        start, length = pad_ref[0, e], pad_ref[1, e]
        head = jnp.bitwise_and(-start, SUBLANES - 1)
        lax.fori_loop(0, head, lambda j, c: (zero_rows(start + j, 1).start(), c)[1], 0)
        rest = length - head
        off = start + head
        for sz in block_sizes:
            if sz < SUBLANES:
                break
            take = jnp.bitwise_and(rest, sz) != 0
            pl.when(take)(lambda off=off, sz=sz: zero_rows(pl.multiple_of(off, SUBLANES), sz).start())
            off = off + jnp.where(take, sz, 0)
        return carry

    def drain_expert(e, carry):
        length = pad_ref[1, e]
        for sz in block_sizes:
            pl.when(jnp.bitwise_and(length, sz) != 0)(lambda sz=sz: zero_rows(0, sz).wait())
        return carry

    def tail_tile_copy(tile):
        return pltpu.make_async_copy(zeros, xs_hbm.at[pl.ds(tile * MOE_TILE, MOE_TILE)], zsem)

    @pl.when(i == 0)
    def _():
        zeros[...] = jnp.zeros_like(zeros)
        lax.fori_loop(0, N_EXP, fill_expert, 0)
        lax.fori_loop(nv_ref[0], n_tiles, lambda tile, c: (tail_tile_copy(tile).start(), c)[1], 0)

    @pl.when(i == n - 1)
    def _():
        pl.when(n >= 2)(lambda: wait_rows(1 - slot))
        wait_rows(slot)
        lax.fori_loop(0, N_EXP, drain_expert, 0)
        lax.fori_loop(nv_ref[0], n_tiles, lambda tile, c: (tail_tile_copy(tile).wait(), c)[1], 0)


def _dispatch_call(x, g, pos, n_valid, pad_meta, rows):
    t, d = x.shape
    grid_spec = pltpu.PrefetchScalarGridSpec(
        num_scalar_prefetch=3,
        grid=(t // ROW_TILE,),
        in_specs=[pl.BlockSpec((ROW_TILE, d), lambda i, pos_ref, nv, pad: (i, 0)),
                  pl.BlockSpec((1, d), lambda i, pos_ref, nv, pad: (0, 0))],
        out_specs=pl.BlockSpec(memory_space=pl.ANY),
        scratch_shapes=[pltpu.VMEM((2, ROW_TILE, d), F32), pltpu.VMEM((MOE_TILE, d), F32),
                        pltpu.SemaphoreType.DMA((2,)), pltpu.SemaphoreType.DMA(())],
    )
    return pl.pallas_call(
        _dispatch_kernel,
        grid_spec=grid_spec,
        out_shape=jax.ShapeDtypeStruct((rows, d), F32),
        compiler_params=_params(("arbitrary",), 32),
        name="moe_dispatch",
    )(pos, n_valid, pad_meta, x, g.reshape(1, d))


def _moe_ffn_kernel(tm_ref, nv_ref, x_ref, wgu_hbm, wd_hbm, o_ref,
                    wgu_f, wd_f, wsems, wgu_b, wd_b, *, layer):
    i = pl.program_id(0)
    nv = nv_ref[0]
    expert, first, wslot, next_expert = tm_ref[0, i], tm_ref[1, i], tm_ref[2, i], tm_ref[3, i]

    def weight_copies(e, s):
        return (pltpu.make_async_copy(wgu_hbm.at[layer, e], wgu_f.at[s], wsems.at[0, s]),
                pltpu.make_async_copy(wd_hbm.at[layer, e], wd_f.at[s], wsems.at[1, s]))

    @pl.when(i == 0)
    def _():
        for c in weight_copies(expert, 0):
            c.start()

    @pl.when(first == 1)
    def _():
        @pl.when(next_expert >= 0)
        def _():
            for c in weight_copies(next_expert, 1 - wslot):
                c.start()

        for c in weight_copies(expert, wslot):
            c.wait()
        wgu_b[...] = wgu_f[wslot].astype(BF16)
        wd_b[...] = wd_f[wslot].astype(BF16)

    @pl.when(i < nv)
    def _():
        h = jnp.dot(x_ref[...].astype(BF16), wgu_b[...], preferred_element_type=F32)
        hg = h[:, :D_FF_E]
        hu = h[:, D_FF_E:]
        act = (hg * _sigmoid(hg) * hu).astype(BF16)
        o_ref[...] = jnp.dot(act, wd_b[...], preferred_element_type=F32)


def _moe_ffn_call(xs, w_gu, w_down, layer, tile_meta, n_valid):
    rows, d = xs.shape
    n_tiles = tile_meta.shape[1]
    valid_tile = lambda i, tm, nv: (jnp.minimum(i, nv[0] - 1), 0)
    grid_spec = pltpu.PrefetchScalarGridSpec(
        num_scalar_prefetch=2,
        grid=(n_tiles,),
        in_specs=[pl.BlockSpec((MOE_TILE, d), valid_tile),
                  pl.BlockSpec(memory_space=pl.ANY), pl.BlockSpec(memory_space=pl.ANY)],
        out_specs=pl.BlockSpec((MOE_TILE, d), valid_tile),
        scratch_shapes=[
            pltpu.VMEM((2, d, 2 * D_FF_E), F32),
            pltpu.VMEM((2, D_FF_E, d), F32),
            pltpu.SemaphoreType.DMA((2, 2)),
            pltpu.VMEM((d, 2 * D_FF_E), BF16),
            pltpu.VMEM((D_FF_E, d), BF16),
        ],
    )
    return pl.pallas_call(
        functools.partial(_moe_ffn_kernel, layer=layer),
        grid_spec=grid_spec,
        out_shape=jax.ShapeDtypeStruct((rows, d), F32),
        input_output_aliases={2: 0},
        compiler_params=_params(("arbitrary",), 48),
        name="moe_experts",
    )(tile_meta, n_valid, xs, w_gu, w_down)


def _combine_kernel(pos_ref, x_ref, w_ref, y_hbm, g_ref, *rest, emit_x, norm_tiles):
    outs = rest[:int(emit_x) + len(norm_tiles)]
    buf, sems = rest[int(emit_x) + len(norm_tiles):]
    i = pl.program_id(0)
    n = pl.num_programs(0)
    slot = lax.rem(i, 2)
    tm = x_ref.shape[0]

    def gather(tile, dst_slot):
        n_tok = n * tm
        for r in range(tm):
            for kk in range(2):
                pltpu.make_async_copy(y_hbm.at[pl.ds(pos_ref[kk * n_tok + tile * tm + r], 1)],
                                      buf.at[dst_slot, kk, pl.ds(r, 1)], sems.at[dst_slot]).start(priority=kk)

    @pl.when(i == 0)
    def _():
        gather(0, 0)

    for s in range(2):
        pl.when(jnp.logical_and(i + 1 < n, slot == s))(functools.partial(gather, i + 1, 1 - s))

    for kk in range(2):
        pltpu.make_async_copy(y_hbm.at[pl.ds(0, tm)], buf.at[slot, kk], sems.at[slot]).wait()
    xo = x_ref[...] + (w_ref[:, 0:1] * buf[slot, 0] + w_ref[:, 1:2] * buf[slot, 1])
    norm_refs = outs[int(emit_x):]
    if emit_x:
        outs[0][...] = xo

    def write_norm(p):
        norm_refs[p][...] = _rms(xo, g_ref[...]).astype(norm_refs[p].dtype)

    _for_each_part(i, norm_tiles, write_norm)


def _combine_call(x, ws, y_rows, pos, g, emit_x, norm_dtype, norm_rows):
    t, d = x.shape
    tm = MOE_TILE
    row = lambda i, pos_ref: (i, 0)
    out_specs, out_shape = [], []
    if emit_x:
        out_specs.append(pl.BlockSpec((tm, d), row))
        out_shape.append(jax.ShapeDtypeStruct((t, d), F32))
    norm_parts = [jax.ShapeDtypeStruct((r, d), norm_dtype) for r in norm_rows]
    out_specs += _part_specs(norm_parts, tm, d, row)
    out_shape += norm_parts
    grid_spec = pltpu.PrefetchScalarGridSpec(
        num_scalar_prefetch=1,
        grid=(t // tm,),
        in_specs=[pl.BlockSpec((tm, d), row), pl.BlockSpec((tm, LANES), row), pl.BlockSpec(memory_space=pl.ANY),
                  pl.BlockSpec((1, d), lambda i, pos_ref: (0, 0))],
        out_specs=out_specs,
        scratch_shapes=[pltpu.VMEM((2, 2, tm, d), F32), pltpu.SemaphoreType.DMA((2,))],
    )
    return pl.pallas_call(
        functools.partial(_combine_kernel, emit_x=emit_x, norm_tiles=_part_tiles(norm_parts, tm)),
        grid_spec=grid_spec,
        out_shape=out_shape,
        compiler_params=_params(("arbitrary",), 40),
        name="moe_combine",
    )(pos, x, ws, y_rows, g.reshape(1, d))


def _moe_layer(x, g_ffn, w_grp, b_grp, w_exp, b_exp, w_gu, w_down, layer, g_next, emit_x, norm_dtype, norm_rows):
    ids, ws = _router_call(x, g_ffn, w_grp, b_grp, w_exp, b_exp)
    tile_meta, n_valid, pad_meta, pos = _moe_plan(ids[:, 0], ids[:, 1])
    xs = _dispatch_call(x, g_ffn, pos, n_valid, pad_meta, tile_meta.shape[1] * MOE_TILE)
    y_rows = _moe_ffn_call(xs, w_gu, w_down, layer, tile_meta, n_valid)
    return _combine_call(x, ws, y_rows, pos, g_next, emit_x, norm_dtype, norm_rows)


def _s5_prep_kernel(are_ref, aim_ref, ldt_ref, bre_ref, bim_ref, lre_ref, lim_ref, bbr_ref, bbi_ref):
    are = are_ref[...]
    aim = aim_ref[...]
    dt = jnp.exp(ldt_ref[...])
    mag = jnp.exp(are * dt)
    lr = mag * jnp.cos(aim * dt)
    li = mag * jnp.sin(aim * dt)
    lre_ref[...] = lr
    lim_ref[...] = li
    den = are * are + aim * aim
    nr = lr - 1.0
    cr = (nr * are + li * aim) / den
    ci = (li * are - nr * aim) / den
    bre = bre_ref[...]
    bim = bim_ref[...]
    bbr_ref[...] = cr * bre - ci * bim
    bbi_ref[...] = cr * bim + ci * bre


def _s5_prep_call(a_re, a_im, log_dt, b_re, b_im):
    g, p, c = b_re.shape
    shp = (g, p * c)
    expand = lambda a: jnp.broadcast_to(a[:, :, None], (g, p, c)).reshape(shp)
    ldt = jnp.broadcast_to(log_dt[:, None], shp)
    outs = pl.pallas_call(
        _s5_prep_kernel,
        out_shape=[jax.ShapeDtypeStruct(shp, F32)] * 4,
        name="s5_discretize",
    )(expand(a_re), expand(a_im), ldt, b_re.reshape(shp), b_im.reshape(shp))
    lre, lim, bbr, bbi = outs
    return lre[:, ::c], lim[:, ::c], bbr, bbi


def _s5_compact(bbr, bbi, c_re, c_im):
    def b_part(bb):
        return bb.reshape(S5_FB, 16, S5_P, S5_GC).transpose(0, 1, 3, 2).reshape(S5_FB, 16 * S5_GC, S5_P)

    def c_part(cc):
        return cc.reshape(S5_FB, 16, S5_GC, S5_P).transpose(0, 3, 1, 2).reshape(S5_FB, S5_P, 16 * S5_GC)

    return b_part(bbr), b_part(bbi), c_part(c_re), c_part(c_im)


def _s5_build_mats(bbr_ref, bbi_ref, ccr_ref, cci_ref, bd_ref, cd_ref):
    nch, nst = 16 * S5_GC, 16 * S5_P
    iota = lambda shape, dim: lax.broadcasted_iota(jnp.int32, shape, dim)
    rep = ((iota((S5_P, nst), 1) & (S5_P - 1)) == iota((S5_P, nst), 0)).astype(BF16)
    rep_t = ((iota((nst, S5_P), 0) & (S5_P - 1)) == iota((nst, S5_P), 1)).astype(BF16)
    gc_bits, p_bits = S5_GC.bit_length() - 1, S5_P.bit_length() - 1
    diag = (iota((nch, nst), 0) >> gc_bits) == (iota((nch, nst), 1) >> p_bits)
    diag_t = (iota((nst, nch), 0) >> p_bits) == (iota((nst, nch), 1) >> gc_bits)
    for fb in range(S5_FB):
        for part, src in enumerate((bbr_ref, bbi_ref)):
            full = jnp.dot(src[fb].astype(BF16), rep, preferred_element_type=F32)
            bd_ref[fb, :, part * nst:(part + 1) * nst] = jnp.where(diag, full, 0.0).astype(BF16)
        for part, (src, sign) in enumerate(((ccr_ref, 1.0), (cci_ref, -1.0))):
            full = jnp.dot(rep_t, src[fb].astype(BF16), preferred_element_type=F32)
            cd_ref[fb, part * nst:(part + 1) * nst, :] = jnp.where(diag_t, sign * full, 0.0).astype(BF16)


def _s5_kernel(u_ref, bbr_ref, bbi_ref, ccr_ref, cci_ref, lre_ref, lim_ref, d_ref, h0r_ref, h0i_ref,
               z_ref, hr_ref, hi_ref, scr, hst, bd_ref, cd_ref, *, seg, carry):
    rows = u_ref.shape[0]
    n_seq = rows // seg
    ch = 16 * S5_GC
    half = SUBLANES * S5_PITCH
    blk = 2 * half

    @pl.when(pl.program_id(0) == 0)
    def _():
        _s5_build_mats(bbr_ref, bbi_ref, ccr_ref, cci_ref, bd_ref, cd_ref)

    for fb in range(S5_FB):
        bu = jnp.dot(u_ref[:, fb * ch:(fb + 1) * ch].astype(BF16), bd_ref[fb], preferred_element_type=F32)
        for c in range(2 * SUBLANES):
            r0 = fb * blk + c * S5_PITCH
            scr[r0:r0 + rows, :] = bu[:, c * LANES:(c + 1) * LANES]

    lre = [lre_ref[fb * SUBLANES:(fb + 1) * SUBLANES, :] for fb in range(S5_FB)]
    lim = [lim_ref[fb * SUBLANES:(fb + 1) * SUBLANES, :] for fb in range(S5_FB)]

    def scan(row0, state):
        def step(l, st):
            new = []
            for fb in range(S5_FB):
                hr, hi = st[2 * fb], st[2 * fb + 1]
                ire = pl.ds(fb * blk + row0 + l, SUBLANES, stride=S5_PITCH)
                iim = pl.ds(fb * blk + half + row0 + l, SUBLANES, stride=S5_PITCH)
                nr = lre[fb] * hr - lim[fb] * hi + scr[ire, :]
                ni = lre[fb] * hi + lim[fb] * hr + scr[iim, :]
                scr[ire, :] = nr
                scr[iim, :] = ni
                new += [nr, ni]
            return tuple(new)
        return lax.fori_loop(0, seg, step, state, unroll=8)

    def load_state(rref, iref, lead):
        st = []
        for fb in range(S5_FB):
            st += [rref[lead + (slice(fb * SUBLANES, (fb + 1) * SUBLANES), slice(None))],
                   iref[lead + (slice(fb * SUBLANES, (fb + 1) * SUBLANES), slice(None))]]
        return tuple(st)

    def store_state(st, rref, iref, lead):
        for fb in range(S5_FB):
            rref[lead + (slice(fb * SUBLANES, (fb + 1) * SUBLANES), slice(None))] = st[2 * fb]
            iref[lead + (slice(fb * SUBLANES, (fb + 1) * SUBLANES), slice(None))] = st[2 * fb + 1]

    if carry:
        @pl.when(pl.program_id(0) == 0)
        def _():
            hst[0] = h0r_ref[0]
            hst[1] = h0i_ref[0]
        hre_scr = hst.at[0]
        him_scr = hst.at[1]
        st = scan(0, load_state(hre_scr, him_scr, ()))
        store_state(st, hre_scr, him_scr, ())
        store_state(st, hr_ref, hi_ref, (0,))
    else:
        for s in range(n_seq):
            st = scan(s * seg, load_state(h0r_ref, h0i_ref, (s,)))
            store_state(st, hr_ref, hi_ref, (s,))

    for fb in range(S5_FB):
        hf = jnp.concatenate(
            [scr[fb * blk + c * S5_PITCH: fb * blk + c * S5_PITCH + rows, :] for c in range(2 * SUBLANES)],
            axis=1).astype(BF16)
        y = jnp.dot(hf, cd_ref[fb], preferred_element_type=F32)
        cs = slice(fb * ch, (fb + 1) * ch)
        yy = y + d_ref[:, cs] * u_ref[:, cs]
        z_ref[:, cs] = jax.nn.gelu(yy).astype(z_ref.dtype)


def _s5_call(u, mats, lre, lim, d_skip, h0r, h0i, row_block0, n_blocks, seg, carry):
    t, d = u.shape
    ch, nst = 16 * S5_GC, 16 * S5_P
    n_seq = S5_ROWS // seg
    ns_total = h0r.shape[0]
    st_idx = (lambda i: (0, 0, 0)) if carry else (lambda i: (i, 0, 0))
    fixed2 = lambda i: (0, 0)
    fixed3 = lambda i: (0, 0, 0)
    st_rows = S5_FB * SUBLANES
    return pl.pallas_call(
        functools.partial(_s5_kernel, seg=seg, carry=carry),
        grid=(n_blocks,),
        in_specs=[
            pl.BlockSpec((S5_ROWS, d), lambda i: (row_block0 + i, 0)),
            *[pl.BlockSpec(m.shape, fixed3) for m in mats],
            pl.BlockSpec(lre.shape, fixed2),
            pl.BlockSpec(lim.shape, fixed2),
            pl.BlockSpec((1, d), fixed2),
            pl.BlockSpec((n_seq, st_rows, LANES), st_idx),
            pl.BlockSpec((n_seq, st_rows, LANES), st_idx),
        ],
        out_specs=[
            pl.BlockSpec((S5_ROWS, d), lambda i: (i, 0)),
            pl.BlockSpec((n_seq, st_rows, LANES), st_idx),
            pl.BlockSpec((n_seq, st_rows, LANES), st_idx),
        ],
        out_shape=[
            jax.ShapeDtypeStruct((n_blocks * S5_ROWS, d), BF16),
            jax.ShapeDtypeStruct((ns_total, st_rows, LANES), F32),
            jax.ShapeDtypeStruct((ns_total, st_rows, LANES), F32),
        ],
        scratch_shapes=[
            pltpu.VMEM((2 * SUBLANES * S5_FB * S5_PITCH, LANES), F32),
            pltpu.VMEM((2, st_rows, LANES), F32),
            pltpu.VMEM((S5_FB, ch, 2 * nst), BF16),
            pltpu.VMEM((S5_FB, 2 * nst, ch), BF16),
        ],
        compiler_params=_params(("arbitrary",), 56),
        name="s5_scan",
    )(u, *mats, lre, lim, d_skip.reshape(1, d), h0r, h0i)


def kernel(x_prompt, x_sample, state_ret, state_s5_re, state_s5_im, norm_mix_g, norm_ffn_g, norm_final_g,
           ret_w_in, ret_gn_g, ret_w_out, s5_w_in, s5_a_re, s5_a_im, s5_log_dt, s5_b_re, s5_b_im,
           s5_c_re, s5_c_im, s5_d, s5_w_out, moe_w_grp, moe_b_grp, moe_w_exp, moe_b_exp, moe_w_gu, moe_w_down):
    bp, seq, d = x_prompt.shape
    db, dseq, _ = x_sample.shape
    assert d == D_MODEL and bp == 1
    assert seq % max(ROW_TILE, RET_ROWS) == 0 and seq % CHUNK == 0
    assert dseq < CHUNK and dseq % 16 == 0 and S5_ROWS % dseq == 0
    tp = bp * seq
    ts = db * dseq
    t = tp + ts
    assert ts % ROW_TILE == 0 and ts % S5_ROWS == 0

    x_parts = [x_prompt.reshape(tp, d), x_sample.reshape(ts, d)]

    half = RET_DK // 2
    freqs = ROPE_BASE ** (-jnp.arange(half, dtype=F32) / half)
    pos = jnp.concatenate([jnp.arange(seq, dtype=F32), jnp.tile(PAST_LEN + jnp.arange(dseq, dtype=F32), db)])
    ang = pos[:, None] * freqs[None, :]
    cos_t = jnp.cos(ang)
    sin_t = jnp.sin(ang)

    xn = _norm_call(x_parts, norm_mix_g[0], BF16)
    tn = 2048
    first_col = lambda j: 0
    qk = _mm_call([xn], ret_w_in[0], [0], tn, 2 * D_MODEL, [([cos_t], half, first_col), ([sin_t], half, first_col)],
                  BF16, functools.partial(_epi_rope, n_q_blocks=D_MODEL // tn), 56, "ret_qk_proj")
    vg = _mm_call([xn], ret_w_in[0], [2 * D_MODEL // tn], tn, 2 * RET_VW, [], BF16, _epi_plain, 56, "ret_vg_proj")
    og_p, ret_p = _ret_prompt_call(qk, vg, ret_gn_g[0], tp)
    og_s, ret_s = _ret_sample_call(qk, vg, ret_gn_g[0], state_ret[0], tp, dseq)
    tn = 512
    same_col = lambda j: j
    x1 = _mm_resident_call([og_p, og_s], ret_w_out[0], [0], tn, d, [(x_parts, tn, same_col)], F32, _epi_residual,
                           58, "ret_out_proj")

    x2, xn = _moe_layer(x1, norm_ffn_g[0], moe_w_grp[0], moe_b_grp[0], moe_w_exp[0], moe_b_exp[0],
                        moe_w_gu, moe_w_down, 0, norm_mix_g[1], True, BF16, (t,))

    u = _mm_call([xn], s5_w_in[0], [0], 2048, d, [], F32, _epi_plain, 56, "s5_in_proj")
    lre, lim, bbr, bbi = _s5_prep_call(s5_a_re[0], s5_a_im[0], s5_log_dt[0], s5_b_re[0], s5_b_im[0])
    mats = _s5_compact(bbr, bbi, s5_c_re[0], s5_c_im[0])
    st_shape = (S5_FB * SUBLANES, LANES)
    lre = lre.reshape(st_shape)
    lim = lim.reshape(st_shape)
    zero_state = jnp.zeros((1,) + st_shape, F32)
    z_p, hpr, hpi = _s5_call(u, mats, lre, lim, s5_d[0], zero_state, zero_state,
                             0, tp // S5_ROWS, S5_ROWS, True)
    z_s, hsr, hsi = _s5_call(u, mats, lre, lim, s5_d[0],
                             state_s5_re[0].reshape((db,) + st_shape), state_s5_im[0].reshape((db,) + st_shape),
                             tp // S5_ROWS, ts // S5_ROWS, dseq, False)
    x3 = _mm_resident_call([z_p, z_s], s5_w_out[0], [0, d // tn], tn, d, [([x2], tn, same_col)], F32,
                           _epi_glu_residual, 48, "s5_out_proj")

    y_p, y_s = _moe_layer(x3, norm_ffn_g[1], moe_w_grp[1], moe_b_grp[1], moe_w_exp[1], moe_b_exp[1],
                          moe_w_gu, moe_w_down, 1, norm_final_g, False, F32, (tp, ts))

    s5_shape = (1, -1, S5_GROUPS, S5_P)
    return (y_p.reshape(bp, seq, d), y_s.reshape(db, dseq, d),
            ret_p.reshape(1, bp, RET_HEADS, RET_DK, RET_DV), ret_s.reshape((1,) + ret_s.shape),
            hpr.reshape(s5_shape), hpi.reshape(s5_shape), hsr.reshape(s5_shape), hsi.reshape(s5_shape))
```

```python
import functools

import numpy as np
import jax
import jax.numpy as jnp
from jax import lax
from jax.experimental import pallas as pl
from jax.experimental.pallas import tpu as pltpu

F32 = jnp.float32
BF16 = jnp.bfloat16

D_MODEL = 2048
PAST_LEN = 2048
CHUNK = 64
RET_HEADS = 8
RET_DK = D_MODEL // RET_HEADS
RET_DV = 2 * RET_DK
RET_VW = RET_HEADS * RET_DV
ROPE_BASE = 10000.0
S5_GC = 16
S5_GROUPS = D_MODEL // S5_GC
S5_P = 64
MOE_GROUPS = 4
MOE_PER_GROUP = 8
N_EXP = MOE_GROUPS * MOE_PER_GROUP
D_FF_E = 512
EPS = 1e-6

LANES = 128
SUBLANES = 8
ROW_TILE = 512
MM_CHUNK = 256
RET_ROWS = 256
RET_HPS = 4
MOE_TILE = 256
S5_ROWS = 256
S5_FB = S5_GROUPS // 16
S5_PITCH = S5_ROWS + 4
MIB = 1024 * 1024


def _params(sem, vmem_mib):
    return pltpu.CompilerParams(dimension_semantics=sem, vmem_limit_bytes=vmem_mib * MIB)


def _rms(x, g):
    ms = jnp.mean(x * x, axis=-1, keepdims=True)
    return x * lax.rsqrt(ms + EPS) * g


def _sigmoid(x):
    return 1.0 / (1.0 + jnp.exp(-x))


def _part_specs(parts, rows, cols, rowcol):
    specs = []
    start = 0
    for p in parts:
        nt = p.shape[0] // rows

        def imap(*g, start=start, nt=nt):
            i, jc = rowcol(*g)
            return (jnp.clip(i - start, 0, nt - 1), jc)

        specs.append(pl.BlockSpec((rows, cols), imap))
        start += nt
    return specs


def _part_tiles(parts, rows):
    return tuple(p.shape[0] // rows for p in parts)


def _for_each_part(i, tiles, fn):
    if len(tiles) == 1:
        fn(0)
        return
    start = 0
    for p, nt in enumerate(tiles):
        pl.when(jnp.logical_and(i >= start, i < start + nt))(functools.partial(fn, p))
        start += nt


def _pick(refs, p):
    return refs[min(p, len(refs) - 1)]


def _norm_kernel(*refs, tiles):
    n = len(tiles)
    x_refs, g_ref, o_ref = refs[:n], refs[n], refs[n + 1]

    def run(p):
        o_ref[...] = _rms(x_refs[p][...], g_ref[...]).astype(o_ref.dtype)

    _for_each_part(pl.program_id(0), tiles, run)


def _norm_call(x_parts, g, out_dtype):
    d = x_parts[0].shape[1]
    t = sum(p.shape[0] for p in x_parts)
    return pl.pallas_call(
        functools.partial(_norm_kernel, tiles=_part_tiles(x_parts, ROW_TILE)),
        grid=(t // ROW_TILE,),
        in_specs=_part_specs(x_parts, ROW_TILE, d, lambda i: (i, 0)) + [pl.BlockSpec((1, d), lambda i: (0, 0))],
        out_specs=pl.BlockSpec((ROW_TILE, d), lambda i: (i, 0)),
        out_shape=jax.ShapeDtypeStruct((t, d), out_dtype),
        compiler_params=_params(("arbitrary",), 32),
        name="rmsnorm",
    )(*x_parts, g.reshape(1, d))


def _mm_kernel(*refs, tiles, n_x, n_w, n_extra, epi):
    x_refs = refs[:n_x]
    w_refs = refs[n_x:n_x + n_w]
    pos = n_x + n_w
    extra = []
    for n in n_extra:
        extra.append(refs[pos:pos + n])
        pos += n
    o_ref = refs[pos]
    wb = refs[pos + 1:]

    @pl.when(pl.program_id(1) == 0)
    def _():
        for w_ref, b in zip(w_refs, wb):
            b[...] = w_ref[...].astype(BF16)

    def run(p):
        x = _pick(x_refs, p)[...]
        for c in range(o_ref.shape[1] // MM_CHUNK):
            cs = slice(c * MM_CHUNK, (c + 1) * MM_CHUNK)
            accs = [jnp.dot(x, b[:, cs], preferred_element_type=F32) for b in wb]
            epi(accs, [_pick(e, p) for e in extra], o_ref, cs)

    _for_each_part(pl.program_id(1), tiles, run)


def _mm_call(x_parts, w, w_col_offsets, tn, n_out, extras, out_dtype, epi, vmem_mib, name):
    k = x_parts[0].shape[1]
    t = sum(p.shape[0] for p in x_parts)
    n_w = len(w_col_offsets)
    split = [parts for parts in [x_parts] + [e[0] for e in extras] if len(parts) > 1]
    tiles = _part_tiles(split[0], ROW_TILE) if split else (t // ROW_TILE,)
    assert all(_part_tiles(parts, ROW_TILE) == tiles for parts in split)
    in_specs = _part_specs(x_parts, ROW_TILE, k, lambda j, i: (i, 0))
    for off in w_col_offsets:
        in_specs.append(pl.BlockSpec((k, tn), lambda j, i, off=off: (0, j + off)))
    extra_arrays = []
    for parts, cols, colfn in extras:
        in_specs += _part_specs(parts, ROW_TILE, cols, lambda j, i, colfn=colfn: (i, colfn(j)))
        extra_arrays += list(parts)
    return pl.pallas_call(
        functools.partial(_mm_kernel, tiles=tiles, n_x=len(x_parts), n_w=n_w,
                          n_extra=tuple(len(e[0]) for e in extras), epi=epi),
        grid=(n_out // tn, t // ROW_TILE),
        in_specs=in_specs,
        out_specs=pl.BlockSpec((ROW_TILE, tn), lambda j, i: (i, j)),
        out_shape=jax.ShapeDtypeStruct((t, n_out), out_dtype),
        scratch_shapes=[pltpu.VMEM((k, tn), BF16) for _ in range(n_w)],
        compiler_params=_params(("arbitrary", "arbitrary"), vmem_mib),
        name=name,
    )(*x_parts, *([w] * n_w), *extra_arrays)


def _mm_resident_kernel(*refs, tiles, n_x, n_w, n_extra, epi):
    x_refs = refs[:n_x]
    w_refs = refs[n_x:n_x + n_w]
    pos = n_x + n_w
    extra = []
    for n in n_extra:
        extra.append(refs[pos:pos + n])
        pos += n
    o_ref = refs[pos]
    wb = refs[pos + 1:]
    i = pl.program_id(0)
    j = pl.program_id(1)

    @pl.when(i == 0)
    def _():
        for w_ref, b in zip(w_refs, wb):
            b[j] = w_ref[...].astype(BF16)

    def run(p):
        x = _pick(x_refs, p)[...]
        for c in range(o_ref.shape[1] // MM_CHUNK):
            cs = slice(c * MM_CHUNK, (c + 1) * MM_CHUNK)
            accs = [jnp.dot(x, b[j, :, cs], preferred_element_type=F32) for b in wb]
            epi(accs, [_pick(e, p) for e in extra], o_ref, cs)

    _for_each_part(i, tiles, run)


def _mm_resident_call(x_parts, w, w_col_offsets, tn, n_out, extras, out_dtype, epi, vmem_mib, name):
    k = x_parts[0].shape[1]
    t = sum(p.shape[0] for p in x_parts)
    n_w = len(w_col_offsets)
    nj = n_out // tn
    split = [parts for parts in [x_parts] + [e[0] for e in extras] if len(parts) > 1]
    tiles = _part_tiles(split[0], ROW_TILE) if split else (t // ROW_TILE,)
    assert all(_part_tiles(parts, ROW_TILE) == tiles for parts in split)
    in_specs = _part_specs(x_parts, ROW_TILE, k, lambda i, j: (i, 0))
    for off in w_col_offsets:
        in_specs.append(pl.BlockSpec((k, tn), lambda i, j, off=off: (0, jnp.where(i == 0, j, nj - 1) + off)))
    extra_arrays = []
    for parts, cols, colfn in extras:
        in_specs += _part_specs(parts, ROW_TILE, cols, lambda i, j, colfn=colfn: (i, colfn(j)))
        extra_arrays += list(parts)
    return pl.pallas_call(
        functools.partial(_mm_resident_kernel, tiles=tiles, n_x=len(x_parts), n_w=n_w,
                          n_extra=tuple(len(e[0]) for e in extras), epi=epi),
        grid=(t // ROW_TILE, nj),
        in_specs=in_specs,
        out_specs=pl.BlockSpec((ROW_TILE, tn), lambda i, j: (i, j)),
        out_shape=jax.ShapeDtypeStruct((t, n_out), out_dtype),
        scratch_shapes=[pltpu.VMEM((nj, k, tn), BF16) for _ in range(n_w)],
        compiler_params=_params(("arbitrary", "arbitrary"), vmem_mib),
        name=name,
    )(*x_parts, *([w] * n_w), *extra_arrays)


def _epi_plain(accs, extra, o_ref, cs):
    o_ref[:, cs] = accs[0].astype(o_ref.dtype)


def _epi_residual(accs, extra, o_ref, cs):
    o_ref[:, cs] = (extra[0][:, cs] + accs[0]).astype(o_ref.dtype)


def _epi_glu_residual(accs, extra, o_ref, cs):
    a, gt = accs
    o_ref[:, cs] = (extra[0][:, cs] + a * _sigmoid(gt)).astype(o_ref.dtype)


def _epi_rope(accs, extra, o_ref, cs, *, n_q_blocks):
    acc = accs[0]
    assert acc.shape[1] == RET_DK
    cos_ref, sin_ref = extra
    half = RET_DK // 2
    scale = jnp.where(pl.program_id(0) >= n_q_blocks, RET_DK ** -0.5, 1.0).astype(F32)
    c = cos_ref[...]
    s = sin_ref[...]
    t1 = acc[:, :half]
    t2 = acc[:, half:]
    o_ref[:, cs.start:cs.start + half] = ((t1 * c - t2 * s) * scale).astype(o_ref.dtype)
    o_ref[:, cs.start + half:cs.stop] = ((t1 * s + t2 * c) * scale).astype(o_ref.dtype)


def _ret_tables(rows, chunk):
    lg = jnp.log(jnp.asarray(1.0 - 2.0 ** (-5.0 - np.arange(RET_HEADS)), dtype=F32))[:, None, None]
    n = jnp.arange(rows, dtype=F32)
    ci = np.arange(rows) // chunk
    same_or_earlier = jnp.asarray(ci[None, :] <= ci[:, None])
    dmat = jnp.where(same_or_earlier[None], jnp.exp(jnp.abs(n[:, None] - n[None, :])[None] * lg), 0.0)
    dq = jnp.exp((n + 1.0)[None, :, None] * lg) * jnp.ones((1, 1, LANES), F32)
    dk = jnp.exp((rows - 1.0 - n)[None, :, None] * lg) * jnp.ones((1, 1, LANES), F32)
    gr = jnp.exp(rows * lg) * jnp.ones((1, 1, RET_DV), F32)
    return dmat, dq, dk, gr


def _ret_block(q, k, v, g, s_prev, dmat, dq, dk, gr, gn):
    s = lax.dot_general(q, k, (((1,), (1,)), ((), ())), preferred_element_type=F32) * dmat
    inter = jnp.dot(q, s_prev.astype(BF16), preferred_element_type=F32)
    o = jnp.dot(s.astype(BF16), v, preferred_element_type=F32) + jnp.concatenate([dq] * (RET_DV // LANES), axis=1) * inter
    kd = (k.astype(F32) * jnp.concatenate([dk] * (RET_DK // LANES), axis=1)).astype(BF16)
    s_new = s_prev * gr + lax.dot_general(kd, v, (((0,), (0,)), ((), ())), preferred_element_type=F32)
    mu = jnp.mean(o, axis=-1, keepdims=True)
    oc = o - mu
    var = jnp.mean(oc * oc, axis=-1, keepdims=True)
    on = oc * lax.rsqrt(var + EPS) * gn
    gf = g.astype(F32)
    return (gf * _sigmoid(gf) * on).astype(BF16), s_new


def _ret_prompt_kernel(q_ref, k_ref, v_ref, g_ref, dm_ref, dq_ref, dk_ref, gr_ref, gn_ref,
                       og_ref, st_ref, s_scr):
    @pl.when(pl.program_id(1) == 0)
    def _():
        s_scr[...] = jnp.zeros_like(s_scr)

    for hh in range(RET_HPS):
        ks = slice(hh * RET_DK, (hh + 1) * RET_DK)
        vs = slice(hh * RET_DV, (hh + 1) * RET_DV)
        og, s_new = _ret_block(q_ref[:, ks], k_ref[:, ks], v_ref[:, vs], g_ref[:, vs], s_scr[hh],
                               dm_ref[hh], dq_ref[hh], dk_ref[hh], gr_ref[hh], gn_ref[:, vs])
        og_ref[:, vs] = og
        s_scr[hh] = s_new
        st_ref[hh] = s_new


def _ret_prompt_call(qk, vg, gn, t_prompt):
    dm, dq, dk, gr = _ret_tables(RET_ROWS, CHUNK)
    hps = RET_HPS
    hb = RET_HEADS // hps
    return pl.pallas_call(
        _ret_prompt_kernel,
        grid=(hb, t_prompt // RET_ROWS),
        in_specs=[
            pl.BlockSpec((RET_ROWS, hps * RET_DK), lambda hh, c: (c, hh)),
            pl.BlockSpec((RET_ROWS, hps * RET_DK), lambda hh, c: (c, hb + hh)),
            pl.BlockSpec((RET_ROWS, hps * RET_DV), lambda hh, c: (c, hh)),
            pl.BlockSpec((RET_ROWS, hps * RET_DV), lambda hh, c: (c, hb + hh)),
            pl.BlockSpec((hps, RET_ROWS, RET_ROWS), lambda hh, c: (hh, 0, 0)),
            pl.BlockSpec((hps, RET_ROWS, LANES), lambda hh, c: (hh, 0, 0)),
            pl.BlockSpec((hps, RET_ROWS, LANES), lambda hh, c: (hh, 0, 0)),
            pl.BlockSpec((hps, 1, RET_DV), lambda hh, c: (hh, 0, 0)),
            pl.BlockSpec((1, hps * RET_DV), lambda hh, c: (0, hh)),
        ],
        out_specs=[
            pl.BlockSpec((RET_ROWS, hps * RET_DV), lambda hh, c: (c, hh)),
            pl.BlockSpec((hps, RET_DK, RET_DV), lambda hh, c: (hh, 0, 0)),
        ],
        out_shape=[
            jax.ShapeDtypeStruct((t_prompt, RET_VW), BF16),
            jax.ShapeDtypeStruct((RET_HEADS, RET_DK, RET_DV), F32),
        ],
        scratch_shapes=[pltpu.VMEM((hps, RET_DK, RET_DV), F32)],
        compiler_params=_params(("arbitrary", "arbitrary"), 40),
        name="retention_prompt",
    )(qk, qk, vg, vg, dm, dq, dk, gr, gn.reshape(1, RET_VW))


def _ret_sample_kernel(q_ref, k_ref, v_ref, g_ref, s0_ref, dm_ref, dq_ref, dk_ref, gr_ref, gn_ref,
                       og_ref, st_ref):
    for hh in range(RET_HEADS):
        ks = slice(hh * RET_DK, (hh + 1) * RET_DK)
        vs = slice(hh * RET_DV, (hh + 1) * RET_DV)
        og, s_new = _ret_block(q_ref[:, ks], k_ref[:, ks], v_ref[:, vs], g_ref[:, vs], s0_ref[0, hh],
                               dm_ref[hh], dq_ref[hh], dk_ref[hh], gr_ref[hh], gn_ref[:, vs])
        og_ref[:, vs] = og
        st_ref[0, hh] = s_new


def _ret_sample_call(qk, vg, gn, state, t_prompt, dec_seq):
    dm, dq, dk, gr = _ret_tables(dec_seq, dec_seq)
    b = state.shape[0]
    r0 = t_prompt // dec_seq
    const3 = lambda bb: (0, 0, 0)
    return pl.pallas_call(
        _ret_sample_kernel,
        grid=(b,),
        in_specs=[
            pl.BlockSpec((dec_seq, D_MODEL), lambda bb: (r0 + bb, 0)),
            pl.BlockSpec((dec_seq, D_MODEL), lambda bb: (r0 + bb, 1)),
            pl.BlockSpec((dec_seq, RET_VW), lambda bb: (r0 + bb, 0)),
            pl.BlockSpec((dec_seq, RET_VW), lambda bb: (r0 + bb, 1)),
            pl.BlockSpec((1, RET_HEADS, RET_DK, RET_DV), lambda bb: (bb, 0, 0, 0)),
            pl.BlockSpec(dm.shape, const3),
            pl.BlockSpec(dq.shape, const3),
            pl.BlockSpec(dk.shape, const3),
            pl.BlockSpec(gr.shape, const3),
            pl.BlockSpec((1, RET_VW), lambda bb: (0, 0)),
        ],
        out_specs=[
            pl.BlockSpec((dec_seq, RET_VW), lambda bb: (bb, 0)),
            pl.BlockSpec((1, RET_HEADS, RET_DK, RET_DV), lambda bb: (bb, 0, 0, 0)),
        ],
        out_shape=[
            jax.ShapeDtypeStruct((b * dec_seq, RET_VW), BF16),
            jax.ShapeDtypeStruct(state.shape, F32),
        ],
        compiler_params=_params(("arbitrary",), 40),
        name="retention_sample",
    )(qk, qk, vg, vg, state, dm, dq, dk, gr, gn.reshape(1, RET_VW))


def _router_kernel(x_ref, g_ref, wr_ref, br_ref, id_ref, w_ref):
    xn = _rms(x_ref[...], g_ref[...])
    xh = xn.astype(BF16)
    xl = (xn - xh.astype(F32)).astype(BF16)
    w = wr_ref[...]
    wh = w.astype(BF16)
    wl = (w - wh.astype(F32)).astype(BF16)
    hi = jnp.dot(xh, jnp.concatenate([wh, wl], axis=1), preferred_element_type=F32)
    lg = (hi[:, :LANES] + hi[:, LANES:] + jnp.dot(xl, wh, preferred_element_type=F32)) + br_ref[...]
    lane = lax.broadcasted_iota(jnp.int32, lg.shape, 1).astype(F32)
    neg = jnp.float32(-jnp.inf)
    big = jnp.float32(LANES)
    gl = jnp.where(lane < MOE_GROUPS, lg, neg)
    gmax = jnp.max(gl, axis=-1, keepdims=True)
    gsel = jnp.min(jnp.where(gl == gmax, lane, big), axis=-1, keepdims=True)
    gw = 1.0 / jnp.sum(jnp.exp(gl - gmax), axis=-1, keepdims=True)
    lo = MOE_GROUPS + gsel * MOE_PER_GROUP
    el = jnp.where((lane >= lo) & (lane < lo + MOE_PER_GROUP), lg, neg)
    m1 = jnp.max(el, axis=-1, keepdims=True)
    i1 = jnp.min(jnp.where(el == m1, lane, big), axis=-1, keepdims=True)
    el2 = jnp.where(lane == i1, neg, el)
    m2 = jnp.max(el2, axis=-1, keepdims=True)
    i2 = jnp.min(jnp.where(el2 == m2, lane, big), axis=-1, keepdims=True)
    z = jnp.sum(jnp.exp(el - m1), axis=-1, keepdims=True)
    p1 = 1.0 / z
    p2 = jnp.exp(m2 - m1) / z
    tot = p1 + p2
    w1 = p1 / tot * gw
    w2 = p2 / tot * gw
    ids = jnp.where(lane == 0, i1 - MOE_GROUPS, jnp.where(lane == 1, i2 - MOE_GROUPS, 0.0))
    id_ref[...] = ids.astype(jnp.int32)
    w_ref[...] = jnp.where(lane == 0, w1, jnp.where(lane == 1, w2, 0.0))


def _router_call(x, g, w_grp, b_grp, w_exp, b_exp):
    t, d = x.shape
    pad = LANES - MOE_GROUPS - N_EXP
    wr = jnp.concatenate([w_grp, w_exp, jnp.zeros((d, pad), F32)], axis=1)
    br = jnp.concatenate([b_grp, b_exp, jnp.zeros((pad,), F32)]).reshape(1, LANES)
    row = lambda i: (i, 0)
    fixed = lambda i: (0, 0)
    return pl.pallas_call(
        _router_kernel,
        grid=(t // ROW_TILE,),
        in_specs=[pl.BlockSpec((ROW_TILE, d), row), pl.BlockSpec((1, d), fixed),
                  pl.BlockSpec((d, LANES), fixed), pl.BlockSpec((1, LANES), fixed)],
        out_specs=[pl.BlockSpec((ROW_TILE, LANES), row), pl.BlockSpec((ROW_TILE, LANES), row)],
        out_shape=[jax.ShapeDtypeStruct((t, LANES), jnp.int32), jax.ShapeDtypeStruct((t, LANES), F32)],
        compiler_params=_params(("arbitrary",), 40),
        name="moe_router",
    )(x, g.reshape(1, d), wr, br)


def _cumsum_rows(hot):
    t, n = hot.shape
    blk = LANES
    h3 = hot.reshape(t // blk, blk, n).astype(F32)
    tri = jnp.tril(jnp.ones((blk, blk), F32))
    inner = jnp.einsum("ij,bjn->bin", tri, h3).astype(jnp.int32)
    totals = inner[:, -1, :]
    offsets = jnp.cumsum(totals, axis=0) - totals
    return (inner + offsets[:, None, :]).reshape(t, n)


def _moe_plan(e0, e1):
    t = e0.shape[0]
    n_tiles = 2 * t // MOE_TILE + N_EXP
    experts = jnp.arange(N_EXP, dtype=jnp.int32)[None, :]
    hot0 = (e0[:, None] == experts).astype(jnp.int32)
    hot1 = (e1[:, None] == experts).astype(jnp.int32)
    csum0 = _cumsum_rows(hot0)
    csum1 = _cumsum_rows(hot1)
    counts0 = csum0[-1]
    counts = counts0 + csum1[-1]
    tiles_e = (counts + MOE_TILE - 1) // MOE_TILE
    tiles_end = jnp.cumsum(tiles_e)
    row_start = (tiles_end - tiles_e) * MOE_TILE
    pos0 = jnp.sum(hot0 * (row_start[None, :] + csum0 - 1), axis=1)
    pos1 = jnp.sum(hot1 * (row_start[None, :] + counts0[None, :] + csum1 - 1), axis=1)
    pos = jnp.concatenate([pos0, pos1])
    n_valid = tiles_end[-1]
    tile_id = jnp.minimum(jnp.arange(n_tiles, dtype=jnp.int32), n_valid - 1)
    tile_expert = jnp.sum((tile_id[:, None] >= tiles_end[None, :]).astype(jnp.int32), axis=1)
    tile_expert = jnp.minimum(tile_expert, N_EXP - 1)
    changed = jnp.concatenate([jnp.ones((1,), bool), tile_expert[1:] != tile_expert[:-1]])
    first = jnp.logical_and(changed, jnp.arange(n_tiles) < n_valid).astype(jnp.int32)
    wslot = (jnp.cumsum(first) - 1) % 2
    next_start = tiles_end[tile_expert]
    next_expert = jnp.where(next_start < n_valid, tile_expert[jnp.minimum(next_start, n_tiles - 1)], -1)
    tile_meta = jnp.stack([tile_expert, first, wslot, next_expert]).astype(jnp.int32)
    pad_meta = jnp.stack([row_start + counts, tiles_e * MOE_TILE - counts]).astype(jnp.int32)
    return tile_meta, n_valid.reshape(1).astype(jnp.int32), pad_meta, pos.astype(jnp.int32)


def _dispatch_kernel(pos_ref, nv_ref, pad_ref, x_ref, g_ref, xs_hbm, xnbuf, zeros, sems, zsem):
    i = pl.program_id(0)
    n = pl.num_programs(0)
    tm = x_ref.shape[0]
    n_tiles = xs_hbm.shape[0] // MOE_TILE
    n_tok = n * tm
    slot = lax.rem(i, 2)

    def wait_rows(s):
        for kk in range(2):
            pltpu.make_async_copy(xnbuf.at[s], xs_hbm.at[pl.ds(0, tm)], sems.at[s]).wait()

    pl.when(i >= 2)(lambda: wait_rows(slot))
    xnbuf[slot] = _rms(x_ref[...], g_ref[...])
    for r in range(tm):
        for kk in range(2):
            pltpu.make_async_copy(xnbuf.at[slot, pl.ds(r, 1)],
                                  xs_hbm.at[pl.ds(pos_ref[kk * n_tok + i * tm + r], 1)],
                                  sems.at[slot]).start(priority=kk)

    def zero_rows(start, n):
        return pltpu.make_async_copy(zeros.at[pl.ds(0, n)], xs_hbm.at[pl.ds(start, n)], zsem)

    block_sizes = [1 << b for b in range(MOE_TILE.bit_length() - 2, -1, -1)]

    def fill_expert(e, carry):
        start, length = pad_ref[0, e], pad_ref[1, e]
        head = jnp.bitwise_and(-start, SUBLANES - 1)
        lax.fori_loop(0, head, lambda j, c: (zero_rows(start + j, 1).start(), c)[1], 0)
        rest = length - head
        off = start + head
        for sz in block_sizes:
            if sz < SUBLANES:
                break
            take = jnp.bitwise_and(rest, sz) != 0
            pl.when(take)(lambda off=off, sz=sz: zero_rows(pl.multiple_of(off, SUBLANES), sz).start())
            off = off + jnp.where(take, sz, 0)
        return carry

    def drain_expert(e, carry):
        length = pad_ref[1, e]
        for sz in block_sizes:
            pl.when(jnp.bitwise_and(length, sz) != 0)(lambda sz=sz: zero_rows(0, sz).wait())
        return carry

    def tail_tile_copy(tile):
        return pltpu.make_async_copy(zeros, xs_hbm.at[pl.ds(tile * MOE_TILE, MOE_TILE)], zsem)

    @pl.when(i == 0)
    def _():
        zeros[...] = jnp.zeros_like(zeros)
        lax.fori_loop(0, N_EXP, fill_expert, 0)
        lax.fori_loop(nv_ref[0], n_tiles, lambda tile, c: (tail_tile_copy(tile).start(), c)[1], 0)

    @pl.when(i == n - 1)
    def _():
        pl.when(n >= 2)(lambda: wait_rows(1 - slot))
        wait_rows(slot)
        lax.fori_loop(0, N_EXP, drain_expert, 0)
        lax.fori_loop(nv_ref[0], n_tiles, lambda tile, c: (tail_tile_copy(tile).wait(), c)[1], 0)


def _dispatch_call(x, g, pos, n_valid, pad_meta, rows):
    t, d = x.shape
    grid_spec = pltpu.PrefetchScalarGridSpec(
        num_scalar_prefetch=3,
        grid=(t // ROW_TILE,),
        in_specs=[pl.BlockSpec((ROW_TILE, d), lambda i, pos_ref, nv, pad: (i, 0)),
                  pl.BlockSpec((1, d), lambda i, pos_ref, nv, pad: (0, 0))],
        out_specs=pl.BlockSpec(memory_space=pl.ANY),
        scratch_shapes=[pltpu.VMEM((2, ROW_TILE, d), F32), pltpu.VMEM((MOE_TILE, d), F32),
                        pltpu.SemaphoreType.DMA((2,)), pltpu.SemaphoreType.DMA(())],
    )
    return pl.pallas_call(
        _dispatch_kernel,
        grid_spec=grid_spec,
        out_shape=jax.ShapeDtypeStruct((rows, d), F32),
        compiler_params=_params(("arbitrary",), 32),
        name="moe_dispatch",
    )(pos, n_valid, pad_meta, x, g.reshape(1, d))


def _moe_ffn_kernel(tm_ref, nv_ref, x_ref, wgu_hbm, wd_hbm, o_ref,
                    wgu_f, wd_f, wsems, wgu_b, wd_b, *, layer):
    i = pl.program_id(0)
    nv = nv_ref[0]
    expert, first, wslot, next_expert = tm_ref[0, i], tm_ref[1, i], tm_ref[2, i], tm_ref[3, i]

    def weight_copies(e, s):
        return (pltpu.make_async_copy(wgu_hbm.at[layer, e], wgu_f.at[s], wsems.at[0, s]),
                pltpu.make_async_copy(wd_hbm.at[layer, e], wd_f.at[s], wsems.at[1, s]))

    @pl.when(i == 0)
    def _():
        for c in weight_copies(expert, 0):
            c.start()

    @pl.when(first == 1)
    def _():
        @pl.when(next_expert >= 0)
        def _():
            for c in weight_copies(next_expert, 1 - wslot):
                c.start()

        for c in weight_copies(expert, wslot):
            c.wait()
        wgu_b[...] = wgu_f[wslot].astype(BF16)
        wd_b[...] = wd_f[wslot].astype(BF16)

    @pl.when(i < nv)
    def _():
        h = jnp.dot(x_ref[...].astype(BF16), wgu_b[...], preferred_element_type=F32)
        hg = h[:, :D_FF_E]
        hu = h[:, D_FF_E:]
        act = (hg * _sigmoid(hg) * hu).astype(BF16)
        o_ref[...] = jnp.dot(act, wd_b[...], preferred_element_type=F32)


def _moe_ffn_call(xs, w_gu, w_down, layer, tile_meta, n_valid):
    rows, d = xs.shape
    n_tiles = tile_meta.shape[1]
    valid_tile = lambda i, tm, nv: (jnp.minimum(i, nv[0] - 1), 0)
    grid_spec = pltpu.PrefetchScalarGridSpec(
        num_scalar_prefetch=2,
        grid=(n_tiles,),
        in_specs=[pl.BlockSpec((MOE_TILE, d), valid_tile),
                  pl.BlockSpec(memory_space=pl.ANY), pl.BlockSpec(memory_space=pl.ANY)],
        out_specs=pl.BlockSpec((MOE_TILE, d), valid_tile),
        scratch_shapes=[
            pltpu.VMEM((2, d, 2 * D_FF_E), F32),
            pltpu.VMEM((2, D_FF_E, d), F32),
            pltpu.SemaphoreType.DMA((2, 2)),
            pltpu.VMEM((d, 2 * D_FF_E), BF16),
            pltpu.VMEM((D_FF_E, d), BF16),
        ],
    )
    return pl.pallas_call(
        functools.partial(_moe_ffn_kernel, layer=layer),
        grid_spec=grid_spec,
        out_shape=jax.ShapeDtypeStruct((rows, d), F32),
        input_output_aliases={2: 0},
        compiler_params=_params(("arbitrary",), 48),
        name="moe_experts",
    )(tile_meta, n_valid, xs, w_gu, w_down)


def _combine_kernel(pos_ref, x_ref, w_ref, y_hbm, g_ref, *rest, emit_x, norm_tiles):
    outs = rest[:int(emit_x) + len(norm_tiles)]
    buf, sems = rest[int(emit_x) + len(norm_tiles):]
    i = pl.program_id(0)
    n = pl.num_programs(0)
    slot = lax.rem(i, 2)
    tm = x_ref.shape[0]

    def gather(tile, dst_slot):
        n_tok = n * tm
        for r in range(tm):
            for kk in range(2):
                pltpu.make_async_copy(y_hbm.at[pl.ds(pos_ref[kk * n_tok + tile * tm + r], 1)],
                                      buf.at[dst_slot, kk, pl.ds(r, 1)], sems.at[dst_slot]).start(priority=kk)

    @pl.when(i == 0)
    def _():
        gather(0, 0)

    for s in range(2):
        pl.when(jnp.logical_and(i + 1 < n, slot == s))(functools.partial(gather, i + 1, 1 - s))

    for kk in range(2):
        pltpu.make_async_copy(y_hbm.at[pl.ds(0, tm)], buf.at[slot, kk], sems.at[slot]).wait()
    xo = x_ref[...] + (w_ref[:, 0:1] * buf[slot, 0] + w_ref[:, 1:2] * buf[slot, 1])
    norm_refs = outs[int(emit_x):]
    if emit_x:
        outs[0][...] = xo

    def write_norm(p):
        norm_refs[p][...] = _rms(xo, g_ref[...]).astype(norm_refs[p].dtype)

    _for_each_part(i, norm_tiles, write_norm)


def _combine_call(x, ws, y_rows, pos, g, emit_x, norm_dtype, norm_rows):
    t, d = x.shape
    tm = MOE_TILE
    row = lambda i, pos_ref: (i, 0)
    out_specs, out_shape = [], []
    if emit_x:
        out_specs.append(pl.BlockSpec((tm, d), row))
        out_shape.append(jax.ShapeDtypeStruct((t, d), F32))
    norm_parts = [jax.ShapeDtypeStruct((r, d), norm_dtype) for r in norm_rows]
    out_specs += _part_specs(norm_parts, tm, d, row)
    out_shape += norm_parts
    grid_spec = pltpu.PrefetchScalarGridSpec(
        num_scalar_prefetch=1,
        grid=(t // tm,),
        in_specs=[pl.BlockSpec((tm, d), row), pl.BlockSpec((tm, LANES), row), pl.BlockSpec(memory_space=pl.ANY),
                  pl.BlockSpec((1, d), lambda i, pos_ref: (0, 0))],
        out_specs=out_specs,
        scratch_shapes=[pltpu.VMEM((2, 2, tm, d), F32), pltpu.SemaphoreType.DMA((2,))],
    )
    return pl.pallas_call(
        functools.partial(_combine_kernel, emit_x=emit_x, norm_tiles=_part_tiles(norm_parts, tm)),
        grid_spec=grid_spec,
        out_shape=out_shape,
        compiler_params=_params(("arbitrary",), 40),
        name="moe_combine",
    )(pos, x, ws, y_rows, g.reshape(1, d))


def _moe_layer(x, g_ffn, w_grp, b_grp, w_exp, b_exp, w_gu, w_down, layer, g_next, emit_x, norm_dtype, norm_rows):
    ids, ws = _router_call(x, g_ffn, w_grp, b_grp, w_exp, b_exp)
    tile_meta, n_valid, pad_meta, pos = _moe_plan(ids[:, 0], ids[:, 1])
    xs = _dispatch_call(x, g_ffn, pos, n_valid, pad_meta, tile_meta.shape[1] * MOE_TILE)
    y_rows = _moe_ffn_call(xs, w_gu, w_down, layer, tile_meta, n_valid)
    return _combine_call(x, ws, y_rows, pos, g_next, emit_x, norm_dtype, norm_rows)


def _s5_prep_kernel(are_ref, aim_ref, ldt_ref, bre_ref, bim_ref, lre_ref, lim_ref, bbr_ref, bbi_ref):
    are = are_ref[...]
    aim = aim_ref[...]
    dt = jnp.exp(ldt_ref[...])
    mag = jnp.exp(are * dt)
    lr = mag * jnp.cos(aim * dt)
    li = mag * jnp.sin(aim * dt)
    lre_ref[...] = lr
    lim_ref[...] = li
    den = are * are + aim * aim
    nr = lr - 1.0
    cr = (nr * are + li * aim) / den
    ci = (li * are - nr * aim) / den
    bre = bre_ref[...]
    bim = bim_ref[...]
    bbr_ref[...] = cr * bre - ci * bim
    bbi_ref[...] = cr * bim + ci * bre


def _s5_prep_call(a_re, a_im, log_dt, b_re, b_im):
    g, p, c = b_re.shape
    shp = (g, p * c)
    expand = lambda a: jnp.broadcast_to(a[:, :, None], (g, p, c)).reshape(shp)
    ldt = jnp.broadcast_to(log_dt[:, None], shp)
    outs = pl.pallas_call(
        _s5_prep_kernel,
        out_shape=[jax.ShapeDtypeStruct(shp, F32)] * 4,
        name="s5_discretize",
    )(expand(a_re), expand(a_im), ldt, b_re.reshape(shp), b_im.reshape(shp))
    lre, lim, bbr, bbi = outs
    return lre[:, ::c], lim[:, ::c], bbr, bbi


def _s5_compact(bbr, bbi, c_re, c_im):
    def b_part(bb):
        return bb.reshape(S5_FB, 16, S5_P, S5_GC).transpose(0, 1, 3, 2).reshape(S5_FB, 16 * S5_GC, S5_P)

    def c_part(cc):
        return cc.reshape(S5_FB, 16, S5_GC, S5_P).transpose(0, 3, 1, 2).reshape(S5_FB, S5_P, 16 * S5_GC)

    return b_part(bbr), b_part(bbi), c_part(c_re), c_part(c_im)


def _s5_build_mats(bbr_ref, bbi_ref, ccr_ref, cci_ref, bd_ref, cd_ref):
    nch, nst = 16 * S5_GC, 16 * S5_P
    iota = lambda shape, dim: lax.broadcasted_iota(jnp.int32, shape, dim)
    rep = ((iota((S5_P, nst), 1) & (S5_P - 1)) == iota((S5_P, nst), 0)).astype(BF16)
    rep_t = ((iota((nst, S5_P), 0) & (S5_P - 1)) == iota((nst, S5_P), 1)).astype(BF16)
    gc_bits, p_bits = S5_GC.bit_length() - 1, S5_P.bit_length() - 1
    diag = (iota((nch, nst), 0) >> gc_bits) == (iota((nch, nst), 1) >> p_bits)
    diag_t = (iota((nst, nch), 0) >> p_bits) == (iota((nst, nch), 1) >> gc_bits)
    for fb in range(S5_FB):
        for part, src in enumerate((bbr_ref, bbi_ref)):
            full = jnp.dot(src[fb].astype(BF16), rep, preferred_element_type=F32)
            bd_ref[fb, :, part * nst:(part + 1) * nst] = jnp.where(diag, full, 0.0).astype(BF16)
        for part, (src, sign) in enumerate(((ccr_ref, 1.0), (cci_ref, -1.0))):
            full = jnp.dot(rep_t, src[fb].astype(BF16), preferred_element_type=F32)
            cd_ref[fb, part * nst:(part + 1) * nst, :] = jnp.where(diag_t, sign * full, 0.0).astype(BF16)


def _s5_kernel(u_ref, bbr_ref, bbi_ref, ccr_ref, cci_ref, lre_ref, lim_ref, d_ref, h0r_ref, h0i_ref,
               z_ref, hr_ref, hi_ref, scr, hst, bd_ref, cd_ref, *, seg, carry):
    rows = u_ref.shape[0]
    n_seq = rows // seg
    ch = 16 * S5_GC
    half = SUBLANES * S5_PITCH
    blk = 2 * half

    @pl.when(pl.program_id(0) == 0)
    def _():
        _s5_build_mats(bbr_ref, bbi_ref, ccr_ref, cci_ref, bd_ref, cd_ref)

    for fb in range(S5_FB):
        bu = jnp.dot(u_ref[:, fb * ch:(fb + 1) * ch].astype(BF16), bd_ref[fb], preferred_element_type=F32)
        for c in range(2 * SUBLANES):
            r0 = fb * blk + c * S5_PITCH
            scr[r0:r0 + rows, :] = bu[:, c * LANES:(c + 1) * LANES]

    lre = [lre_ref[fb * SUBLANES:(fb + 1) * SUBLANES, :] for fb in range(S5_FB)]
    lim = [lim_ref[fb * SUBLANES:(fb + 1) * SUBLANES, :] for fb in range(S5_FB)]

    def scan(row0, state):
        def step(l, st):
            new = []
            for fb in range(S5_FB):
                hr, hi = st[2 * fb], st[2 * fb + 1]
                ire = pl.ds(fb * blk + row0 + l, SUBLANES, stride=S5_PITCH)
                iim = pl.ds(fb * blk + half + row0 + l, SUBLANES, stride=S5_PITCH)
                nr = lre[fb] * hr - lim[fb] * hi + scr[ire, :]
                ni = lre[fb] * hi + lim[fb] * hr + scr[iim, :]
                scr[ire, :] = nr
                scr[iim, :] = ni
                new += [nr, ni]
            return tuple(new)
        return lax.fori_loop(0, seg, step, state, unroll=8)

    def load_state(rref, iref, lead):
        st = []
        for fb in range(S5_FB):
            st += [rref[lead + (slice(fb * SUBLANES, (fb + 1) * SUBLANES), slice(None))],
                   iref[lead + (slice(fb * SUBLANES, (fb + 1) * SUBLANES), slice(None))]]
        return tuple(st)

    def store_state(st, rref, iref, lead):
        for fb in range(S5_FB):
            rref[lead + (slice(fb * SUBLANES, (fb + 1) * SUBLANES), slice(None))] = st[2 * fb]
            iref[lead + (slice(fb * SUBLANES, (fb + 1) * SUBLANES), slice(None))] = st[2 * fb + 1]

    if carry:
        @pl.when(pl.program_id(0) == 0)
        def _():
            hst[0] = h0r_ref[0]
            hst[1] = h0i_ref[0]
        hre_scr = hst.at[0]
        him_scr = hst.at[1]
        st = scan(0, load_state(hre_scr, him_scr, ()))
        store_state(st, hre_scr, him_scr, ())
        store_state(st, hr_ref, hi_ref, (0,))
    else:
        for s in range(n_seq):
            st = scan(s * seg, load_state(h0r_ref, h0i_ref, (s,)))
            store_state(st, hr_ref, hi_ref, (s,))

    for fb in range(S5_FB):
        hf = jnp.concatenate(
            [scr[fb * blk + c * S5_PITCH: fb * blk + c * S5_PITCH + rows, :] for c in range(2 * SUBLANES)],
            axis=1).astype(BF16)
        y = jnp.dot(hf, cd_ref[fb], preferred_element_type=F32)
        cs = slice(fb * ch, (fb + 1) * ch)
        yy = y + d_ref[:, cs] * u_ref[:, cs]
        z_ref[:, cs] = jax.nn.gelu(yy).astype(z_ref.dtype)


def _s5_call(u, mats, lre, lim, d_skip, h0r, h0i, row_block0, n_blocks, seg, carry):
    t, d = u.shape
    ch, nst = 16 * S5_GC, 16 * S5_P
    n_seq = S5_ROWS // seg
    ns_total = h0r.shape[0]
    st_idx = (lambda i: (0, 0, 0)) if carry else (lambda i: (i, 0, 0))
    fixed2 = lambda i: (0, 0)
    fixed3 = lambda i: (0, 0, 0)
    st_rows = S5_FB * SUBLANES
    return pl.pallas_call(
        functools.partial(_s5_kernel, seg=seg, carry=carry),
        grid=(n_blocks,),
        in_specs=[
            pl.BlockSpec((S5_ROWS, d), lambda i: (row_block0 + i, 0)),
            *[pl.BlockSpec(m.shape, fixed3) for m in mats],
            pl.BlockSpec(lre.shape, fixed2),
            pl.BlockSpec(lim.shape, fixed2),
            pl.BlockSpec((1, d), fixed2),
            pl.BlockSpec((n_seq, st_rows, LANES), st_idx),
            pl.BlockSpec((n_seq, st_rows, LANES), st_idx),
        ],
        out_specs=[
            pl.BlockSpec((S5_ROWS, d), lambda i: (i, 0)),
            pl.BlockSpec((n_seq, st_rows, LANES), st_idx),
            pl.BlockSpec((n_seq, st_rows, LANES), st_idx),
        ],
        out_shape=[
            jax.ShapeDtypeStruct((n_blocks * S5_ROWS, d), BF16),
            jax.ShapeDtypeStruct((ns_total, st_rows, LANES), F32),
            jax.ShapeDtypeStruct((ns_total, st_rows, LANES), F32),
        ],
        scratch_shapes=[
            pltpu.VMEM((2 * SUBLANES * S5_FB * S5_PITCH, LANES), F32),
            pltpu.VMEM((2, st_rows, LANES), F32),
            pltpu.VMEM((S5_FB, ch, 2 * nst), BF16),
            pltpu.VMEM((S5_FB, 2 * nst, ch), BF16),
        ],
        compiler_params=_params(("arbitrary",), 56),
        name="s5_scan",
    )(u, *mats, lre, lim, d_skip.reshape(1, d), h0r, h0i)


def kernel(x_prompt, x_sample, state_ret, state_s5_re, state_s5_im, norm_mix_g, norm_ffn_g, norm_final_g,
           ret_w_in, ret_gn_g, ret_w_out, s5_w_in, s5_a_re, s5_a_im, s5_log_dt, s5_b_re, s5_b_im,
           s5_c_re, s5_c_im, s5_d, s5_w_out, moe_w_grp, moe_b_grp, moe_w_exp, moe_b_exp, moe_w_gu, moe_w_down):
    bp, seq, d = x_prompt.shape
    db, dseq, _ = x_sample.shape
    assert d == D_MODEL and bp == 1
    assert seq % max(ROW_TILE, RET_ROWS) == 0 and seq % CHUNK == 0
    assert dseq < CHUNK and dseq % 16 == 0 and S5_ROWS % dseq == 0
    tp = bp * seq
    ts = db * dseq
    t = tp + ts
    assert ts % ROW_TILE == 0 and ts % S5_ROWS == 0

    x_parts = [x_prompt.reshape(tp, d), x_sample.reshape(ts, d)]

    half = RET_DK // 2
    freqs = ROPE_BASE ** (-jnp.arange(half, dtype=F32) / half)
    pos = jnp.concatenate([jnp.arange(seq, dtype=F32), jnp.tile(PAST_LEN + jnp.arange(dseq, dtype=F32), db)])
    ang = pos[:, None] * freqs[None, :]
    cos_t = jnp.cos(ang)
    sin_t = jnp.sin(ang)

    xn = _norm_call(x_parts, norm_mix_g[0], BF16)
    tn = 2048
    first_col = lambda j: 0
    qk = _mm_call([xn], ret_w_in[0], [0], tn, 2 * D_MODEL, [([cos_t], half, first_col), ([sin_t], half, first_col)],
                  BF16, functools.partial(_epi_rope, n_q_blocks=D_MODEL // tn), 56, "ret_qk_proj")
    vg = _mm_call([xn], ret_w_in[0], [2 * D_MODEL // tn], tn, 2 * RET_VW, [], BF16, _epi_plain, 56, "ret_vg_proj")
    og_p, ret_p = _ret_prompt_call(qk, vg, ret_gn_g[0], tp)
    og_s, ret_s = _ret_sample_call(qk, vg, ret_gn_g[0], state_ret[0], tp, dseq)
    tn = 512
    same_col = lambda j: j
    x1 = _mm_resident_call([og_p, og_s], ret_w_out[0], [0], tn, d, [(x_parts, tn, same_col)], F32, _epi_residual,
                           58, "ret_out_proj")

    x2, xn = _moe_layer(x1, norm_ffn_g[0], moe_w_grp[0], moe_b_grp[0], moe_w_exp[0], moe_b_exp[0],
                        moe_w_gu, moe_w_down, 0, norm_mix_g[1], True, BF16, (t,))

    u = _mm_call([xn], s5_w_in[0], [0], 2048, d, [], F32, _epi_plain, 56, "s5_in_proj")
    lre, lim, bbr, bbi = _s5_prep_call(s5_a_re[0], s5_a_im[0], s5_log_dt[0], s5_b_re[0], s5_b_im[0])
    mats = _s5_compact(bbr, bbi, s5_c_re[0], s5_c_im[0])
    st_shape = (S5_FB * SUBLANES, LANES)
    lre = lre.reshape(st_shape)
    lim = lim.reshape(st_shape)
    zero_state = jnp.zeros((1,) + st_shape, F32)
    z_p, hpr, hpi = _s5_call(u, mats, lre, lim, s5_d[0], zero_state, zero_state,
                             0, tp // S5_ROWS, S5_ROWS, True)
    z_s, hsr, hsi = _s5_call(u, mats, lre, lim, s5_d[0],
                             state_s5_re[0].reshape((db,) + st_shape), state_s5_im[0].reshape((db,) + st_shape),
                             tp // S5_ROWS, ts // S5_ROWS, dseq, False)
    x3 = _mm_resident_call([z_p, z_s], s5_w_out[0], [0, d // tn], tn, d, [([x2], tn, same_col)], F32,
                           _epi_glu_residual, 48, "s5_out_proj")

    y_p, y_s = _moe_layer(x3, norm_ffn_g[1], moe_w_grp[1], moe_b_grp[1], moe_w_exp[1], moe_b_exp[1],
                          moe_w_gu, moe_w_down, 1, norm_final_g, False, F32, (tp, ts))

    s5_shape = (1, -1, S5_GROUPS, S5_P)
    return (y_p.reshape(bp, seq, d), y_s.reshape(db, dseq, d),
            ret_p.reshape(1, bp, RET_HEADS, RET_DK, RET_DV), ret_s.reshape((1,) + ret_s.shape),
            hpr.reshape(s5_shape), hpi.reshape(s5_shape), hsr.reshape(s5_shape), hsi.reshape(s5_shape))
```

```python
import functools

import numpy as np
import jax
import jax.numpy as jnp
from jax import lax
from jax.experimental import pallas as pl
from jax.experimental.pallas import tpu as pltpu

F32 = jnp.float32
BF16 = jnp.bfloat16

D_MODEL = 2048
PAST_LEN = 2048
CHUNK = 64
RET_HEADS = 8
RET_DK = D_MODEL // RET_HEADS
RET_DV = 2 * RET_DK
RET_VW = RET_HEADS * RET_DV
ROPE_BASE = 10000.0
S5_GC = 16
S5_GROUPS = D_MODEL // S5_GC
S5_P = 64
MOE_GROUPS = 4
MOE_PER_GROUP = 8
N_EXP = MOE_GROUPS * MOE_PER_GROUP
D_FF_E = 512
EPS = 1e-6

LANES = 128
SUBLANES = 8
ROW_TILE = 512
MM_CHUNK = 256
RET_ROWS = 256
RET_HPS = 8
MOE_TILE = 256
S5_ROWS = 256
S5_FB = S5_GROUPS // 16
S5_PITCH = S5_ROWS + 4
MIB = 1024 * 1024


def _params(sem, vmem_mib):
    return pltpu.CompilerParams(dimension_semantics=sem, vmem_limit_bytes=vmem_mib * MIB)


def _rms(x, g):
    ms = jnp.mean(x * x, axis=-1, keepdims=True)
    return x * lax.rsqrt(ms + EPS) * g


def _sigmoid(x):
    return 1.0 / (1.0 + jnp.exp(-x))


def _part_specs(parts, rows, cols, rowcol):
    specs = []
    start = 0
    for p in parts:
        nt = p.shape[0] // rows

        def imap(*g, start=start, nt=nt):
            i, jc = rowcol(*g)
            return (jnp.clip(i - start, 0, nt - 1), jc)

        specs.append(pl.BlockSpec((rows, cols), imap))
        start += nt
    return specs


def _part_tiles(parts, rows):
    return tuple(p.shape[0] // rows for p in parts)


def _for_each_part(i, tiles, fn):
    if len(tiles) == 1:
        fn(0)
        return
    start = 0
    for p, nt in enumerate(tiles):
        pl.when(jnp.logical_and(i >= start, i < start + nt))(functools.partial(fn, p))
        start += nt


def _pick(refs, p):
    return refs[min(p, len(refs) - 1)]


def _norm_kernel(*refs, tiles):
    n = len(tiles)
    x_refs, g_ref, o_ref = refs[:n], refs[n], refs[n + 1]

    def run(p):
        o_ref[...] = _rms(x_refs[p][...], g_ref[...]).astype(o_ref.dtype)

    _for_each_part(pl.program_id(0), tiles, run)


def _norm_call(x_parts, g, out_dtype):
    d = x_parts[0].shape[1]
    t = sum(p.shape[0] for p in x_parts)
    return pl.pallas_call(
        functools.partial(_norm_kernel, tiles=_part_tiles(x_parts, ROW_TILE)),
        grid=(t // ROW_TILE,),
        in_specs=_part_specs(x_parts, ROW_TILE, d, lambda i: (i, 0)) + [pl.BlockSpec((1, d), lambda i: (0, 0))],
        out_specs=pl.BlockSpec((ROW_TILE, d), lambda i: (i, 0)),
        out_shape=jax.ShapeDtypeStruct((t, d), out_dtype),
        compiler_params=_params(("arbitrary",), 32),
        name="rmsnorm",
    )(*x_parts, g.reshape(1, d))


def _mm_kernel(*refs, tiles, n_x, n_w, n_extra, epi):
    x_refs = refs[:n_x]
    w_refs = refs[n_x:n_x + n_w]
    pos = n_x + n_w
    extra = []
    for n in n_extra:
        extra.append(refs[pos:pos + n])
        pos += n
    o_ref = refs[pos]
    wb = refs[pos + 1:]

    @pl.when(pl.program_id(1) == 0)
    def _():
        for w_ref, b in zip(w_refs, wb):
            b[...] = w_ref[...].astype(BF16)

    def run(p):
        x = _pick(x_refs, p)[...]
        for c in range(o_ref.shape[1] // MM_CHUNK):
            cs = slice(c * MM_CHUNK, (c + 1) * MM_CHUNK)
            accs = [jnp.dot(x, b[:, cs], preferred_element_type=F32) for b in wb]
            epi(accs, [_pick(e, p) for e in extra], o_ref, cs)

    _for_each_part(pl.program_id(1), tiles, run)


def _mm_call(x_parts, w, w_col_offsets, tn, n_out, extras, out_dtype, epi, vmem_mib, name):
    k = x_parts[0].shape[1]
    t = sum(p.shape[0] for p in x_parts)
    n_w = len(w_col_offsets)
    split = [parts for parts in [x_parts] + [e[0] for e in extras] if len(parts) > 1]
    tiles = _part_tiles(split[0], ROW_TILE) if split else (t // ROW_TILE,)
    assert all(_part_tiles(parts, ROW_TILE) == tiles for parts in split)
    in_specs = _part_specs(x_parts, ROW_TILE, k, lambda j, i: (i, 0))
    for off in w_col_offsets:
        in_specs.append(pl.BlockSpec((k, tn), lambda j, i, off=off: (0, j + off)))
    extra_arrays = []
    for parts, cols, colfn in extras:
        in_specs += _part_specs(parts, ROW_TILE, cols, lambda j, i, colfn=colfn: (i, colfn(j)))
        extra_arrays += list(parts)
    return pl.pallas_call(
        functools.partial(_mm_kernel, tiles=tiles, n_x=len(x_parts), n_w=n_w,
                          n_extra=tuple(len(e[0]) for e in extras), epi=epi),
        grid=(n_out // tn, t // ROW_TILE),
        in_specs=in_specs,
        out_specs=pl.BlockSpec((ROW_TILE, tn), lambda j, i: (i, j)),
        out_shape=jax.ShapeDtypeStruct((t, n_out), out_dtype),
        scratch_shapes=[pltpu.VMEM((k, tn), BF16) for _ in range(n_w)],
        compiler_params=_params(("arbitrary", "arbitrary"), vmem_mib),
        name=name,
    )(*x_parts, *([w] * n_w), *extra_arrays)


def _mm_resident_kernel(*refs, tiles, n_x, n_w, n_extra, epi):
    x_refs = refs[:n_x]
    w_refs = refs[n_x:n_x + n_w]
    pos = n_x + n_w
    extra = []
    for n in n_extra:
        extra.append(refs[pos:pos + n])
        pos += n
    o_ref = refs[pos]
    wb = refs[pos + 1:]
    i = pl.program_id(0)
    j = pl.program_id(1)

    @pl.when(i == 0)
    def _():
        for w_ref, b in zip(w_refs, wb):
            b[j] = w_ref[...].astype(BF16)

    def run(p):
        x = _pick(x_refs, p)[...]
        for c in range(o_ref.shape[1] // MM_CHUNK):
            cs = slice(c * MM_CHUNK, (c + 1) * MM_CHUNK)
            accs = [jnp.dot(x, b[j, :, cs], preferred_element_type=F32) for b in wb]
            epi(accs, [_pick(e, p) for e in extra], o_ref, cs)

    _for_each_part(i, tiles, run)


def _mm_resident_call(x_parts, w, w_col_offsets, tn, n_out, extras, out_dtype, epi, vmem_mib, name):
    k = x_parts[0].shape[1]
    t = sum(p.shape[0] for p in x_parts)
    n_w = len(w_col_offsets)
    nj = n_out // tn
    split = [parts for parts in [x_parts] + [e[0] for e in extras] if len(parts) > 1]
    tiles = _part_tiles(split[0], ROW_TILE) if split else (t // ROW_TILE,)
    assert all(_part_tiles(parts, ROW_TILE) == tiles for parts in split)
    in_specs = _part_specs(x_parts, ROW_TILE, k, lambda i, j: (i, 0))
    for off in w_col_offsets:
        in_specs.append(pl.BlockSpec((k, tn), lambda i, j, off=off: (0, jnp.where(i == 0, j, nj - 1) + off)))
    extra_arrays = []
    for parts, cols, colfn in extras:
        in_specs += _part_specs(parts, ROW_TILE, cols, lambda i, j, colfn=colfn: (i, colfn(j)))
        extra_arrays += list(parts)
    return pl.pallas_call(
        functools.partial(_mm_resident_kernel, tiles=tiles, n_x=len(x_parts), n_w=n_w,
                          n_extra=tuple(len(e[0]) for e in extras), epi=epi),
        grid=(t // ROW_TILE, nj),
        in_specs=in_specs,
        out_specs=pl.BlockSpec((ROW_TILE, tn), lambda i, j: (i, j)),
        out_shape=jax.ShapeDtypeStruct((t, n_out), out_dtype),
        scratch_shapes=[pltpu.VMEM((nj, k, tn), BF16) for _ in range(n_w)],
        compiler_params=_params(("arbitrary", "arbitrary"), vmem_mib),
        name=name,
    )(*x_parts, *([w] * n_w), *extra_arrays)


def _epi_plain(accs, extra, o_ref, cs):
    o_ref[:, cs] = accs[0].astype(o_ref.dtype)


def _epi_residual(accs, extra, o_ref, cs):
    o_ref[:, cs] = (extra[0][:, cs] + accs[0]).astype(o_ref.dtype)


def _epi_glu_residual(accs, extra, o_ref, cs):
    a, gt = accs
    o_ref[:, cs] = (extra[0][:, cs] + a * _sigmoid(gt)).astype(o_ref.dtype)


def _epi_rope(accs, extra, o_ref, cs, *, n_q_blocks):
    acc = accs[0]
    assert acc.shape[1] == RET_DK
    cos_ref, sin_ref = extra
    half = RET_DK // 2
    scale = jnp.where(pl.program_id(0) >= n_q_blocks, RET_DK ** -0.5, 1.0).astype(F32)
    c = cos_ref[...]
    s = sin_ref[...]
    t1 = acc[:, :half]
    t2 = acc[:, half:]
    o_ref[:, cs.start:cs.start + half] = ((t1 * c - t2 * s) * scale).astype(o_ref.dtype)
    o_ref[:, cs.start + half:cs.stop] = ((t1 * s + t2 * c) * scale).astype(o_ref.dtype)


def _ret_tables(rows, chunk):
    lg = jnp.log(jnp.asarray(1.0 - 2.0 ** (-5.0 - np.arange(RET_HEADS)), dtype=F32))[:, None, None]
    n = jnp.arange(rows, dtype=F32)
    ci = np.arange(rows) // chunk
    same_or_earlier = jnp.asarray(ci[None, :] <= ci[:, None])
    dmat = jnp.where(same_or_earlier[None], jnp.exp(jnp.abs(n[:, None] - n[None, :])[None] * lg), 0.0)
    dq = jnp.exp((n + 1.0)[None, :, None] * lg) * jnp.ones((1, 1, LANES), F32)
    dk = jnp.exp((rows - 1.0 - n)[None, :, None] * lg) * jnp.ones((1, 1, LANES), F32)
    gr = jnp.exp(rows * lg) * jnp.ones((1, 1, RET_DV), F32)
    return dmat, dq, dk, gr


def _ret_block(q, k, v, g, s_prev, dmat, dq, dk, gr, gn):
    s = lax.dot_general(q, k, (((1,), (1,)), ((), ())), preferred_element_type=F32) * dmat
    inter = jnp.dot(q, s_prev.astype(BF16), preferred_element_type=F32)
    o = jnp.dot(s.astype(BF16), v, preferred_element_type=F32) + jnp.concatenate([dq] * (RET_DV // LANES), axis=1) * inter
    kd = (k.astype(F32) * jnp.concatenate([dk] * (RET_DK // LANES), axis=1)).astype(BF16)
    s_new = s_prev * gr + lax.dot_general(kd, v, (((0,), (0,)), ((), ())), preferred_element_type=F32)
    mu = jnp.mean(o, axis=-1, keepdims=True)
    oc = o - mu
    var = jnp.mean(oc * oc, axis=-1, keepdims=True)
    on = oc * lax.rsqrt(var + EPS) * gn
    gf = g.astype(F32)
    return (gf * _sigmoid(gf) * on).astype(BF16), s_new


def _ret_prompt_kernel(q_ref, k_ref, v_ref, g_ref, dm_ref, dq_ref, dk_ref, gr_ref, gn_ref,
                       og_ref, st_ref, s_scr):
    @pl.when(pl.program_id(1) == 0)
    def _():
        s_scr[...] = jnp.zeros_like(s_scr)

    for hh in range(RET_HPS):
        ks = slice(hh * RET_DK, (hh + 1) * RET_DK)
        vs = slice(hh * RET_DV, (hh + 1) * RET_DV)
        og, s_new = _ret_block(q_ref[:, ks], k_ref[:, ks], v_ref[:, vs], g_ref[:, vs], s_scr[hh],
                               dm_ref[hh], dq_ref[hh], dk_ref[hh], gr_ref[hh], gn_ref[:, vs])
        og_ref[:, vs] = og
        s_scr[hh] = s_new
        st_ref[hh] = s_new


def _ret_prompt_call(qk, vg, gn, t_prompt):
    dm, dq, dk, gr = _ret_tables(RET_ROWS, CHUNK)
    hps = RET_HPS
    hb = RET_HEADS // hps
    return pl.pallas_call(
        _ret_prompt_kernel,
        grid=(hb, t_prompt // RET_ROWS),
        in_specs=[
            pl.BlockSpec((RET_ROWS, hps * RET_DK), lambda hh, c: (c, hh)),
            pl.BlockSpec((RET_ROWS, hps * RET_DK), lambda hh, c: (c, hb + hh)),
            pl.BlockSpec((RET_ROWS, hps * RET_DV), lambda hh, c: (c, hh)),
            pl.BlockSpec((RET_ROWS, hps * RET_DV), lambda hh, c: (c, hb + hh)),
            pl.BlockSpec((hps, RET_ROWS, RET_ROWS), lambda hh, c: (hh, 0, 0)),
            pl.BlockSpec((hps, RET_ROWS, LANES), lambda hh, c: (hh, 0, 0)),
            pl.BlockSpec((hps, RET_ROWS, LANES), lambda hh, c: (hh, 0, 0)),
            pl.BlockSpec((hps, 1, RET_DV), lambda hh, c: (hh, 0, 0)),
            pl.BlockSpec((1, hps * RET_DV), lambda hh, c: (0, hh)),
        ],
        out_specs=[
            pl.BlockSpec((RET_ROWS, hps * RET_DV), lambda hh, c: (c, hh)),
            pl.BlockSpec((hps, RET_DK, RET_DV), lambda hh, c: (hh, 0, 0)),
        ],
        out_shape=[
            jax.ShapeDtypeStruct((t_prompt, RET_VW), BF16),
            jax.ShapeDtypeStruct((RET_HEADS, RET_DK, RET_DV), F32),
        ],
        scratch_shapes=[pltpu.VMEM((hps, RET_DK, RET_DV), F32)],
        compiler_params=_params(("arbitrary", "arbitrary"), 40),
        name="retention_prompt",
    )(qk, qk, vg, vg, dm, dq, dk, gr, gn.reshape(1, RET_VW))


def _ret_sample_kernel(q_ref, k_ref, v_ref, g_ref, s0_ref, dm_ref, dq_ref, dk_ref, gr_ref, gn_ref,
                       og_ref, st_ref):
    for hh in range(RET_HEADS):
        ks = slice(hh * RET_DK, (hh + 1) * RET_DK)
        vs = slice(hh * RET_DV, (hh + 1) * RET_DV)
        og, s_new = _ret_block(q_ref[:, ks], k_ref[:, ks], v_ref[:, vs], g_ref[:, vs], s0_ref[0, hh],
                               dm_ref[hh], dq_ref[hh], dk_ref[hh], gr_ref[hh], gn_ref[:, vs])
        og_ref[:, vs] = og
        st_ref[0, hh] = s_new


def _ret_sample_call(qk, vg, gn, state, t_prompt, dec_seq):
    dm, dq, dk, gr = _ret_tables(dec_seq, dec_seq)
    b = state.shape[0]
    r0 = t_prompt // dec_seq
    const3 = lambda bb: (0, 0, 0)
    return pl.pallas_call(
        _ret_sample_kernel,
        grid=(b,),
        in_specs=[
            pl.BlockSpec((dec_seq, D_MODEL), lambda bb: (r0 + bb, 0)),
            pl.BlockSpec((dec_seq, D_MODEL), lambda bb: (r0 + bb, 1)),
            pl.BlockSpec((dec_seq, RET_VW), lambda bb: (r0 + bb, 0)),
            pl.BlockSpec((dec_seq, RET_VW), lambda bb: (r0 + bb, 1)),
            pl.BlockSpec((1, RET_HEADS, RET_DK, RET_DV), lambda bb: (bb, 0, 0, 0)),
            pl.BlockSpec(dm.shape, const3),
            pl.BlockSpec(dq.shape, const3),
            pl.BlockSpec(dk.shape, const3),
            pl.BlockSpec(gr.shape, const3),
            pl.BlockSpec((1, RET_VW), lambda bb: (0, 0)),
        ],
        out_specs=[
            pl.BlockSpec((dec_seq, RET_VW), lambda bb: (bb, 0)),
            pl.BlockSpec((1, RET_HEADS, RET_DK, RET_DV), lambda bb: (bb, 0, 0, 0)),
        ],
        out_shape=[
            jax.ShapeDtypeStruct((b * dec_seq, RET_VW), BF16),
            jax.ShapeDtypeStruct(state.shape, F32),
        ],
        compiler_params=_params(("arbitrary",), 40),
        name="retention_sample",
    )(qk, qk, vg, vg, state, dm, dq, dk, gr, gn.reshape(1, RET_VW))


def _router_kernel(x_ref, g_ref, wr_ref, br_ref, id_ref, w_ref):
    xn = _rms(x_ref[...], g_ref[...])
    xh = xn.astype(BF16)
    xl = (xn - xh.astype(F32)).astype(BF16)
    w = wr_ref[...]
    wh = w.astype(BF16)
    wl = (w - wh.astype(F32)).astype(BF16)
    hi = jnp.dot(xh, jnp.concatenate([wh, wl], axis=1), preferred_element_type=F32)
    lg = (hi[:, :LANES] + hi[:, LANES:] + jnp.dot(xl, wh, preferred_element_type=F32)) + br_ref[...]
    lane = lax.broadcasted_iota(jnp.int32, lg.shape, 1).astype(F32)
    neg = jnp.float32(-jnp.inf)
    big = jnp.float32(LANES)
    gl = jnp.where(lane < MOE_GROUPS, lg, neg)
    gmax = jnp.max(gl, axis=-1, keepdims=True)
    gsel = jnp.min(jnp.where(gl == gmax, lane, big), axis=-1, keepdims=True)
    gw = 1.0 / jnp.sum(jnp.exp(gl - gmax), axis=-1, keepdims=True)
    lo = MOE_GROUPS + gsel * MOE_PER_GROUP
    el = jnp.where((lane >= lo) & (lane < lo + MOE_PER_GROUP), lg, neg)
    m1 = jnp.max(el, axis=-1, keepdims=True)
    i1 = jnp.min(jnp.where(el == m1, lane, big), axis=-1, keepdims=True)
    el2 = jnp.where(lane == i1, neg, el)
    m2 = jnp.max(el2, axis=-1, keepdims=True)
    i2 = jnp.min(jnp.where(el2 == m2, lane, big), axis=-1, keepdims=True)
    z = jnp.sum(jnp.exp(el - m1), axis=-1, keepdims=True)
    p1 = 1.0 / z
    p2 = jnp.exp(m2 - m1) / z
    tot = p1 + p2
    w1 = p1 / tot * gw
    w2 = p2 / tot * gw
    ids = jnp.where(lane == 0, i1 - MOE_GROUPS, jnp.where(lane == 1, i2 - MOE_GROUPS, 0.0))
    id_ref[...] = ids.astype(jnp.int32)
    w_ref[...] = jnp.where(lane == 0, w1, jnp.where(lane == 1, w2, 0.0))


def _router_call(x, g, w_grp, b_grp, w_exp, b_exp):
    t, d = x.shape
    pad = LANES - MOE_GROUPS - N_EXP
    wr = jnp.concatenate([w_grp, w_exp, jnp.zeros((d, pad), F32)], axis=1)
    br = jnp.concatenate([b_grp, b_exp, jnp.zeros((pad,), F32)]).reshape(1, LANES)
    row = lambda i: (i, 0)
    fixed = lambda i: (0, 0)
    return pl.pallas_call(
        _router_kernel,
        grid=(t // ROW_TILE,),
        in_specs=[pl.BlockSpec((ROW_TILE, d), row), pl.BlockSpec((1, d), fixed),
                  pl.BlockSpec((d, LANES), fixed), pl.BlockSpec((1, LANES), fixed)],
        out_specs=[pl.BlockSpec((ROW_TILE, LANES), row), pl.BlockSpec((ROW_TILE, LANES), row)],
        out_shape=[jax.ShapeDtypeStruct((t, LANES), jnp.int32), jax.ShapeDtypeStruct((t, LANES), F32)],
        compiler_params=_params(("arbitrary",), 40),
        name="moe_router",
    )(x, g.reshape(1, d), wr, br)


def _cumsum_rows(hot):
    t, n = hot.shape
    blk = LANES
    h3 = hot.reshape(t // blk, blk, n).astype(F32)
    tri = jnp.tril(jnp.ones((blk, blk), F32))
    inner = jnp.einsum("ij,bjn->bin", tri, h3).astype(jnp.int32)
    totals = inner[:, -1, :]
    offsets = jnp.cumsum(totals, axis=0) - totals
    return (inner + offsets[:, None, :]).reshape(t, n)


def _moe_plan(e0, e1):
    t = e0.shape[0]
    n_tiles = 2 * t // MOE_TILE + N_EXP
    experts = jnp.arange(N_EXP, dtype=jnp.int32)[None, :]
    hot0 = (e0[:, None] == experts).astype(jnp.int32)
    hot1 = (e1[:, None] == experts).astype(jnp.int32)
    csum0 = _cumsum_rows(hot0)
    csum1 = _cumsum_rows(hot1)
    counts0 = csum0[-1]
    counts = counts0 + csum1[-1]
    tiles_e = (counts + MOE_TILE - 1) // MOE_TILE
    tiles_end = jnp.cumsum(tiles_e)
    row_start = (tiles_end - tiles_e) * MOE_TILE
    pos0 = jnp.sum(hot0 * (row_start[None, :] + csum0 - 1), axis=1)
    pos1 = jnp.sum(hot1 * (row_start[None, :] + counts0[None, :] + csum1 - 1), axis=1)
    pos = jnp.concatenate([pos0, pos1])
    n_valid = tiles_end[-1]
    tile_id = jnp.minimum(jnp.arange(n_tiles, dtype=jnp.int32), n_valid - 1)
    tile_expert = jnp.sum((tile_id[:, None] >= tiles_end[None, :]).astype(jnp.int32), axis=1)
    tile_expert = jnp.minimum(tile_expert, N_EXP - 1)
    changed = jnp.concatenate([jnp.ones((1,), bool), tile_expert[1:] != tile_expert[:-1]])
    first = jnp.logical_and(changed, jnp.arange(n_tiles) < n_valid).astype(jnp.int32)
    wslot = (jnp.cumsum(first) - 1) % 2
    next_start = tiles_end[tile_expert]
    next_expert = jnp.where(next_start < n_valid, tile_expert[jnp.minimum(next_start, n_tiles - 1)], -1)
    tile_meta = jnp.stack([tile_expert, first, wslot, next_expert]).astype(jnp.int32)
    pad_meta = jnp.stack([row_start + counts, tiles_e * MOE_TILE - counts]).astype(jnp.int32)
    return tile_meta, n_valid.reshape(1).astype(jnp.int32), pad_meta, pos.astype(jnp.int32)


def _dispatch_kernel(pos_ref, nv_ref, pad_ref, x_ref, g_ref, xs_hbm, xnbuf, zeros, sems, zsem):
    i = pl.program_id(0)
    n = pl.num_programs(0)
    tm = x_ref.shape[0]
    n_tiles = xs_hbm.shape[0] // MOE_TILE
    n_tok = n * tm
    slot = lax.rem(i, 2)

    def wait_rows(s):
        for kk in range(2):
            pltpu.make_async_copy(xnbuf.at[s], xs_hbm.at[pl.ds(0, tm)], sems.at[s]).wait()

    pl.when(i >= 2)(lambda: wait_rows(slot))
    xnbuf[slot] = _rms(x_ref[...], g_ref[...])
    for r in range(tm):
        for kk in range(2):
            pltpu.make_async_copy(xnbuf.at[slot, pl.ds(r, 1)],
                                  xs_hbm.at[pl.ds(pos_ref[kk * n_tok + i * tm + r], 1)],
                                  sems.at[slot]).start(priority=kk)

    def zero_rows(start, n):
        return pltpu.make_async_copy(zeros.at[pl.ds(0, n)], xs_hbm.at[pl.ds(start, n)], zsem)

    block_sizes = [1 << b for b in range(MOE_TILE.bit_length() - 2, -1, -1)]

    def fill_expert(e, carry):
        start, length = pad_ref[0, e], pad_ref[1, e]
        head = jnp.bitwise_and(-start, SUBLANES - 1)
        lax.fori_loop(0, head, lambda j, c: (zero_rows(start + j, 1).start(), c)[1], 0)
        rest = length - head
        off = start + head
        for sz in block_sizes:
            if sz < SUBLANES:
                break
            take = jnp.bitwise_and(rest, sz) != 0
            pl.when(take)(lambda off=off, sz=sz: zero_rows(pl.multiple_of(off, SUBLANES), sz).start())
            off = off + jnp.where(take, sz, 0)
        return carry

    def drain_expert(e, carry):
        length = pad_ref[1, e]
        for sz in block_sizes:
            pl.when(jnp.bitwise_and(length, sz) != 0)(lambda sz=sz: zero_rows(0, sz).wait())
        return carry

    def tail_tile_copy(tile):
        return pltpu.make_async_copy(zeros, xs_hbm.at[pl.ds(tile * MOE_TILE, MOE_TILE)], zsem)

    @pl.when(i == 0)
    def _():
        zeros[...] = jnp.zeros_like(zeros)
        lax.fori_loop(0, N_EXP, fill_expert, 0)
        lax.fori_loop(nv_ref[0], n_tiles, lambda tile, c: (tail_tile_copy(tile).start(), c)[1], 0)

    @pl.when(i == n - 1)
    def _():
        pl.when(n >= 2)(lambda: wait_rows(1 - slot))
        wait_rows(slot)
        lax.fori_loop(0, N_EXP, drain_expert, 0)
        lax.fori_loop(nv_ref[0], n_tiles, lambda tile, c: (tail_tile_copy(tile).wait(), c)[1], 0)


def _dispatch_call(x, g, pos, n_valid, pad_meta, rows):
    t, d = x.shape
    grid_spec = pltpu.PrefetchScalarGridSpec(
        num_scalar_prefetch=3,
        grid=(t // ROW_TILE,),
        in_specs=[pl.BlockSpec((ROW_TILE, d), lambda i, pos_ref, nv, pad: (i, 0)),
                  pl.BlockSpec((1, d), lambda i, pos_ref, nv, pad: (0, 0))],
        out_specs=pl.BlockSpec(memory_space=pl.ANY),
        scratch_shapes=[pltpu.VMEM((2, ROW_TILE, d), F32), pltpu.VMEM((MOE_TILE, d), F32),
                        pltpu.SemaphoreType.DMA((2,)), pltpu.SemaphoreType.DMA(())],
    )
    return pl.pallas_call(
        _dispatch_kernel,
        grid_spec=grid_spec,
        out_shape=jax.ShapeDtypeStruct((rows, d), F32),
        compiler_params=_params(("arbitrary",), 32),
        name="moe_dispatch",
    )(pos, n_valid, pad_meta, x, g.reshape(1, d))


def _moe_ffn_kernel(tm_ref, nv_ref, x_ref, wgu_hbm, wd_hbm, o_ref,
                    wgu_f, wd_f, wsems, wgu_b, wd_b, *, layer):
    i = pl.program_id(0)
    nv = nv_ref[0]
    expert, first, wslot, next_expert = tm_ref[0, i], tm_ref[1, i], tm_ref[2, i], tm_ref[3, i]

    def weight_copies(e, s):
        return (pltpu.make_async_copy(wgu_hbm.at[layer, e], wgu_f.at[s], wsems.at[0, s]),
                pltpu.make_async_copy(wd_hbm.at[layer, e], wd_f.at[s], wsems.at[1, s]))

    @pl.when(i == 0)
    def _():
        for c in weight_copies(expert, 0):
            c.start()

    @pl.when(first == 1)
    def _():
        @pl.when(next_expert >= 0)
        def _():
            for c in weight_copies(next_expert, 1 - wslot):
                c.start()

        for c in weight_copies(expert, wslot):
            c.wait()
        wgu_b[...] = wgu_f[wslot].astype(BF16)
        wd_b[...] = wd_f[wslot].astype(BF16)

    @pl.when(i < nv)
    def _():
        h = jnp.dot(x_ref[...].astype(BF16), wgu_b[...], preferred_element_type=F32)
        hg = h[:, :D_FF_E]
        hu = h[:, D_FF_E:]
        act = (hg * _sigmoid(hg) * hu).astype(BF16)
        o_ref[...] = jnp.dot(act, wd_b[...], preferred_element_type=F32)


def _moe_ffn_call(xs, w_gu, w_down, layer, tile_meta, n_valid):
    rows, d = xs.shape
    n_tiles = tile_meta.shape[1]
    valid_tile = lambda i, tm, nv: (jnp.minimum(i, nv[0] - 1), 0)
    grid_spec = pltpu.PrefetchScalarGridSpec(
        num_scalar_prefetch=2,
        grid=(n_tiles,),
        in_specs=[pl.BlockSpec((MOE_TILE, d), valid_tile),
                  pl.BlockSpec(memory_space=pl.ANY), pl.BlockSpec(memory_space=pl.ANY)],
        out_specs=pl.BlockSpec((MOE_TILE, d), valid_tile),
        scratch_shapes=[
            pltpu.VMEM((2, d, 2 * D_FF_E), F32),
            pltpu.VMEM((2, D_FF_E, d), F32),
            pltpu.SemaphoreType.DMA((2, 2)),
            pltpu.VMEM((d, 2 * D_FF_E), BF16),
            pltpu.VMEM((D_FF_E, d), BF16),
        ],
    )
    return pl.pallas_call(
        functools.partial(_moe_ffn_kernel, layer=layer),
        grid_spec=grid_spec,
        out_shape=jax.ShapeDtypeStruct((rows, d), F32),
        input_output_aliases={2: 0},
        compiler_params=_params(("arbitrary",), 48),
        name="moe_experts",
    )(tile_meta, n_valid, xs, w_gu, w_down)


def _combine_kernel(pos_ref, x_ref, w_ref, y_hbm, g_ref, *rest, emit_x, norm_tiles):
    outs = rest[:int(emit_x) + len(norm_tiles)]
    buf, sems = rest[int(emit_x) + len(norm_tiles):]
    i = pl.program_id(0)
    n = pl.num_programs(0)
    slot = lax.rem(i, 2)
    tm = x_ref.shape[0]

    def gather(tile, dst_slot):
        n_tok = n * tm
        for r in range(tm):
            for kk in range(2):
                pltpu.make_async_copy(y_hbm.at[pl.ds(pos_ref[kk * n_tok + tile * tm + r], 1)],
                                      buf.at[dst_slot, kk, pl.ds(r, 1)], sems.at[dst_slot]).start(priority=kk)

    @pl.when(i == 0)
    def _():
        gather(0, 0)

    for s in range(2):
        pl.when(jnp.logical_and(i + 1 < n, slot == s))(functools.partial(gather, i + 1, 1 - s))

    for kk in range(2):
        pltpu.make_async_copy(y_hbm.at[pl.ds(0, tm)], buf.at[slot, kk], sems.at[slot]).wait()
    xo = x_ref[...] + (w_ref[:, 0:1] * buf[slot, 0] + w_ref[:, 1:2] * buf[slot, 1])
    norm_refs = outs[int(emit_x):]
    if emit_x:
        outs[0][...] = xo

    def write_norm(p):
        norm_refs[p][...] = _rms(xo, g_ref[...]).astype(norm_refs[p].dtype)

    _for_each_part(i, norm_tiles, write_norm)


def _combine_call(x, ws, y_rows, pos, g, emit_x, norm_dtype, norm_rows):
    t, d = x.shape
    tm = MOE_TILE
    row = lambda i, pos_ref: (i, 0)
    out_specs, out_shape = [], []
    if emit_x:
        out_specs.append(pl.BlockSpec((tm, d), row))
        out_shape.append(jax.ShapeDtypeStruct((t, d), F32))
    norm_parts = [jax.ShapeDtypeStruct((r, d), norm_dtype) for r in norm_rows]
    out_specs += _part_specs(norm_parts, tm, d, row)
    out_shape += norm_parts
    grid_spec = pltpu.PrefetchScalarGridSpec(
        num_scalar_prefetch=1,
        grid=(t // tm,),
        in_specs=[pl.BlockSpec((tm, d), row), pl.BlockSpec((tm, LANES), row), pl.BlockSpec(memory_space=pl.ANY),
                  pl.BlockSpec((1, d), lambda i, pos_ref: (0, 0))],
        out_specs=out_specs,
        scratch_shapes=[pltpu.VMEM((2, 2, tm, d), F32), pltpu.SemaphoreType.DMA((2,))],
    )
    return pl.pallas_call(
        functools.partial(_combine_kernel, emit_x=emit_x, norm_tiles=_part_tiles(norm_parts, tm)),
        grid_spec=grid_spec,
        out_shape=out_shape,
        compiler_params=_params(("arbitrary",), 40),
        name="moe_combine",
    )(pos, x, ws, y_rows, g.reshape(1, d))


def _moe_layer(x, g_ffn, w_grp, b_grp, w_exp, b_exp, w_gu, w_down, layer, g_next, emit_x, norm_dtype, norm_rows):
    ids, ws = _router_call(x, g_ffn, w_grp, b_grp, w_exp, b_exp)
    tile_meta, n_valid, pad_meta, pos = _moe_plan(ids[:, 0], ids[:, 1])
    xs = _dispatch_call(x, g_ffn, pos, n_valid, pad_meta, tile_meta.shape[1] * MOE_TILE)
    y_rows = _moe_ffn_call(xs, w_gu, w_down, layer, tile_meta, n_valid)
    return _combine_call(x, ws, y_rows, pos, g_next, emit_x, norm_dtype, norm_rows)


def _s5_prep_kernel(are_ref, aim_ref, ldt_ref, bre_ref, bim_ref, lre_ref, lim_ref, bbr_ref, bbi_ref):
    are = are_ref[...]
    aim = aim_ref[...]
    dt = jnp.exp(ldt_ref[...])
    mag = jnp.exp(are * dt)
    lr = mag * jnp.cos(aim * dt)
    li = mag * jnp.sin(aim * dt)
    lre_ref[...] = lr
    lim_ref[...] = li
    den = are * are + aim * aim
    nr = lr - 1.0
    cr = (nr * are + li * aim) / den
    ci = (li * are - nr * aim) / den
    bre = bre_ref[...]
    bim = bim_ref[...]
    bbr_ref[...] = cr * bre - ci * bim
    bbi_ref[...] = cr * bim + ci * bre


def _s5_prep_call(a_re, a_im, log_dt, b_re, b_im):
    g, p, c = b_re.shape
    shp = (g, p * c)
    expand = lambda a: jnp.broadcast_to(a[:, :, None], (g, p, c)).reshape(shp)
    ldt = jnp.broadcast_to(log_dt[:, None], shp)
    outs = pl.pallas_call(
        _s5_prep_kernel,
        out_shape=[jax.ShapeDtypeStruct(shp, F32)] * 4,
        name="s5_discretize",
    )(expand(a_re), expand(a_im), ldt, b_re.reshape(shp), b_im.reshape(shp))
    lre, lim, bbr, bbi = outs
    return lre[:, ::c], lim[:, ::c], bbr, bbi


def _s5_compact(bbr, bbi, c_re, c_im):
    def b_part(bb):
        return bb.reshape(S5_FB, 16, S5_P, S5_GC).transpose(0, 1, 3, 2).reshape(S5_FB, 16 * S5_GC, S5_P)

    def c_part(cc):
        return cc.reshape(S5_FB, 16, S5_GC, S5_P).transpose(0, 3, 1, 2).reshape(S5_FB, S5_P, 16 * S5_GC)

    return b_part(bbr), b_part(bbi), c_part(c_re), c_part(c_im)


def _s5_build_mats(bbr_ref, bbi_ref, ccr_ref, cci_ref, bd_ref, cd_ref):
    nch, nst = 16 * S5_GC, 16 * S5_P
    iota = lambda shape, dim: lax.broadcasted_iota(jnp.int32, shape, dim)
    rep = ((iota((S5_P, nst), 1) & (S5_P - 1)) == iota((S5_P, nst), 0)).astype(BF16)
    rep_t = ((iota((nst, S5_P), 0) & (S5_P - 1)) == iota((nst, S5_P), 1)).astype(BF16)
    gc_bits, p_bits = S5_GC.bit_length() - 1, S5_P.bit_length() - 1
    diag = (iota((nch, nst), 0) >> gc_bits) == (iota((nch, nst), 1) >> p_bits)
    diag_t = (iota((nst, nch), 0) >> p_bits) == (iota((nst, nch), 1) >> gc_bits)
    for fb in range(S5_FB):
        for part, src in enumerate((bbr_ref, bbi_ref)):
            full = jnp.dot(src[fb].astype(BF16), rep, preferred_element_type=F32)
            bd_ref[fb, :, part * nst:(part + 1) * nst] = jnp.where(diag, full, 0.0).astype(BF16)
        for part, (src, sign) in enumerate(((ccr_ref, 1.0), (cci_ref, -1.0))):
            full = jnp.dot(rep_t, src[fb].astype(BF16), preferred_element_type=F32)
            cd_ref[fb, part * nst:(part + 1) * nst, :] = jnp.where(diag_t, sign * full, 0.0).astype(BF16)


def _s5_kernel(u_ref, bbr_ref, bbi_ref, ccr_ref, cci_ref, lre_ref, lim_ref, d_ref, h0r_ref, h0i_ref,
               z_ref, hr_ref, hi_ref, scr, hst, bd_ref, cd_ref, *, seg, carry):
    rows = u_ref.shape[0]
    n_seq = rows // seg
    ch = 16 * S5_GC
    half = SUBLANES * S5_PITCH
    blk = 2 * half

    @pl.when(pl.program_id(0) == 0)
    def _():
        _s5_build_mats(bbr_ref, bbi_ref, ccr_ref, cci_ref, bd_ref, cd_ref)

    for fb in range(S5_FB):
        bu = jnp.dot(u_ref[:, fb * ch:(fb + 1) * ch].astype(BF16), bd_ref[fb], preferred_element_type=F32)
        for c in range(2 * SUBLANES):
            r0 = fb * blk + c * S5_PITCH
            scr[r0:r0 + rows, :] = bu[:, c * LANES:(c + 1) * LANES]

    lre = [lre_ref[fb * SUBLANES:(fb + 1) * SUBLANES, :] for fb in range(S5_FB)]
    lim = [lim_ref[fb * SUBLANES:(fb + 1) * SUBLANES, :] for fb in range(S5_FB)]

    def scan(row0, state):
        def step(l, st):
            new = []
            for fb in range(S5_FB):
                hr, hi = st[2 * fb], st[2 * fb + 1]
                ire = pl.ds(fb * blk + row0 + l, SUBLANES, stride=S5_PITCH)
                iim = pl.ds(fb * blk + half + row0 + l, SUBLANES, stride=S5_PITCH)
                nr = lre[fb] * hr - lim[fb] * hi + scr[ire, :]
                ni = lre[fb] * hi + lim[fb] * hr + scr[iim, :]
                scr[ire, :] = nr
                scr[iim, :] = ni
                new += [nr, ni]
            return tuple(new)
        return lax.fori_loop(0, seg, step, state, unroll=8)

    def load_state(rref, iref, lead):
        st = []
        for fb in range(S5_FB):
            st += [rref[lead + (slice(fb * SUBLANES, (fb + 1) * SUBLANES), slice(None))],
                   iref[lead + (slice(fb * SUBLANES, (fb + 1) * SUBLANES), slice(None))]]
        return tuple(st)

    def store_state(st, rref, iref, lead):
        for fb in range(S5_FB):
            rref[lead + (slice(fb * SUBLANES, (fb + 1) * SUBLANES), slice(None))] = st[2 * fb]
            iref[lead + (slice(fb * SUBLANES, (fb + 1) * SUBLANES), slice(None))] = st[2 * fb + 1]

    if carry:
        @pl.when(pl.program_id(0) == 0)
        def _():
            hst[0] = h0r_ref[0]
            hst[1] = h0i_ref[0]
        hre_scr = hst.at[0]
        him_scr = hst.at[1]
        st = scan(0, load_state(hre_scr, him_scr, ()))
        store_state(st, hre_scr, him_scr, ())
        store_state(st, hr_ref, hi_ref, (0,))
    else:
        for s in range(n_seq):
            st = scan(s * seg, load_state(h0r_ref, h0i_ref, (s,)))
            store_state(st, hr_ref, hi_ref, (s,))

    for fb in range(S5_FB):
        hf = jnp.concatenate(
            [scr[fb * blk + c * S5_PITCH: fb * blk + c * S5_PITCH + rows, :] for c in range(2 * SUBLANES)],
            axis=1).astype(BF16)
        y = jnp.dot(hf, cd_ref[fb], preferred_element_type=F32)
        cs = slice(fb * ch, (fb + 1) * ch)
        yy = y + d_ref[:, cs] * u_ref[:, cs]
        z_ref[:, cs] = jax.nn.gelu(yy).astype(z_ref.dtype)


def _s5_call(u, mats, lre, lim, d_skip, h0r, h0i, row_block0, n_blocks, seg, carry):
    t, d = u.shape
    ch, nst = 16 * S5_GC, 16 * S5_P
    n_seq = S5_ROWS // seg
    ns_total = h0r.shape[0]
    st_idx = (lambda i: (0, 0, 0)) if carry else (lambda i: (i, 0, 0))
    fixed2 = lambda i: (0, 0)
    fixed3 = lambda i: (0, 0, 0)
    st_rows = S5_FB * SUBLANES
    return pl.pallas_call(
        functools.partial(_s5_kernel, seg=seg, carry=carry),
        grid=(n_blocks,),
        in_specs=[
            pl.BlockSpec((S5_ROWS, d), lambda i: (row_block0 + i, 0)),
            *[pl.BlockSpec(m.shape, fixed3) for m in mats],
            pl.BlockSpec(lre.shape, fixed2),
            pl.BlockSpec(lim.shape, fixed2),
            pl.BlockSpec((1, d), fixed2),
            pl.BlockSpec((n_seq, st_rows, LANES), st_idx),
            pl.BlockSpec((n_seq, st_rows, LANES), st_idx),
        ],
        out_specs=[
            pl.BlockSpec((S5_ROWS, d), lambda i: (i, 0)),
            pl.BlockSpec((n_seq, st_rows, LANES), st_idx),
            pl.BlockSpec((n_seq, st_rows, LANES), st_idx),
        ],
        out_shape=[
            jax.ShapeDtypeStruct((n_blocks * S5_ROWS, d), BF16),
            jax.ShapeDtypeStruct((ns_total, st_rows, LANES), F32),
            jax.ShapeDtypeStruct((ns_total, st_rows, LANES), F32),
        ],
        scratch_shapes=[
            pltpu.VMEM((2 * SUBLANES * S5_FB * S5_PITCH, LANES), F32),
            pltpu.VMEM((2, st_rows, LANES), F32),
            pltpu.VMEM((S5_FB, ch, 2 * nst), BF16),
            pltpu.VMEM((S5_FB, 2 * nst, ch), BF16),
        ],
        compiler_params=_params(("arbitrary",), 56),
        name="s5_scan",
    )(u, *mats, lre, lim, d_skip.reshape(1, d), h0r, h0i)


def kernel(x_prompt, x_sample, state_ret, state_s5_re, state_s5_im, norm_mix_g, norm_ffn_g, norm_final_g,
           ret_w_in, ret_gn_g, ret_w_out, s5_w_in, s5_a_re, s5_a_im, s5_log_dt, s5_b_re, s5_b_im,
           s5_c_re, s5_c_im, s5_d, s5_w_out, moe_w_grp, moe_b_grp, moe_w_exp, moe_b_exp, moe_w_gu, moe_w_down):
    bp, seq, d = x_prompt.shape
    db, dseq, _ = x_sample.shape
    assert d == D_MODEL and bp == 1
    assert seq % max(ROW_TILE, RET_ROWS) == 0 and seq % CHUNK == 0
    assert dseq < CHUNK and dseq % 16 == 0 and S5_ROWS % dseq == 0
    tp = bp * seq
    ts = db * dseq
    t = tp + ts
    assert ts % ROW_TILE == 0 and ts % S5_ROWS == 0

    x_parts = [x_prompt.reshape(tp, d), x_sample.reshape(ts, d)]

    half = RET_DK // 2
    freqs = ROPE_BASE ** (-jnp.arange(half, dtype=F32) / half)
    pos = jnp.concatenate([jnp.arange(seq, dtype=F32), jnp.tile(PAST_LEN + jnp.arange(dseq, dtype=F32), db)])
    ang = pos[:, None] * freqs[None, :]
    cos_t = jnp.cos(ang)
    sin_t = jnp.sin(ang)

    xn = _norm_call(x_parts, norm_mix_g[0], BF16)
    tn = 2048
    first_col = lambda j: 0
    qk = _mm_call([xn], ret_w_in[0], [0], tn, 2 * D_MODEL, [([cos_t], half, first_col), ([sin_t], half, first_col)],
                  BF16, functools.partial(_epi_rope, n_q_blocks=D_MODEL // tn), 56, "ret_qk_proj")
    vg = _mm_call([xn], ret_w_in[0], [2 * D_MODEL // tn], tn, 2 * RET_VW, [], BF16, _epi_plain, 56, "ret_vg_proj")
    og_p, ret_p = _ret_prompt_call(qk, vg, ret_gn_g[0], tp)
    og_s, ret_s = _ret_sample_call(qk, vg, ret_gn_g[0], state_ret[0], tp, dseq)
    tn = 512
    same_col = lambda j: j
    x1 = _mm_resident_call([og_p, og_s], ret_w_out[0], [0], tn, d, [(x_parts, tn, same_col)], F32, _epi_residual,
                           58, "ret_out_proj")

    x2, xn = _moe_layer(x1, norm_ffn_g[0], moe_w_grp[0], moe_b_grp[0], moe_w_exp[0], moe_b_exp[0],
                        moe_w_gu, moe_w_down, 0, norm_mix_g[1], True, BF16, (t,))

    u = _mm_call([xn], s5_w_in[0], [0], 2048, d, [], F32, _epi_plain, 56, "s5_in_proj")
    lre, lim, bbr, bbi = _s5_prep_call(s5_a_re[0], s5_a_im[0], s5_log_dt[0], s5_b_re[0], s5_b_im[0])
    mats = _s5_compact(bbr, bbi, s5_c_re[0], s5_c_im[0])
    st_shape = (S5_FB * SUBLANES, LANES)
    lre = lre.reshape(st_shape)
    lim = lim.reshape(st_shape)
    zero_state = jnp.zeros((1,) + st_shape, F32)
    z_p, hpr, hpi = _s5_call(u, mats, lre, lim, s5_d[0], zero_state, zero_state,
                             0, tp // S5_ROWS, S5_ROWS, True)
    z_s, hsr, hsi = _s5_call(u, mats, lre, lim, s5_d[0],
                             state_s5_re[0].reshape((db,) + st_shape), state_s5_im[0].reshape((db,) + st_shape),
                             tp // S5_ROWS, ts // S5_ROWS, dseq, False)
    x3 = _mm_resident_call([z_p, z_s], s5_w_out[0], [0, d // tn], tn, d, [([x2], tn, same_col)], F32,
                           _epi_glu_residual, 48, "s5_out_proj")

    y_p, y_s = _moe_layer(x3, norm_ffn_g[1], moe_w_grp[1], moe_b_grp[1], moe_w_exp[1], moe_b_exp[1],
                          moe_w_gu, moe_w_down, 1, norm_final_g, False, F32, (tp, ts))

    s5_shape = (1, -1, S5_GROUPS, S5_P)
    return (y_p.reshape(bp, seq, d), y_s.reshape(db, dseq, d),
            ret_p.reshape(1, bp, RET_HEADS, RET_DK, RET_DV), ret_s.reshape((1,) + ret_s.shape),
            hpr.reshape(s5_shape), hpi.reshape(s5_shape), hsr.reshape(s5_shape), hsi.reshape(s5_shape))
```
